```python
import math
import jax, jax.numpy as jnp
from jax import lax
import numpy as np


D_MODEL = 1024
BATCH = 2
SEQ = 8192
DEPTH = 2
DEC_BATCH = 128
DEC_SEQ = 1
PAST_LEN = 2048
PAGE_SIZE = 128

D_A = 512
CONV_A = 31
N_HEADS_B = 8
HEAD_DIM = 64
D_B = N_HEADS_B * HEAD_DIM
MOBA_BLOCK = 256
MOBA_TOPK = 3
Q_CHUNK = 128
NUM_BUCKETS = 32
MAX_DISTANCE = 128
D_C = 1024
CONV_C = 3
D_FF = 2816
CONV_F = 3
N_EVEN = (DEPTH + 1) // 2
N_ODD = DEPTH // 2
EPS = 1e-6

kernel_name = 'hybrid_conformer_moba_shortconv_decode_step'


def rmsnorm(x, g):
    xf = x.astype(jnp.float32)
    y = xf * lax.rsqrt(jnp.mean(xf * xf, axis=-1, keepdims=True) + EPS)
    return (y * g.astype(jnp.float32)).astype(x.dtype)


def layernorm(x, g, b):
    xf = x.astype(jnp.float32)
    mu = jnp.mean(xf, axis=-1, keepdims=True)
    var = jnp.mean(jnp.square(xf - mu), axis=-1, keepdims=True)
    y = (xf - mu) * lax.rsqrt(var + EPS)
    return (y * g.astype(jnp.float32) + b.astype(jnp.float32)).astype(x.dtype)


def causal_dwconv(x_ext, w):
    c = x_ext.shape[-1]
    return lax.conv_general_dilated(x_ext, w.astype(x_ext.dtype)[:, None, :], window_strides=(1,), padding='VALID',
                                    dimension_numbers=('NWC', 'WIO', 'NWC'), feature_group_count=c)


def t5_bucket(dist):
    n = jnp.maximum(dist, 0)
    max_exact = NUM_BUCKETS // 2
    nf = jnp.maximum(n, 1).astype(jnp.float32)
    large = max_exact + (jnp.log(nf / max_exact) / math.log(MAX_DISTANCE / max_exact)
                         * (NUM_BUCKETS - max_exact)).astype(jnp.int32)
    large = jnp.minimum(large, NUM_BUCKETS - 1)
    return jnp.where(n < max_exact, n, large)


def moba_select_attend(q, q_pos, kb, vb, k_mean, rel_bias):
    t = q.shape[1]
    nb = kb.shape[2]
    tk = min(MOBA_TOPK, nb)
    own = q_pos // MOBA_BLOCK
    gate = jnp.einsum('bthd,bhnd->bhtn', q.astype(jnp.float32), k_mean)
    fully_past = jnp.arange(nb, dtype=jnp.int32)[None, :] < own[:, None]
    gate = jnp.where(fully_past, gate, -jnp.inf)
    _, top = lax.top_k(gate, tk)
    own_b = jnp.broadcast_to(own[None, None, :, None], top.shape[:3] + (1,))
    sel = jnp.concatenate([top.astype(jnp.int32), own_b.astype(jnp.int32)], axis=-1)
    slot_ok = jnp.concatenate([jnp.arange(tk, dtype=jnp.int32)[None, :] < own[:, None],
                               jnp.ones((t, 1), dtype=bool)], axis=-1)
    gather = jax.vmap(jax.vmap(lambda blocks, idx: blocks[idx]))
    k_sel = gather(kb, sel)
    v_sel = gather(vb, sel)
    key_pos = sel[..., None] * MOBA_BLOCK + jnp.arange(MOBA_BLOCK, dtype=jnp.int32)
    dist = q_pos[:, None, None] - key_pos
    mask = slot_ok[:, :, None] & (dist >= 0)
    bias = rel_bias.T.astype(jnp.float32)[jnp.arange(N_HEADS_B)[:, None, None, None], t5_bucket(dist)]
    logits = jnp.einsum('bthd,bhtsjd->bhtsj', q, k_sel, preferred_element_type=jnp.float32) * (HEAD_DIM ** -0.5) + bias
    logits = jnp.where(mask, logits, -jnp.inf)
    b, h, _, s, blk = logits.shape
    probs = jax.nn.softmax(logits.reshape(b, h, t, s * blk), axis=-1).reshape(b, h, t, s, blk)
    return jnp.einsum('bhtsj,bhtsjd->bthd', probs.astype(v_sel.dtype), v_sel)


def moba_attention(q, k_new, v_new, past_k, past_v, rel_bias):
    b, t = q.shape[:2]
    p = past_k.shape[1]
    l = p + t
    l_pad = -(-l // MOBA_BLOCK) * MOBA_BLOCK
    pad = ((0, 0), (0, l_pad - l), (0, 0), (0, 0))
    k_all = jnp.pad(jnp.concatenate([past_k, k_new], axis=1), pad)
    v_all = jnp.pad(jnp.concatenate([past_v, v_new], axis=1), pad)
    nb = l_pad // MOBA_BLOCK
    kb = k_all.reshape(b, nb, MOBA_BLOCK, N_HEADS_B, HEAD_DIM).transpose(0, 3, 1, 2, 4)
    vb = v_all.reshape(b, nb, MOBA_BLOCK, N_HEADS_B, HEAD_DIM).transpose(0, 3, 1, 2, 4)
    k_mean = jnp.mean(kb.astype(jnp.float32), axis=3)
    q_pos = p + jnp.arange(t, dtype=jnp.int32)
    if t > Q_CHUNK and t % Q_CHUNK == 0:
        nc = t // Q_CHUNK
        qc = q.reshape(b, nc, Q_CHUNK, N_HEADS_B, HEAD_DIM).transpose(1, 0, 2, 3, 4)
        pc = q_pos.reshape(nc, Q_CHUNK)
        out = lax.map(lambda qp: moba_select_attend(qp[0], qp[1], kb, vb, k_mean, rel_bias), (qc, pc))
        return out.transpose(1, 0, 2, 3, 4).reshape(b, t, N_HEADS_B, HEAD_DIM)
    return moba_select_attend(q, q_pos, kb, vb, k_mean, rel_bias)


def even_mixer(h, prefix_a, past_k, past_v, rel_bias, w_in, conv_w, conv_b, ln_g, ln_b, q_g, k_g, w_out):
    b, t, _ = h.shape
    z = h @ w_in
    a_val, a_gate, q, k, v = jnp.split(z, [D_A, 2 * D_A, 2 * D_A + D_B, 2 * D_A + 2 * D_B], axis=-1)
    a = a_val * jax.nn.sigmoid(a_gate)
    a_ext = jnp.concatenate([prefix_a, a], axis=1)
    a_out = jax.nn.silu(layernorm(causal_dwconv(a_ext, conv_w) + conv_b, ln_g, ln_b))
    q = rmsnorm(q.reshape(b, t, N_HEADS_B, HEAD_DIM), q_g)
    k = rmsnorm(k.reshape(b, t, N_HEADS_B, HEAD_DIM), k_g)
    v = v.reshape(b, t, N_HEADS_B, HEAD_DIM)
    o = moba_attention(q, k, v, past_k, past_v, rel_bias)
    y = jnp.concatenate([a_out, o.reshape(b, t, D_B)], axis=-1) @ w_out
    return y, a_ext[:, -(CONV_A - 1):], k, v


def odd_mixer(h, prefix_c, w_in, conv_w, w_out):
    g_b, g_c, u = jnp.split(h @ w_in, 3, axis=-1)
    ext = jnp.concatenate([prefix_c, g_c * u], axis=1)
    y = (g_b * causal_dwconv(ext, conv_w)) @ w_out
    return y, ext[:, -(CONV_C - 1):]


def conv_ffn(h, prefix_f, w_up, conv_w, conv_b, w_down):
    ext = jnp.concatenate([prefix_f, h @ w_up], axis=1)
    g, u = jnp.split(causal_dwconv(ext, conv_w) + conv_b, 2, axis=-1)
    return (jax.nn.silu(g) * u) @ w_down, ext[:, -(CONV_F - 1):]


def run_trunk(x, past_k, past_v, pre_a, pre_c, pre_f, rel_bias, w_even, w_odd, w_ffn):
    norm_mix_e, w_in_e, conv_a_w, conv_a_b, ln_a_g, ln_a_b, q_norm_g, k_norm_g, w_out_e = w_even
    norm_mix_o, w_in_o, conv_c_w, w_out_o = w_odd
    norm_ffn, w_up, conv_f_w, conv_f_b, w_down = w_ffn
    new_k, new_v, new_a, new_c, new_f = [], [], [], [], []
    for layer in range(DEPTH):
        i = layer // 2
        if layer % 2 == 0:
            y, sa, k, v = even_mixer(rmsnorm(x, norm_mix_e[i]), pre_a[i], past_k[i], past_v[i], rel_bias,
                                     w_in_e[i], conv_a_w[i], conv_a_b[i], ln_a_g[i], ln_a_b[i],
                                     q_norm_g[i], k_norm_g[i], w_out_e[i])
            new_a.append(sa)
            new_k.append(k)
            new_v.append(v)
        else:
            y, sc = odd_mixer(rmsnorm(x, norm_mix_o[i]), pre_c[i], w_in_o[i], conv_c_w[i], w_out_o[i])
            new_c.append(sc)
        x = x + y
        y, sf = conv_ffn(rmsnorm(x, norm_ffn[layer]), pre_f[layer], w_up[layer], conv_f_w[layer],
                         conv_f_b[layer], w_down[layer])
        new_f.append(sf)
        x = x + y
    return x, jnp.stack(new_k), jnp.stack(new_v), jnp.stack(new_a), jnp.stack(new_c), jnp.stack(new_f)


def setup_inputs(seed: int = 0) -> dict:
    key = jax.random.key(seed)
    ks = jax.random.split(key, 32)
    f32 = jnp.float32
    n_pages = PAST_LEN // PAGE_SIZE
    n_used = DEC_BATCH * n_pages
    n_pool = (n_used * 5) // 4

    def nrm(k, shape, scale):
        return scale * jax.random.normal(k, shape, f32)

    def gain(k, shape):
        return 1.0 + 0.01 * jax.random.normal(k, shape, f32)

    page_table = jax.random.permutation(ks[0], n_pool)[:n_used].reshape(DEC_BATCH, n_pages).astype(jnp.int32)
    return {
        'x_prompt': nrm(ks[1], (BATCH, SEQ, D_MODEL), 1.0),
        'x_sample': nrm(ks[2], (DEC_BATCH, DEC_SEQ, D_MODEL), 1.0),
        'cache_k': nrm(ks[3], (N_EVEN, n_pool, PAGE_SIZE, N_HEADS_B, HEAD_DIM), 1.0),
        'cache_v': nrm(ks[4], (N_EVEN, n_pool, PAGE_SIZE, N_HEADS_B, HEAD_DIM), 1.0),
        'state_conv_a': nrm(ks[5], (N_EVEN, DEC_BATCH, CONV_A - 1, D_A), 0.5),
        'state_conv_c': nrm(ks[6], (N_ODD, DEC_BATCH, CONV_C - 1, D_C), 0.5),
        'state_ffn': nrm(ks[7], (DEPTH, DEC_BATCH, CONV_F - 1, 2 * D_FF), 1.0),
        'page_table': page_table,
        'rel_bias': nrm(ks[8], (NUM_BUCKETS, N_HEADS_B), 0.5),
        'norm_mix_e': gain(ks[9], (N_EVEN, D_MODEL)),
        'w_in_e': nrm(ks[10], (N_EVEN, D_MODEL, 2 * D_A + 3 * D_B), D_MODEL ** -0.5),
        'conv_a_w': nrm(ks[11], (N_EVEN, CONV_A, D_A), CONV_A ** -0.5),
        'conv_a_b': nrm(ks[12], (N_EVEN, D_A), 0.01),
        'ln_a_g': gain(ks[13], (N_EVEN, D_A)),
        'ln_a_b': nrm(ks[14], (N_EVEN, D_A), 0.01),
        'q_norm_g': gain(ks[15], (N_EVEN, HEAD_DIM)),
        'k_norm_g': gain(ks[16], (N_EVEN, HEAD_DIM)),
        'w_out_e': nrm(ks[17], (N_EVEN, D_A + D_B, D_MODEL), (D_A + D_B) ** -0.5),
        'norm_mix_o': gain(ks[18], (N_ODD, D_MODEL)),
        'w_in_o': nrm(ks[19], (N_ODD, D_MODEL, 3 * D_C), D_MODEL ** -0.5),
        'conv_c_w': nrm(ks[20], (N_ODD, CONV_C, D_C), CONV_C ** -0.5),
        'w_out_o': nrm(ks[21], (N_ODD, D_C, D_MODEL), D_C ** -0.5),
        'norm_ffn': gain(ks[22], (DEPTH, D_MODEL)),
        'w_up': nrm(ks[23], (DEPTH, D_MODEL, 2 * D_FF), D_MODEL ** -0.5),
        'conv_f_w': nrm(ks[24], (DEPTH, CONV_F, 2 * D_FF), CONV_F ** -0.5),
        'conv_f_b': nrm(ks[25], (DEPTH, 2 * D_FF), 0.01),
        'w_down': nrm(ks[26], (DEPTH, D_FF, D_MODEL), D_FF ** -0.5),
    }


def reference(x_prompt, x_sample, cache_k, cache_v, state_conv_a, state_conv_c, state_ffn, page_table,
              rel_bias, norm_mix_e, w_in_e, conv_a_w, conv_a_b, ln_a_g, ln_a_b, q_norm_g, k_norm_g, w_out_e,
              norm_mix_o, w_in_o, conv_c_w, w_out_o, norm_ffn, w_up, conv_f_w, conv_f_b, w_down):
    w_even = (norm_mix_e, w_in_e, conv_a_w, conv_a_b, ln_a_g, ln_a_b, q_norm_g, k_norm_g, w_out_e)
    w_odd = (norm_mix_o, w_in_o, conv_c_w, w_out_o)
    w_ffn = (norm_ffn, w_up, conv_f_w, conv_f_b, w_down)
    n_ev, _, page, nh, hd = cache_k.shape

    b, t = x_prompt.shape[:2]
    dt = x_prompt.dtype
    empty = jnp.zeros((n_ev, b, 0, nh, hd), dt)
    pre_a = jnp.zeros((n_ev, b, CONV_A - 1, D_A), dt)
    pre_c = jnp.zeros((state_conv_c.shape[0], b, CONV_C - 1, D_C), dt)
    pre_f = jnp.zeros((DEPTH, b, CONV_F - 1, 2 * D_FF), dt)
    y_prompt, k_p, v_p, a_p, c_p, f_p = run_trunk(x_prompt, empty, empty, pre_a, pre_c, pre_f, rel_bias,
                                                  w_even, w_odd, w_ffn)
    k_prompt = k_p.reshape(n_ev, b, t // page, page, nh, hd)
    v_prompt = v_p.reshape(n_ev, b, t // page, page, nh, hd)

    bs, n_pages = page_table.shape
    past_k = cache_k[:, page_table].reshape(n_ev, bs, n_pages * page, nh, hd)
    past_v = cache_v[:, page_table].reshape(n_ev, bs, n_pages * page, nh, hd)
    y_sample, k_s, v_s, a_s, c_s, f_s = run_trunk(x_sample, past_k, past_v, state_conv_a, state_conv_c, state_ffn,
                                                  rel_bias, w_even, w_odd, w_ffn)
    return (y_prompt, y_sample, k_prompt, v_prompt, a_p, c_p, f_p, k_s, v_s, a_s, c_s, f_s)
```

```python
import functools
import math

import jax
import jax.numpy as jnp
from jax import lax
from jax.experimental import pallas as pl
from jax.experimental.pallas import tpu as pltpu

F32 = jnp.float32
BF16 = jnp.bfloat16

EPS = 1e-6
D_MODEL = 1024
D_A = 512
CONV_A = 31
N_HEADS = 8
HEAD_DIM = 64
D_B = N_HEADS * HEAD_DIM
MOBA_BLOCK = 256
MOBA_TOPK = 3
NUM_BUCKETS = 32
MAX_DISTANCE = 128
D_C = 1024
CONV_C = 3
D_FF = 2816
CONV_F = 3
QK_SCALE = HEAD_DIM ** -0.5

TILE_M = 512
CONV_ROWS = 32
HALO_A = 32
CH = 256
SUBLANES = 8
VMEM_LIMIT = 56 * 1024 * 1024


def _t5_thresholds():
    max_exact = NUM_BUCKETS // 2
    th = list(range(1, max_exact + 1))
    for k in range(1, NUM_BUCKETS - max_exact):
        th.append(math.ceil(max_exact * (MAX_DISTANCE / max_exact) ** (k / (NUM_BUCKETS - max_exact))))
    return tuple(th)


T5_THRESH = _t5_thresholds()


def _cparams(n_grid):
    return pltpu.CompilerParams(dimension_semantics=("arbitrary",) * n_grid, vmem_limit_bytes=VMEM_LIMIT)


def _const_spec(shape):
    nd = len(shape)
    return pl.BlockSpec(shape, lambda *_: (0,) * nd, pipeline_mode=pl.Buffered(1))


def _rmsnorm(x, g):
    return x * lax.rsqrt(jnp.mean(x * x, axis=-1, keepdims=True) + EPS) * g


def _split_dot(x, w_bf16):
    hi = x.astype(BF16)
    lo = (x - hi.astype(F32)).astype(BF16)
    return (jnp.dot(hi, w_bf16, preferred_element_type=F32)
            + jnp.dot(lo, w_bf16, preferred_element_type=F32))


def _head_rmsnorm(x, g, head_ones):
    ss = _split_dot(x * x, head_ones)
    return x * lax.rsqrt(ss * (1.0 / HEAD_DIM) + EPS) * g


def _silu(x):
    return x * jax.nn.sigmoid(x)


def _layernorm(x, g, b):
    mu = jnp.mean(x, axis=-1, keepdims=True)
    xc = x - mu
    var = jnp.mean(xc * xc, axis=-1, keepdims=True)
    return xc * lax.rsqrt(var + EPS) * g + b


def _in_proj_even(x, g, w, qg, kg, head_ones):
    h = _rmsnorm(x, g).astype(BF16)
    z = jnp.dot(h, w, preferred_element_type=F32)
    a = z[:, :D_A] * jax.nn.sigmoid(z[:, D_A:2 * D_A])
    q = _head_rmsnorm(z[:, 2 * D_A:2 * D_A + D_B], qg, head_ones) * QK_SCALE
    k = _head_rmsnorm(z[:, 2 * D_A + D_B:2 * D_A + 2 * D_B], kg, head_ones)
    v = z[:, 2 * D_A + 2 * D_B:]
    return a, q, k, v


def _prompt_inproj_body(x_ref, g_ref, w_ref, qg_ref, kg_ref, ho_ref,
                        a_ref, k_ref, v_ref, qt_ref, kh_ref, vt_ref, km_ref, *, tm):
    a, q, k, v = _in_proj_even(x_ref[0], g_ref[...], w_ref[...], qg_ref[...], kg_ref[...], ho_ref[...])
    a_ref[0] = a
    k_ref[0] = k
    v_ref[0] = v
    qt = q.T
    vt = v.T
    kb = k.astype(BF16)
    for i in range(tm // MOBA_BLOCK):
        r0 = i * MOBA_BLOCK
        qt_ref[0, :, i] = qt[:, r0:r0 + MOBA_BLOCK].reshape(N_HEADS, HEAD_DIM, MOBA_BLOCK).astype(BF16)
        vt_ref[0, :, i] = vt[:, r0:r0 + MOBA_BLOCK].reshape(N_HEADS, HEAD_DIM, MOBA_BLOCK).astype(BF16)
        for hh in range(N_HEADS):
            kh_ref[0, hh, i] = kb[r0:r0 + MOBA_BLOCK, hh * HEAD_DIM:(hh + 1) * HEAD_DIM]
        km_ref[0, i] = jnp.mean(k[r0:r0 + MOBA_BLOCK], axis=0, keepdims=True)


def _prompt_inproj(x, g, w, qg, kg, head_ones):
    b, t, d = x.shape
    tm = TILE_M
    nt = t // tm
    nb = t // MOBA_BLOCK
    bpt = tm // MOBA_BLOCK
    n_out = w.shape[1]
    tok = lambda bi, ti: (bi, ti, 0)
    blk5 = lambda bi, ti: (bi, 0, ti, 0, 0)
    return pl.pallas_call(
        functools.partial(_prompt_inproj_body, tm=tm),
        grid=(b, nt),
        in_specs=[
            pl.BlockSpec((1, tm, d), tok),
            _const_spec((1, d)),
            _const_spec((d, n_out)),
            _const_spec((1, D_B)),
            _const_spec((1, D_B)),
            _const_spec((D_B, D_B)),
        ],
        out_specs=[
            pl.BlockSpec((1, tm, D_A), tok),
            pl.BlockSpec((1, tm, D_B), tok),
            pl.BlockSpec((1, tm, D_B), tok),
            pl.BlockSpec((1, N_HEADS, bpt, HEAD_DIM, MOBA_BLOCK), blk5),
            pl.BlockSpec((1, N_HEADS, bpt, MOBA_BLOCK, HEAD_DIM), blk5),
            pl.BlockSpec((1, N_HEADS, bpt, HEAD_DIM, MOBA_BLOCK), blk5),
            pl.BlockSpec((1, bpt, 1, D_B), lambda bi, ti: (bi, ti, 0, 0)),
        ],
        out_shape=[
            jax.ShapeDtypeStruct((b, t, D_A), F32),
            jax.ShapeDtypeStruct((b, t, D_B), F32),
            jax.ShapeDtypeStruct((b, t, D_B), F32),
            jax.ShapeDtypeStruct((b, N_HEADS, nb, HEAD_DIM, MOBA_BLOCK), BF16),
            jax.ShapeDtypeStruct((b, N_HEADS, nb, MOBA_BLOCK, HEAD_DIM), BF16),
            jax.ShapeDtypeStruct((b, N_HEADS, nb, HEAD_DIM, MOBA_BLOCK), BF16),
            jax.ShapeDtypeStruct((b, nb, 1, D_B), F32),
        ],
        compiler_params=_cparams(2),
        name="prompt_inproj",
    )(x, g, w, qg, kg, head_ones)


def _t5_bias_scalar_table(dist, rb_ref, h):
    out = jnp.full(dist.shape, rb_ref[0, h], F32)
    for i, th in enumerate(T5_THRESH):
        out = jnp.where(dist >= th, rb_ref[i + 1, h], out)
    return out


def _moba_prompt_body(rb_ref, qt_ref, kh_ref, vt_ref, km_ref, ot_ref,
                      bdiag_ref, bsub_ref, selb_ref, m_ref, l_ref, acc_ref, *, nb):
    h = pl.program_id(1)
    n = pl.program_id(2)
    blk = MOBA_BLOCK

    @pl.when(n == 0)
    def _():
        ki = lax.broadcasted_iota(jnp.int32, (blk, blk), 0)
        qi = lax.broadcasted_iota(jnp.int32, (blk, blk), 1)
        d0 = qi - ki
        bdiag_ref[...] = jnp.where(d0 >= 0, _t5_bias_scalar_table(jnp.maximum(d0, 0), rb_ref, h), -jnp.inf)
        bsub_ref[...] = _t5_bias_scalar_table(d0 + blk, rb_ref, h)

    qt = qt_ref[0, 0, 0]

    km = km_ref[0, 0]
    km_hi = km.astype(BF16)
    km_lo = (km - km_hi.astype(F32)).astype(BF16)
    gate = (jnp.dot(km_hi, qt, preferred_element_type=F32)
            + jnp.dot(km_lo, qt, preferred_element_type=F32))
    bi = lax.broadcasted_iota(jnp.int32, (nb, blk), 0)
    avail = jnp.where(bi < n, 1.0, 0.0)
    far_bias = rb_ref[NUM_BUCKETS - 1, h]
    selb = jnp.full((nb, blk), -jnp.inf, F32)
    for _ in range(MOBA_TOPK):
        gm = jnp.where(avail > 0.0, gate, -jnp.inf)
        top = jnp.max(gm, axis=0, keepdims=True)
        first = jnp.where(avail > 0.0, jnp.where(gm == top, bi, nb), nb)
        pick = bi == jnp.min(first, axis=0, keepdims=True)
        selb = jnp.where(pick, jnp.where(bi == n - 1, 0.0, far_bias), selb)
        avail = jnp.where(pick, 0.0, avail)
    selb_ref[...] = selb

    s = jnp.dot(kh_ref[0, 0, n], qt, preferred_element_type=F32) + bdiag_ref[...]
    m0 = jnp.max(s, axis=0, keepdims=True)
    p = jnp.exp(s - m0)
    m_ref[...] = m0
    l_ref[...] = jnp.sum(p, axis=0, keepdims=True)
    acc_ref[...] = jnp.dot(vt_ref[0, 0, n], p.astype(BF16), preferred_element_type=F32)

    def update(j, bias, m, l, acc):
        s = jnp.dot(kh_ref[0, 0, j], qt, preferred_element_type=F32) + bias
        mn = jnp.maximum(m, jnp.max(s, axis=0, keepdims=True))
        alpha = jnp.exp(m - mn)
        p = jnp.exp(s - mn)
        l = alpha * l + jnp.sum(p, axis=0, keepdims=True)
        acc = alpha * acc + jnp.dot(vt_ref[0, 0, j], p.astype(BF16), preferred_element_type=F32)
        return mn, l, acc

    @pl.when(n >= 1)
    def _():
        j = n - 1
        bias = bsub_ref[...] + selb_ref[pl.ds(j, 1), :]
        m, l, acc = update(j, bias, m_ref[...], l_ref[...], acc_ref[...])
        m_ref[...] = m
        l_ref[...] = l
        acc_ref[...] = acc

    def far(j, carry):
        return update(j, selb_ref[pl.ds(j, 1), :], *carry)

    m, l, acc = lax.fori_loop(0, n - 1, far, (m_ref[...], l_ref[...], acc_ref[...]))
    ot_ref[0] = acc / l


def _moba_prompt(rel_bias, qt, kh, vt, kmh):
    b, nh, nb, hd, blk = qt.shape
    t = nb * blk
    return pl.pallas_call(
        functools.partial(_moba_prompt_body, nb=nb),
        grid=(b, nh, nb),
        in_specs=[
            pl.BlockSpec(memory_space=pltpu.SMEM),
            pl.BlockSpec((1, 1, 1, hd, blk), lambda bi, hi, ni: (bi, hi, ni, 0, 0)),
            pl.BlockSpec((1, 1, nb, blk, hd), lambda bi, hi, ni: (bi, hi, 0, 0, 0)),
            pl.BlockSpec((1, 1, nb, hd, blk), lambda bi, hi, ni: (bi, hi, 0, 0, 0)),
            pl.BlockSpec((1, 1, nb, hd), lambda bi, hi, ni: (bi, hi, 0, 0)),
        ],
        out_specs=pl.BlockSpec((1, hd, blk), lambda bi, hi, ni: (bi, hi, ni)),
        out_shape=jax.ShapeDtypeStruct((b, nh * hd, t), F32),
        scratch_shapes=[
            pltpu.VMEM((blk, blk), F32),
            pltpu.VMEM((blk, blk), F32),
            pltpu.VMEM((nb, blk), F32),
            pltpu.VMEM((1, blk), F32),
            pltpu.VMEM((1, blk), F32),
            pltpu.VMEM((hd, blk), F32),
        ],
        compiler_params=_cparams(3),
        name="prompt_moba",
    )(rel_bias, qt, kh, vt, kmh)


def _prompt_mix_out_body(a_ref, halo_ref, cw_ref, cb_ref, lg_ref, lb_ref, ot_ref, x_ref, w_ref,
                         y_ref, sh_ref, ao_ref, *, tm):
    t = pl.program_id(1)
    rows = tm + HALO_A
    sh_ref[0, 0:HALO_A] = jnp.where(t == 0, 0.0, halo_ref[0])
    sh_ref[0, HALO_A:rows] = a_ref[0]
    for s in range(1, SUBLANES):
        sh_ref[s, 0:rows - SUBLANES] = sh_ref[0, s:s + rows - SUBLANES]

    cb = cb_ref[...]
    lg = lg_ref[...]
    lb = lb_ref[...]

    def chunk(c, carry):
        r0 = pl.multiple_of(c * CONV_ROWS, CONV_ROWS)
        acc = jnp.zeros((CONV_ROWS, D_A), F32) + cb
        for j in range(CONV_A):
            off = j + HALO_A - (CONV_A - 1)
            acc = acc + cw_ref[j:j + 1, :] * sh_ref[off % SUBLANES,
                                                    pl.ds(r0 + (off // SUBLANES) * SUBLANES, CONV_ROWS), :]
        ao_ref[pl.ds(r0, CONV_ROWS), :] = _silu(_layernorm(acc, lg, lb)).astype(BF16)
        return carry

    lax.fori_loop(0, tm // CONV_ROWS, chunk, 0)

    o = ot_ref[0].T.astype(BF16)
    y = (jnp.dot(ao_ref[...], w_ref[0:D_A, :], preferred_element_type=F32)
         + jnp.dot(o, w_ref[D_A:D_A + D_B, :], preferred_element_type=F32))
    y_ref[0] = x_ref[0] + y


def _prompt_mix_out(a, cw, cb, lg, lb, ot, x, w):
    b, t, d = x.shape
    tm = TILE_M
    nt = t // tm
    hpt = tm // HALO_A
    tok = lambda bi, ti: (bi, ti, 0)
    return pl.pallas_call(
        functools.partial(_prompt_mix_out_body, tm=tm),
        grid=(b, nt),
        in_specs=[
            pl.BlockSpec((1, tm, D_A), tok),
            pl.BlockSpec((1, HALO_A, D_A), lambda bi, ti: (bi, jnp.maximum(ti * hpt - 1, 0), 0)),
            _const_spec((CONV_A, D_A)),
            _const_spec((1, D_A)),
            _const_spec((1, D_A)),
            _const_spec((1, D_A)),
            pl.BlockSpec((1, D_B, tm), lambda bi, ti: (bi, 0, ti)),
            pl.BlockSpec((1, tm, d), tok),
            _const_spec((D_A + D_B, d)),
        ],
        out_specs=pl.BlockSpec((1, tm, d), tok),
        out_shape=jax.ShapeDtypeStruct((b, t, d), F32),
        scratch_shapes=[
            pltpu.VMEM((SUBLANES, tm + HALO_A, D_A), F32),
            pltpu.VMEM((tm, D_A), BF16),
        ],
        compiler_params=_cparams(2),
        name="prompt_mix_out",
    )(a, a, cw, cb, lg, lb, ot, x, w)


def _conv3_chunk(u, ext_ref, carry_ref, cw_ref, c0, tm):
    ext_ref[0:SUBLANES] = carry_ref[:, c0:c0 + CH]
    ext_ref[SUBLANES:SUBLANES + tm] = u
    carry_ref[:, c0:c0 + CH] = u[tm - SUBLANES:tm]
    return (cw_ref[0:1, c0:c0 + CH] * ext_ref[SUBLANES - 2:SUBLANES - 2 + tm]
            + cw_ref[1:2, c0:c0 + CH] * ext_ref[SUBLANES - 1:SUBLANES - 1 + tm]
            + cw_ref[2:3, c0:c0 + CH] * u)


def _prompt_ffn_body(x_ref, g_ref, wu_ref, cw_ref, cb_ref, wd_ref, y_ref, st_ref,
                     carry_ref, extg_ref, extu_ref, act_ref, *, tm):
    @pl.when(pl.program_id(1) == 0)
    def _():
        carry_ref[...] = jnp.zeros_like(carry_ref)

    x = x_ref[0]
    h = _rmsnorm(x, g_ref[...]).astype(BF16)
    for c in range(D_FF // CH):
        cg = c * CH
        cu = D_FF + c * CH
        ug = jnp.dot(h, wu_ref[:, cg:cg + CH], preferred_element_type=F32)
        uu = jnp.dot(h, wu_ref[:, cu:cu + CH], preferred_element_type=F32)
        gg = _conv3_chunk(ug, extg_ref, carry_ref, cw_ref, cg, tm) + cb_ref[:, cg:cg + CH]
        gu = _conv3_chunk(uu, extu_ref, carry_ref, cw_ref, cu, tm) + cb_ref[:, cu:cu + CH]
        act_ref[:, cg:cg + CH] = (_silu(gg) * gu).astype(BF16)
    st_ref[0] = carry_ref[...]
    y_ref[0] = x + jnp.dot(act_ref[...], wd_ref[...], preferred_element_type=F32)


def _prompt_ffn(x, g, wu, cw, cb, wd):
    b, t, d = x.shape
    tm = TILE_M
    nt = t // tm
    tok = lambda bi, ti: (bi, ti, 0)
    return pl.pallas_call(
        functools.partial(_prompt_ffn_body, tm=tm),
        grid=(b, nt),
        in_specs=[
            pl.BlockSpec((1, tm, d), tok),
            _const_spec((1, d)),
            _const_spec((d, 2 * D_FF)),
            _const_spec((CONV_F, 2 * D_FF)),
            _const_spec((1, 2 * D_FF)),
            _const_spec((D_FF, d)),
        ],
        out_specs=[
            pl.BlockSpec((1, tm, d), tok),
            pl.BlockSpec((1, SUBLANES, 2 * D_FF), lambda bi, ti: (bi, 0, 0)),
        ],
        out_shape=[
            jax.ShapeDtypeStruct((b, t, d), F32),
            jax.ShapeDtypeStruct((b, SUBLANES, 2 * D_FF), F32),
        ],
        scratch_shapes=[
            pltpu.VMEM((SUBLANES, 2 * D_FF), F32),
            pltpu.VMEM((tm + SUBLANES, CH), F32),
            pltpu.VMEM((tm + SUBLANES, CH), F32),
            pltpu.VMEM((tm, D_FF), BF16),
        ],
        compiler_params=_cparams(2),
        name="prompt_ffn",
    )(x, g, wu, cw, cb, wd)


def _prompt_mixc_body(x_ref, g_ref, wi_ref, cw_ref, wo_ref, y_ref, st_ref,
                      carry_ref, ext_ref, z_ref, *, tm):
    @pl.when(pl.program_id(1) == 0)
    def _():
        carry_ref[...] = jnp.zeros_like(carry_ref)

    x = x_ref[0]
    h = _rmsnorm(x, g_ref[...]).astype(BF16)
    for c in range(D_C // CH):
        c0 = c * CH
        gb = jnp.dot(h, wi_ref[:, c0:c0 + CH], preferred_element_type=F32)
        gc = jnp.dot(h, wi_ref[:, D_C + c0:D_C + c0 + CH], preferred_element_type=F32)
        u = jnp.dot(h, wi_ref[:, 2 * D_C + c0:2 * D_C + c0 + CH], preferred_element_type=F32)
        conv = _conv3_chunk(gc * u, ext_ref, carry_ref, cw_ref, c0, tm)
        z_ref[:, c0:c0 + CH] = (gb * conv).astype(BF16)
    st_ref[0] = carry_ref[...]
    y_ref[0] = x + jnp.dot(z_ref[...], wo_ref[...], preferred_element_type=F32)


def _prompt_mixc(x, g, wi, cw, wo):
    b, t, d = x.shape
    tm = TILE_M
    nt = t // tm
    tok = lambda bi, ti: (bi, ti, 0)
    return pl.pallas_call(
        functools.partial(_prompt_mixc_body, tm=tm),
        grid=(b, nt),
        in_specs=[
            pl.BlockSpec((1, tm, d), tok),
            _const_spec((1, d)),
            _const_spec((d, 3 * D_C)),
            _const_spec((CONV_C, D_C)),
            _const_spec((D_C, d)),
        ],
        out_specs=[
            pl.BlockSpec((1, tm, d), tok),
            pl.BlockSpec((1, SUBLANES, D_C), lambda bi, ti: (bi, 0, 0)),
        ],
        out_shape=[
            jax.ShapeDtypeStruct((b, t, d), F32),
            jax.ShapeDtypeStruct((b, SUBLANES, D_C), F32),
        ],
        scratch_shapes=[
            pltpu.VMEM((SUBLANES, D_C), F32),
            pltpu.VMEM((tm + SUBLANES, CH), F32),
            pltpu.VMEM((tm, D_C), BF16),
        ],
        compiler_params=_cparams(2),
        name="prompt_mixc",
    )(x, g, wi, cw, wo)


def _sample_even_body(x_ref, g_ref, w_ref, qg_ref, kg_ref, ho_ref, sa_ref, cw_ref, cb_ref, lg_ref, lb_ref,
                      ao_ref, san_ref, q_ref, k_ref, v_ref):
    a, q, k, v = _in_proj_even(x_ref[...], g_ref[...], w_ref[...], qg_ref[...], kg_ref[...], ho_ref[...])
    q_ref[...] = q
    k_ref[...] = k
    v_ref[...] = v
    hist = CONV_A - 1
    conv = cb_ref[...] + cw_ref[hist:hist + 1, :] * a
    for j in range(hist):
        conv = conv + cw_ref[j:j + 1, :] * sa_ref[:, j * D_A:(j + 1) * D_A]
    ao_ref[...] = _silu(_layernorm(conv, lg_ref[...], lb_ref[...])).astype(BF16)
    san_ref[:, 0:(hist - 1) * D_A] = sa_ref[:, D_A:hist * D_A]
    san_ref[:, (hist - 1) * D_A:hist * D_A] = a


def _sample_even(x, g, w, qg, kg, head_ones, sa, cw, cb, lg, lb):
    m = x.shape[0]
    return pl.pallas_call(
        _sample_even_body,
        out_shape=[
            jax.ShapeDtypeStruct((m, D_A), BF16),
            jax.ShapeDtypeStruct(sa.shape, F32),
            jax.ShapeDtypeStruct((m, D_B), F32),
            jax.ShapeDtypeStruct((m, D_B), F32),
            jax.ShapeDtypeStruct((m, D_B), F32),
        ],
        compiler_params=pltpu.CompilerParams(vmem_limit_bytes=VMEM_LIMIT),
        name="sample_even",
    )(x, g, w, qg, kg, head_ones, sa, cw, cb, lg, lb)


def _moba_sample_body(pt_ref, q_ref, kn_ref, vn_ref, rbt_ref, *refs, n_pages, page):
    del pt_ref
    k_pages = refs[:n_pages]
    v_pages = refs[n_pages:2 * n_pages]
    o_ref, bias_ref, kb_ref, vb_ref = refs[2 * n_pages:]
    past = n_pages * page
    n_past_blocks = past // MOBA_BLOCK
    pages_per_block = MOBA_BLOCK // page
    n_sel = min(MOBA_TOPK, n_past_blocks)

    @pl.when(pl.program_id(0) == 0)
    def _():
        pos = lax.broadcasted_iota(jnp.int32, (N_HEADS, past), 1)
        dist = past - pos
        out = jnp.zeros((N_HEADS, past), F32) + rbt_ref[:, 0:1]
        for i, th in enumerate(T5_THRESH):
            out = jnp.where(dist >= th, rbt_ref[:, i + 1:i + 2], out)
        bias_ref[...] = out

    head_of_lane = lax.broadcasted_iota(jnp.int32, (N_HEADS, D_B), 1) // HEAD_DIM
    hmask = head_of_lane == lax.broadcasted_iota(jnp.int32, (N_HEADS, D_B), 0)
    qbd = jnp.where(hmask, q_ref[0], 0.0)

    gates = []
    for j in range(n_past_blocks):
        ksum = jnp.zeros((1, D_B), F32)
        for pg in range(j * pages_per_block, (j + 1) * pages_per_block):
            kp = k_pages[pg][0]
            ksum = ksum + jnp.sum(kp, axis=0, keepdims=True)
            kb_ref[pg * page:(pg + 1) * page, :] = kp.astype(BF16)
            vb_ref[pg * page:(pg + 1) * page, :] = v_pages[pg][0].astype(BF16)
        gates.append(jnp.sum(qbd * (ksum * (1.0 / MOBA_BLOCK)), axis=-1, keepdims=True))

    s = lax.dot_general(qbd.astype(BF16), kb_ref[...], (((1,), (1,)), ((), ())),
                        preferred_element_type=F32) + bias_ref[...]
    pieces = []
    for j in range(n_past_blocks):
        rank = jnp.zeros((N_HEADS, 1), jnp.int32)
        for i in range(n_past_blocks):
            if i == j:
                continue
            ahead = (gates[i] >= gates[j]) if i < j else (gates[i] > gates[j])
            rank = rank + jnp.where(ahead, 1, 0)
        pieces.append(s[:, j * MOBA_BLOCK:(j + 1) * MOBA_BLOCK] + jnp.where(rank < n_sel, 0.0, -jnp.inf))
    s = jnp.concatenate(pieces, axis=-1)

    s_new = jnp.sum(qbd * kn_ref[0], axis=-1, keepdims=True) + rbt_ref[:, 0:1]
    m = jnp.maximum(jnp.max(s, axis=-1, keepdims=True), s_new)
    p = jnp.exp(s - m)
    p_new = jnp.exp(s_new - m)
    l = jnp.sum(p, axis=-1, keepdims=True) + p_new
    r = jnp.dot(p.astype(BF16), vb_ref[...], preferred_element_type=F32)
    r = (r + p_new * vn_ref[0]) / l
    o_ref[0] = jnp.sum(jnp.where(hmask, r, 0.0), axis=0, keepdims=True)


def _moba_sample(page_table, q, k_new, v_new, rel_bias_t, cache_k, cache_v):
    m, n_pages = page_table.shape
    n_pool, page = cache_k.shape[0], cache_k.shape[1]
    row = lambda bi, pt: (bi, 0, 0)

    def page_spec(pg):
        return pl.BlockSpec((1, page, D_B), lambda bi, pt: (pt[bi, pg], 0, 0))

    grid_spec = pltpu.PrefetchScalarGridSpec(
        num_scalar_prefetch=1,
        grid=(m,),
        in_specs=[
            pl.BlockSpec((1, 1, D_B), row),
            pl.BlockSpec((1, 1, D_B), row),
            pl.BlockSpec((1, 1, D_B), row),
            pl.BlockSpec((N_HEADS, NUM_BUCKETS), lambda bi, pt: (0, 0)),
        ] + [page_spec(pg) for pg in range(n_pages)] * 2,
        out_specs=pl.BlockSpec((1, 1, D_B), row),
        scratch_shapes=[
            pltpu.VMEM((N_HEADS, n_pages * page), F32),
            pltpu.VMEM((n_pages * page, D_B), BF16),
            pltpu.VMEM((n_pages * page, D_B), BF16),
        ],
    )
    return pl.pallas_call(
        functools.partial(_moba_sample_body, n_pages=n_pages, page=page),
        grid_spec=grid_spec,
        out_shape=jax.ShapeDtypeStruct((m, 1, D_B), F32),
        compiler_params=_cparams(1),
        name="sample_moba",
    )(page_table, q.reshape(m, 1, D_B), k_new.reshape(m, 1, D_B), v_new.reshape(m, 1, D_B), rel_bias_t,
      *([cache_k] * n_pages), *([cache_v] * n_pages))


def _sample_ffn(x, st_ref, g_ref, wu_ref, cw_ref, cb_ref, wd_ref, stn_ref):
    w2 = 2 * D_FF
    h = _rmsnorm(x, g_ref[...]).astype(BF16)
    up = jnp.dot(h, wu_ref[...], preferred_element_type=F32)
    conv = (cw_ref[0:1, :] * st_ref[:, 0:w2] + cw_ref[1:2, :] * st_ref[:, w2:2 * w2]
            + cw_ref[2:3, :] * up + cb_ref[...])
    act = (_silu(conv[:, :D_FF]) * conv[:, D_FF:]).astype(BF16)
    stn_ref[:, 0:w2] = st_ref[:, w2:2 * w2]
    stn_ref[:, w2:2 * w2] = up
    return x + jnp.dot(act, wd_ref[...], preferred_element_type=F32)


def _sample_out_ffn_body(ao_ref, o_ref, x_ref, wo_ref, st_ref, g_ref, wu_ref, cw_ref, cb_ref, wd_ref,
                         y_ref, stn_ref):
    x1 = (x_ref[...]
          + jnp.dot(ao_ref[...], wo_ref[0:D_A, :], preferred_element_type=F32)
          + jnp.dot(o_ref[...].astype(BF16), wo_ref[D_A:D_A + D_B, :], preferred_element_type=F32))
    y_ref[...] = _sample_ffn(x1, st_ref, g_ref, wu_ref, cw_ref, cb_ref, wd_ref, stn_ref)


def _sample_out_ffn(ao, o, x, wo, st, g, wu, cw, cb, wd):
    return pl.pallas_call(
        _sample_out_ffn_body,
        out_shape=[jax.ShapeDtypeStruct(x.shape, F32), jax.ShapeDtypeStruct(st.shape, F32)],
        compiler_params=pltpu.CompilerParams(vmem_limit_bytes=VMEM_LIMIT),
        name="sample_out_ffn",
    )(ao, o, x, wo, st, g, wu, cw, cb, wd)


def _sample_odd_body(x_ref, gm_ref, wi_ref, ccw_ref, sc_ref, wo_ref, st_ref, g_ref, wu_ref, cw_ref, cb_ref,
                     wd_ref, y_ref, scn_ref, stn_ref):
    x = x_ref[...]
    h = _rmsnorm(x, gm_ref[...]).astype(BF16)
    z = jnp.dot(h, wi_ref[...], preferred_element_type=F32)
    gcu = z[:, D_C:2 * D_C] * z[:, 2 * D_C:]
    conv = ccw_ref[0:1, :] * sc_ref[:, 0:D_C] + ccw_ref[1:2, :] * sc_ref[:, D_C:2 * D_C] + ccw_ref[2:3, :] * gcu
    scn_ref[:, 0:D_C] = sc_ref[:, D_C:2 * D_C]
    scn_ref[:, D_C:2 * D_C] = gcu
    x1 = x + jnp.dot((z[:, :D_C] * conv).astype(BF16), wo_ref[...], preferred_element_type=F32)
    y_ref[...] = _sample_ffn(x1, st_ref, g_ref, wu_ref, cw_ref, cb_ref, wd_ref, stn_ref)


def _sample_odd(x, gm, wi, ccw, sc, wo, st, g, wu, cw, cb, wd):
    return pl.pallas_call(
        _sample_odd_body,
        out_shape=[jax.ShapeDtypeStruct(x.shape, F32), jax.ShapeDtypeStruct(sc.shape, F32),
                   jax.ShapeDtypeStruct(st.shape, F32)],
        compiler_params=pltpu.CompilerParams(vmem_limit_bytes=VMEM_LIMIT),
        name="sample_odd",
    )(x, gm, wi, ccw, sc, wo, st, g, wu, cw, cb, wd)


def kernel(x_prompt, x_sample, cache_k, cache_v, state_conv_a, state_conv_c, state_ffn, page_table, rel_bias,
           norm_mix_e, w_in_e, conv_a_w, conv_a_b, ln_a_g, ln_a_b, q_norm_g, k_norm_g, w_out_e,
           norm_mix_o, w_in_o, conv_c_w, w_out_o, norm_ffn, w_up, conv_f_w, conv_f_b, w_down):
    b, t, d = x_prompt.shape
    m = x_sample.shape[0]
    n_pool, page = cache_k.shape[1], cache_k.shape[2]
    n_pages = page_table.shape[1]
    assert norm_mix_e.shape[0] == 1 and norm_mix_o.shape[0] == 1 and norm_ffn.shape[0] == 2
    assert x_sample.shape[1] == 1 and t % TILE_M == 0 and t % page == 0
    assert (n_pages * page) % MOBA_BLOCK == 0 and MOBA_BLOCK % page == 0

    row = lambda v: v.reshape(1, -1)
    w_in_e_b = w_in_e[0].astype(BF16)
    w_out_e_b = w_out_e[0].astype(BF16)
    w_in_o_b = w_in_o[0].astype(BF16)
    w_out_o_b = w_out_o[0].astype(BF16)
    w_up_b = w_up.astype(BF16)
    w_down_b = w_down.astype(BF16)
    qg = row(jnp.tile(q_norm_g[0], N_HEADS))
    kg = row(jnp.tile(k_norm_g[0], N_HEADS))
    lane_head = jnp.arange(D_B, dtype=jnp.int32) // HEAD_DIM
    head_ones = (lane_head[:, None] == lane_head[None, :]).astype(BF16)
    g_e, g_o = row(norm_mix_e[0]), row(norm_mix_o[0])
    cab, lag, lab = row(conv_a_b[0]), row(ln_a_g[0]), row(ln_a_b[0])

    a_p, k_p, v_p, qt, kh, vt, km = _prompt_inproj(x_prompt, g_e, w_in_e_b, qg, kg, head_ones)
    nb = t // MOBA_BLOCK
    kmh = km.reshape(b, nb, N_HEADS, HEAD_DIM).transpose(0, 2, 1, 3)
    ot = _moba_prompt(rel_bias, qt, kh, vt, kmh)
    x1 = _prompt_mix_out(a_p, conv_a_w[0], cab, lag, lab, ot, x_prompt, w_out_e_b)
    x2, f0 = _prompt_ffn(x1, row(norm_ffn[0]), w_up_b[0], conv_f_w[0], row(conv_f_b[0]), w_down_b[0])
    x3, c_st = _prompt_mixc(x2, g_o, w_in_o_b, conv_c_w[0], w_out_o_b)
    y_prompt, f1 = _prompt_ffn(x3, row(norm_ffn[1]), w_up_b[1], conv_f_w[1], row(conv_f_b[1]), w_down_b[1])

    k_prompt = k_p.reshape(1, b, t // page, page, N_HEADS, HEAD_DIM)
    v_prompt = v_p.reshape(1, b, t // page, page, N_HEADS, HEAD_DIM)
    a_prompt = a_p[:, t - (CONV_A - 1):, :][None]
    c_prompt = c_st[:, SUBLANES - (CONV_C - 1):, :][None]
    f_prompt = jnp.stack([f0[:, SUBLANES - (CONV_F - 1):, :], f1[:, SUBLANES - (CONV_F - 1):, :]])

    xs = x_sample.reshape(m, d)
    sa = state_conv_a[0].reshape(m, (CONV_A - 1) * D_A)
    sc = state_conv_c[0].reshape(m, (CONV_C - 1) * D_C)
    sf = state_ffn.reshape(2, m, (CONV_F - 1) * 2 * D_FF)
    ao_s, sa_new, q_s, k_s, v_s = _sample_even(xs, g_e, w_in_e_b, qg, kg, head_ones, sa, conv_a_w[0], cab, lag, lab)
    o_s = _moba_sample(page_table, q_s, k_s, v_s, rel_bias.T,
                       cache_k[0].reshape(n_pool, page, D_B), cache_v[0].reshape(n_pool, page, D_B))
    xs1, sf0 = _sample_out_ffn(ao_s, o_s.reshape(m, D_B), xs, w_out_e_b, sf[0], row(norm_ffn[0]), w_up_b[0],
                               conv_f_w[0], row(conv_f_b[0]), w_down_b[0])
    ys, sc_new, sf1 = _sample_odd(xs1, g_o, w_in_o_b, conv_c_w[0], sc, w_out_o_b, sf[1], row(norm_ffn[1]),
                                  w_up_b[1], conv_f_w[1], row(conv_f_b[1]), w_down_b[1])

    y_sample = ys.reshape(m, 1, d)
    k_sample = k_s.reshape(1, m, 1, N_HEADS, HEAD_DIM)
    v_sample = v_s.reshape(1, m, 1, N_HEADS, HEAD_DIM)
    a_sample = sa_new.reshape(1, m, CONV_A - 1, D_A)
    c_sample = sc_new.reshape(1, m, CONV_C - 1, D_C)
    f_sample = jnp.stack([sf0, sf1]).reshape(2, m, CONV_F - 1, 2 * D_FF)
    return (y_prompt, y_sample, k_prompt, v_prompt, a_prompt, c_prompt, f_prompt,
            k_sample, v_sample, a_sample, c_sample, f_sample)
```

```python
import functools
import math

import jax
import jax.numpy as jnp
from jax import lax
from jax.experimental import pallas as pl
from jax.experimental.pallas import tpu as pltpu

F32 = jnp.float32
BF16 = jnp.bfloat16

EPS = 1e-6
D_MODEL = 1024
D_A = 512
CONV_A = 31
N_HEADS = 8
HEAD_DIM = 64
D_B = N_HEADS * HEAD_DIM
MOBA_BLOCK = 256
MOBA_TOPK = 3
NUM_BUCKETS = 32
MAX_DISTANCE = 128
D_C = 1024
CONV_C = 3
D_FF = 2816
CONV_F = 3
LOG2E = math.log2(math.e)
Q_SCALE = HEAD_DIM ** -0.5 * LOG2E
VT_ROWS = HEAD_DIM + 16

TILE_M = 512
CONV_ROWS = 32
HALO_A = 32
HEAD_GROUP = 4
SOFTMAX_ROWS = 64
CH = 256
SUBLANES = 8
VMEM_LIMIT = 56 * 1024 * 1024


def _t5_thresholds():
    max_exact = NUM_BUCKETS // 2
    th = list(range(1, max_exact + 1))
    for k in range(1, NUM_BUCKETS - max_exact):
        th.append(math.ceil(max_exact * (MAX_DISTANCE / max_exact) ** (k / (NUM_BUCKETS - max_exact))))
    return tuple(th)


T5_THRESH = _t5_thresholds()


def _cparams(n_grid):
    return pltpu.CompilerParams(dimension_semantics=("arbitrary",) * n_grid, vmem_limit_bytes=VMEM_LIMIT)


def _const_spec(shape):
    nd = len(shape)
    return pl.BlockSpec(shape, lambda *_: (0,) * nd, pipeline_mode=pl.Buffered(1))


def _rmsnorm(x, g):
    return x * lax.rsqrt(jnp.mean(x * x, axis=-1, keepdims=True) + EPS) * g


def _split_dot(x, w_bf16):
    hi = x.astype(BF16)
    lo = (x - hi.astype(F32)).astype(BF16)
    return (jnp.dot(hi, w_bf16, preferred_element_type=F32)
            + jnp.dot(lo, w_bf16, preferred_element_type=F32))


def _head_rmsnorm(x, g, head_ones):
    ss = _split_dot(x * x, head_ones)
    return x * lax.rsqrt(ss * (1.0 / HEAD_DIM) + EPS) * g


def _silu(x):
    return x * jax.nn.sigmoid(x)


def _layernorm(x, g, b):
    mu = jnp.mean(x, axis=-1, keepdims=True)
    xc = x - mu
    var = jnp.mean(xc * xc, axis=-1, keepdims=True)
    return xc * lax.rsqrt(var + EPS) * g + b


def _in_proj_even(x, g, w, qg, kg, head_ones):
    h = _rmsnorm(x, g).astype(BF16)
    z = jnp.dot(h, w, preferred_element_type=F32)
    a = z[:, :D_A] * jax.nn.sigmoid(z[:, D_A:2 * D_A])
    q = _head_rmsnorm(z[:, 2 * D_A:2 * D_A + D_B], qg, head_ones) * Q_SCALE
    k = _head_rmsnorm(z[:, 2 * D_A + D_B:2 * D_A + 2 * D_B], kg, head_ones)
    v = z[:, 2 * D_A + 2 * D_B:]
    return a, q, k, v


def _prompt_inproj_body(x_ref, g_ref, w_ref, qg_ref, kg_ref, ho_ref,
                        a_ref, k_ref, v_ref, qt_ref, kh_ref, vt_ref, km_ref, *, tm):
    a, q, k, v = _in_proj_even(x_ref[0], g_ref[...], w_ref[...], qg_ref[...], kg_ref[...], ho_ref[...])
    a_ref[0] = a
    k_ref[0] = k
    v_ref[0] = v
    qt = q.T
    vt = v.T
    kb = k.astype(BF16)
    pad_row = lax.broadcasted_iota(jnp.int32, (N_HEADS, VT_ROWS - HEAD_DIM, MOBA_BLOCK), 1)
    ones_rows = jnp.where(pad_row == 0, 1.0, 0.0).astype(BF16)
    for i in range(tm // MOBA_BLOCK):
        r0 = i * MOBA_BLOCK
        qt_ref[0, :, i] = qt[:, r0:r0 + MOBA_BLOCK].reshape(N_HEADS, HEAD_DIM, MOBA_BLOCK).astype(BF16)
        vt_ref[0, :, i, 0:HEAD_DIM, :] = (
            vt[:, r0:r0 + MOBA_BLOCK].reshape(N_HEADS, HEAD_DIM, MOBA_BLOCK).astype(BF16))
        vt_ref[0, :, i, HEAD_DIM:VT_ROWS, :] = ones_rows
        for hh in range(N_HEADS):
            kh_ref[0, hh, i] = kb[r0:r0 + MOBA_BLOCK, hh * HEAD_DIM:(hh + 1) * HEAD_DIM]
        km_ref[0, i] = jnp.mean(k[r0:r0 + MOBA_BLOCK], axis=0, keepdims=True)


def _prompt_inproj(x, g, w, qg, kg, head_ones):
    b, t, d = x.shape
    tm = TILE_M
    nt = t // tm
    nb = t // MOBA_BLOCK
    bpt = tm // MOBA_BLOCK
    n_out = w.shape[1]
    tok = lambda bi, ti: (bi, ti, 0)
    blk5 = lambda bi, ti: (bi, 0, ti, 0, 0)
    return pl.pallas_call(
        functools.partial(_prompt_inproj_body, tm=tm),
        grid=(b, nt),
        in_specs=[
            pl.BlockSpec((1, tm, d), tok),
            _const_spec((1, d)),
            _const_spec((d, n_out)),
            _const_spec((1, D_B)),
            _const_spec((1, D_B)),
            _const_spec((D_B, D_B)),
        ],
        out_specs=[
            pl.BlockSpec((1, tm, D_A), tok),
            pl.BlockSpec((1, tm, D_B), tok),
            pl.BlockSpec((1, tm, D_B), tok),
            pl.BlockSpec((1, N_HEADS, bpt, HEAD_DIM, MOBA_BLOCK), blk5),
            pl.BlockSpec((1, N_HEADS, bpt, MOBA_BLOCK, HEAD_DIM), blk5),
            pl.BlockSpec((1, N_HEADS, bpt, VT_ROWS, MOBA_BLOCK), blk5),
            pl.BlockSpec((1, bpt, 1, D_B), lambda bi, ti: (bi, ti, 0, 0)),
        ],
        out_shape=[
            jax.ShapeDtypeStruct((b, t, D_A), F32),
            jax.ShapeDtypeStruct((b, t, D_B), F32),
            jax.ShapeDtypeStruct((b, t, D_B), F32),
            jax.ShapeDtypeStruct((b, N_HEADS, nb, HEAD_DIM, MOBA_BLOCK), BF16),
            jax.ShapeDtypeStruct((b, N_HEADS, nb, MOBA_BLOCK, HEAD_DIM), BF16),
            jax.ShapeDtypeStruct((b, N_HEADS, nb, VT_ROWS, MOBA_BLOCK), BF16),
            jax.ShapeDtypeStruct((b, nb, 1, D_B), F32),
        ],
        compiler_params=_cparams(2),
        name="prompt_inproj",
    )(x, g, w, qg, kg, head_ones)


def _t5_bias_scalar_table(dist, rb_ref, h):
    out = jnp.full(dist.shape, rb_ref[0, h] * LOG2E, F32)
    for i, th in enumerate(T5_THRESH):
        out = jnp.where(dist >= th, rb_ref[i + 1, h] * LOG2E, out)
    return out


def _moba_prompt_body(rb_ref, qt_ref, kh_ref, vt_ref, km_ref, ot_ref,
                      bdiag_ref, bsub_ref, selb_ref, m_ref, alpha_ref, acc_ref, s_ref, p_ref, *, nb, hg):
    h0 = pl.program_id(1) * hg
    n = pl.program_id(2)
    blk = MOBA_BLOCK

    @pl.when(n == 0)
    def _():
        ki = lax.broadcasted_iota(jnp.int32, (blk, blk), 0)
        qi = lax.broadcasted_iota(jnp.int32, (blk, blk), 1)
        d0 = qi - ki
        for g in range(hg):
            bdiag_ref[g] = jnp.where(d0 >= 0, _t5_bias_scalar_table(jnp.maximum(d0, 0), rb_ref, h0 + g),
                                     -jnp.inf)
            bsub_ref[g] = _t5_bias_scalar_table(d0 + blk, rb_ref, h0 + g)

    bi = lax.broadcasted_iota(jnp.int32, (nb, blk), 0)
    qts = [qt_ref[0, g, 0] for g in range(hg)]

    for g in range(hg):
        km = km_ref[0, g]
        km_hi = km.astype(BF16)
        km_lo = (km - km_hi.astype(F32)).astype(BF16)
        gate = (jnp.dot(km_hi, qts[g], preferred_element_type=F32)
                + jnp.dot(km_lo, qts[g], preferred_element_type=F32))
        avail = jnp.where(bi < n, 1.0, 0.0)
        far_bias = rb_ref[NUM_BUCKETS - 1, h0 + g] * LOG2E
        selb = jnp.full((nb, blk), -jnp.inf, F32)
        for _ in range(MOBA_TOPK):
            gm = jnp.where(avail > 0.0, gate, -jnp.inf)
            top = jnp.max(gm, axis=0, keepdims=True)
            first = jnp.where(avail > 0.0, jnp.where(gm == top, bi, nb), nb)
            pick = bi == jnp.min(first, axis=0, keepdims=True)
            selb = jnp.where(pick, jnp.where(bi == n - 1, 0.0, far_bias), selb)
            avail = jnp.where(pick, 0.0, avail)
        selb_ref[g] = selb

    n_chunks = blk // SOFTMAX_ROWS

    def logits(g, j, slot, bias):
        s = jnp.dot(kh_ref[0, g, j], qts[g], preferred_element_type=F32)
        s_ref[slot, g] = s if bias is None else s + bias

    def chunk(g, slot, c):
        return s_ref[slot, g, c * SOFTMAX_ROWS:(c + 1) * SOFTMAX_ROWS, :].reshape(
            SOFTMAX_ROWS // SUBLANES, SUBLANES, blk)

    def softmax(g, slot, row):
        m8 = jnp.max(chunk(g, slot, 0), axis=0)
        for c in range(1, n_chunks):
            m8 = jnp.maximum(m8, jnp.max(chunk(g, slot, c), axis=0))
        m_blk = jnp.max(m8, axis=0, keepdims=True)
        if row is None:
            mn = m_blk
            shift = mn
        else:
            m_old = m_ref[g]
            mn = jnp.maximum(m_old, m_blk + row)
            alpha_ref[slot, g] = jnp.exp2(m_old - mn)
            shift = mn - row
        for c in range(n_chunks):
            p = jnp.exp2(chunk(g, slot, c) - shift)
            p_ref[slot, g, c * SOFTMAX_ROWS:(c + 1) * SOFTMAX_ROWS, :] = (
                p.reshape(SOFTMAX_ROWS, blk).astype(BF16))
        m_ref[g] = mn

    def weighted_values(g, j, slot, first):
        pv = jnp.dot(vt_ref[0, g, j], p_ref[slot, g], preferred_element_type=F32)
        acc_ref[g] = pv if first else alpha_ref[slot, g] * acc_ref[g] + pv

    for g in range(hg):
        logits(g, n, 0, bdiag_ref[g])
    for g in range(hg):
        softmax(g, 0, None)
    for g in range(hg):
        weighted_values(g, n, 0, True)

    @pl.when(n >= 1)
    def _():
        for g in range(hg):
            logits(g, n - 1, 1, bsub_ref[g])
        for g in range(hg):
            logits(g, 0, 0, None)
        for g in range(hg):
            softmax(g, 1, selb_ref[g, pl.ds(n - 1, 1), :])

        def trip(j, slot):
            nxt = jnp.minimum(j + 1, n - 2)
            prev = jnp.where(j == 0, n - 1, j - 1)
            for g in range(hg):
                logits(g, nxt, 1 - slot, None)
            for g in range(hg):
                weighted_values(g, prev, 1 - slot, False)
            for g in range(hg):
                softmax(g, slot, selb_ref[g, pl.ds(j, 1), :])

        def far(i, carry):
            trip(2 * i, 0)
            trip(2 * i + 1, 1)
            return carry

        lax.fori_loop(0, (n - 1) // 2, far, 0)

        @pl.when((n - 1) % 2 == 1)
        def _():
            trip(n - 2, 0)

        last = jnp.where(n >= 2, n - 2, n - 1)
        for g in range(hg):
            weighted_values(g, last, n & 1, False)

    for g in range(hg):
        ot_ref[0, g * HEAD_DIM:(g + 1) * HEAD_DIM, :] = (
            acc_ref[g, 0:HEAD_DIM, :] / acc_ref[g, HEAD_DIM:HEAD_DIM + 1, :])


def _moba_prompt(rel_bias, qt, kh, vt, kmh):
    b, nh, nb, hd, blk = qt.shape
    t = nb * blk
    hg = HEAD_GROUP
    grp = lambda bi, hi, ni: (bi, hi, 0, 0, 0)
    return pl.pallas_call(
        functools.partial(_moba_prompt_body, nb=nb, hg=hg),
        grid=(b, nh // hg, nb),
        in_specs=[
            pl.BlockSpec(memory_space=pltpu.SMEM),
            pl.BlockSpec((1, hg, 1, hd, blk), lambda bi, hi, ni: (bi, hi, ni, 0, 0)),
            pl.BlockSpec((1, hg, nb, blk, hd), grp, pipeline_mode=pl.Buffered(1)),
            pl.BlockSpec((1, hg, nb, VT_ROWS, blk), grp, pipeline_mode=pl.Buffered(1)),
            pl.BlockSpec((1, hg, nb, hd), lambda bi, hi, ni: (bi, hi, 0, 0)),
        ],
        out_specs=pl.BlockSpec((1, hg * hd, blk), lambda bi, hi, ni: (bi, hi, ni)),
        out_shape=jax.ShapeDtypeStruct((b, nh * hd, t), F32),
        scratch_shapes=[
            pltpu.VMEM((hg, blk, blk), F32),
            pltpu.VMEM((hg, blk, blk), F32),
            pltpu.VMEM((hg, nb, blk), F32),
            pltpu.VMEM((hg, 1, blk), F32),
            pltpu.VMEM((2, hg, 1, blk), F32),
            pltpu.VMEM((hg, VT_ROWS, blk), F32),
            pltpu.VMEM((2, hg, blk, blk), F32),
            pltpu.VMEM((2, hg, blk, blk), BF16),
        ],
        compiler_params=_cparams(3),
        name="prompt_moba",
    )(rel_bias, qt, kh, vt, kmh)


def _prompt_mix_out_body(a_ref, halo_ref, cw_ref, cb_ref, lg_ref, lb_ref, ot_ref, x_ref, w_ref,
                         y_ref, sh_ref, ao_ref, *, tm):
    t = pl.program_id(1)
    rows = tm + HALO_A
    sh_ref[0, 0:HALO_A] = jnp.where(t == 0, 0.0, halo_ref[0])
    sh_ref[0, HALO_A:rows] = a_ref[0]
    for s in range(1, SUBLANES):
        sh_ref[s, 0:rows - SUBLANES] = sh_ref[0, s:s + rows - SUBLANES]

    cb = cb_ref[...]
    lg = lg_ref[...]
    lb = lb_ref[...]

    def chunk(c, carry):
        r0 = pl.multiple_of(c * CONV_ROWS, CONV_ROWS)
        acc = jnp.zeros((CONV_ROWS, D_A), F32) + cb
        for j in range(CONV_A):
            off = j + HALO_A - (CONV_A - 1)
            acc = acc + cw_ref[j:j + 1, :] * sh_ref[off % SUBLANES,
                                                    pl.ds(r0 + (off // SUBLANES) * SUBLANES, CONV_ROWS), :]
        ao_ref[pl.ds(r0, CONV_ROWS), :] = _silu(_layernorm(acc, lg, lb)).astype(BF16)
        return carry

    lax.fori_loop(0, tm // CONV_ROWS, chunk, 0)

    o = ot_ref[0].T.astype(BF16)
    y = (jnp.dot(ao_ref[...], w_ref[0:D_A, :], preferred_element_type=F32)
         + jnp.dot(o, w_ref[D_A:D_A + D_B, :], preferred_element_type=F32))
    y_ref[0] = x_ref[0] + y


def _prompt_mix_out(a, cw, cb, lg, lb, ot, x, w):
    b, t, d = x.shape
    tm = TILE_M
    nt = t // tm
    hpt = tm // HALO_A
    tok = lambda bi, ti: (bi, ti, 0)
    return pl.pallas_call(
        functools.partial(_prompt_mix_out_body, tm=tm),
        grid=(b, nt),
        in_specs=[
            pl.BlockSpec((1, tm, D_A), tok),
            pl.BlockSpec((1, HALO_A, D_A), lambda bi, ti: (bi, jnp.maximum(ti * hpt - 1, 0), 0)),
            _const_spec((CONV_A, D_A)),
            _const_spec((1, D_A)),
            _const_spec((1, D_A)),
            _const_spec((1, D_A)),
            pl.BlockSpec((1, D_B, tm), lambda bi, ti: (bi, 0, ti)),
            pl.BlockSpec((1, tm, d), tok),
            _const_spec((D_A + D_B, d)),
        ],
        out_specs=pl.BlockSpec((1, tm, d), tok),
        out_shape=jax.ShapeDtypeStruct((b, t, d), F32),
        scratch_shapes=[
            pltpu.VMEM((SUBLANES, tm + HALO_A, D_A), F32),
            pltpu.VMEM((tm, D_A), BF16),
        ],
        compiler_params=_cparams(2),
        name="prompt_mix_out",
    )(a, a, cw, cb, lg, lb, ot, x, w)


def _conv3_chunk(u, ext_ref, carry_ref, cw_ref, c0, tm):
    ext_ref[0:SUBLANES] = carry_ref[:, c0:c0 + CH]
    ext_ref[SUBLANES:SUBLANES + tm] = u
    carry_ref[:, c0:c0 + CH] = u[tm - SUBLANES:tm]
    return (cw_ref[0:1, c0:c0 + CH] * ext_ref[SUBLANES - 2:SUBLANES - 2 + tm]
            + cw_ref[1:2, c0:c0 + CH] * ext_ref[SUBLANES - 1:SUBLANES - 1 + tm]
            + cw_ref[2:3, c0:c0 + CH] * u)


def _prompt_ffn_body(x_ref, g_ref, wu_ref, cw_ref, cb_ref, wd_ref, y_ref, st_ref,
                     carry_ref, extg_ref, extu_ref, act_ref, *, tm):
    @pl.when(pl.program_id(1) == 0)
    def _():
        carry_ref[...] = jnp.zeros_like(carry_ref)

    x = x_ref[0]
    h = _rmsnorm(x, g_ref[...]).astype(BF16)
    for c in range(D_FF // CH):
        cg = c * CH
        cu = D_FF + c * CH
        ug = jnp.dot(h, wu_ref[:, cg:cg + CH], preferred_element_type=F32)
        uu = jnp.dot(h, wu_ref[:, cu:cu + CH], preferred_element_type=F32)
        gg = _conv3_chunk(ug, extg_ref, carry_ref, cw_ref, cg, tm) + cb_ref[:, cg:cg + CH]
        gu = _conv3_chunk(uu, extu_ref, carry_ref, cw_ref, cu, tm) + cb_ref[:, cu:cu + CH]
        act_ref[:, cg:cg + CH] = (_silu(gg) * gu).astype(BF16)
    st_ref[0] = carry_ref[...]
    y_ref[0] = x + jnp.dot(act_ref[...], wd_ref[...], preferred_element_type=F32)


def _prompt_ffn(x, g, wu, cw, cb, wd):
    b, t, d = x.shape
    tm = TILE_M
    nt = t // tm
    tok = lambda bi, ti: (bi, ti, 0)
    return pl.pallas_call(
        functools.partial(_prompt_ffn_body, tm=tm),
        grid=(b, nt),
        in_specs=[
            pl.BlockSpec((1, tm, d), tok),
            _const_spec((1, d)),
            _const_spec((d, 2 * D_FF)),
            _const_spec((CONV_F, 2 * D_FF)),
            _const_spec((1, 2 * D_FF)),
            _const_spec((D_FF, d)),
        ],
        out_specs=[
            pl.BlockSpec((1, tm, d), tok),
            pl.BlockSpec((1, SUBLANES, 2 * D_FF), lambda bi, ti: (bi, 0, 0)),
        ],
        out_shape=[
            jax.ShapeDtypeStruct((b, t, d), F32),
            jax.ShapeDtypeStruct((b, SUBLANES, 2 * D_FF), F32),
        ],
        scratch_shapes=[
            pltpu.VMEM((SUBLANES, 2 * D_FF), F32),
            pltpu.VMEM((tm + SUBLANES, CH), F32),
            pltpu.VMEM((tm + SUBLANES, CH), F32),
            pltpu.VMEM((tm, D_FF), BF16),
        ],
        compiler_params=_cparams(2),
        name="prompt_ffn",
    )(x, g, wu, cw, cb, wd)


def _prompt_mixc_body(x_ref, g_ref, wi_ref, cw_ref, wo_ref, y_ref, st_ref,
                      carry_ref, ext_ref, z_ref, *, tm):
    @pl.when(pl.program_id(1) == 0)
    def _():
        carry_ref[...] = jnp.zeros_like(carry_ref)

    x = x_ref[0]
    h = _rmsnorm(x, g_ref[...]).astype(BF16)
    for c in range(D_C // CH):
        c0 = c * CH
        gb = jnp.dot(h, wi_ref[:, c0:c0 + CH], preferred_element_type=F32)
        gc = jnp.dot(h, wi_ref[:, D_C + c0:D_C + c0 + CH], preferred_element_type=F32)
        u = jnp.dot(h, wi_ref[:, 2 * D_C + c0:2 * D_C + c0 + CH], preferred_element_type=F32)
        conv = _conv3_chunk(gc * u, ext_ref, carry_ref, cw_ref, c0, tm)
        z_ref[:, c0:c0 + CH] = (gb * conv).astype(BF16)
    st_ref[0] = carry_ref[...]
    y_ref[0] = x + jnp.dot(z_ref[...], wo_ref[...], preferred_element_type=F32)


def _prompt_mixc(x, g, wi, cw, wo):
    b, t, d = x.shape
    tm = TILE_M
    nt = t // tm
    tok = lambda bi, ti: (bi, ti, 0)
    return pl.pallas_call(
        functools.partial(_prompt_mixc_body, tm=tm),
        grid=(b, nt),
        in_specs=[
            pl.BlockSpec((1, tm, d), tok),
            _const_spec((1, d)),
            _const_spec((d, 3 * D_C)),
            _const_spec((CONV_C, D_C)),
            _const_spec((D_C, d)),
        ],
        out_specs=[
            pl.BlockSpec((1, tm, d), tok),
            pl.BlockSpec((1, SUBLANES, D_C), lambda bi, ti: (bi, 0, 0)),
        ],
        out_shape=[
            jax.ShapeDtypeStruct((b, t, d), F32),
            jax.ShapeDtypeStruct((b, SUBLANES, D_C), F32),
        ],
        scratch_shapes=[
            pltpu.VMEM((SUBLANES, D_C), F32),
            pltpu.VMEM((tm + SUBLANES, CH), F32),
            pltpu.VMEM((tm, D_C), BF16),
        ],
        compiler_params=_cparams(2),
        name="prompt_mixc",
    )(x, g, wi, cw, wo)


def _sample_even_body(x_ref, g_ref, w_ref, qg_ref, kg_ref, ho_ref, sa_ref, cw_ref, cb_ref, lg_ref, lb_ref,
                      ao_ref, san_ref, q_ref, k_ref, v_ref):
    a, q, k, v = _in_proj_even(x_ref[...], g_ref[...], w_ref[...], qg_ref[...], kg_ref[...], ho_ref[...])
    q_ref[...] = q
    k_ref[...] = k
    v_ref[...] = v
    hist = CONV_A - 1
    conv = cb_ref[...] + cw_ref[hist:hist + 1, :] * a
    for j in range(hist):
        conv = conv + cw_ref[j:j + 1, :] * sa_ref[:, j * D_A:(j + 1) * D_A]
    ao_ref[...] = _silu(_layernorm(conv, lg_ref[...], lb_ref[...])).astype(BF16)
    san_ref[:, 0:(hist - 1) * D_A] = sa_ref[:, D_A:hist * D_A]
    san_ref[:, (hist - 1) * D_A:hist * D_A] = a


def _sample_even(x, g, w, qg, kg, head_ones, sa, cw, cb, lg, lb):
    m = x.shape[0]
    return pl.pallas_call(
        _sample_even_body,
        out_shape=[
            jax.ShapeDtypeStruct((m, D_A), BF16),
            jax.ShapeDtypeStruct(sa.shape, F32),
            jax.ShapeDtypeStruct((m, D_B), F32),
            jax.ShapeDtypeStruct((m, D_B), F32),
            jax.ShapeDtypeStruct((m, D_B), F32),
        ],
        compiler_params=pltpu.CompilerParams(vmem_limit_bytes=VMEM_LIMIT),
        name="sample_even",
    )(x, g, w, qg, kg, head_ones, sa, cw, cb, lg, lb)


def _moba_sample_body(pt_ref, q_ref, qdh_ref, kn_ref, vn_ref, rbt_ref, *refs, n_pages, page):
    del pt_ref
    k_pages = refs[:n_pages]
    v_pages = refs[n_pages:2 * n_pages]
    o_ref, bias_ref, qb_ref, s_ref, p_ref = refs[2 * n_pages:]
    past = n_pages * page
    n_past_blocks = past // MOBA_BLOCK
    n_sel = min(MOBA_TOPK, n_past_blocks)

    @pl.when(pl.program_id(0) == 0)
    def _():
        pos = lax.broadcasted_iota(jnp.int32, (N_HEADS, past), 1)
        dist = past - pos
        out = jnp.zeros((N_HEADS, past), F32) + rbt_ref[:, 0:1]
        for i, th in enumerate(T5_THRESH):
            out = jnp.where(dist >= th, rbt_ref[:, i + 1:i + 2], out)
        bias_ref[...] = out * LOG2E

    head_of_lane = lax.broadcasted_iota(jnp.int32, (N_HEADS, D_B), 1) // HEAD_DIM
    hmask = head_of_lane == lax.broadcasted_iota(jnp.int32, (N_HEADS, D_B), 0)
    qbd = jnp.where(hmask, q_ref[0], 0.0)

    qdh = qdh_ref[0]
    for h in range(N_HEADS):
        qb_ref[h] = jnp.broadcast_to(qdh[:, h:h + 1], (HEAD_DIM, page))

    for pg in range(n_pages):
        rows = [jnp.sum(k_pages[pg][0, h] * qb_ref[h], axis=0, keepdims=True) for h in range(N_HEADS)]
        s_ref[:, pg * page:(pg + 1) * page] = jnp.concatenate(rows, axis=0)

    gates = [jnp.sum(s_ref[:, j * MOBA_BLOCK:(j + 1) * MOBA_BLOCK], axis=-1, keepdims=True)
             for j in range(n_past_blocks)]
    pieces = []
    for j in range(n_past_blocks):
        rank = jnp.zeros((N_HEADS, 1), jnp.int32)
        for i in range(n_past_blocks):
            if i == j:
                continue
            ahead = (gates[i] >= gates[j]) if i < j else (gates[i] > gates[j])
            rank = rank + jnp.where(ahead, 1, 0)
        lanes = slice(j * MOBA_BLOCK, (j + 1) * MOBA_BLOCK)
        pieces.append(s_ref[:, lanes] + bias_ref[:, lanes] + jnp.where(rank < n_sel, 0.0, -jnp.inf))
    s = jnp.concatenate(pieces, axis=-1)

    s_new = jnp.sum(qbd * kn_ref[0], axis=-1, keepdims=True) + rbt_ref[:, 0:1] * LOG2E
    m = jnp.maximum(jnp.max(s, axis=-1, keepdims=True), s_new)
    p = jnp.exp2(s - m)
    p_new = jnp.exp2(s_new - m)
    l = jnp.sum(p, axis=-1, keepdims=True) + p_new
    p_ref[...] = p

    ones = jnp.ones((SUBLANES, page), BF16)
    lane_sum = lambda x: lax.dot_general(ones, x, (((1,), (1,)), ((), ())), preferred_element_type=F32)
    outs = []
    for h in range(N_HEADS):
        acc = jnp.zeros((HEAD_DIM, page), F32)
        for pg in range(n_pages):
            acc = acc + v_pages[pg][0, h] * p_ref[h:h + 1, pg * page:(pg + 1) * page]
        hi = acc.astype(BF16)
        lo = (acc - hi.astype(F32)).astype(BF16)
        outs.append((lane_sum(hi) + lane_sum(lo))[0:1, :])
    o_past = jnp.concatenate(outs, axis=-1)
    spread = lambda col: jnp.sum(jnp.where(hmask, col, 0.0), axis=0, keepdims=True)
    o_ref[0] = (o_past + spread(p_new) * vn_ref[0]) / spread(l)


def _moba_sample(page_table, q, k_new, v_new, rel_bias_t, cache_kt, cache_vt):
    m, n_pages = page_table.shape
    page = cache_kt.shape[-1]
    row = lambda bi, pt: (bi, 0, 0)

    def page_spec(pg):
        return pl.BlockSpec((1, N_HEADS, HEAD_DIM, page), lambda bi, pt: (pt[bi, pg], 0, 0, 0))

    grid_spec = pltpu.PrefetchScalarGridSpec(
        num_scalar_prefetch=1,
        grid=(m,),
        in_specs=[
            pl.BlockSpec((1, 1, D_B), row),
            pl.BlockSpec((1, HEAD_DIM, N_HEADS), row),
            pl.BlockSpec((1, 1, D_B), row),
            pl.BlockSpec((1, 1, D_B), row),
            pl.BlockSpec((N_HEADS, NUM_BUCKETS), lambda bi, pt: (0, 0)),
        ] + [page_spec(pg) for pg in range(n_pages)] * 2,
        out_specs=pl.BlockSpec((1, 1, D_B), row),
        scratch_shapes=[
            pltpu.VMEM((N_HEADS, n_pages * page), F32),
            pltpu.VMEM((N_HEADS, HEAD_DIM, page), F32),
            pltpu.VMEM((N_HEADS, n_pages * page), F32),
            pltpu.VMEM((N_HEADS, n_pages * page), F32),
        ],
    )
    q_dh = q.reshape(m, N_HEADS, HEAD_DIM).transpose(0, 2, 1)
    return pl.pallas_call(
        functools.partial(_moba_sample_body, n_pages=n_pages, page=page),
        grid_spec=grid_spec,
        out_shape=jax.ShapeDtypeStruct((m, 1, D_B), F32),
        compiler_params=_cparams(1),
        name="sample_moba",
    )(page_table, q.reshape(m, 1, D_B), q_dh, k_new.reshape(m, 1, D_B), v_new.reshape(m, 1, D_B), rel_bias_t,
      *([cache_kt] * n_pages), *([cache_vt] * n_pages))


def _sample_ffn(x, st_ref, g_ref, wu_ref, cw_ref, cb_ref, wd_ref, stn_ref):
    w2 = 2 * D_FF
    h = _rmsnorm(x, g_ref[...]).astype(BF16)
    up = jnp.dot(h, wu_ref[...], preferred_element_type=F32)
    conv = (cw_ref[0:1, :] * st_ref[:, 0:w2] + cw_ref[1:2, :] * st_ref[:, w2:2 * w2]
            + cw_ref[2:3, :] * up + cb_ref[...])
    act = (_silu(conv[:, :D_FF]) * conv[:, D_FF:]).astype(BF16)
    stn_ref[:, 0:w2] = st_ref[:, w2:2 * w2]
    stn_ref[:, w2:2 * w2] = up
    return x + jnp.dot(act, wd_ref[...], preferred_element_type=F32)


def _sample_out_ffn_body(ao_ref, o_ref, x_ref, wo_ref, st_ref, g_ref, wu_ref, cw_ref, cb_ref, wd_ref,
                         y_ref, stn_ref):
    x1 = (x_ref[...]
          + jnp.dot(ao_ref[...], wo_ref[0:D_A, :], preferred_element_type=F32)
          + jnp.dot(o_ref[...].astype(BF16), wo_ref[D_A:D_A + D_B, :], preferred_element_type=F32))
    y_ref[...] = _sample_ffn(x1, st_ref, g_ref, wu_ref, cw_ref, cb_ref, wd_ref, stn_ref)


def _sample_out_ffn(ao, o, x, wo, st, g, wu, cw, cb, wd):
    return pl.pallas_call(
        _sample_out_ffn_body,
        out_shape=[jax.ShapeDtypeStruct(x.shape, F32), jax.ShapeDtypeStruct(st.shape, F32)],
        compiler_params=pltpu.CompilerParams(vmem_limit_bytes=VMEM_LIMIT),
        name="sample_out_ffn",
    )(ao, o, x, wo, st, g, wu, cw, cb, wd)


def _sample_odd_body(x_ref, gm_ref, wi_ref, ccw_ref, sc_ref, wo_ref, st_ref, g_ref, wu_ref, cw_ref, cb_ref,
                     wd_ref, y_ref, scn_ref, stn_ref):
    x = x_ref[...]
    h = _rmsnorm(x, gm_ref[...]).astype(BF16)
    z = jnp.dot(h, wi_ref[...], preferred_element_type=F32)
    gcu = z[:, D_C:2 * D_C] * z[:, 2 * D_C:]
    conv = ccw_ref[0:1, :] * sc_ref[:, 0:D_C] + ccw_ref[1:2, :] * sc_ref[:, D_C:2 * D_C] + ccw_ref[2:3, :] * gcu
    scn_ref[:, 0:D_C] = sc_ref[:, D_C:2 * D_C]
    scn_ref[:, D_C:2 * D_C] = gcu
    x1 = x + jnp.dot((z[:, :D_C] * conv).astype(BF16), wo_ref[...], preferred_element_type=F32)
    y_ref[...] = _sample_ffn(x1, st_ref, g_ref, wu_ref, cw_ref, cb_ref, wd_ref, stn_ref)


def _sample_odd(x, gm, wi, ccw, sc, wo, st, g, wu, cw, cb, wd):
    return pl.pallas_call(
        _sample_odd_body,
        out_shape=[jax.ShapeDtypeStruct(x.shape, F32), jax.ShapeDtypeStruct(sc.shape, F32),
                   jax.ShapeDtypeStruct(st.shape, F32)],
        compiler_params=pltpu.CompilerParams(vmem_limit_bytes=VMEM_LIMIT),
        name="sample_odd",
    )(x, gm, wi, ccw, sc, wo, st, g, wu, cw, cb, wd)


def kernel(x_prompt, x_sample, cache_k, cache_v, state_conv_a, state_conv_c, state_ffn, page_table, rel_bias,
           norm_mix_e, w_in_e, conv_a_w, conv_a_b, ln_a_g, ln_a_b, q_norm_g, k_norm_g, w_out_e,
           norm_mix_o, w_in_o, conv_c_w, w_out_o, norm_ffn, w_up, conv_f_w, conv_f_b, w_down):
    b, t, d = x_prompt.shape
    m = x_sample.shape[0]
    n_pool, page = cache_k.shape[1], cache_k.shape[2]
    n_pages = page_table.shape[1]
    assert norm_mix_e.shape[0] == 1 and norm_mix_o.shape[0] == 1 and norm_ffn.shape[0] == 2
    assert x_sample.shape[1] == 1 and t % TILE_M == 0 and t % page == 0
    assert (n_pages * page) % MOBA_BLOCK == 0 and MOBA_BLOCK % page == 0

    row = lambda v: v.reshape(1, -1)
    w_in_e_b = w_in_e[0].astype(BF16)
    w_out_e_b = w_out_e[0].astype(BF16)
    w_in_o_b = w_in_o[0].astype(BF16)
    w_out_o_b = w_out_o[0].astype(BF16)
    w_up_b = w_up.astype(BF16)
    w_down_b = w_down.astype(BF16)
    qg = row(jnp.tile(q_norm_g[0], N_HEADS))
    kg = row(jnp.tile(k_norm_g[0], N_HEADS))
    lane_head = jnp.arange(D_B, dtype=jnp.int32) // HEAD_DIM
    head_ones = (lane_head[:, None] == lane_head[None, :]).astype(BF16)
    g_e, g_o = row(norm_mix_e[0]), row(norm_mix_o[0])
    cab, lag, lab = row(conv_a_b[0]), row(ln_a_g[0]), row(ln_a_b[0])

    a_p, k_p, v_p, qt, kh, vt, km = _prompt_inproj(x_prompt, g_e, w_in_e_b, qg, kg, head_ones)
    nb = t // MOBA_BLOCK
    kmh = km.reshape(b, nb, N_HEADS, HEAD_DIM).transpose(0, 2, 1, 3)
    ot = _moba_prompt(rel_bias, qt, kh, vt, kmh)
    x1 = _prompt_mix_out(a_p, conv_a_w[0], cab, lag, lab, ot, x_prompt, w_out_e_b)
    x2, f0 = _prompt_ffn(x1, row(norm_ffn[0]), w_up_b[0], conv_f_w[0], row(conv_f_b[0]), w_down_b[0])
    x3, c_st = _prompt_mixc(x2, g_o, w_in_o_b, conv_c_w[0], w_out_o_b)
    y_prompt, f1 = _prompt_ffn(x3, row(norm_ffn[1]), w_up_b[1], conv_f_w[1], row(conv_f_b[1]), w_down_b[1])

    k_prompt = k_p.reshape(1, b, t // page, page, N_HEADS, HEAD_DIM)
    v_prompt = v_p.reshape(1, b, t // page, page, N_HEADS, HEAD_DIM)
    a_prompt = a_p[:, t - (CONV_A - 1):, :][None]
    c_prompt = c_st[:, SUBLANES - (CONV_C - 1):, :][None]
    f_prompt = jnp.stack([f0[:, SUBLANES - (CONV_F - 1):, :], f1[:, SUBLANES - (CONV_F - 1):, :]])

    xs = x_sample.reshape(m, d)
    sa = state_conv_a[0].reshape(m, (CONV_A - 1) * D_A)
    sc = state_conv_c[0].reshape(m, (CONV_C - 1) * D_C)
    sf = state_ffn.reshape(2, m, (CONV_F - 1) * 2 * D_FF)
    ao_s, sa_new, q_s, k_s, v_s = _sample_even(xs, g_e, w_in_e_b, qg, kg, head_ones, sa, conv_a_w[0], cab, lag, lab)
    o_s = _moba_sample(page_table, q_s, k_s, v_s, rel_bias.T,
                       cache_k[0].transpose(0, 2, 3, 1), cache_v[0].transpose(0, 2, 3, 1))
    xs1, sf0 = _sample_out_ffn(ao_s, o_s.reshape(m, D_B), xs, w_out_e_b, sf[0], row(norm_ffn[0]), w_up_b[0],
                               conv_f_w[0], row(conv_f_b[0]), w_down_b[0])
    ys, sc_new, sf1 = _sample_odd(xs1, g_o, w_in_o_b, conv_c_w[0], sc, w_out_o_b, sf[1], row(norm_ffn[1]),
                                  w_up_b[1], conv_f_w[1], row(conv_f_b[1]), w_down_b[1])

    y_sample = ys.reshape(m, 1, d)
    k_sample = k_s.reshape(1, m, 1, N_HEADS, HEAD_DIM)
    v_sample = v_s.reshape(1, m, 1, N_HEADS, HEAD_DIM)
    a_sample = sa_new.reshape(1, m, CONV_A - 1, D_A)
    c_sample = sc_new.reshape(1, m, CONV_C - 1, D_C)
    f_sample = jnp.stack([sf0, sf1]).reshape(2, m, CONV_F - 1, 2 * D_FF)
    return (y_prompt, y_sample, k_prompt, v_prompt, a_prompt, c_prompt, f_prompt,
            k_sample, v_sample, a_sample, c_sample, f_sample)
```

```python
import functools
import math

import jax
import jax.numpy as jnp
from jax import lax
from jax.experimental import pallas as pl
from jax.experimental.pallas import tpu as pltpu

F32 = jnp.float32
BF16 = jnp.bfloat16

EPS = 1e-6
D_MODEL = 1024
D_A = 512
CONV_A = 31
N_HEADS = 8
HEAD_DIM = 64
D_B = N_HEADS * HEAD_DIM
MOBA_BLOCK = 256
MOBA_TOPK = 3
NUM_BUCKETS = 32
MAX_DISTANCE = 128
D_C = 1024
CONV_C = 3
D_FF = 2816
CONV_F = 3
LOG2E = math.log2(math.e)
Q_SCALE = HEAD_DIM ** -0.5 * LOG2E
VT_ROWS = HEAD_DIM + 16

TILE_M = 512
CONV_ROWS = 64
CONV_LANES = 256
HALO_A = 32
HEAD_GROUP = 4
SOFTMAX_ROWS = 64
FAR_UNROLL = 4
CH = 256
SUBLANES = 8
VMEM_LIMIT = 56 * 1024 * 1024


def _t5_thresholds():
    max_exact = NUM_BUCKETS // 2
    th = list(range(1, max_exact + 1))
    for k in range(1, NUM_BUCKETS - max_exact):
        th.append(math.ceil(max_exact * (MAX_DISTANCE / max_exact) ** (k / (NUM_BUCKETS - max_exact))))
    return tuple(th)


T5_THRESH = _t5_thresholds()


def _cparams(n_grid):
    return pltpu.CompilerParams(dimension_semantics=("arbitrary",) * n_grid, vmem_limit_bytes=VMEM_LIMIT)


def _const_spec(shape):
    nd = len(shape)
    return pl.BlockSpec(shape, lambda *_: (0,) * nd, pipeline_mode=pl.Buffered(1))


def _rmsnorm(x, g):
    return x * lax.rsqrt(jnp.mean(x * x, axis=-1, keepdims=True) + EPS) * g


def _split_dot(x, w_bf16):
    hi = x.astype(BF16)
    lo = (x - hi.astype(F32)).astype(BF16)
    return (jnp.dot(hi, w_bf16, preferred_element_type=F32)
            + jnp.dot(lo, w_bf16, preferred_element_type=F32))


def _head_rmsnorm(x, g, head_ones):
    ss = _split_dot(x * x, head_ones)
    return x * lax.rsqrt(ss * (1.0 / HEAD_DIM) + EPS) * g


def _silu(x):
    return x * jax.nn.sigmoid(x)


def _layernorm(x, g, b):
    mu = jnp.mean(x, axis=-1, keepdims=True)
    xc = x - mu
    var = jnp.mean(xc * xc, axis=-1, keepdims=True)
    return xc * lax.rsqrt(var + EPS) * g + b


def _in_proj_even(x, g, w, qg, kg, head_ones):
    h = _rmsnorm(x, g).astype(BF16)
    z = jnp.dot(h, w, preferred_element_type=F32)
    a = z[:, :D_A] * jax.nn.sigmoid(z[:, D_A:2 * D_A])
    q = _head_rmsnorm(z[:, 2 * D_A:2 * D_A + D_B], qg, head_ones) * Q_SCALE
    k = _head_rmsnorm(z[:, 2 * D_A + D_B:2 * D_A + 2 * D_B], kg, head_ones)
    v = z[:, 2 * D_A + 2 * D_B:]
    return a, q, k, v


def _prompt_inproj_body(x_ref, g_ref, w_ref, qg_ref, kg_ref, ho_ref,
                        a_ref, kp_ref, vp_ref, qt_ref, kh_ref, vt_ref, km_ref, *, tm, page):
    a, q, k, v = _in_proj_even(x_ref[0], g_ref[...], w_ref[...], qg_ref[...], kg_ref[...], ho_ref[...])
    a_ref[0] = a
    qt = q.T
    kt = k.T
    vt = v.T
    for pg in range(tm // page):
        kp_ref[0, pg] = kt[:, pg * page:(pg + 1) * page].reshape(N_HEADS, HEAD_DIM, page)
        vp_ref[0, pg] = vt[:, pg * page:(pg + 1) * page].reshape(N_HEADS, HEAD_DIM, page)
    kb = k.astype(BF16)
    pad_row = lax.broadcasted_iota(jnp.int32, (N_HEADS, VT_ROWS - HEAD_DIM, MOBA_BLOCK), 1)
    ones_rows = jnp.where(pad_row == 0, 1.0, 0.0).astype(BF16)
    for i in range(tm // MOBA_BLOCK):
        r0 = i * MOBA_BLOCK
        qt_ref[0, :, i] = qt[:, r0:r0 + MOBA_BLOCK].reshape(N_HEADS, HEAD_DIM, MOBA_BLOCK).astype(BF16)
        vt_ref[0, :, i, 0:HEAD_DIM, :] = (
            vt[:, r0:r0 + MOBA_BLOCK].reshape(N_HEADS, HEAD_DIM, MOBA_BLOCK).astype(BF16))
        vt_ref[0, :, i, HEAD_DIM:VT_ROWS, :] = ones_rows
        for hh in range(N_HEADS):
            kh_ref[0, hh, i] = kb[r0:r0 + MOBA_BLOCK, hh * HEAD_DIM:(hh + 1) * HEAD_DIM]
        km_ref[0, i] = jnp.mean(k[r0:r0 + MOBA_BLOCK], axis=0, keepdims=True)


def _prompt_inproj(x, g, w, qg, kg, head_ones, page):
    b, t, d = x.shape
    tm = TILE_M
    nt = t // tm
    nb = t // MOBA_BLOCK
    bpt = tm // MOBA_BLOCK
    ppt = tm // page
    n_out = w.shape[1]
    tok = lambda bi, ti: (bi, ti, 0)
    blk5 = lambda bi, ti: (bi, 0, ti, 0, 0)
    pages = lambda bi, ti: (bi, ti, 0, 0, 0)
    return pl.pallas_call(
        functools.partial(_prompt_inproj_body, tm=tm, page=page),
        grid=(b, nt),
        in_specs=[
            pl.BlockSpec((1, tm, d), tok),
            _const_spec((1, d)),
            _const_spec((d, n_out)),
            _const_spec((1, D_B)),
            _const_spec((1, D_B)),
            _const_spec((D_B, D_B)),
        ],
        out_specs=[
            pl.BlockSpec((1, tm, D_A), tok),
            pl.BlockSpec((1, ppt, N_HEADS, HEAD_DIM, page), pages),
            pl.BlockSpec((1, ppt, N_HEADS, HEAD_DIM, page), pages),
            pl.BlockSpec((1, N_HEADS, bpt, HEAD_DIM, MOBA_BLOCK), blk5),
            pl.BlockSpec((1, N_HEADS, bpt, MOBA_BLOCK, HEAD_DIM), blk5),
            pl.BlockSpec((1, N_HEADS, bpt, VT_ROWS, MOBA_BLOCK), blk5),
            pl.BlockSpec((1, bpt, 1, D_B), lambda bi, ti: (bi, ti, 0, 0)),
        ],
        out_shape=[
            jax.ShapeDtypeStruct((b, t, D_A), F32),
            jax.ShapeDtypeStruct((b, t // page, N_HEADS, HEAD_DIM, page), F32),
            jax.ShapeDtypeStruct((b, t // page, N_HEADS, HEAD_DIM, page), F32),
            jax.ShapeDtypeStruct((b, N_HEADS, nb, HEAD_DIM, MOBA_BLOCK), BF16),
            jax.ShapeDtypeStruct((b, N_HEADS, nb, MOBA_BLOCK, HEAD_DIM), BF16),
            jax.ShapeDtypeStruct((b, N_HEADS, nb, VT_ROWS, MOBA_BLOCK), BF16),
            jax.ShapeDtypeStruct((b, nb, 1, D_B), F32),
        ],
        compiler_params=_cparams(2),
        name="prompt_inproj",
    )(x, g, w, qg, kg, head_ones)


def _t5_bias_scalar_table(dist, rb_ref, h):
    out = jnp.full(dist.shape, rb_ref[0, h] * LOG2E, F32)
    for i, th in enumerate(T5_THRESH):
        out = jnp.where(dist >= th, rb_ref[i + 1, h] * LOG2E, out)
    return out


def _moba_prompt_body(rb_ref, qt_ref, kh_ref, vt_ref, km_ref, ot_ref,
                      bdiag_ref, bsub_ref, selb_ref, m_ref, alpha_ref, acc_ref, s_ref, p_ref, *, nb, hg):
    h0 = pl.program_id(1) * hg
    n = pl.program_id(2)
    blk = MOBA_BLOCK

    @pl.when(n == 0)
    def _():
        ki = lax.broadcasted_iota(jnp.int32, (blk, blk), 0)
        qi = lax.broadcasted_iota(jnp.int32, (blk, blk), 1)
        d0 = qi - ki
        for g in range(hg):
            bdiag_ref[g] = jnp.where(d0 >= 0, _t5_bias_scalar_table(jnp.maximum(d0, 0), rb_ref, h0 + g),
                                     -jnp.inf)
            bsub_ref[g] = _t5_bias_scalar_table(d0 + blk, rb_ref, h0 + g)

    bi = lax.broadcasted_iota(jnp.int32, (nb, blk), 0)
    qts = [qt_ref[0, g, 0] for g in range(hg)]

    for g in range(hg):
        km = km_ref[0, g]
        km_hi = km.astype(BF16)
        km_lo = (km - km_hi.astype(F32)).astype(BF16)
        gate = (jnp.dot(km_hi, qts[g], preferred_element_type=F32)
                + jnp.dot(km_lo, qts[g], preferred_element_type=F32))
        avail = jnp.where(bi < n, 1.0, 0.0)
        far_bias = rb_ref[NUM_BUCKETS - 1, h0 + g] * LOG2E
        selb = jnp.full((nb, blk), -jnp.inf, F32)
        for _ in range(MOBA_TOPK):
            gm = jnp.where(avail > 0.0, gate, -jnp.inf)
            top = jnp.max(gm, axis=0, keepdims=True)
            first = jnp.where(avail > 0.0, jnp.where(gm == top, bi, nb), nb)
            pick = bi == jnp.min(first, axis=0, keepdims=True)
            selb = jnp.where(pick, jnp.where(bi == n - 1, 0.0, far_bias), selb)
            avail = jnp.where(pick, 0.0, avail)
        selb_ref[g] = selb

    n_chunks = blk // SOFTMAX_ROWS

    def logits(g, j, slot, bias):
        s = jnp.dot(kh_ref[0, g, j], qts[g], preferred_element_type=F32)
        s_ref[slot, g] = s if bias is None else s + bias

    def chunk(g, slot, c):
        return s_ref[slot, g, c * SOFTMAX_ROWS:(c + 1) * SOFTMAX_ROWS, :].reshape(
            SOFTMAX_ROWS // SUBLANES, SUBLANES, blk)

    def softmax(g, slot, row):
        m8 = jnp.max(chunk(g, slot, 0), axis=0)
        for c in range(1, n_chunks):
            m8 = jnp.maximum(m8, jnp.max(chunk(g, slot, c), axis=0))
        m_blk = jnp.max(m8, axis=0, keepdims=True)
        if row is None:
            mn = m_blk
            shift = mn
        else:
            m_old = m_ref[g]
            mn = jnp.maximum(m_old, m_blk + row)
            alpha_ref[slot, g] = jnp.exp2(m_old - mn)
            shift = mn - row
        for c in range(n_chunks):
            p = jnp.exp2(chunk(g, slot, c) - shift)
            p_ref[slot, g, c * SOFTMAX_ROWS:(c + 1) * SOFTMAX_ROWS, :] = (
                p.reshape(SOFTMAX_ROWS, blk).astype(BF16))
        m_ref[g] = mn

    def weighted_values(g, j, slot, first):
        pv = jnp.dot(vt_ref[0, g, j], p_ref[slot, g], preferred_element_type=F32)
        acc_ref[g] = pv if first else alpha_ref[slot, g] * acc_ref[g] + pv

    for g in range(hg):
        logits(g, n, 0, bdiag_ref[g])
    for g in range(hg):
        logits(g, jnp.maximum(n - 1, 0), 1, bsub_ref[g])
    for g in range(hg):
        softmax(g, 0, None)

    @pl.when(n == 0)
    def _():
        for g in range(hg):
            weighted_values(g, n, 0, True)

    @pl.when(n >= 1)
    def _():
        for g in range(hg):
            logits(g, 0, 0, None)
        for g in range(hg):
            weighted_values(g, n, 0, True)
        for g in range(hg):
            softmax(g, 1, selb_ref[g, pl.ds(n - 1, 1), :])

        def trip(j, slot):
            nxt = jnp.minimum(j + 1, n - 2)
            prev = jnp.where(j == 0, n - 1, j - 1)
            for g in range(hg):
                logits(g, nxt, 1 - slot, None)
            for g in range(hg):
                weighted_values(g, prev, 1 - slot, False)
            for g in range(hg):
                softmax(g, slot, selb_ref[g, pl.ds(j, 1), :])

        def far(i, carry):
            for r in range(FAR_UNROLL):
                trip(FAR_UNROLL * i + r, r & 1)
            return carry

        n_far = n - 1
        lax.fori_loop(0, n_far // FAR_UNROLL, far, 0)
        done = (n_far // FAR_UNROLL) * FAR_UNROLL
        piece = FAR_UNROLL // 2
        while piece >= 1:
            @pl.when(((n_far - done) & piece) != 0)
            def _(piece=piece, done=done):
                for r in range(piece):
                    trip(done + r, r & 1)
            done = done + ((n_far - done) & piece)
            piece //= 2

        last = jnp.where(n >= 2, n - 2, n - 1)
        for g in range(hg):
            weighted_values(g, last, n & 1, False)

    for g in range(hg):
        ot_ref[0, g * HEAD_DIM:(g + 1) * HEAD_DIM, :] = (
            acc_ref[g, 0:HEAD_DIM, :] / acc_ref[g, HEAD_DIM:HEAD_DIM + 1, :])


def _moba_prompt(rel_bias, qt, kh, vt, kmh):
    b, nh, nb, hd, blk = qt.shape
    t = nb * blk
    hg = HEAD_GROUP
    grp = lambda bi, hi, ni: (bi, hi, 0, 0, 0)
    return pl.pallas_call(
        functools.partial(_moba_prompt_body, nb=nb, hg=hg),
        grid=(b, nh // hg, nb),
        in_specs=[
            pl.BlockSpec(memory_space=pltpu.SMEM),
            pl.BlockSpec((1, hg, 1, hd, blk), lambda bi, hi, ni: (bi, hi, ni, 0, 0)),
            pl.BlockSpec((1, hg, nb, blk, hd), grp, pipeline_mode=pl.Buffered(1)),
            pl.BlockSpec((1, hg, nb, VT_ROWS, blk), grp, pipeline_mode=pl.Buffered(1)),
            pl.BlockSpec((1, hg, nb, hd), lambda bi, hi, ni: (bi, hi, 0, 0)),
        ],
        out_specs=pl.BlockSpec((1, hg * hd, blk), lambda bi, hi, ni: (bi, hi, ni)),
        out_shape=jax.ShapeDtypeStruct((b, nh * hd, t), F32),
        scratch_shapes=[
            pltpu.VMEM((hg, blk, blk), F32),
            pltpu.VMEM((hg, blk, blk), F32),
            pltpu.VMEM((hg, nb, blk), F32),
            pltpu.VMEM((hg, 1, blk), F32),
            pltpu.VMEM((2, hg, 1, blk), F32),
            pltpu.VMEM((hg, VT_ROWS, blk), F32),
            pltpu.VMEM((2, hg, blk, blk), F32),
            pltpu.VMEM((2, hg, blk, blk), BF16),
        ],
        compiler_params=_cparams(3),
        name="prompt_moba",
    )(rel_bias, qt, kh, vt, kmh)


def _prompt_mix_out_body(a_ref, halo_ref, cw_ref, cb_ref, lg_ref, lb_ref, ot_ref, x_ref, w_ref,
                         y_ref, sh_ref, conv_ref, wb_ref, *, tm):
    t = pl.program_id(1)
    rows = tm + HALO_A

    @pl.when((pl.program_id(0) == 0) & (t == 0))
    def _():
        for j in range(CONV_A):
            wb_ref[j] = jnp.broadcast_to(cw_ref[j:j + 1, :], (SUBLANES, D_A))

    sh_ref[0, 0:HALO_A] = jnp.where(t == 0, 0.0, halo_ref[0])
    sh_ref[0, HALO_A:rows] = a_ref[0]
    for s in range(1, SUBLANES):
        sh_ref[s, 0:rows - SUBLANES] = sh_ref[0, s:s + rows - SUBLANES]

    groups = CONV_ROWS // SUBLANES
    first_off = HALO_A - (CONV_A - 1)

    def chunk(c, carry):
        r0 = pl.multiple_of(c * CONV_ROWS, CONV_ROWS)
        for l0 in range(0, D_A, CONV_LANES):
            lanes = slice(l0, l0 + CONV_LANES)
            acc = [jnp.zeros((SUBLANES, CONV_LANES), F32) for _ in range(groups)]
            for s in range(SUBLANES):
                offs = [o for o in range(first_off, first_off + CONV_A) if o % SUBLANES == s]
                qs = [o // SUBLANES for o in offs]
                slab = {gq: sh_ref[s, pl.ds(r0 + gq * SUBLANES, SUBLANES), lanes]
                        for gq in range(min(qs), max(qs) + groups)}
                for o, q in zip(offs, qs):
                    w = wb_ref[o - first_off, :, lanes]
                    for g in range(groups):
                        acc[g] = acc[g] + w * slab[g + q]
            for g in range(groups):
                conv_ref[pl.ds(r0 + g * SUBLANES, SUBLANES), lanes] = acc[g]
        return carry

    lax.fori_loop(0, tm // CONV_ROWS, chunk, 0)

    ao = _silu(_layernorm(conv_ref[...] + cb_ref[...], lg_ref[...], lb_ref[...])).astype(BF16)
    o = ot_ref[0].T.astype(BF16)
    y = (jnp.dot(ao, w_ref[0:D_A, :], preferred_element_type=F32)
         + jnp.dot(o, w_ref[D_A:D_A + D_B, :], preferred_element_type=F32))
    y_ref[0] = x_ref[0] + y


def _prompt_mix_out(a, cw, cb, lg, lb, ot, x, w):
    b, t, d = x.shape
    tm = TILE_M
    nt = t // tm
    hpt = tm // HALO_A
    tok = lambda bi, ti: (bi, ti, 0)
    return pl.pallas_call(
        functools.partial(_prompt_mix_out_body, tm=tm),
        grid=(b, nt),
        in_specs=[
            pl.BlockSpec((1, tm, D_A), tok),
            pl.BlockSpec((1, HALO_A, D_A), lambda bi, ti: (bi, jnp.maximum(ti * hpt - 1, 0), 0)),
            _const_spec((CONV_A, D_A)),
            _const_spec((1, D_A)),
            _const_spec((1, D_A)),
            _const_spec((1, D_A)),
            pl.BlockSpec((1, D_B, tm), lambda bi, ti: (bi, 0, ti)),
            pl.BlockSpec((1, tm, d), tok),
            _const_spec((D_A + D_B, d)),
        ],
        out_specs=pl.BlockSpec((1, tm, d), tok),
        out_shape=jax.ShapeDtypeStruct((b, t, d), F32),
        scratch_shapes=[
            pltpu.VMEM((SUBLANES, tm + HALO_A, D_A), F32),
            pltpu.VMEM((tm, D_A), F32),
            pltpu.VMEM((CONV_A, SUBLANES, D_A), F32),
        ],
        compiler_params=_cparams(2),
        name="prompt_mix_out",
    )(a, a, cw, cb, lg, lb, ot, x, w)


def _conv3_chunk(u, ext_ref, carry_ref, cw_ref, c0, tm):
    ext_ref[0:SUBLANES] = carry_ref[:, c0:c0 + CH]
    ext_ref[SUBLANES:SUBLANES + tm] = u
    carry_ref[:, c0:c0 + CH] = u[tm - SUBLANES:tm]
    return (cw_ref[0:1, c0:c0 + CH] * ext_ref[SUBLANES - 2:SUBLANES - 2 + tm]
            + cw_ref[1:2, c0:c0 + CH] * ext_ref[SUBLANES - 1:SUBLANES - 1 + tm]
            + cw_ref[2:3, c0:c0 + CH] * u)


def _prompt_ffn_body(x_ref, g_ref, wu_ref, cw_ref, cb_ref, wd_ref, y_ref, st_ref,
                     carry_ref, extg_ref, extu_ref, act_ref, *, tm):
    @pl.when(pl.program_id(1) == 0)
    def _():
        carry_ref[...] = jnp.zeros_like(carry_ref)

    x = x_ref[0]
    h = _rmsnorm(x, g_ref[...]).astype(BF16)
    for c in range(D_FF // CH):
        cg = c * CH
        cu = D_FF + c * CH
        ug = jnp.dot(h, wu_ref[:, cg:cg + CH], preferred_element_type=F32)
        uu = jnp.dot(h, wu_ref[:, cu:cu + CH], preferred_element_type=F32)
        gg = _conv3_chunk(ug, extg_ref, carry_ref, cw_ref, cg, tm) + cb_ref[:, cg:cg + CH]
        gu = _conv3_chunk(uu, extu_ref, carry_ref, cw_ref, cu, tm) + cb_ref[:, cu:cu + CH]
        act_ref[:, cg:cg + CH] = (_silu(gg) * gu).astype(BF16)
    st_ref[0] = carry_ref[...]
    y_ref[0] = x + jnp.dot(act_ref[...], wd_ref[...], preferred_element_type=F32)


def _prompt_ffn(x, g, wu, cw, cb, wd):
    b, t, d = x.shape
    tm = TILE_M
    nt = t // tm
    tok = lambda bi, ti: (bi, ti, 0)
    return pl.pallas_call(
        functools.partial(_prompt_ffn_body, tm=tm),
        grid=(b, nt),
        in_specs=[
            pl.BlockSpec((1, tm, d), tok),
            _const_spec((1, d)),
            _const_spec((d, 2 * D_FF)),
            _const_spec((CONV_F, 2 * D_FF)),
            _const_spec((1, 2 * D_FF)),
            _const_spec((D_FF, d)),
        ],
        out_specs=[
            pl.BlockSpec((1, tm, d), tok),
            pl.BlockSpec((1, SUBLANES, 2 * D_FF), lambda bi, ti: (bi, 0, 0)),
        ],
        out_shape=[
            jax.ShapeDtypeStruct((b, t, d), F32),
            jax.ShapeDtypeStruct((b, SUBLANES, 2 * D_FF), F32),
        ],
        scratch_shapes=[
            pltpu.VMEM((SUBLANES, 2 * D_FF), F32),
            pltpu.VMEM((tm + SUBLANES, CH), F32),
            pltpu.VMEM((tm + SUBLANES, CH), F32),
            pltpu.VMEM((tm, D_FF), BF16),
        ],
        compiler_params=_cparams(2),
        name="prompt_ffn",
    )(x, g, wu, cw, cb, wd)


def _prompt_mixc_body(x_ref, g_ref, wi_ref, cw_ref, wo_ref, y_ref, st_ref,
                      carry_ref, ext_ref, z_ref, *, tm):
    @pl.when(pl.program_id(1) == 0)
    def _():
        carry_ref[...] = jnp.zeros_like(carry_ref)

    x = x_ref[0]
    h = _rmsnorm(x, g_ref[...]).astype(BF16)
    for c in range(D_C // CH):
        c0 = c * CH
        gb = jnp.dot(h, wi_ref[:, c0:c0 + CH], preferred_element_type=F32)
        gc = jnp.dot(h, wi_ref[:, D_C + c0:D_C + c0 + CH], preferred_element_type=F32)
        u = jnp.dot(h, wi_ref[:, 2 * D_C + c0:2 * D_C + c0 + CH], preferred_element_type=F32)
        conv = _conv3_chunk(gc * u, ext_ref, carry_ref, cw_ref, c0, tm)
        z_ref[:, c0:c0 + CH] = (gb * conv).astype(BF16)
    st_ref[0] = carry_ref[...]
    y_ref[0] = x + jnp.dot(z_ref[...], wo_ref[...], preferred_element_type=F32)


def _prompt_mixc(x, g, wi, cw, wo):
    b, t, d = x.shape
    tm = TILE_M
    nt = t // tm
    tok = lambda bi, ti: (bi, ti, 0)
    return pl.pallas_call(
        functools.partial(_prompt_mixc_body, tm=tm),
        grid=(b, nt),
        in_specs=[
            pl.BlockSpec((1, tm, d), tok),
            _const_spec((1, d)),
            _const_spec((d, 3 * D_C)),
            _const_spec((CONV_C, D_C)),
            _const_spec((D_C, d)),
        ],
        out_specs=[
            pl.BlockSpec((1, tm, d), tok),
            pl.BlockSpec((1, SUBLANES, D_C), lambda bi, ti: (bi, 0, 0)),
        ],
        out_shape=[
            jax.ShapeDtypeStruct((b, t, d), F32),
            jax.ShapeDtypeStruct((b, SUBLANES, D_C), F32),
        ],
        scratch_shapes=[
            pltpu.VMEM((SUBLANES, D_C), F32),
            pltpu.VMEM((tm + SUBLANES, CH), F32),
            pltpu.VMEM((tm, D_C), BF16),
        ],
        compiler_params=_cparams(2),
        name="prompt_mixc",
    )(x, g, wi, cw, wo)


def _sample_even_body(x_ref, g_ref, w_ref, qg_ref, kg_ref, ho_ref, sa_ref, cw_ref, cb_ref, lg_ref, lb_ref,
                      ao_ref, san_ref, q_ref, k_ref, v_ref):
    a, q, k, v = _in_proj_even(x_ref[...], g_ref[...], w_ref[...], qg_ref[...], kg_ref[...], ho_ref[...])
    q_ref[...] = q
    k_ref[...] = k
    v_ref[...] = v
    hist = CONV_A - 1
    conv = cb_ref[...] + cw_ref[hist:hist + 1, :] * a
    for j in range(hist):
        conv = conv + cw_ref[j:j + 1, :] * sa_ref[j]
    ao_ref[...] = _silu(_layernorm(conv, lg_ref[...], lb_ref[...])).astype(BF16)
    san_ref[0:hist - 1] = sa_ref[1:hist]
    san_ref[hist - 1] = a


def _sample_even(x, g, w, qg, kg, head_ones, sa, cw, cb, lg, lb):
    m = x.shape[0]
    return pl.pallas_call(
        _sample_even_body,
        out_shape=[
            jax.ShapeDtypeStruct((m, D_A), BF16),
            jax.ShapeDtypeStruct(sa.shape, F32),
            jax.ShapeDtypeStruct((m, D_B), F32),
            jax.ShapeDtypeStruct((m, D_B), F32),
            jax.ShapeDtypeStruct((m, D_B), F32),
        ],
        compiler_params=pltpu.CompilerParams(vmem_limit_bytes=VMEM_LIMIT),
        name="sample_even",
    )(x, g, w, qg, kg, head_ones, sa, cw, cb, lg, lb)


def _moba_sample_body(pt_ref, q_ref, qdh_ref, kn_ref, vn_ref, rbt_ref, *refs, n_pages, page):
    del pt_ref
    k_pages = refs[:n_pages]
    v_pages = refs[n_pages:2 * n_pages]
    o_ref, bias_ref, qb_ref, s_ref, p_ref = refs[2 * n_pages:]
    past = n_pages * page
    n_past_blocks = past // MOBA_BLOCK
    n_sel = min(MOBA_TOPK, n_past_blocks)

    @pl.when(pl.program_id(0) == 0)
    def _():
        pos = lax.broadcasted_iota(jnp.int32, (N_HEADS, past), 1)
        dist = past - pos
        out = jnp.zeros((N_HEADS, past), F32) + rbt_ref[:, 0:1]
        for i, th in enumerate(T5_THRESH):
            out = jnp.where(dist >= th, rbt_ref[:, i + 1:i + 2], out)
        bias_ref[...] = out * LOG2E

    head_of_lane = lax.broadcasted_iota(jnp.int32, (N_HEADS, D_B), 1) // HEAD_DIM
    hmask = head_of_lane == lax.broadcasted_iota(jnp.int32, (N_HEADS, D_B), 0)
    qbd = jnp.where(hmask, q_ref[0], 0.0)

    qdh = qdh_ref[0]
    for h in range(N_HEADS):
        qb_ref[h] = jnp.broadcast_to(qdh[:, h:h + 1], (HEAD_DIM, page))

    for pg in range(n_pages):
        rows = [jnp.sum(k_pages[pg][0, h] * qb_ref[h], axis=0, keepdims=True) for h in range(N_HEADS)]
        s_ref[:, pg * page:(pg + 1) * page] = jnp.concatenate(rows, axis=0)

    gates = [jnp.sum(s_ref[:, j * MOBA_BLOCK:(j + 1) * MOBA_BLOCK], axis=-1, keepdims=True)
             for j in range(n_past_blocks)]
    pieces = []
    for j in range(n_past_blocks):
        rank = jnp.zeros((N_HEADS, 1), jnp.int32)
        for i in range(n_past_blocks):
            if i == j:
                continue
            ahead = (gates[i] >= gates[j]) if i < j else (gates[i] > gates[j])
            rank = rank + jnp.where(ahead, 1, 0)
        lanes = slice(j * MOBA_BLOCK, (j + 1) * MOBA_BLOCK)
        pieces.append(s_ref[:, lanes] + bias_ref[:, lanes] + jnp.where(rank < n_sel, 0.0, -jnp.inf))
    s = jnp.concatenate(pieces, axis=-1)

    s_new = jnp.sum(qbd * kn_ref[0], axis=-1, keepdims=True) + rbt_ref[:, 0:1] * LOG2E
    m = jnp.maximum(jnp.max(s, axis=-1, keepdims=True), s_new)
    p = jnp.exp2(s - m)
    p_new = jnp.exp2(s_new - m)
    l = jnp.sum(p, axis=-1, keepdims=True) + p_new
    p_ref[...] = p

    ones = jnp.ones((SUBLANES, page), BF16)
    lane_sum = lambda x: lax.dot_general(ones, x, (((1,), (1,)), ((), ())), preferred_element_type=F32)
    outs = []
    for h in range(N_HEADS):
        acc = jnp.zeros((HEAD_DIM, page), F32)
        for pg in range(n_pages):
            acc = acc + v_pages[pg][0, h] * p_ref[h:h + 1, pg * page:(pg + 1) * page]
        hi = acc.astype(BF16)
        lo = (acc - hi.astype(F32)).astype(BF16)
        outs.append((lane_sum(hi) + lane_sum(lo))[0:1, :])
    o_past = jnp.concatenate(outs, axis=-1)
    spread = lambda col: jnp.sum(jnp.where(hmask, col, 0.0), axis=0, keepdims=True)
    o_ref[0] = (o_past + spread(p_new) * vn_ref[0]) / spread(l)


def _moba_sample(page_table, q, k_new, v_new, rel_bias_t, cache_kt, cache_vt):
    m, n_pages = page_table.shape
    page = cache_kt.shape[-1]
    row = lambda bi, pt: (bi, 0, 0)

    def page_spec(pg):
        return pl.BlockSpec((1, N_HEADS, HEAD_DIM, page), lambda bi, pt: (pt[bi, pg], 0, 0, 0))

    grid_spec = pltpu.PrefetchScalarGridSpec(
        num_scalar_prefetch=1,
        grid=(m,),
        in_specs=[
            pl.BlockSpec((1, 1, D_B), row),
            pl.BlockSpec((1, HEAD_DIM, N_HEADS), row),
            pl.BlockSpec((1, 1, D_B), row),
            pl.BlockSpec((1, 1, D_B), row),
            pl.BlockSpec((N_HEADS, NUM_BUCKETS), lambda bi, pt: (0, 0)),
        ] + [page_spec(pg) for pg in range(n_pages)] * 2,
        out_specs=pl.BlockSpec((1, 1, D_B), row),
        scratch_shapes=[
            pltpu.VMEM((N_HEADS, n_pages * page), F32),
            pltpu.VMEM((N_HEADS, HEAD_DIM, page), F32),
            pltpu.VMEM((N_HEADS, n_pages * page), F32),
            pltpu.VMEM((N_HEADS, n_pages * page), F32),
        ],
    )
    q_dh = q.reshape(m, N_HEADS, HEAD_DIM).transpose(0, 2, 1)
    return pl.pallas_call(
        functools.partial(_moba_sample_body, n_pages=n_pages, page=page),
        grid_spec=grid_spec,
        out_shape=jax.ShapeDtypeStruct((m, 1, D_B), F32),
        compiler_params=_cparams(1),
        name="sample_moba",
    )(page_table, q.reshape(m, 1, D_B), q_dh, k_new.reshape(m, 1, D_B), v_new.reshape(m, 1, D_B), rel_bias_t,
      *([cache_kt] * n_pages), *([cache_vt] * n_pages))


def _sample_ffn(x, st_ref, g_ref, wu_ref, cw_ref, cb_ref, wd_ref, stn_ref):
    h = _rmsnorm(x, g_ref[...]).astype(BF16)
    up = jnp.dot(h, wu_ref[...], preferred_element_type=F32)
    older, newer = st_ref[:, 0, :], st_ref[:, 1, :]
    conv = cw_ref[0:1, :] * older + cw_ref[1:2, :] * newer + cw_ref[2:3, :] * up + cb_ref[...]
    act = (_silu(conv[:, :D_FF]) * conv[:, D_FF:]).astype(BF16)
    stn_ref[:, 0, :] = newer
    stn_ref[:, 1, :] = up
    return x + jnp.dot(act, wd_ref[...], preferred_element_type=F32)


def _sample_out_ffn_body(ao_ref, o_ref, x_ref, wo_ref, st_ref, g_ref, wu_ref, cw_ref, cb_ref, wd_ref,
                         y_ref, stn_ref):
    x1 = (x_ref[...]
          + jnp.dot(ao_ref[...], wo_ref[0:D_A, :], preferred_element_type=F32)
          + jnp.dot(o_ref[...].astype(BF16), wo_ref[D_A:D_A + D_B, :], preferred_element_type=F32))
    y_ref[...] = _sample_ffn(x1, st_ref, g_ref, wu_ref, cw_ref, cb_ref, wd_ref, stn_ref)


def _sample_out_ffn(ao, o, x, wo, st, g, wu, cw, cb, wd):
    return pl.pallas_call(
        _sample_out_ffn_body,
        out_shape=[jax.ShapeDtypeStruct(x.shape, F32), jax.ShapeDtypeStruct(st.shape, F32)],
        compiler_params=pltpu.CompilerParams(vmem_limit_bytes=VMEM_LIMIT),
        name="sample_out_ffn",
    )(ao, o, x, wo, st, g, wu, cw, cb, wd)


def _sample_odd_body(x_ref, gm_ref, wi_ref, ccw_ref, sc_ref, wo_ref, st_ref, g_ref, wu_ref, cw_ref, cb_ref,
                     wd_ref, y_ref, scn_ref, stn_ref):
    x = x_ref[...]
    h = _rmsnorm(x, gm_ref[...]).astype(BF16)
    z = jnp.dot(h, wi_ref[...], preferred_element_type=F32)
    gcu = z[:, D_C:2 * D_C] * z[:, 2 * D_C:]
    older, newer = sc_ref[:, 0, :], sc_ref[:, 1, :]
    conv = ccw_ref[0:1, :] * older + ccw_ref[1:2, :] * newer + ccw_ref[2:3, :] * gcu
    scn_ref[:, 0, :] = newer
    scn_ref[:, 1, :] = gcu
    x1 = x + jnp.dot((z[:, :D_C] * conv).astype(BF16), wo_ref[...], preferred_element_type=F32)
    y_ref[...] = _sample_ffn(x1, st_ref, g_ref, wu_ref, cw_ref, cb_ref, wd_ref, stn_ref)


def _sample_odd(x, gm, wi, ccw, sc, wo, st, g, wu, cw, cb, wd):
    return pl.pallas_call(
        _sample_odd_body,
        out_shape=[jax.ShapeDtypeStruct(x.shape, F32), jax.ShapeDtypeStruct(sc.shape, F32),
                   jax.ShapeDtypeStruct(st.shape, F32)],
        compiler_params=pltpu.CompilerParams(vmem_limit_bytes=VMEM_LIMIT),
        name="sample_odd",
    )(x, gm, wi, ccw, sc, wo, st, g, wu, cw, cb, wd)


def kernel(x_prompt, x_sample, cache_k, cache_v, state_conv_a, state_conv_c, state_ffn, page_table, rel_bias,
           norm_mix_e, w_in_e, conv_a_w, conv_a_b, ln_a_g, ln_a_b, q_norm_g, k_norm_g, w_out_e,
           norm_mix_o, w_in_o, conv_c_w, w_out_o, norm_ffn, w_up, conv_f_w, conv_f_b, w_down):
    b, t, d = x_prompt.shape
    m = x_sample.shape[0]
    n_pool, page = cache_k.shape[1], cache_k.shape[2]
    n_pages = page_table.shape[1]
    assert norm_mix_e.shape[0] == 1 and norm_mix_o.shape[0] == 1 and norm_ffn.shape[0] == 2
    assert x_sample.shape[1] == 1 and t % TILE_M == 0 and t % page == 0
    assert (n_pages * page) % MOBA_BLOCK == 0 and MOBA_BLOCK % page == 0

    row = lambda v: v.reshape(1, -1)
    w_in_e_b = w_in_e[0].astype(BF16)
    w_out_e_b = w_out_e[0].astype(BF16)
    w_in_o_b = w_in_o[0].astype(BF16)
    w_out_o_b = w_out_o[0].astype(BF16)
    w_up_b = w_up.astype(BF16)
    w_down_b = w_down.astype(BF16)
    qg = row(jnp.tile(q_norm_g[0], N_HEADS))
    kg = row(jnp.tile(k_norm_g[0], N_HEADS))
    lane_head = jnp.arange(D_B, dtype=jnp.int32) // HEAD_DIM
    head_ones = (lane_head[:, None] == lane_head[None, :]).astype(BF16)
    g_e, g_o = row(norm_mix_e[0]), row(norm_mix_o[0])
    cab, lag, lab = row(conv_a_b[0]), row(ln_a_g[0]), row(ln_a_b[0])

    a_p, k_p, v_p, qt, kh, vt, km = _prompt_inproj(x_prompt, g_e, w_in_e_b, qg, kg, head_ones, page)
    nb = t // MOBA_BLOCK
    kmh = km.reshape(b, nb, N_HEADS, HEAD_DIM).transpose(0, 2, 1, 3)
    ot = _moba_prompt(rel_bias, qt, kh, vt, kmh)
    x1 = _prompt_mix_out(a_p, conv_a_w[0], cab, lag, lab, ot, x_prompt, w_out_e_b)
    x2, f0 = _prompt_ffn(x1, row(norm_ffn[0]), w_up_b[0], conv_f_w[0], row(conv_f_b[0]), w_down_b[0])
    x3, c_st = _prompt_mixc(x2, g_o, w_in_o_b, conv_c_w[0], w_out_o_b)
    y_prompt, f1 = _prompt_ffn(x3, row(norm_ffn[1]), w_up_b[1], conv_f_w[1], row(conv_f_b[1]), w_down_b[1])

    k_prompt = k_p.transpose(0, 1, 4, 2, 3)[None]
    v_prompt = v_p.transpose(0, 1, 4, 2, 3)[None]
    a_prompt = a_p[:, t - (CONV_A - 1):, :][None]
    c_prompt = c_st[:, SUBLANES - (CONV_C - 1):, :][None]
    f_prompt = jnp.stack([f0[:, SUBLANES - (CONV_F - 1):, :], f1[:, SUBLANES - (CONV_F - 1):, :]])

    xs = x_sample.reshape(m, d)
    sa = state_conv_a[0].transpose(1, 0, 2)
    sc = state_conv_c[0]
    sf = state_ffn
    ao_s, sa_new, q_s, k_s, v_s = _sample_even(xs, g_e, w_in_e_b, qg, kg, head_ones, sa, conv_a_w[0], cab, lag, lab)
    o_s = _moba_sample(page_table, q_s, k_s, v_s, rel_bias.T,
                       cache_k[0].transpose(0, 2, 3, 1), cache_v[0].transpose(0, 2, 3, 1))
    xs1, sf0 = _sample_out_ffn(ao_s, o_s.reshape(m, D_B), xs, w_out_e_b, sf[0], row(norm_ffn[0]), w_up_b[0],
                               conv_f_w[0], row(conv_f_b[0]), w_down_b[0])
    ys, sc_new, sf1 = _sample_odd(xs1, g_o, w_in_o_b, conv_c_w[0], sc, w_out_o_b, sf[1], row(norm_ffn[1]),
                                  w_up_b[1], conv_f_w[1], row(conv_f_b[1]), w_down_b[1])

    y_sample = ys.reshape(m, 1, d)
    k_sample = k_s.reshape(1, m, 1, N_HEADS, HEAD_DIM)
    v_sample = v_s.reshape(1, m, 1, N_HEADS, HEAD_DIM)
    a_sample = sa_new.transpose(1, 0, 2)[None]
    c_sample = sc_new[None]
    f_sample = jnp.stack([sf0, sf1])
    return (y_prompt, y_sample, k_prompt, v_prompt, a_prompt, c_prompt, f_prompt,
            k_sample, v_sample, a_sample, c_sample, f_sample)
```

```python
import functools
import math

import jax
import jax.numpy as jnp
from jax import lax
from jax.experimental import pallas as pl
from jax.experimental.pallas import tpu as pltpu

F32 = jnp.float32
BF16 = jnp.bfloat16

EPS = 1e-6
D_MODEL = 1024
D_A = 512
CONV_A = 31
N_HEADS = 8
HEAD_DIM = 64
D_B = N_HEADS * HEAD_DIM
MOBA_BLOCK = 256
MOBA_TOPK = 3
NUM_BUCKETS = 32
MAX_DISTANCE = 128
D_C = 1024
CONV_C = 3
D_FF = 2816
CONV_F = 3
LOG2E = math.log2(math.e)
Q_SCALE = HEAD_DIM ** -0.5 * LOG2E
VT_ROWS = HEAD_DIM + 16

TILE_M = 512
CONV_ROWS = 64
CONV_LANES = 256
HALO_A = 32
HEAD_GROUP = 8
SOFTMAX_ROWS = 64
FAR_UNROLL = 2
CH = 256
SUBLANES = 8
VMEM_LIMIT = 56 * 1024 * 1024


def _t5_thresholds():
    max_exact = NUM_BUCKETS // 2
    th = list(range(1, max_exact + 1))
    for k in range(1, NUM_BUCKETS - max_exact):
        th.append(math.ceil(max_exact * (MAX_DISTANCE / max_exact) ** (k / (NUM_BUCKETS - max_exact))))
    return tuple(th)


T5_THRESH = _t5_thresholds()


def _cparams(n_grid):
    return pltpu.CompilerParams(dimension_semantics=("arbitrary",) * n_grid, vmem_limit_bytes=VMEM_LIMIT)


def _const_spec(shape):
    nd = len(shape)
    return pl.BlockSpec(shape, lambda *_: (0,) * nd, pipeline_mode=pl.Buffered(1))


def _rmsnorm(x, g):
    return x * lax.rsqrt(jnp.mean(x * x, axis=-1, keepdims=True) + EPS) * g


def _split_dot(x, w_bf16):
    hi = x.astype(BF16)
    lo = (x - hi.astype(F32)).astype(BF16)
    return (jnp.dot(hi, w_bf16, preferred_element_type=F32)
            + jnp.dot(lo, w_bf16, preferred_element_type=F32))


def _head_rmsnorm(x, g, head_ones):
    ss = _split_dot(x * x, head_ones)
    return x * lax.rsqrt(ss * (1.0 / HEAD_DIM) + EPS) * g


def _silu(x):
    return x * jax.nn.sigmoid(x)


def _layernorm(x, g, b):
    mu = jnp.mean(x, axis=-1, keepdims=True)
    xc = x - mu
    var = jnp.mean(xc * xc, axis=-1, keepdims=True)
    return xc * lax.rsqrt(var + EPS) * g + b


def _in_proj_even(x, g, w, qg, kg, head_ones):
    h = _rmsnorm(x, g).astype(BF16)
    z = jnp.dot(h, w, preferred_element_type=F32)
    a = z[:, :D_A] * jax.nn.sigmoid(z[:, D_A:2 * D_A])
    q = _head_rmsnorm(z[:, 2 * D_A:2 * D_A + D_B], qg, head_ones) * Q_SCALE
    k = _head_rmsnorm(z[:, 2 * D_A + D_B:2 * D_A + 2 * D_B], kg, head_ones)
    v = z[:, 2 * D_A + 2 * D_B:]
    return a, q, k, v


def _prompt_inproj_body(x_ref, g_ref, w_ref, qg_ref, kg_ref, ho_ref,
                        a_ref, kp_ref, vp_ref, qt_ref, kh_ref, vt_ref, km_ref, *, tm, page):
    a, q, k, v = _in_proj_even(x_ref[0], g_ref[...], w_ref[...], qg_ref[...], kg_ref[...], ho_ref[...])
    a_ref[0] = a
    qt = q.T
    kt = k.T
    vt = v.T
    for pg in range(tm // page):
        kp_ref[0, pg] = kt[:, pg * page:(pg + 1) * page].reshape(N_HEADS, HEAD_DIM, page)
        vp_ref[0, pg] = vt[:, pg * page:(pg + 1) * page].reshape(N_HEADS, HEAD_DIM, page)
    kb = k.astype(BF16)
    pad_row = lax.broadcasted_iota(jnp.int32, (N_HEADS, VT_ROWS - HEAD_DIM, MOBA_BLOCK), 1)
    ones_rows = jnp.where(pad_row == 0, 1.0, 0.0).astype(BF16)
    for i in range(tm // MOBA_BLOCK):
        r0 = i * MOBA_BLOCK
        qt_ref[0, :, i] = qt[:, r0:r0 + MOBA_BLOCK].reshape(N_HEADS, HEAD_DIM, MOBA_BLOCK).astype(BF16)
        vt_ref[0, :, i, 0:HEAD_DIM, :] = (
            vt[:, r0:r0 + MOBA_BLOCK].reshape(N_HEADS, HEAD_DIM, MOBA_BLOCK).astype(BF16))
        vt_ref[0, :, i, HEAD_DIM:VT_ROWS, :] = ones_rows
        for hh in range(N_HEADS):
            kh_ref[0, hh, i] = kb[r0:r0 + MOBA_BLOCK, hh * HEAD_DIM:(hh + 1) * HEAD_DIM]
        km_ref[0, i] = jnp.mean(k[r0:r0 + MOBA_BLOCK], axis=0, keepdims=True)


def _prompt_inproj(x, g, w, qg, kg, head_ones, page):
    b, t, d = x.shape
    tm = TILE_M
    nt = t // tm
    nb = t // MOBA_BLOCK
    bpt = tm // MOBA_BLOCK
    ppt = tm // page
    n_out = w.shape[1]
    tok = lambda bi, ti: (bi, ti, 0)
    blk5 = lambda bi, ti: (bi, 0, ti, 0, 0)
    pages = lambda bi, ti: (bi, ti, 0, 0, 0)
    return pl.pallas_call(
        functools.partial(_prompt_inproj_body, tm=tm, page=page),
        grid=(b, nt),
        in_specs=[
            pl.BlockSpec((1, tm, d), tok),
            _const_spec((1, d)),
            _const_spec((d, n_out)),
            _const_spec((1, D_B)),
            _const_spec((1, D_B)),
            _const_spec((D_B, D_B)),
        ],
        out_specs=[
            pl.BlockSpec((1, tm, D_A), tok),
            pl.BlockSpec((1, ppt, N_HEADS, HEAD_DIM, page), pages),
            pl.BlockSpec((1, ppt, N_HEADS, HEAD_DIM, page), pages),
            pl.BlockSpec((1, N_HEADS, bpt, HEAD_DIM, MOBA_BLOCK), blk5),
            pl.BlockSpec((1, N_HEADS, bpt, MOBA_BLOCK, HEAD_DIM), blk5),
            pl.BlockSpec((1, N_HEADS, bpt, VT_ROWS, MOBA_BLOCK), blk5),
            pl.BlockSpec((1, bpt, 1, D_B), lambda bi, ti: (bi, ti, 0, 0)),
        ],
        out_shape=[
            jax.ShapeDtypeStruct((b, t, D_A), F32),
            jax.ShapeDtypeStruct((b, t // page, N_HEADS, HEAD_DIM, page), F32),
            jax.ShapeDtypeStruct((b, t // page, N_HEADS, HEAD_DIM, page), F32),
            jax.ShapeDtypeStruct((b, N_HEADS, nb, HEAD_DIM, MOBA_BLOCK), BF16),
            jax.ShapeDtypeStruct((b, N_HEADS, nb, MOBA_BLOCK, HEAD_DIM), BF16),
            jax.ShapeDtypeStruct((b, N_HEADS, nb, VT_ROWS, MOBA_BLOCK), BF16),
            jax.ShapeDtypeStruct((b, nb, 1, D_B), F32),
        ],
        compiler_params=_cparams(2),
        name="prompt_inproj",
    )(x, g, w, qg, kg, head_ones)


def _t5_bias_scalar_table(dist, rb_ref, h):
    out = jnp.full(dist.shape, rb_ref[0, h] * LOG2E, F32)
    for i, th in enumerate(T5_THRESH):
        out = jnp.where(dist >= th, rb_ref[i + 1, h] * LOG2E, out)
    return out


def _moba_prompt_body(rb_ref, qt_ref, kh_ref, vt_ref, km_ref, ot_ref,
                      bdiag_ref, bsub_ref, selb_ref, m_ref, alpha_ref, acc_ref, s_ref, p_ref, *, nb, hg):
    h0 = pl.program_id(1) * hg
    n = pl.program_id(2)
    blk = MOBA_BLOCK

    @pl.when(n == 0)
    def _():
        ki = lax.broadcasted_iota(jnp.int32, (blk, blk), 0)
        qi = lax.broadcasted_iota(jnp.int32, (blk, blk), 1)
        d0 = qi - ki
        for g in range(hg):
            bdiag_ref[g] = jnp.where(d0 >= 0, _t5_bias_scalar_table(jnp.maximum(d0, 0), rb_ref, h0 + g),
                                     -jnp.inf)
            bsub_ref[g] = _t5_bias_scalar_table(d0 + blk, rb_ref, h0 + g)

    bi = lax.broadcasted_iota(jnp.int32, (nb, blk), 0)
    qts = [qt_ref[0, g, 0] for g in range(hg)]

    for g in range(hg):
        km = km_ref[0, g]
        km_hi = km.astype(BF16)
        km_lo = (km - km_hi.astype(F32)).astype(BF16)
        gate = (jnp.dot(km_hi, qts[g], preferred_element_type=F32)
                + jnp.dot(km_lo, qts[g], preferred_element_type=F32))
        avail = jnp.where(bi < n, 1.0, 0.0)
        far_bias = rb_ref[NUM_BUCKETS - 1, h0 + g] * LOG2E
        selb = jnp.full((nb, blk), -jnp.inf, F32)
        for _ in range(MOBA_TOPK):
            gm = jnp.where(avail > 0.0, gate, -jnp.inf)
            top = jnp.max(gm, axis=0, keepdims=True)
            first = jnp.where(avail > 0.0, jnp.where(gm == top, bi, nb), nb)
            pick = bi == jnp.min(first, axis=0, keepdims=True)
            selb = jnp.where(pick, jnp.where(bi == n - 1, 0.0, far_bias), selb)
            avail = jnp.where(pick, 0.0, avail)
        selb_ref[g] = selb

    n_chunks = blk // SOFTMAX_ROWS

    def logits(g, j, slot, bias):
        s = jnp.dot(kh_ref[0, g, j], qts[g], preferred_element_type=F32)
        s_ref[slot, g] = s if bias is None else s + bias

    def chunk(g, slot, c):
        return s_ref[slot, g, c * SOFTMAX_ROWS:(c + 1) * SOFTMAX_ROWS, :].reshape(
            SOFTMAX_ROWS // SUBLANES, SUBLANES, blk)

    def softmax(g, slot, row):
        m8 = jnp.max(chunk(g, slot, 0), axis=0)
        for c in range(1, n_chunks):
            m8 = jnp.maximum(m8, jnp.max(chunk(g, slot, c), axis=0))
        m_blk = jnp.max(m8, axis=0, keepdims=True)
        if row is None:
            mn = m_blk
            shift = mn
        else:
            m_old = m_ref[g]
            mn = jnp.maximum(m_old, m_blk + row)
            alpha_ref[slot, g] = jnp.exp2(m_old - mn)
            shift = mn - row
        for c in range(n_chunks):
            p = jnp.exp2(chunk(g, slot, c) - shift)
            p_ref[slot, g, c * SOFTMAX_ROWS:(c + 1) * SOFTMAX_ROWS, :] = (
                p.reshape(SOFTMAX_ROWS, blk).astype(BF16))
        m_ref[g] = mn

    def weighted_values(g, j, slot, first):
        pv = jnp.dot(vt_ref[0, g, j], p_ref[slot, g], preferred_element_type=F32)
        acc_ref[g] = pv if first else alpha_ref[slot, g] * acc_ref[g] + pv

    for g in range(hg):
        logits(g, n, 0, bdiag_ref[g])
    for g in range(hg):
        logits(g, jnp.maximum(n - 1, 0), 1, bsub_ref[g])
    for g in range(hg):
        softmax(g, 0, None)

    @pl.when(n == 0)
    def _():
        for g in range(hg):
            weighted_values(g, n, 0, True)

    @pl.when(n >= 1)
    def _():
        for g in range(hg):
            logits(g, 0, 0, None)
        for g in range(hg):
            weighted_values(g, n, 0, True)
        for g in range(hg):
            softmax(g, 1, selb_ref[g, pl.ds(n - 1, 1), :])

        def trip(j, slot):
            nxt = jnp.minimum(j + 1, n - 2)
            prev = jnp.where(j == 0, n - 1, j - 1)
            for g in range(hg):
                logits(g, nxt, 1 - slot, None)
            for g in range(hg):
                weighted_values(g, prev, 1 - slot, False)
            for g in range(hg):
                softmax(g, slot, selb_ref[g, pl.ds(j, 1), :])

        def far(i, carry):
            for r in range(FAR_UNROLL):
                trip(FAR_UNROLL * i + r, r & 1)
            return carry

        n_far = n - 1
        lax.fori_loop(0, n_far // FAR_UNROLL, far, 0)
        done = (n_far // FAR_UNROLL) * FAR_UNROLL
        piece = FAR_UNROLL // 2
        while piece >= 1:
            @pl.when(((n_far - done) & piece) != 0)
            def _(piece=piece, done=done):
                for r in range(piece):
                    trip(done + r, r & 1)
            done = done + ((n_far - done) & piece)
            piece //= 2

        last = jnp.where(n >= 2, n - 2, n - 1)
        for g in range(hg):
            weighted_values(g, last, n & 1, False)

    for g in range(hg):
        ot_ref[0, g * HEAD_DIM:(g + 1) * HEAD_DIM, :] = (
            acc_ref[g, 0:HEAD_DIM, :] / acc_ref[g, HEAD_DIM:HEAD_DIM + 1, :])


def _moba_prompt(rel_bias, qt, kh, vt, kmh):
    b, nh, nb, hd, blk = qt.shape
    t = nb * blk
    hg = HEAD_GROUP
    grp = lambda bi, hi, ni: (bi, hi, 0, 0, 0)
    return pl.pallas_call(
        functools.partial(_moba_prompt_body, nb=nb, hg=hg),
        grid=(b, nh // hg, nb),
        in_specs=[
            pl.BlockSpec(memory_space=pltpu.SMEM),
            pl.BlockSpec((1, hg, 1, hd, blk), lambda bi, hi, ni: (bi, hi, ni, 0, 0)),
            pl.BlockSpec((1, hg, nb, blk, hd), grp, pipeline_mode=pl.Buffered(1)),
            pl.BlockSpec((1, hg, nb, VT_ROWS, blk), grp, pipeline_mode=pl.Buffered(1)),
            pl.BlockSpec((1, hg, nb, hd), lambda bi, hi, ni: (bi, hi, 0, 0)),
        ],
        out_specs=pl.BlockSpec((1, hg * hd, blk), lambda bi, hi, ni: (bi, hi, ni)),
        out_shape=jax.ShapeDtypeStruct((b, nh * hd, t), F32),
        scratch_shapes=[
            pltpu.VMEM((hg, blk, blk), F32),
            pltpu.VMEM((hg, blk, blk), F32),
            pltpu.VMEM((hg, nb, blk), F32),
            pltpu.VMEM((hg, 1, blk), F32),
            pltpu.VMEM((2, hg, 1, blk), F32),
            pltpu.VMEM((hg, VT_ROWS, blk), F32),
            pltpu.VMEM((2, hg, blk, blk), F32),
            pltpu.VMEM((2, hg, blk, blk), BF16),
        ],
        compiler_params=_cparams(3),
        name="prompt_moba",
    )(rel_bias, qt, kh, vt, kmh)


def _prompt_mix_out_body(a_ref, halo_ref, cw_ref, cb_ref, lg_ref, lb_ref, ot_ref, x_ref, w_ref,
                         y_ref, sh_ref, conv_ref, wb_ref, *, tm):
    t = pl.program_id(1)
    rows = tm + HALO_A

    @pl.when((pl.program_id(0) == 0) & (t == 0))
    def _():
        for j in range(CONV_A):
            wb_ref[j] = jnp.broadcast_to(cw_ref[j:j + 1, :], (SUBLANES, D_A))

    sh_ref[0, 0:HALO_A] = jnp.where(t == 0, 0.0, halo_ref[0])
    sh_ref[0, HALO_A:rows] = a_ref[0]
    for s in range(1, SUBLANES):
        sh_ref[s, 0:rows - SUBLANES] = sh_ref[0, s:s + rows - SUBLANES]

    groups = CONV_ROWS // SUBLANES
    first_off = HALO_A - (CONV_A - 1)

    def chunk(c, carry):
        r0 = pl.multiple_of(c * CONV_ROWS, CONV_ROWS)
        for l0 in range(0, D_A, CONV_LANES):
            lanes = slice(l0, l0 + CONV_LANES)
            acc = [jnp.zeros((SUBLANES, CONV_LANES), F32) for _ in range(groups)]
            for s in range(SUBLANES):
                offs = [o for o in range(first_off, first_off + CONV_A) if o % SUBLANES == s]
                qs = [o // SUBLANES for o in offs]
                slab = {gq: sh_ref[s, pl.ds(r0 + gq * SUBLANES, SUBLANES), lanes]
                        for gq in range(min(qs), max(qs) + groups)}
                for o, q in zip(offs, qs):
                    w = wb_ref[o - first_off, :, lanes]
                    for g in range(groups):
                        acc[g] = acc[g] + w * slab[g + q]
            for g in range(groups):
                conv_ref[pl.ds(r0 + g * SUBLANES, SUBLANES), lanes] = acc[g]
        return carry

    lax.fori_loop(0, tm // CONV_ROWS, chunk, 0)

    ao = _silu(_layernorm(conv_ref[...] + cb_ref[...], lg_ref[...], lb_ref[...])).astype(BF16)
    o = ot_ref[0].T.astype(BF16)
    y = (jnp.dot(ao, w_ref[0:D_A, :], preferred_element_type=F32)
         + jnp.dot(o, w_ref[D_A:D_A + D_B, :], preferred_element_type=F32))
    y_ref[0] = x_ref[0] + y


def _prompt_mix_out(a, cw, cb, lg, lb, ot, x, w):
    b, t, d = x.shape
    tm = TILE_M
    nt = t // tm
    hpt = tm // HALO_A
    tok = lambda bi, ti: (bi, ti, 0)
    return pl.pallas_call(
        functools.partial(_prompt_mix_out_body, tm=tm),
        grid=(b, nt),
        in_specs=[
            pl.BlockSpec((1, tm, D_A), tok),
            pl.BlockSpec((1, HALO_A, D_A), lambda bi, ti: (bi, jnp.maximum(ti * hpt - 1, 0), 0)),
            _const_spec((CONV_A, D_A)),
            _const_spec((1, D_A)),
            _const_spec((1, D_A)),
            _const_spec((1, D_A)),
            pl.BlockSpec((1, D_B, tm), lambda bi, ti: (bi, 0, ti)),
            pl.BlockSpec((1, tm, d), tok),
            _const_spec((D_A + D_B, d)),
        ],
        out_specs=pl.BlockSpec((1, tm, d), tok),
        out_shape=jax.ShapeDtypeStruct((b, t, d), F32),
        scratch_shapes=[
            pltpu.VMEM((SUBLANES, tm + HALO_A, D_A), F32),
            pltpu.VMEM((tm, D_A), F32),
            pltpu.VMEM((CONV_A, SUBLANES, D_A), F32),
        ],
        compiler_params=_cparams(2),
        name="prompt_mix_out",
    )(a, a, cw, cb, lg, lb, ot, x, w)


def _conv3_chunk(u, ext_ref, carry_ref, cw_ref, c0, tm):
    ext_ref[0:SUBLANES] = carry_ref[:, c0:c0 + CH]
    ext_ref[SUBLANES:SUBLANES + tm] = u
    carry_ref[:, c0:c0 + CH] = u[tm - SUBLANES:tm]
    return (cw_ref[0:1, c0:c0 + CH] * ext_ref[SUBLANES - 2:SUBLANES - 2 + tm]
            + cw_ref[1:2, c0:c0 + CH] * ext_ref[SUBLANES - 1:SUBLANES - 1 + tm]
            + cw_ref[2:3, c0:c0 + CH] * u)


def _prompt_ffn_body(x_ref, g_ref, wu_ref, cw_ref, cb_ref, wd_ref, y_ref, st_ref,
                     carry_ref, extg_ref, extu_ref, act_ref, *, tm):
    @pl.when(pl.program_id(1) == 0)
    def _():
        carry_ref[...] = jnp.zeros_like(carry_ref)

    x = x_ref[0]
    h = _rmsnorm(x, g_ref[...]).astype(BF16)
    for c in range(D_FF // CH):
        cg = c * CH
        cu = D_FF + c * CH
        ug = jnp.dot(h, wu_ref[:, cg:cg + CH], preferred_element_type=F32)
        uu = jnp.dot(h, wu_ref[:, cu:cu + CH], preferred_element_type=F32)
        gg = _conv3_chunk(ug, extg_ref, carry_ref, cw_ref, cg, tm) + cb_ref[:, cg:cg + CH]
        gu = _conv3_chunk(uu, extu_ref, carry_ref, cw_ref, cu, tm) + cb_ref[:, cu:cu + CH]
        act_ref[:, cg:cg + CH] = (_silu(gg) * gu).astype(BF16)
    st_ref[0] = carry_ref[...]
    y_ref[0] = x + jnp.dot(act_ref[...], wd_ref[...], preferred_element_type=F32)


def _prompt_ffn(x, g, wu, cw, cb, wd):
    b, t, d = x.shape
    tm = TILE_M
    nt = t // tm
    tok = lambda bi, ti: (bi, ti, 0)
    return pl.pallas_call(
        functools.partial(_prompt_ffn_body, tm=tm),
        grid=(b, nt),
        in_specs=[
            pl.BlockSpec((1, tm, d), tok),
            _const_spec((1, d)),
            _const_spec((d, 2 * D_FF)),
            _const_spec((CONV_F, 2 * D_FF)),
            _const_spec((1, 2 * D_FF)),
            _const_spec((D_FF, d)),
        ],
        out_specs=[
            pl.BlockSpec((1, tm, d), tok),
            pl.BlockSpec((1, SUBLANES, 2 * D_FF), lambda bi, ti: (bi, 0, 0)),
        ],
        out_shape=[
            jax.ShapeDtypeStruct((b, t, d), F32),
            jax.ShapeDtypeStruct((b, SUBLANES, 2 * D_FF), F32),
        ],
        scratch_shapes=[
            pltpu.VMEM((SUBLANES, 2 * D_FF), F32),
            pltpu.VMEM((tm + SUBLANES, CH), F32),
            pltpu.VMEM((tm + SUBLANES, CH), F32),
            pltpu.VMEM((tm, D_FF), BF16),
        ],
        compiler_params=_cparams(2),
        name="prompt_ffn",
    )(x, g, wu, cw, cb, wd)


def _prompt_mixc_body(x_ref, g_ref, wi_ref, cw_ref, wo_ref, y_ref, st_ref,
                      carry_ref, ext_ref, z_ref, *, tm):
    @pl.when(pl.program_id(1) == 0)
    def _():
        carry_ref[...] = jnp.zeros_like(carry_ref)

    x = x_ref[0]
    h = _rmsnorm(x, g_ref[...]).astype(BF16)
    for c in range(D_C // CH):
        c0 = c * CH
        gb = jnp.dot(h, wi_ref[:, c0:c0 + CH], preferred_element_type=F32)
        gc = jnp.dot(h, wi_ref[:, D_C + c0:D_C + c0 + CH], preferred_element_type=F32)
        u = jnp.dot(h, wi_ref[:, 2 * D_C + c0:2 * D_C + c0 + CH], preferred_element_type=F32)
        conv = _conv3_chunk(gc * u, ext_ref, carry_ref, cw_ref, c0, tm)
        z_ref[:, c0:c0 + CH] = (gb * conv).astype(BF16)
    st_ref[0] = carry_ref[...]
    y_ref[0] = x + jnp.dot(z_ref[...], wo_ref[...], preferred_element_type=F32)


def _prompt_mixc(x, g, wi, cw, wo):
    b, t, d = x.shape
    tm = TILE_M
    nt = t // tm
    tok = lambda bi, ti: (bi, ti, 0)
    return pl.pallas_call(
        functools.partial(_prompt_mixc_body, tm=tm),
        grid=(b, nt),
        in_specs=[
            pl.BlockSpec((1, tm, d), tok),
            _const_spec((1, d)),
            _const_spec((d, 3 * D_C)),
            _const_spec((CONV_C, D_C)),
            _const_spec((D_C, d)),
        ],
        out_specs=[
            pl.BlockSpec((1, tm, d), tok),
            pl.BlockSpec((1, SUBLANES, D_C), lambda bi, ti: (bi, 0, 0)),
        ],
        out_shape=[
            jax.ShapeDtypeStruct((b, t, d), F32),
            jax.ShapeDtypeStruct((b, SUBLANES, D_C), F32),
        ],
        scratch_shapes=[
            pltpu.VMEM((SUBLANES, D_C), F32),
            pltpu.VMEM((tm + SUBLANES, CH), F32),
            pltpu.VMEM((tm, D_C), BF16),
        ],
        compiler_params=_cparams(2),
        name="prompt_mixc",
    )(x, g, wi, cw, wo)


def _sample_even_body(x_ref, g_ref, w_ref, qg_ref, kg_ref, ho_ref, sa_ref, cw_ref, cb_ref, lg_ref, lb_ref,
                      ao_ref, san_ref, q_ref, k_ref, v_ref):
    a, q, k, v = _in_proj_even(x_ref[...], g_ref[...], w_ref[...], qg_ref[...], kg_ref[...], ho_ref[...])
    q_ref[...] = q
    k_ref[...] = k
    v_ref[...] = v
    hist = CONV_A - 1
    conv = cb_ref[...] + cw_ref[hist:hist + 1, :] * a
    for j in range(hist):
        conv = conv + cw_ref[j:j + 1, :] * sa_ref[j]
    ao_ref[...] = _silu(_layernorm(conv, lg_ref[...], lb_ref[...])).astype(BF16)
    san_ref[0:hist - 1] = sa_ref[1:hist]
    san_ref[hist - 1] = a


def _sample_even(x, g, w, qg, kg, head_ones, sa, cw, cb, lg, lb):
    m = x.shape[0]
    return pl.pallas_call(
        _sample_even_body,
        out_shape=[
            jax.ShapeDtypeStruct((m, D_A), BF16),
            jax.ShapeDtypeStruct(sa.shape, F32),
            jax.ShapeDtypeStruct((m, D_B), F32),
            jax.ShapeDtypeStruct((m, D_B), F32),
            jax.ShapeDtypeStruct((m, D_B), F32),
        ],
        compiler_params=pltpu.CompilerParams(vmem_limit_bytes=VMEM_LIMIT),
        name="sample_even",
    )(x, g, w, qg, kg, head_ones, sa, cw, cb, lg, lb)


def _moba_sample_keys_body(pt_ref, q_ref, qdh_ref, kn_ref, rbt_ref, *refs, n_pages, page):
    del pt_ref
    k_pages = refs[:n_pages]
    pc_ref, pn_ref, l_ref, sel_ref, bias_ref, qb_ref, s_ref = refs[n_pages:]
    past = n_pages * page
    n_past_blocks = past // MOBA_BLOCK
    n_sel = min(MOBA_TOPK, n_past_blocks)

    @pl.when(pl.program_id(0) == 0)
    def _():
        pos = lax.broadcasted_iota(jnp.int32, (N_HEADS, past), 1)
        dist = past - pos
        out = jnp.zeros((N_HEADS, past), F32) + rbt_ref[:, 0:1]
        for i, th in enumerate(T5_THRESH):
            out = jnp.where(dist >= th, rbt_ref[:, i + 1:i + 2], out)
        bias_ref[...] = out * LOG2E

    head_of_lane = lax.broadcasted_iota(jnp.int32, (N_HEADS, D_B), 1) // HEAD_DIM
    hmask = head_of_lane == lax.broadcasted_iota(jnp.int32, (N_HEADS, D_B), 0)
    qbd = jnp.where(hmask, q_ref[0], 0.0)

    qdh = qdh_ref[0]
    for h in range(N_HEADS):
        qb_ref[h] = jnp.broadcast_to(qdh[:, h:h + 1], (HEAD_DIM, page))

    for pg in range(n_pages):
        rows = [jnp.sum(k_pages[pg][0, h] * qb_ref[h], axis=0, keepdims=True) for h in range(N_HEADS)]
        s_ref[:, pg * page:(pg + 1) * page] = jnp.concatenate(rows, axis=0)

    gates = [jnp.sum(s_ref[:, j * MOBA_BLOCK:(j + 1) * MOBA_BLOCK], axis=-1, keepdims=True)
             for j in range(n_past_blocks)]
    ranks = []
    pieces = []
    for j in range(n_past_blocks):
        rank = jnp.zeros((N_HEADS, 1), jnp.int32)
        for i in range(n_past_blocks):
            if i == j:
                continue
            ahead = (gates[i] >= gates[j]) if i < j else (gates[i] > gates[j])
            rank = rank + jnp.where(ahead, 1, 0)
        ranks.append(rank)
        lanes = slice(j * MOBA_BLOCK, (j + 1) * MOBA_BLOCK)
        pieces.append(s_ref[:, lanes] + bias_ref[:, lanes] + jnp.where(rank < n_sel, 0.0, -jnp.inf))
    s = jnp.concatenate(pieces, axis=-1)

    s_new = jnp.sum(qbd * kn_ref[0], axis=-1, keepdims=True) + rbt_ref[:, 0:1] * LOG2E
    m = jnp.maximum(jnp.max(s, axis=-1, keepdims=True), s_new)
    p = jnp.exp2(s - m)
    p_new = jnp.exp2(s_new - m)
    pn_ref[0] = jnp.broadcast_to(p_new, (N_HEADS, 128))
    l_ref[0] = jnp.broadcast_to(jnp.sum(p, axis=-1, keepdims=True) + p_new, (N_HEADS, 128))

    lane = lax.broadcasted_iota(jnp.int32, (N_HEADS, 128), 1)
    sel = jnp.zeros((N_HEADS, 128), jnp.int32)
    for r in range(n_sel):
        picked = jnp.zeros((N_HEADS, MOBA_BLOCK), F32)
        block_id = jnp.zeros((N_HEADS, 1), jnp.int32)
        for j in range(n_past_blocks):
            picked = picked + jnp.where(ranks[j] == r, p[:, j * MOBA_BLOCK:(j + 1) * MOBA_BLOCK], 0.0)
            block_id = block_id + jnp.where(ranks[j] == r, j, 0)
        pc_ref[0, :, r * MOBA_BLOCK:(r + 1) * MOBA_BLOCK] = picked
        sel = jnp.where(lane == r, block_id, sel)
    sel_ref[0] = sel


def _moba_sample_values_body(pt_ref, sel_ref, pc_ref, pn_ref, l_ref, vn_ref, v_hbm, o_ref, buf_ref, sem_ref,
                             *, n_sel, ppb, page):
    b = pl.program_id(0)
    n_tiles = n_sel * ppb

    def tile_copy(seq, slot, h, i):
        block = sel_ref[seq * (N_HEADS * n_sel) + h * n_sel + i // ppb]
        pool_page = pt_ref[seq, block * ppb + i % ppb]
        return pltpu.make_async_copy(v_hbm.at[pool_page, h], buf_ref.at[slot, h * n_tiles + i], sem_ref.at[slot])

    def for_all_tiles(seq, slot, act):
        for h in range(N_HEADS):
            for i in range(n_tiles):
                act(tile_copy(seq, slot, h, i))

    @pl.when(b == 0)
    def _():
        for_all_tiles(0, 0, lambda cp: cp.start())

    @pl.when(b + 1 < pl.num_programs(0))
    def _():
        for_all_tiles(b + 1, (b + 1) % 2, lambda cp: cp.start())

    slot = b % 2
    for_all_tiles(b, slot, lambda cp: cp.wait())

    head_of_lane = lax.broadcasted_iota(jnp.int32, (N_HEADS, D_B), 1) // HEAD_DIM
    hmask = head_of_lane == lax.broadcasted_iota(jnp.int32, (N_HEADS, D_B), 0)
    ones = jnp.ones((SUBLANES, page), BF16)
    lane_sum = lambda x: lax.dot_general(ones, x, (((1,), (1,)), ((), ())), preferred_element_type=F32)
    outs = []
    for h in range(N_HEADS):
        acc = jnp.zeros((HEAD_DIM, page), F32)
        for i in range(n_tiles):
            acc = acc + buf_ref[slot, h * n_tiles + i] * pc_ref[0, h:h + 1, i * page:(i + 1) * page]
        hi = acc.astype(BF16)
        lo = (acc - hi.astype(F32)).astype(BF16)
        outs.append((lane_sum(hi) + lane_sum(lo))[0:1, :])
    o_past = jnp.concatenate(outs, axis=-1)
    spread = lambda col: jnp.sum(jnp.where(hmask, col, 0.0), axis=0, keepdims=True)
    o_ref[0] = (o_past + spread(pn_ref[0, :, 0:1]) * vn_ref[0]) / spread(l_ref[0, :, 0:1])


def _moba_sample(page_table, q, k_new, v_new, rel_bias_t, cache_kt, cache_vt):
    m, n_pages = page_table.shape
    page = cache_kt.shape[-1]
    past = n_pages * page
    n_sel = min(MOBA_TOPK, past // MOBA_BLOCK)
    ppb = MOBA_BLOCK // page
    row = lambda bi, *_: (bi, 0, 0)

    keys_spec = pltpu.PrefetchScalarGridSpec(
        num_scalar_prefetch=1,
        grid=(m,),
        in_specs=[
            pl.BlockSpec((1, 1, D_B), row),
            pl.BlockSpec((1, HEAD_DIM, N_HEADS), row),
            pl.BlockSpec((1, 1, D_B), row),
            pl.BlockSpec((N_HEADS, NUM_BUCKETS), lambda bi, pt: (0, 0)),
        ] + [pl.BlockSpec((1, N_HEADS, HEAD_DIM, page), lambda bi, pt, pg=pg: (pt[bi, pg], 0, 0, 0))
             for pg in range(n_pages)],
        out_specs=[
            pl.BlockSpec((1, N_HEADS, n_sel * MOBA_BLOCK), row),
            pl.BlockSpec((1, N_HEADS, 128), row),
            pl.BlockSpec((1, N_HEADS, 128), row),
            pl.BlockSpec((1, N_HEADS, 128), row),
        ],
        scratch_shapes=[
            pltpu.VMEM((N_HEADS, past), F32),
            pltpu.VMEM((N_HEADS, HEAD_DIM, page), F32),
            pltpu.VMEM((N_HEADS, past), F32),
        ],
    )
    q_dh = q.reshape(m, N_HEADS, HEAD_DIM).transpose(0, 2, 1)
    pc, pn, l, sel = pl.pallas_call(
        functools.partial(_moba_sample_keys_body, n_pages=n_pages, page=page),
        grid_spec=keys_spec,
        out_shape=[
            jax.ShapeDtypeStruct((m, N_HEADS, n_sel * MOBA_BLOCK), F32),
            jax.ShapeDtypeStruct((m, N_HEADS, 128), F32),
            jax.ShapeDtypeStruct((m, N_HEADS, 128), F32),
            jax.ShapeDtypeStruct((m, N_HEADS, 128), jnp.int32),
        ],
        compiler_params=_cparams(1),
        name="sample_moba_keys",
    )(page_table, q.reshape(m, 1, D_B), q_dh, k_new.reshape(m, 1, D_B), rel_bias_t, *([cache_kt] * n_pages))

    sel_flat = sel[:, :, :n_sel].reshape(m * N_HEADS * n_sel)
    values_spec = pltpu.PrefetchScalarGridSpec(
        num_scalar_prefetch=2,
        grid=(m,),
        in_specs=[
            pl.BlockSpec((1, N_HEADS, n_sel * MOBA_BLOCK), row),
            pl.BlockSpec((1, N_HEADS, 128), row),
            pl.BlockSpec((1, N_HEADS, 128), row),
            pl.BlockSpec((1, 1, D_B), row),
            pl.BlockSpec(memory_space=pl.ANY),
        ],
        out_specs=pl.BlockSpec((1, 1, D_B), row),
        scratch_shapes=[
            pltpu.VMEM((2, N_HEADS * n_sel * ppb, HEAD_DIM, page), F32),
            pltpu.SemaphoreType.DMA((2,)),
        ],
    )
    return pl.pallas_call(
        functools.partial(_moba_sample_values_body, n_sel=n_sel, ppb=ppb, page=page),
        grid_spec=values_spec,
        out_shape=jax.ShapeDtypeStruct((m, 1, D_B), F32),
        compiler_params=_cparams(1),
        name="sample_moba_values",
    )(page_table, sel_flat, pc, pn, l, v_new.reshape(m, 1, D_B), cache_vt)


def _sample_ffn(x, st_ref, g_ref, wu_ref, cw_ref, cb_ref, wd_ref, stn_ref):
    h = _rmsnorm(x, g_ref[...]).astype(BF16)
    up = jnp.dot(h, wu_ref[...], preferred_element_type=F32)
    older, newer = st_ref[:, 0, :], st_ref[:, 1, :]
    conv = cw_ref[0:1, :] * older + cw_ref[1:2, :] * newer + cw_ref[2:3, :] * up + cb_ref[...]
    act = (_silu(conv[:, :D_FF]) * conv[:, D_FF:]).astype(BF16)
    stn_ref[:, 0, :] = newer
    stn_ref[:, 1, :] = up
    return x + jnp.dot(act, wd_ref[...], preferred_element_type=F32)


def _sample_out_ffn_body(ao_ref, o_ref, x_ref, wo_ref, st_ref, g_ref, wu_ref, cw_ref, cb_ref, wd_ref,
                         y_ref, stn_ref):
    x1 = (x_ref[...]
          + jnp.dot(ao_ref[...], wo_ref[0:D_A, :], preferred_element_type=F32)
          + jnp.dot(o_ref[...].astype(BF16), wo_ref[D_A:D_A + D_B, :], preferred_element_type=F32))
    y_ref[...] = _sample_ffn(x1, st_ref, g_ref, wu_ref, cw_ref, cb_ref, wd_ref, stn_ref)


def _sample_out_ffn(ao, o, x, wo, st, g, wu, cw, cb, wd):
    return pl.pallas_call(
        _sample_out_ffn_body,
        out_shape=[jax.ShapeDtypeStruct(x.shape, F32), jax.ShapeDtypeStruct(st.shape, F32)],
        compiler_params=pltpu.CompilerParams(vmem_limit_bytes=VMEM_LIMIT),
        name="sample_out_ffn",
    )(ao, o, x, wo, st, g, wu, cw, cb, wd)


def _sample_odd_body(x_ref, gm_ref, wi_ref, ccw_ref, sc_ref, wo_ref, st_ref, g_ref, wu_ref, cw_ref, cb_ref,
                     wd_ref, y_ref, scn_ref, stn_ref):
    x = x_ref[...]
    h = _rmsnorm(x, gm_ref[...]).astype(BF16)
    z = jnp.dot(h, wi_ref[...], preferred_element_type=F32)
    gcu = z[:, D_C:2 * D_C] * z[:, 2 * D_C:]
    older, newer = sc_ref[:, 0, :], sc_ref[:, 1, :]
    conv = ccw_ref[0:1, :] * older + ccw_ref[1:2, :] * newer + ccw_ref[2:3, :] * gcu
    scn_ref[:, 0, :] = newer
    scn_ref[:, 1, :] = gcu
    x1 = x + jnp.dot((z[:, :D_C] * conv).astype(BF16), wo_ref[...], preferred_element_type=F32)
    y_ref[...] = _sample_ffn(x1, st_ref, g_ref, wu_ref, cw_ref, cb_ref, wd_ref, stn_ref)


def _sample_odd(x, gm, wi, ccw, sc, wo, st, g, wu, cw, cb, wd):
    return pl.pallas_call(
        _sample_odd_body,
        out_shape=[jax.ShapeDtypeStruct(x.shape, F32), jax.ShapeDtypeStruct(sc.shape, F32),
                   jax.ShapeDtypeStruct(st.shape, F32)],
        compiler_params=pltpu.CompilerParams(vmem_limit_bytes=VMEM_LIMIT),
        name="sample_odd",
    )(x, gm, wi, ccw, sc, wo, st, g, wu, cw, cb, wd)


def kernel(x_prompt, x_sample, cache_k, cache_v, state_conv_a, state_conv_c, state_ffn, page_table, rel_bias,
           norm_mix_e, w_in_e, conv_a_w, conv_a_b, ln_a_g, ln_a_b, q_norm_g, k_norm_g, w_out_e,
           norm_mix_o, w_in_o, conv_c_w, w_out_o, norm_ffn, w_up, conv_f_w, conv_f_b, w_down):
    b, t, d = x_prompt.shape
    m = x_sample.shape[0]
    n_pool, page = cache_k.shape[1], cache_k.shape[2]
    n_pages = page_table.shape[1]
    assert norm_mix_e.shape[0] == 1 and norm_mix_o.shape[0] == 1 and norm_ffn.shape[0] == 2
    assert x_sample.shape[1] == 1 and t % TILE_M == 0 and t % page == 0
    assert (n_pages * page) % MOBA_BLOCK == 0 and MOBA_BLOCK % page == 0

    row = lambda v: v.reshape(1, -1)
    w_in_e_b = w_in_e[0].astype(BF16)
    w_out_e_b = w_out_e[0].astype(BF16)
    w_in_o_b = w_in_o[0].astype(BF16)
    w_out_o_b = w_out_o[0].astype(BF16)
    w_up_b = [w_up[i].astype(BF16) for i in range(2)]
    w_down_b = [w_down[i].astype(BF16) for i in range(2)]
    qg = row(jnp.tile(q_norm_g[0], N_HEADS))
    kg = row(jnp.tile(k_norm_g[0], N_HEADS))
    lane_head = jnp.arange(D_B, dtype=jnp.int32) // HEAD_DIM
    head_ones = (lane_head[:, None] == lane_head[None, :]).astype(BF16)
    g_e, g_o = row(norm_mix_e[0]), row(norm_mix_o[0])
    cab, lag, lab = row(conv_a_b[0]), row(ln_a_g[0]), row(ln_a_b[0])

    a_p, k_p, v_p, qt, kh, vt, km = _prompt_inproj(x_prompt, g_e, w_in_e_b, qg, kg, head_ones, page)
    nb = t // MOBA_BLOCK
    kmh = km.reshape(b, nb, N_HEADS, HEAD_DIM).transpose(0, 2, 1, 3)
    ot = _moba_prompt(rel_bias, qt, kh, vt, kmh)
    x1 = _prompt_mix_out(a_p, conv_a_w[0], cab, lag, lab, ot, x_prompt, w_out_e_b)
    x2, f0 = _prompt_ffn(x1, row(norm_ffn[0]), w_up_b[0], conv_f_w[0], row(conv_f_b[0]), w_down_b[0])
    x3, c_st = _prompt_mixc(x2, g_o, w_in_o_b, conv_c_w[0], w_out_o_b)
    y_prompt, f1 = _prompt_ffn(x3, row(norm_ffn[1]), w_up_b[1], conv_f_w[1], row(conv_f_b[1]), w_down_b[1])

    k_prompt = k_p.transpose(0, 1, 4, 2, 3)[None]
    v_prompt = v_p.transpose(0, 1, 4, 2, 3)[None]
    a_prompt = a_p[:, t - (CONV_A - 1):, :][None]
    c_prompt = c_st[:, SUBLANES - (CONV_C - 1):, :][None]
    f_prompt = jnp.stack([f0[:, SUBLANES - (CONV_F - 1):, :], f1[:, SUBLANES - (CONV_F - 1):, :]])

    xs = x_sample.reshape(m, d)
    sa = state_conv_a[0].transpose(1, 0, 2)
    sc = state_conv_c[0]
    sf = state_ffn
    ao_s, sa_new, q_s, k_s, v_s = _sample_even(xs, g_e, w_in_e_b, qg, kg, head_ones, sa, conv_a_w[0], cab, lag, lab)
    o_s = _moba_sample(page_table, q_s, k_s, v_s, rel_bias.T,
                       cache_k[0].transpose(0, 2, 3, 1), cache_v[0].transpose(0, 2, 3, 1))
    xs1, sf0 = _sample_out_ffn(ao_s, o_s.reshape(m, D_B), xs, w_out_e_b, sf[0], row(norm_ffn[0]), w_up_b[0],
                               conv_f_w[0], row(conv_f_b[0]), w_down_b[0])
    ys, sc_new, sf1 = _sample_odd(xs1, g_o, w_in_o_b, conv_c_w[0], sc, w_out_o_b, sf[1], row(norm_ffn[1]),
                                  w_up_b[1], conv_f_w[1], row(conv_f_b[1]), w_down_b[1])

    y_sample = ys.reshape(m, 1, d)
    k_sample = k_s.reshape(1, m, 1, N_HEADS, HEAD_DIM)
    v_sample = v_s.reshape(1, m, 1, N_HEADS, HEAD_DIM)
    a_sample = sa_new.transpose(1, 0, 2)[None]
    c_sample = sc_new[None]
    f_sample = jnp.stack([sf0, sf1])
    return (y_prompt, y_sample, k_prompt, v_prompt, a_prompt, c_prompt, f_prompt,
            k_sample, v_sample, a_sample, c_sample, f_sample)
```

```python
import functools
import math

import jax
import jax.numpy as jnp
from jax import lax
from jax.experimental import pallas as pl
from jax.experimental.pallas import tpu as pltpu

F32 = jnp.float32
BF16 = jnp.bfloat16

EPS = 1e-6
D_MODEL = 1024
D_A = 512
CONV_A = 31
N_HEADS = 8
HEAD_DIM = 64
D_B = N_HEADS * HEAD_DIM
MOBA_BLOCK = 256
MOBA_TOPK = 3
NUM_BUCKETS = 32
MAX_DISTANCE = 128
D_C = 1024
CONV_C = 3
D_FF = 2816
CONV_F = 3
LOG2E = math.log2(math.e)
Q_SCALE = HEAD_DIM ** -0.5 * LOG2E
VT_ROWS = HEAD_DIM + 16

TILE_M = 512
CONV_ROWS = 64
CONV_LANES = 256
HALO_A = 32
HEAD_GROUP = 8
SOFTMAX_ROWS = 64
SEQ_PER_STEP = 2
FAR_UNROLL = 2
CH = 256
SUBLANES = 8
VMEM_LIMIT = 56 * 1024 * 1024


def _t5_thresholds():
    max_exact = NUM_BUCKETS // 2
    th = list(range(1, max_exact + 1))
    for k in range(1, NUM_BUCKETS - max_exact):
        th.append(math.ceil(max_exact * (MAX_DISTANCE / max_exact) ** (k / (NUM_BUCKETS - max_exact))))
    return tuple(th)


T5_THRESH = _t5_thresholds()


def _cparams(n_grid):
    return pltpu.CompilerParams(dimension_semantics=("arbitrary",) * n_grid, vmem_limit_bytes=VMEM_LIMIT)


def _const_spec(shape):
    nd = len(shape)
    return pl.BlockSpec(shape, lambda *_: (0,) * nd, pipeline_mode=pl.Buffered(1))


def _rmsnorm(x, g):
    return x * lax.rsqrt(jnp.mean(x * x, axis=-1, keepdims=True) + EPS) * g


def _split_dot(x, w_bf16):
    hi = x.astype(BF16)
    lo = (x - hi.astype(F32)).astype(BF16)
    return (jnp.dot(hi, w_bf16, preferred_element_type=F32)
            + jnp.dot(lo, w_bf16, preferred_element_type=F32))


def _head_rmsnorm(x, g, head_ones):
    ss = _split_dot(x * x, head_ones)
    return x * lax.rsqrt(ss * (1.0 / HEAD_DIM) + EPS) * g


def _silu(x):
    return x * jax.nn.sigmoid(x)


def _layernorm(x, g, b):
    mu = jnp.mean(x, axis=-1, keepdims=True)
    xc = x - mu
    var = jnp.mean(xc * xc, axis=-1, keepdims=True)
    return xc * lax.rsqrt(var + EPS) * g + b


def _in_proj_even(x, g, w, qg, kg, head_ones):
    h = _rmsnorm(x, g).astype(BF16)
    z = jnp.dot(h, w, preferred_element_type=F32)
    a = z[:, :D_A] * jax.nn.sigmoid(z[:, D_A:2 * D_A])
    q = _head_rmsnorm(z[:, 2 * D_A:2 * D_A + D_B], qg, head_ones) * Q_SCALE
    k = _head_rmsnorm(z[:, 2 * D_A + D_B:2 * D_A + 2 * D_B], kg, head_ones)
    v = z[:, 2 * D_A + 2 * D_B:]
    return a, q, k, v


def _prompt_inproj_body(x_ref, g_ref, w_ref, qg_ref, kg_ref, ho_ref,
                        a_ref, kp_ref, vp_ref, qt_ref, kh_ref, vt_ref, km_ref, *, tm, page):
    a, q, k, v = _in_proj_even(x_ref[0], g_ref[...], w_ref[...], qg_ref[...], kg_ref[...], ho_ref[...])
    a_ref[0] = a
    qt = q.T
    kt = k.T
    vt = v.T
    for pg in range(tm // page):
        kp_ref[0, pg] = kt[:, pg * page:(pg + 1) * page].reshape(N_HEADS, HEAD_DIM, page)
        vp_ref[0, pg] = vt[:, pg * page:(pg + 1) * page].reshape(N_HEADS, HEAD_DIM, page)
    kb = k.astype(BF16)
    pad_row = lax.broadcasted_iota(jnp.int32, (N_HEADS, VT_ROWS - HEAD_DIM, MOBA_BLOCK), 1)
    ones_rows = jnp.where(pad_row == 0, 1.0, 0.0).astype(BF16)
    for i in range(tm // MOBA_BLOCK):
        r0 = i * MOBA_BLOCK
        qt_ref[0, :, i] = qt[:, r0:r0 + MOBA_BLOCK].reshape(N_HEADS, HEAD_DIM, MOBA_BLOCK).astype(BF16)
        vt_ref[0, :, i, 0:HEAD_DIM, :] = (
            vt[:, r0:r0 + MOBA_BLOCK].reshape(N_HEADS, HEAD_DIM, MOBA_BLOCK).astype(BF16))
        vt_ref[0, :, i, HEAD_DIM:VT_ROWS, :] = ones_rows
        for hh in range(N_HEADS):
            kh_ref[0, hh, i] = kb[r0:r0 + MOBA_BLOCK, hh * HEAD_DIM:(hh + 1) * HEAD_DIM]
        km_ref[0, i] = jnp.mean(k[r0:r0 + MOBA_BLOCK], axis=0, keepdims=True)


def _prompt_inproj(x, g, w, qg, kg, head_ones, page):
    b, t, d = x.shape
    tm = TILE_M
    nt = t // tm
    nb = t // MOBA_BLOCK
    bpt = tm // MOBA_BLOCK
    ppt = tm // page
    n_out = w.shape[1]
    tok = lambda bi, ti: (bi, ti, 0)
    blk5 = lambda bi, ti: (bi, 0, ti, 0, 0)
    pages = lambda bi, ti: (bi, ti, 0, 0, 0)
    return pl.pallas_call(
        functools.partial(_prompt_inproj_body, tm=tm, page=page),
        grid=(b, nt),
        in_specs=[
            pl.BlockSpec((1, tm, d), tok),
            _const_spec((1, d)),
            _const_spec((d, n_out)),
            _const_spec((1, D_B)),
            _const_spec((1, D_B)),
            _const_spec((D_B, D_B)),
        ],
        out_specs=[
            pl.BlockSpec((1, tm, D_A), tok),
            pl.BlockSpec((1, ppt, N_HEADS, HEAD_DIM, page), pages),
            pl.BlockSpec((1, ppt, N_HEADS, HEAD_DIM, page), pages),
            pl.BlockSpec((1, N_HEADS, bpt, HEAD_DIM, MOBA_BLOCK), blk5),
            pl.BlockSpec((1, N_HEADS, bpt, MOBA_BLOCK, HEAD_DIM), blk5),
            pl.BlockSpec((1, N_HEADS, bpt, VT_ROWS, MOBA_BLOCK), blk5),
            pl.BlockSpec((1, bpt, 1, D_B), lambda bi, ti: (bi, ti, 0, 0)),
        ],
        out_shape=[
            jax.ShapeDtypeStruct((b, t, D_A), F32),
            jax.ShapeDtypeStruct((b, t // page, N_HEADS, HEAD_DIM, page), F32),
            jax.ShapeDtypeStruct((b, t // page, N_HEADS, HEAD_DIM, page), F32),
            jax.ShapeDtypeStruct((b, N_HEADS, nb, HEAD_DIM, MOBA_BLOCK), BF16),
            jax.ShapeDtypeStruct((b, N_HEADS, nb, MOBA_BLOCK, HEAD_DIM), BF16),
            jax.ShapeDtypeStruct((b, N_HEADS, nb, VT_ROWS, MOBA_BLOCK), BF16),
            jax.ShapeDtypeStruct((b, nb, 1, D_B), F32),
        ],
        compiler_params=_cparams(2),
        name="prompt_inproj",
    )(x, g, w, qg, kg, head_ones)


def _t5_bias_scalar_table(dist, rb_ref, h):
    out = jnp.full(dist.shape, rb_ref[0, h] * LOG2E, F32)
    for i, th in enumerate(T5_THRESH):
        out = jnp.where(dist >= th, rb_ref[i + 1, h] * LOG2E, out)
    return out


def _moba_prompt_body(rb_ref, qt_ref, kh_ref, vt_ref, km_ref, ot_ref,
                      bdiag_ref, bsub_ref, selb_ref, m_ref, alpha_ref, acc_ref, s_ref, p_ref, *, nb, hg):
    h0 = pl.program_id(1) * hg
    n = pl.program_id(2)
    blk = MOBA_BLOCK

    @pl.when(n == 0)
    def _():
        ki = lax.broadcasted_iota(jnp.int32, (blk, blk), 0)
        qi = lax.broadcasted_iota(jnp.int32, (blk, blk), 1)
        d0 = qi - ki
        for g in range(hg):
            bdiag_ref[g] = jnp.where(d0 >= 0, _t5_bias_scalar_table(jnp.maximum(d0, 0), rb_ref, h0 + g),
                                     -jnp.inf)
            bsub_ref[g] = _t5_bias_scalar_table(d0 + blk, rb_ref, h0 + g)

    bi = lax.broadcasted_iota(jnp.int32, (nb, blk), 0)
    qts = [qt_ref[0, g, 0] for g in range(hg)]

    def select(g):
        km = km_ref[0, g]
        km_hi = km.astype(BF16)
        km_lo = (km - km_hi.astype(F32)).astype(BF16)
        gate = (jnp.dot(km_hi, qts[g], preferred_element_type=F32)
                + jnp.dot(km_lo, qts[g], preferred_element_type=F32))
        avail = jnp.where(bi < n, 1.0, 0.0)
        far_bias = rb_ref[NUM_BUCKETS - 1, h0 + g] * LOG2E
        selb = jnp.full((nb, blk), -jnp.inf, F32)
        for _ in range(MOBA_TOPK):
            gm = jnp.where(avail > 0.0, gate, -jnp.inf)
            top = jnp.max(gm, axis=0, keepdims=True)
            first = jnp.where(avail > 0.0, jnp.where(gm == top, bi, nb), nb)
            pick = bi == jnp.min(first, axis=0, keepdims=True)
            selb = jnp.where(pick, jnp.where(bi == n - 1, 0.0, far_bias), selb)
            avail = jnp.where(pick, 0.0, avail)
        selb_ref[g] = selb

    n_chunks = blk // SOFTMAX_ROWS

    def logits(g, j, slot, bias):
        s = jnp.dot(kh_ref[0, g, j], qts[g], preferred_element_type=F32)
        s_ref[slot, g] = s if bias is None else s + bias

    def chunk(g, slot, c):
        return s_ref[slot, g, c * SOFTMAX_ROWS:(c + 1) * SOFTMAX_ROWS, :].reshape(
            SOFTMAX_ROWS // SUBLANES, SUBLANES, blk)

    def softmax(g, slot, row):
        m8 = jnp.max(chunk(g, slot, 0), axis=0)
        for c in range(1, n_chunks):
            m8 = jnp.maximum(m8, jnp.max(chunk(g, slot, c), axis=0))
        m_blk = jnp.max(m8, axis=0, keepdims=True)
        if row is None:
            mn = m_blk
            shift = mn
        else:
            m_old = m_ref[g]
            mn = jnp.maximum(m_old, m_blk + row)
            alpha_ref[slot, g] = jnp.exp2(m_old - mn)
            shift = mn - row
        for c in range(n_chunks):
            p = jnp.exp2(chunk(g, slot, c) - shift)
            p_ref[slot, g, c * SOFTMAX_ROWS:(c + 1) * SOFTMAX_ROWS, :] = (
                p.reshape(SOFTMAX_ROWS, blk).astype(BF16))
        m_ref[g] = mn

    def weighted_values(g, j, slot, first):
        pv = jnp.dot(vt_ref[0, g, j], p_ref[slot, g], preferred_element_type=F32)
        acc_ref[g] = pv if first else alpha_ref[slot, g] * acc_ref[g] + pv

    for g in range(hg):
        logits(g, n, 0, bdiag_ref[g])
    for g in range(hg):
        select(g)
    for g in range(hg):
        logits(g, jnp.maximum(n - 1, 0), 1, bsub_ref[g])
    for g in range(hg):
        softmax(g, 0, None)

    @pl.when(n == 0)
    def _():
        for g in range(hg):
            weighted_values(g, n, 0, True)

    @pl.when(n >= 1)
    def _():
        for g in range(hg):
            weighted_values(g, n, 0, True)
            softmax(g, 1, selb_ref[g, pl.ds(n - 1, 1), :])
            logits(g, 0, 0, None)

        def trip(j, slot):
            nxt = jnp.minimum(j + 1, n - 2)
            prev = jnp.where(j == 0, n - 1, j - 1)
            for g in range(hg):
                logits(g, nxt, 1 - slot, None)
            for g in range(hg):
                weighted_values(g, prev, 1 - slot, False)
            for g in range(hg):
                softmax(g, slot, selb_ref[g, pl.ds(j, 1), :])

        def far(i, carry):
            for r in range(FAR_UNROLL):
                trip(FAR_UNROLL * i + r, r & 1)
            return carry

        n_far = n - 1
        lax.fori_loop(0, n_far // FAR_UNROLL, far, 0)
        done = (n_far // FAR_UNROLL) * FAR_UNROLL
        piece = FAR_UNROLL // 2
        while piece >= 1:
            @pl.when(((n_far - done) & piece) != 0)
            def _(piece=piece, done=done):
                for r in range(piece):
                    trip(done + r, r & 1)
            done = done + ((n_far - done) & piece)
            piece //= 2

        last = jnp.where(n >= 2, n - 2, n - 1)
        for g in range(hg):
            weighted_values(g, last, n & 1, False)

    for g in range(hg):
        ot_ref[0, g * HEAD_DIM:(g + 1) * HEAD_DIM, :] = (
            acc_ref[g, 0:HEAD_DIM, :] / acc_ref[g, HEAD_DIM:HEAD_DIM + 1, :])


def _moba_prompt(rel_bias, qt, kh, vt, kmh):
    b, nh, nb, hd, blk = qt.shape
    t = nb * blk
    hg = HEAD_GROUP
    grp = lambda bi, hi, ni: (bi, hi, 0, 0, 0)
    return pl.pallas_call(
        functools.partial(_moba_prompt_body, nb=nb, hg=hg),
        grid=(b, nh // hg, nb),
        in_specs=[
            pl.BlockSpec(memory_space=pltpu.SMEM),
            pl.BlockSpec((1, hg, 1, hd, blk), lambda bi, hi, ni: (bi, hi, ni, 0, 0)),
            pl.BlockSpec((1, hg, nb, blk, hd), grp, pipeline_mode=pl.Buffered(1)),
            pl.BlockSpec((1, hg, nb, VT_ROWS, blk), grp, pipeline_mode=pl.Buffered(1)),
            pl.BlockSpec((1, hg, nb, hd), lambda bi, hi, ni: (bi, hi, 0, 0)),
        ],
        out_specs=pl.BlockSpec((1, hg * hd, blk), lambda bi, hi, ni: (bi, hi, ni)),
        out_shape=jax.ShapeDtypeStruct((b, nh * hd, t), F32),
        scratch_shapes=[
            pltpu.VMEM((hg, blk, blk), F32),
            pltpu.VMEM((hg, blk, blk), F32),
            pltpu.VMEM((hg, nb, blk), F32),
            pltpu.VMEM((hg, 1, blk), F32),
            pltpu.VMEM((2, hg, 1, blk), F32),
            pltpu.VMEM((hg, VT_ROWS, blk), F32),
            pltpu.VMEM((2, hg, blk, blk), F32),
            pltpu.VMEM((2, hg, blk, blk), BF16),
        ],
        compiler_params=_cparams(3),
        name="prompt_moba",
    )(rel_bias, qt, kh, vt, kmh)


def _prompt_mix_out_body(a_ref, halo_ref, cw_ref, cb_ref, lg_ref, lb_ref, ot_ref, x_ref, w_ref,
                         y_ref, sh_ref, conv_ref, wb_ref, *, tm):
    t = pl.program_id(1)
    rows = tm + HALO_A

    @pl.when((pl.program_id(0) == 0) & (t == 0))
    def _():
        for j in range(CONV_A):
            wb_ref[j] = jnp.broadcast_to(cw_ref[j:j + 1, :], (SUBLANES, D_A))

    sh_ref[0, 0:HALO_A] = jnp.where(t == 0, 0.0, halo_ref[0])
    sh_ref[0, HALO_A:rows] = a_ref[0]
    for s in range(1, SUBLANES):
        sh_ref[s, 0:rows - SUBLANES] = sh_ref[0, s:s + rows - SUBLANES]

    groups = CONV_ROWS // SUBLANES
    first_off = HALO_A - (CONV_A - 1)

    def chunk(c, carry):
        r0 = pl.multiple_of(c * CONV_ROWS, CONV_ROWS)
        for l0 in range(0, D_A, CONV_LANES):
            lanes = slice(l0, l0 + CONV_LANES)
            acc = [jnp.zeros((SUBLANES, CONV_LANES), F32) for _ in range(groups)]
            for s in range(SUBLANES):
                offs = [o for o in range(first_off, first_off + CONV_A) if o % SUBLANES == s]
                qs = [o // SUBLANES for o in offs]
                slab = {gq: sh_ref[s, pl.ds(r0 + gq * SUBLANES, SUBLANES), lanes]
                        for gq in range(min(qs), max(qs) + groups)}
                for o, q in zip(offs, qs):
                    w = wb_ref[o - first_off, :, lanes]
                    for g in range(groups):
                        acc[g] = acc[g] + w * slab[g + q]
            for g in range(groups):
                conv_ref[pl.ds(r0 + g * SUBLANES, SUBLANES), lanes] = acc[g]
        return carry

    lax.fori_loop(0, tm // CONV_ROWS, chunk, 0)

    ao = _silu(_layernorm(conv_ref[...] + cb_ref[...], lg_ref[...], lb_ref[...])).astype(BF16)
    o = ot_ref[0].T.astype(BF16)
    y = (jnp.dot(ao, w_ref[0:D_A, :], preferred_element_type=F32)
         + jnp.dot(o, w_ref[D_A:D_A + D_B, :], preferred_element_type=F32))
    y_ref[0] = x_ref[0] + y


def _prompt_mix_out(a, cw, cb, lg, lb, ot, x, w):
    b, t, d = x.shape
    tm = TILE_M
    nt = t // tm
    hpt = tm // HALO_A
    tok = lambda bi, ti: (bi, ti, 0)
    return pl.pallas_call(
        functools.partial(_prompt_mix_out_body, tm=tm),
        grid=(b, nt),
        in_specs=[
            pl.BlockSpec((1, tm, D_A), tok),
            pl.BlockSpec((1, HALO_A, D_A), lambda bi, ti: (bi, jnp.maximum(ti * hpt - 1, 0), 0)),
            _const_spec((CONV_A, D_A)),
            _const_spec((1, D_A)),
            _const_spec((1, D_A)),
            _const_spec((1, D_A)),
            pl.BlockSpec((1, D_B, tm), lambda bi, ti: (bi, 0, ti)),
            pl.BlockSpec((1, tm, d), tok),
            _const_spec((D_A + D_B, d)),
        ],
        out_specs=pl.BlockSpec((1, tm, d), tok),
        out_shape=jax.ShapeDtypeStruct((b, t, d), F32),
        scratch_shapes=[
            pltpu.VMEM((SUBLANES, tm + HALO_A, D_A), F32),
            pltpu.VMEM((tm, D_A), F32),
            pltpu.VMEM((CONV_A, SUBLANES, D_A), F32),
        ],
        compiler_params=_cparams(2),
        name="prompt_mix_out",
    )(a, a, cw, cb, lg, lb, ot, x, w)


def _conv3_chunk(u, ext_ref, carry_ref, cw_ref, c0, tm):
    ext_ref[0:SUBLANES] = carry_ref[:, c0:c0 + CH]
    ext_ref[SUBLANES:SUBLANES + tm] = u
    carry_ref[:, c0:c0 + CH] = u[tm - SUBLANES:tm]
    return (cw_ref[0:1, c0:c0 + CH] * ext_ref[SUBLANES - 2:SUBLANES - 2 + tm]
            + cw_ref[1:2, c0:c0 + CH] * ext_ref[SUBLANES - 1:SUBLANES - 1 + tm]
            + cw_ref[2:3, c0:c0 + CH] * u)


def _prompt_ffn_body(x_ref, g_ref, wu_ref, cw_ref, cb_ref, wd_ref, y_ref, st_ref,
                     carry_ref, extg_ref, extu_ref, act_ref, *, tm):
    @pl.when(pl.program_id(1) == 0)
    def _():
        carry_ref[...] = jnp.zeros_like(carry_ref)

    x = x_ref[0]
    h = _rmsnorm(x, g_ref[...]).astype(BF16)
    for c in range(D_FF // CH):
        cg = c * CH
        cu = D_FF + c * CH
        ug = jnp.dot(h, wu_ref[:, cg:cg + CH], preferred_element_type=F32)
        uu = jnp.dot(h, wu_ref[:, cu:cu + CH], preferred_element_type=F32)
        gg = _conv3_chunk(ug, extg_ref, carry_ref, cw_ref, cg, tm) + cb_ref[:, cg:cg + CH]
        gu = _conv3_chunk(uu, extu_ref, carry_ref, cw_ref, cu, tm) + cb_ref[:, cu:cu + CH]
        act_ref[:, cg:cg + CH] = (_silu(gg) * gu).astype(BF16)
    st_ref[0] = carry_ref[...]
    y_ref[0] = x + jnp.dot(act_ref[...], wd_ref[...], preferred_element_type=F32)


def _prompt_ffn(x, g, wu, cw, cb, wd):
    b, t, d = x.shape
    tm = TILE_M
    nt = t // tm
    tok = lambda bi, ti: (bi, ti, 0)
    return pl.pallas_call(
        functools.partial(_prompt_ffn_body, tm=tm),
        grid=(b, nt),
        in_specs=[
            pl.BlockSpec((1, tm, d), tok),
            _const_spec((1, d)),
            _const_spec((d, 2 * D_FF)),
            _const_spec((CONV_F, 2 * D_FF)),
            _const_spec((1, 2 * D_FF)),
            _const_spec((D_FF, d)),
        ],
        out_specs=[
            pl.BlockSpec((1, tm, d), tok),
            pl.BlockSpec((1, SUBLANES, 2 * D_FF), lambda bi, ti: (bi, 0, 0)),
        ],
        out_shape=[
            jax.ShapeDtypeStruct((b, t, d), F32),
            jax.ShapeDtypeStruct((b, SUBLANES, 2 * D_FF), F32),
        ],
        scratch_shapes=[
            pltpu.VMEM((SUBLANES, 2 * D_FF), F32),
            pltpu.VMEM((tm + SUBLANES, CH), F32),
            pltpu.VMEM((tm + SUBLANES, CH), F32),
            pltpu.VMEM((tm, D_FF), BF16),
        ],
        compiler_params=_cparams(2),
        name="prompt_ffn",
    )(x, g, wu, cw, cb, wd)


def _prompt_mixc_body(x_ref, g_ref, wi_ref, cw_ref, wo_ref, y_ref, st_ref,
                      carry_ref, ext_ref, z_ref, *, tm):
    @pl.when(pl.program_id(1) == 0)
    def _():
        carry_ref[...] = jnp.zeros_like(carry_ref)

    x = x_ref[0]
    h = _rmsnorm(x, g_ref[...]).astype(BF16)
    for c in range(D_C // CH):
        c0 = c * CH
        gb = jnp.dot(h, wi_ref[:, c0:c0 + CH], preferred_element_type=F32)
        gc = jnp.dot(h, wi_ref[:, D_C + c0:D_C + c0 + CH], preferred_element_type=F32)
        u = jnp.dot(h, wi_ref[:, 2 * D_C + c0:2 * D_C + c0 + CH], preferred_element_type=F32)
        conv = _conv3_chunk(gc * u, ext_ref, carry_ref, cw_ref, c0, tm)
        z_ref[:, c0:c0 + CH] = (gb * conv).astype(BF16)
    st_ref[0] = carry_ref[...]
    y_ref[0] = x + jnp.dot(z_ref[...], wo_ref[...], preferred_element_type=F32)


def _prompt_mixc(x, g, wi, cw, wo):
    b, t, d = x.shape
    tm = TILE_M
    nt = t // tm
    tok = lambda bi, ti: (bi, ti, 0)
    return pl.pallas_call(
        functools.partial(_prompt_mixc_body, tm=tm),
        grid=(b, nt),
        in_specs=[
            pl.BlockSpec((1, tm, d), tok),
            _const_spec((1, d)),
            _const_spec((d, 3 * D_C)),
            _const_spec((CONV_C, D_C)),
            _const_spec((D_C, d)),
        ],
        out_specs=[
            pl.BlockSpec((1, tm, d), tok),
            pl.BlockSpec((1, SUBLANES, D_C), lambda bi, ti: (bi, 0, 0)),
        ],
        out_shape=[
            jax.ShapeDtypeStruct((b, t, d), F32),
            jax.ShapeDtypeStruct((b, SUBLANES, D_C), F32),
        ],
        scratch_shapes=[
            pltpu.VMEM((SUBLANES, D_C), F32),
            pltpu.VMEM((tm + SUBLANES, CH), F32),
            pltpu.VMEM((tm, D_C), BF16),
        ],
        compiler_params=_cparams(2),
        name="prompt_mixc",
    )(x, g, wi, cw, wo)


def _sample_even_body(x_ref, g_ref, w_ref, qg_ref, kg_ref, ho_ref, sa_ref, cw_ref, cb_ref, lg_ref, lb_ref,
                      ao_ref, san_ref, q_ref, k_ref, v_ref):
    a, q, k, v = _in_proj_even(x_ref[...], g_ref[...], w_ref[...], qg_ref[...], kg_ref[...], ho_ref[...])
    q_ref[...] = q
    k_ref[...] = k
    v_ref[...] = v
    hist = CONV_A - 1
    conv = cb_ref[...] + cw_ref[hist:hist + 1, :] * a
    for j in range(hist):
        conv = conv + cw_ref[j:j + 1, :] * sa_ref[j]
    ao_ref[...] = _silu(_layernorm(conv, lg_ref[...], lb_ref[...])).astype(BF16)
    san_ref[0:hist - 1] = sa_ref[1:hist]
    san_ref[hist - 1] = a


def _sample_even(x, g, w, qg, kg, head_ones, sa, cw, cb, lg, lb):
    m = x.shape[0]
    return pl.pallas_call(
        _sample_even_body,
        out_shape=[
            jax.ShapeDtypeStruct((m, D_A), BF16),
            jax.ShapeDtypeStruct(sa.shape, F32),
            jax.ShapeDtypeStruct((m, D_B), F32),
            jax.ShapeDtypeStruct((m, D_B), F32),
            jax.ShapeDtypeStruct((m, D_B), F32),
        ],
        compiler_params=pltpu.CompilerParams(vmem_limit_bytes=VMEM_LIMIT),
        name="sample_even",
    )(x, g, w, qg, kg, head_ones, sa, cw, cb, lg, lb)


def _moba_sample_keys_body(pt_ref, q_ref, qdh_ref, kn_ref, rbt_ref, *refs, n_pages, page):
    del pt_ref
    outs = refs[SEQ_PER_STEP * n_pages:]
    bias_ref = outs[4]
    past = n_pages * page

    @pl.when(pl.program_id(0) == 0)
    def _():
        pos = lax.broadcasted_iota(jnp.int32, (N_HEADS, past), 1)
        dist = past - pos
        out = jnp.zeros((N_HEADS, past), F32) + rbt_ref[:, 0:1]
        for i, th in enumerate(T5_THRESH):
            out = jnp.where(dist >= th, rbt_ref[:, i + 1:i + 2], out)
        bias_ref[...] = out * LOG2E

    for sq in range(SEQ_PER_STEP):
        _moba_sample_keys_one(sq, q_ref, qdh_ref, kn_ref, rbt_ref, refs[sq * n_pages:(sq + 1) * n_pages], *outs,
                              n_pages=n_pages, page=page)


def _moba_sample_keys_one(sq, q_ref, qdh_ref, kn_ref, rbt_ref, k_pages, pc_ref, pn_ref, l_ref, sel_ref,
                          bias_ref, qb_ref, s_ref, *, n_pages, page):
    past = n_pages * page
    n_past_blocks = past // MOBA_BLOCK
    n_sel = min(MOBA_TOPK, n_past_blocks)
    qb_ref = qb_ref.at[sq]
    s_ref = s_ref.at[sq]

    head_of_lane = lax.broadcasted_iota(jnp.int32, (N_HEADS, D_B), 1) // HEAD_DIM
    hmask = head_of_lane == lax.broadcasted_iota(jnp.int32, (N_HEADS, D_B), 0)
    qbd = jnp.where(hmask, q_ref[sq], 0.0)

    qdh = qdh_ref[sq]
    for h in range(N_HEADS):
        qb_ref[h] = jnp.broadcast_to(qdh[:, h:h + 1], (HEAD_DIM, page))

    for pg in range(n_pages):
        rows = [jnp.sum(k_pages[pg][0, h] * qb_ref[h], axis=0, keepdims=True) for h in range(N_HEADS)]
        s_ref[:, pg * page:(pg + 1) * page] = jnp.concatenate(rows, axis=0)

    gates = [jnp.sum(s_ref[:, j * MOBA_BLOCK:(j + 1) * MOBA_BLOCK], axis=-1, keepdims=True)
             for j in range(n_past_blocks)]
    ranks = []
    pieces = []
    for j in range(n_past_blocks):
        rank = jnp.zeros((N_HEADS, 1), jnp.int32)
        for i in range(n_past_blocks):
            if i == j:
                continue
            ahead = (gates[i] >= gates[j]) if i < j else (gates[i] > gates[j])
            rank = rank + jnp.where(ahead, 1, 0)
        ranks.append(rank)
        lanes = slice(j * MOBA_BLOCK, (j + 1) * MOBA_BLOCK)
        pieces.append(s_ref[:, lanes] + bias_ref[:, lanes] + jnp.where(rank < n_sel, 0.0, -jnp.inf))
    s = jnp.concatenate(pieces, axis=-1)

    s_new = jnp.sum(qbd * kn_ref[sq], axis=-1, keepdims=True) + rbt_ref[:, 0:1] * LOG2E
    m = jnp.maximum(jnp.max(s, axis=-1, keepdims=True), s_new)
    p = jnp.exp2(s - m)
    p_new = jnp.exp2(s_new - m)
    pn_ref[sq] = jnp.broadcast_to(p_new, (N_HEADS, 128))
    l_ref[sq] = jnp.broadcast_to(jnp.sum(p, axis=-1, keepdims=True) + p_new, (N_HEADS, 128))

    lane = lax.broadcasted_iota(jnp.int32, (N_HEADS, 128), 1)
    sel = jnp.zeros((N_HEADS, 128), jnp.int32)
    for r in range(n_sel):
        picked = jnp.zeros((N_HEADS, MOBA_BLOCK), F32)
        block_id = jnp.zeros((N_HEADS, 1), jnp.int32)
        for j in range(n_past_blocks):
            picked = picked + jnp.where(ranks[j] == r, p[:, j * MOBA_BLOCK:(j + 1) * MOBA_BLOCK], 0.0)
            block_id = block_id + jnp.where(ranks[j] == r, j, 0)
        pc_ref[sq, :, r * MOBA_BLOCK:(r + 1) * MOBA_BLOCK] = picked
        sel = jnp.where(lane == r, block_id, sel)
    sel_ref[sq] = sel


def _moba_sample_values_body(pt_ref, sel_ref, pc_ref, pn_ref, l_ref, vn_ref, v_hbm, o_ref, buf_ref, sem_ref,
                             *, n_sel, ppb, page):
    b = pl.program_id(0)
    n_tiles = n_sel * ppb

    def tile_copy(seq, slot, h, i):
        block = sel_ref[seq * (N_HEADS * n_sel) + h * n_sel + i // ppb]
        pool_page = pt_ref[seq, block * ppb + i % ppb]
        return pltpu.make_async_copy(v_hbm.at[pool_page, h], buf_ref.at[slot, h * n_tiles + i], sem_ref.at[slot])

    def for_all_tiles(seq, slot, act):
        for h in range(N_HEADS):
            for i in range(n_tiles):
                act(tile_copy(seq, slot, h, i))

    @pl.when(b == 0)
    def _():
        for_all_tiles(0, 0, lambda cp: cp.start())

    @pl.when(b + 1 < pl.num_programs(0))
    def _():
        for_all_tiles(b + 1, (b + 1) % 2, lambda cp: cp.start())

    slot = b % 2
    for_all_tiles(b, slot, lambda cp: cp.wait())

    head_of_lane = lax.broadcasted_iota(jnp.int32, (N_HEADS, D_B), 1) // HEAD_DIM
    hmask = head_of_lane == lax.broadcasted_iota(jnp.int32, (N_HEADS, D_B), 0)
    ones = jnp.ones((SUBLANES, page), BF16)
    lane_sum = lambda x: lax.dot_general(ones, x, (((1,), (1,)), ((), ())), preferred_element_type=F32)
    outs = []
    for h in range(N_HEADS):
        acc = jnp.zeros((HEAD_DIM, page), F32)
        for i in range(n_tiles):
            acc = acc + buf_ref[slot, h * n_tiles + i] * pc_ref[0, h:h + 1, i * page:(i + 1) * page]
        hi = acc.astype(BF16)
        lo = (acc - hi.astype(F32)).astype(BF16)
        outs.append((lane_sum(hi) + lane_sum(lo))[0:1, :])
    o_past = jnp.concatenate(outs, axis=-1)
    spread = lambda col: jnp.sum(jnp.where(hmask, col, 0.0), axis=0, keepdims=True)
    o_ref[0] = (o_past + spread(pn_ref[0, :, 0:1]) * vn_ref[0]) / spread(l_ref[0, :, 0:1])


def _moba_sample(page_table, q, k_new, v_new, rel_bias_t, cache_kt, cache_vt):
    m, n_pages = page_table.shape
    page = cache_kt.shape[-1]
    past = n_pages * page
    n_sel = min(MOBA_TOPK, past // MOBA_BLOCK)
    ppb = MOBA_BLOCK // page
    row = lambda bi, *_: (bi, 0, 0)

    sps = SEQ_PER_STEP
    assert m % sps == 0
    keys_spec = pltpu.PrefetchScalarGridSpec(
        num_scalar_prefetch=1,
        grid=(m // sps,),
        in_specs=[
            pl.BlockSpec((sps, 1, D_B), row),
            pl.BlockSpec((sps, HEAD_DIM, N_HEADS), row),
            pl.BlockSpec((sps, 1, D_B), row),
            pl.BlockSpec((N_HEADS, NUM_BUCKETS), lambda bi, pt: (0, 0)),
        ] + [pl.BlockSpec((1, N_HEADS, HEAD_DIM, page), lambda bi, pt, sq=sq, pg=pg: (pt[bi * sps + sq, pg], 0, 0, 0))
             for sq in range(sps) for pg in range(n_pages)],
        out_specs=[
            pl.BlockSpec((sps, N_HEADS, n_sel * MOBA_BLOCK), row),
            pl.BlockSpec((sps, N_HEADS, 128), row),
            pl.BlockSpec((sps, N_HEADS, 128), row),
            pl.BlockSpec((sps, N_HEADS, 128), row),
        ],
        scratch_shapes=[
            pltpu.VMEM((N_HEADS, past), F32),
            pltpu.VMEM((sps, N_HEADS, HEAD_DIM, page), F32),
            pltpu.VMEM((sps, N_HEADS, past), F32),
        ],
    )
    q_dh = q.reshape(m, N_HEADS, HEAD_DIM).transpose(0, 2, 1)
    pc, pn, l, sel = pl.pallas_call(
        functools.partial(_moba_sample_keys_body, n_pages=n_pages, page=page),
        grid_spec=keys_spec,
        out_shape=[
            jax.ShapeDtypeStruct((m, N_HEADS, n_sel * MOBA_BLOCK), F32),
            jax.ShapeDtypeStruct((m, N_HEADS, 128), F32),
            jax.ShapeDtypeStruct((m, N_HEADS, 128), F32),
            jax.ShapeDtypeStruct((m, N_HEADS, 128), jnp.int32),
        ],
        compiler_params=_cparams(1),
        name="sample_moba_keys",
    )(page_table, q.reshape(m, 1, D_B), q_dh, k_new.reshape(m, 1, D_B), rel_bias_t, *([cache_kt] * (sps * n_pages)))

    sel_flat = sel[:, :, :n_sel].reshape(m * N_HEADS * n_sel)
    values_spec = pltpu.PrefetchScalarGridSpec(
        num_scalar_prefetch=2,
        grid=(m,),
        in_specs=[
            pl.BlockSpec((1, N_HEADS, n_sel * MOBA_BLOCK), row),
            pl.BlockSpec((1, N_HEADS, 128), row),
            pl.BlockSpec((1, N_HEADS, 128), row),
            pl.BlockSpec((1, 1, D_B), row),
            pl.BlockSpec(memory_space=pl.ANY),
        ],
        out_specs=pl.BlockSpec((1, 1, D_B), row),
        scratch_shapes=[
            pltpu.VMEM((2, N_HEADS * n_sel * ppb, HEAD_DIM, page), F32),
            pltpu.SemaphoreType.DMA((2,)),
        ],
    )
    return pl.pallas_call(
        functools.partial(_moba_sample_values_body, n_sel=n_sel, ppb=ppb, page=page),
        grid_spec=values_spec,
        out_shape=jax.ShapeDtypeStruct((m, 1, D_B), F32),
        compiler_params=_cparams(1),
        name="sample_moba_values",
    )(page_table, sel_flat, pc, pn, l, v_new.reshape(m, 1, D_B), cache_vt)


def _sample_ffn(x, st_ref, g_ref, wu_ref, cw_ref, cb_ref, wd_ref, stn_ref):
    h = _rmsnorm(x, g_ref[...]).astype(BF16)
    up = jnp.dot(h, wu_ref[...], preferred_element_type=F32)
    older, newer = st_ref[:, 0, :], st_ref[:, 1, :]
    conv = cw_ref[0:1, :] * older + cw_ref[1:2, :] * newer + cw_ref[2:3, :] * up + cb_ref[...]
    act = (_silu(conv[:, :D_FF]) * conv[:, D_FF:]).astype(BF16)
    stn_ref[:, 0, :] = newer
    stn_ref[:, 1, :] = up
    return x + jnp.dot(act, wd_ref[...], preferred_element_type=F32)


def _sample_out_ffn_body(ao_ref, o_ref, x_ref, wo_ref, st_ref, g_ref, wu_ref, cw_ref, cb_ref, wd_ref,
                         y_ref, stn_ref):
    x1 = (x_ref[...]
          + jnp.dot(ao_ref[...], wo_ref[0:D_A, :], preferred_element_type=F32)
          + jnp.dot(o_ref[...].astype(BF16), wo_ref[D_A:D_A + D_B, :], preferred_element_type=F32))
    y_ref[...] = _sample_ffn(x1, st_ref, g_ref, wu_ref, cw_ref, cb_ref, wd_ref, stn_ref)


def _sample_out_ffn(ao, o, x, wo, st, g, wu, cw, cb, wd):
    return pl.pallas_call(
        _sample_out_ffn_body,
        out_shape=[jax.ShapeDtypeStruct(x.shape, F32), jax.ShapeDtypeStruct(st.shape, F32)],
        compiler_params=pltpu.CompilerParams(vmem_limit_bytes=VMEM_LIMIT),
        name="sample_out_ffn",
    )(ao, o, x, wo, st, g, wu, cw, cb, wd)


def _sample_odd_body(x_ref, gm_ref, wi_ref, ccw_ref, sc_ref, wo_ref, st_ref, g_ref, wu_ref, cw_ref, cb_ref,
                     wd_ref, y_ref, scn_ref, stn_ref):
    x = x_ref[...]
    h = _rmsnorm(x, gm_ref[...]).astype(BF16)
    z = jnp.dot(h, wi_ref[...], preferred_element_type=F32)
    gcu = z[:, D_C:2 * D_C] * z[:, 2 * D_C:]
    older, newer = sc_ref[:, 0, :], sc_ref[:, 1, :]
    conv = ccw_ref[0:1, :] * older + ccw_ref[1:2, :] * newer + ccw_ref[2:3, :] * gcu
    scn_ref[:, 0, :] = newer
    scn_ref[:, 1, :] = gcu
    x1 = x + jnp.dot((z[:, :D_C] * conv).astype(BF16), wo_ref[...], preferred_element_type=F32)
    y_ref[...] = _sample_ffn(x1, st_ref, g_ref, wu_ref, cw_ref, cb_ref, wd_ref, stn_ref)


def _sample_odd(x, gm, wi, ccw, sc, wo, st, g, wu, cw, cb, wd):
    return pl.pallas_call(
        _sample_odd_body,
        out_shape=[jax.ShapeDtypeStruct(x.shape, F32), jax.ShapeDtypeStruct(sc.shape, F32),
                   jax.ShapeDtypeStruct(st.shape, F32)],
        compiler_params=pltpu.CompilerParams(vmem_limit_bytes=VMEM_LIMIT),
        name="sample_odd",
    )(x, gm, wi, ccw, sc, wo, st, g, wu, cw, cb, wd)


def kernel(x_prompt, x_sample, cache_k, cache_v, state_conv_a, state_conv_c, state_ffn, page_table, rel_bias,
           norm_mix_e, w_in_e, conv_a_w, conv_a_b, ln_a_g, ln_a_b, q_norm_g, k_norm_g, w_out_e,
           norm_mix_o, w_in_o, conv_c_w, w_out_o, norm_ffn, w_up, conv_f_w, conv_f_b, w_down):
    b, t, d = x_prompt.shape
    m = x_sample.shape[0]
    n_pool, page = cache_k.shape[1], cache_k.shape[2]
    n_pages = page_table.shape[1]
    assert norm_mix_e.shape[0] == 1 and norm_mix_o.shape[0] == 1 and norm_ffn.shape[0] == 2
    assert x_sample.shape[1] == 1 and t % TILE_M == 0 and t % page == 0
    assert (n_pages * page) % MOBA_BLOCK == 0 and MOBA_BLOCK % page == 0

    row = lambda v: v.reshape(1, -1)
    w_in_e_b = w_in_e[0].astype(BF16)
    w_out_e_b = w_out_e[0].astype(BF16)
    w_in_o_b = w_in_o[0].astype(BF16)
    w_out_o_b = w_out_o[0].astype(BF16)
    w_up_b = [w_up[i].astype(BF16) for i in range(2)]
    w_down_b = [w_down[i].astype(BF16) for i in range(2)]
    qg = row(jnp.tile(q_norm_g[0], N_HEADS))
    kg = row(jnp.tile(k_norm_g[0], N_HEADS))
    lane_head = jnp.arange(D_B, dtype=jnp.int32) // HEAD_DIM
    head_ones = (lane_head[:, None] == lane_head[None, :]).astype(BF16)
    g_e, g_o = row(norm_mix_e[0]), row(norm_mix_o[0])
    cab, lag, lab = row(conv_a_b[0]), row(ln_a_g[0]), row(ln_a_b[0])

    a_p, k_p, v_p, qt, kh, vt, km = _prompt_inproj(x_prompt, g_e, w_in_e_b, qg, kg, head_ones, page)
    nb = t // MOBA_BLOCK
    kmh = km.reshape(b, nb, N_HEADS, HEAD_DIM).transpose(0, 2, 1, 3)
    ot = _moba_prompt(rel_bias, qt, kh, vt, kmh)
    x1 = _prompt_mix_out(a_p, conv_a_w[0], cab, lag, lab, ot, x_prompt, w_out_e_b)
    x2, f0 = _prompt_ffn(x1, row(norm_ffn[0]), w_up_b[0], conv_f_w[0], row(conv_f_b[0]), w_down_b[0])
    x3, c_st = _prompt_mixc(x2, g_o, w_in_o_b, conv_c_w[0], w_out_o_b)
    y_prompt, f1 = _prompt_ffn(x3, row(norm_ffn[1]), w_up_b[1], conv_f_w[1], row(conv_f_b[1]), w_down_b[1])

    k_prompt = k_p.transpose(0, 1, 4, 2, 3)[None]
    v_prompt = v_p.transpose(0, 1, 4, 2, 3)[None]
    a_prompt = a_p[:, t - (CONV_A - 1):, :][None]
    c_prompt = c_st[:, SUBLANES - (CONV_C - 1):, :][None]
    f_prompt = jnp.stack([f0[:, SUBLANES - (CONV_F - 1):, :], f1[:, SUBLANES - (CONV_F - 1):, :]])

    xs = x_sample.reshape(m, d)
    sa = state_conv_a[0].transpose(1, 0, 2)
    sc = state_conv_c[0]
    sf = state_ffn
    ao_s, sa_new, q_s, k_s, v_s = _sample_even(xs, g_e, w_in_e_b, qg, kg, head_ones, sa, conv_a_w[0], cab, lag, lab)
    o_s = _moba_sample(page_table, q_s, k_s, v_s, rel_bias.T,
                       cache_k[0].transpose(0, 2, 3, 1), cache_v[0].transpose(0, 2, 3, 1))
    xs1, sf0 = _sample_out_ffn(ao_s, o_s.reshape(m, D_B), xs, w_out_e_b, sf[0], row(norm_ffn[0]), w_up_b[0],
                               conv_f_w[0], row(conv_f_b[0]), w_down_b[0])
    ys, sc_new, sf1 = _sample_odd(xs1, g_o, w_in_o_b, conv_c_w[0], sc, w_out_o_b, sf[1], row(norm_ffn[1]),
                                  w_up_b[1], conv_f_w[1], row(conv_f_b[1]), w_down_b[1])

    y_sample = ys.reshape(m, 1, d)
    k_sample = k_s.reshape(1, m, 1, N_HEADS, HEAD_DIM)
    v_sample = v_s.reshape(1, m, 1, N_HEADS, HEAD_DIM)
    a_sample = sa_new.transpose(1, 0, 2)[None]
    c_sample = sc_new[None]
    f_sample = jnp.stack([sf0, sf1])
    return (y_prompt, y_sample, k_prompt, v_prompt, a_prompt, c_prompt, f_prompt,
            k_sample, v_sample, a_sample, c_sample, f_sample)
```

```python
import functools
import math

import jax
import jax.numpy as jnp
from jax import lax
from jax.experimental import pallas as pl
from jax.experimental.pallas import tpu as pltpu

F32 = jnp.float32
BF16 = jnp.bfloat16

EPS = 1e-6
D_MODEL = 1024
D_A = 512
CONV_A = 31
N_HEADS = 8
HEAD_DIM = 64
D_B = N_HEADS * HEAD_DIM
MOBA_BLOCK = 256
MOBA_TOPK = 3
NUM_BUCKETS = 32
MAX_DISTANCE = 128
D_C = 1024
CONV_C = 3
D_FF = 2816
CONV_F = 3
LOG2E = math.log2(math.e)
Q_SCALE = HEAD_DIM ** -0.5 * LOG2E
VT_ROWS = HEAD_DIM + 16

TILE_M = 512
CONV_ROWS = 64
CONV_LANES = 256
HALO_A = 32
HEAD_GROUP = 8
SOFTMAX_ROWS = 64
SEQ_PER_STEP = 2
FAR_UNROLL = 2
CH = 256
SUBLANES = 8
VMEM_LIMIT = 56 * 1024 * 1024


def _t5_thresholds():
    max_exact = NUM_BUCKETS // 2
    th = list(range(1, max_exact + 1))
    for k in range(1, NUM_BUCKETS - max_exact):
        th.append(math.ceil(max_exact * (MAX_DISTANCE / max_exact) ** (k / (NUM_BUCKETS - max_exact))))
    return tuple(th)


T5_THRESH = _t5_thresholds()


def _cparams(n_grid):
    return pltpu.CompilerParams(dimension_semantics=("arbitrary",) * n_grid, vmem_limit_bytes=VMEM_LIMIT)


def _const_spec(shape):
    nd = len(shape)
    return pl.BlockSpec(shape, lambda *_: (0,) * nd, pipeline_mode=pl.Buffered(1))


def _layer_spec(shape, layer):
    nd = len(shape)
    return pl.BlockSpec((1,) + tuple(shape), lambda *_: (layer,) + (0,) * nd, pipeline_mode=pl.Buffered(1))


def _rmsnorm(x, g):
    return x * lax.rsqrt(jnp.mean(x * x, axis=-1, keepdims=True) + EPS) * g


def _split_dot(x, w_bf16):
    hi = x.astype(BF16)
    lo = (x - hi.astype(F32)).astype(BF16)
    return (jnp.dot(hi, w_bf16, preferred_element_type=F32)
            + jnp.dot(lo, w_bf16, preferred_element_type=F32))


def _head_rmsnorm(x, g, head_ones):
    ss = _split_dot(x * x, head_ones)
    return x * lax.rsqrt(ss * (1.0 / HEAD_DIM) + EPS) * g


def _silu(x):
    return x * jax.nn.sigmoid(x)


def _layernorm(x, g, b):
    mu = jnp.mean(x, axis=-1, keepdims=True)
    xc = x - mu
    var = jnp.mean(xc * xc, axis=-1, keepdims=True)
    return xc * lax.rsqrt(var + EPS) * g + b


def _in_proj_even(x, g, w, qg, kg, head_ones):
    h = _rmsnorm(x, g).astype(BF16)
    z = jnp.dot(h, w, preferred_element_type=F32)
    a = z[:, :D_A] * jax.nn.sigmoid(z[:, D_A:2 * D_A])
    q = _head_rmsnorm(z[:, 2 * D_A:2 * D_A + D_B], qg, head_ones) * Q_SCALE
    k = _head_rmsnorm(z[:, 2 * D_A + D_B:2 * D_A + 2 * D_B], kg, head_ones)
    v = z[:, 2 * D_A + 2 * D_B:]
    return a, q, k, v


def _prompt_inproj_body(x_ref, g_ref, w_ref, qg_ref, kg_ref, ho_ref,
                        a_ref, kp_ref, vp_ref, qt_ref, kh_ref, vt_ref, km_ref, *, tm, page):
    a, q, k, v = _in_proj_even(x_ref[0], g_ref[...], w_ref[...], qg_ref[...], kg_ref[...], ho_ref[...])
    a_ref[0] = a
    qt = q.T
    kt = k.T
    vt = v.T
    for pg in range(tm // page):
        kp_ref[0, pg] = kt[:, pg * page:(pg + 1) * page].reshape(N_HEADS, HEAD_DIM, page)
        vp_ref[0, pg] = vt[:, pg * page:(pg + 1) * page].reshape(N_HEADS, HEAD_DIM, page)
    kb = k.astype(BF16)
    pad_row = lax.broadcasted_iota(jnp.int32, (N_HEADS, VT_ROWS - HEAD_DIM, MOBA_BLOCK), 1)
    ones_rows = jnp.where(pad_row == 0, 1.0, 0.0).astype(BF16)
    for i in range(tm // MOBA_BLOCK):
        r0 = i * MOBA_BLOCK
        qt_ref[0, :, i] = qt[:, r0:r0 + MOBA_BLOCK].reshape(N_HEADS, HEAD_DIM, MOBA_BLOCK).astype(BF16)
        vt_ref[0, :, i, 0:HEAD_DIM, :] = (
            vt[:, r0:r0 + MOBA_BLOCK].reshape(N_HEADS, HEAD_DIM, MOBA_BLOCK).astype(BF16))
        vt_ref[0, :, i, HEAD_DIM:VT_ROWS, :] = ones_rows
        for hh in range(N_HEADS):
            kh_ref[0, hh, i] = kb[r0:r0 + MOBA_BLOCK, hh * HEAD_DIM:(hh + 1) * HEAD_DIM]
        km_ref[0, i] = jnp.mean(k[r0:r0 + MOBA_BLOCK], axis=0, keepdims=True)


def _prompt_inproj(x, g, w, qg, kg, head_ones, page):
    b, t, d = x.shape
    tm = TILE_M
    nt = t // tm
    nb = t // MOBA_BLOCK
    bpt = tm // MOBA_BLOCK
    ppt = tm // page
    n_out = w.shape[1]
    tok = lambda bi, ti: (bi, ti, 0)
    blk5 = lambda bi, ti: (bi, 0, ti, 0, 0)
    pages = lambda bi, ti: (bi, ti, 0, 0, 0)
    return pl.pallas_call(
        functools.partial(_prompt_inproj_body, tm=tm, page=page),
        grid=(b, nt),
        in_specs=[
            pl.BlockSpec((1, tm, d), tok),
            _const_spec((1, d)),
            _const_spec((d, n_out)),
            _const_spec((1, D_B)),
            _const_spec((1, D_B)),
            _const_spec((D_B, D_B)),
        ],
        out_specs=[
            pl.BlockSpec((1, tm, D_A), tok),
            pl.BlockSpec((1, ppt, N_HEADS, HEAD_DIM, page), pages),
            pl.BlockSpec((1, ppt, N_HEADS, HEAD_DIM, page), pages),
            pl.BlockSpec((1, N_HEADS, bpt, HEAD_DIM, MOBA_BLOCK), blk5),
            pl.BlockSpec((1, N_HEADS, bpt, MOBA_BLOCK, HEAD_DIM), blk5),
            pl.BlockSpec((1, N_HEADS, bpt, VT_ROWS, MOBA_BLOCK), blk5),
            pl.BlockSpec((1, bpt, 1, D_B), lambda bi, ti: (bi, ti, 0, 0)),
        ],
        out_shape=[
            jax.ShapeDtypeStruct((b, t, D_A), F32),
            jax.ShapeDtypeStruct((b, t // page, N_HEADS, HEAD_DIM, page), F32),
            jax.ShapeDtypeStruct((b, t // page, N_HEADS, HEAD_DIM, page), F32),
            jax.ShapeDtypeStruct((b, N_HEADS, nb, HEAD_DIM, MOBA_BLOCK), BF16),
            jax.ShapeDtypeStruct((b, N_HEADS, nb, MOBA_BLOCK, HEAD_DIM), BF16),
            jax.ShapeDtypeStruct((b, N_HEADS, nb, VT_ROWS, MOBA_BLOCK), BF16),
            jax.ShapeDtypeStruct((b, nb, 1, D_B), F32),
        ],
        compiler_params=_cparams(2),
        name="prompt_inproj",
    )(x, g, w, qg, kg, head_ones)


def _t5_bias_scalar_table(dist, rb_ref, h):
    out = jnp.full(dist.shape, rb_ref[0, h] * LOG2E, F32)
    for i, th in enumerate(T5_THRESH):
        out = jnp.where(dist >= th, rb_ref[i + 1, h] * LOG2E, out)
    return out


def _moba_prompt_body(rb_ref, qt_ref, kh_ref, vt_ref, km_ref, ot_ref,
                      bdiag_ref, bsub_ref, selb_ref, m_ref, alpha_ref, acc_ref, s_ref, p_ref, *, nb, hg):
    h0 = pl.program_id(1) * hg
    n = pl.program_id(2)
    blk = MOBA_BLOCK

    @pl.when(n == 0)
    def _():
        ki = lax.broadcasted_iota(jnp.int32, (blk, blk), 0)
        qi = lax.broadcasted_iota(jnp.int32, (blk, blk), 1)
        d0 = qi - ki
        for g in range(hg):
            bdiag_ref[g] = jnp.where(d0 >= 0, _t5_bias_scalar_table(jnp.maximum(d0, 0), rb_ref, h0 + g),
                                     -jnp.inf)
            bsub_ref[g] = _t5_bias_scalar_table(d0 + blk, rb_ref, h0 + g)

    bi = lax.broadcasted_iota(jnp.int32, (nb, blk), 0)
    qts = [qt_ref[0, g, 0] for g in range(hg)]

    def select(g):
        km = km_ref[0, g]
        km_hi = km.astype(BF16)
        km_lo = (km - km_hi.astype(F32)).astype(BF16)
        gate = (jnp.dot(km_hi, qts[g], preferred_element_type=F32)
                + jnp.dot(km_lo, qts[g], preferred_element_type=F32))
        avail = jnp.where(bi < n, 1.0, 0.0)
        far_bias = rb_ref[NUM_BUCKETS - 1, h0 + g] * LOG2E
        selb = jnp.full((nb, blk), -jnp.inf, F32)
        for _ in range(MOBA_TOPK):
            gm = jnp.where(avail > 0.0, gate, -jnp.inf)
            top = jnp.max(gm, axis=0, keepdims=True)
            first = jnp.where(avail > 0.0, jnp.where(gm == top, bi, nb), nb)
            pick = bi == jnp.min(first, axis=0, keepdims=True)
            selb = jnp.where(pick, jnp.where(bi == n - 1, 0.0, far_bias), selb)
            avail = jnp.where(pick, 0.0, avail)
        selb_ref[g] = selb

    n_chunks = blk // SOFTMAX_ROWS

    def logits(g, j, slot, bias):
        s = jnp.dot(kh_ref[0, g, j], qts[g], preferred_element_type=F32)
        s_ref[slot, g] = s if bias is None else s + bias

    def chunk(g, slot, c):
        return s_ref[slot, g, c * SOFTMAX_ROWS:(c + 1) * SOFTMAX_ROWS, :].reshape(
            SOFTMAX_ROWS // SUBLANES, SUBLANES, blk)

    def softmax(g, slot, row):
        m8 = jnp.max(chunk(g, slot, 0), axis=0)
        for c in range(1, n_chunks):
            m8 = jnp.maximum(m8, jnp.max(chunk(g, slot, c), axis=0))
        m_blk = jnp.max(m8, axis=0, keepdims=True)
        if row is None:
            mn = m_blk
            shift = mn
        else:
            m_old = m_ref[g]
            mn = jnp.maximum(m_old, m_blk + row)
            alpha_ref[slot, g] = jnp.exp2(m_old - mn)
            shift = mn - row
        for c in range(n_chunks):
            p = jnp.exp2(chunk(g, slot, c) - shift)
            p_ref[slot, g, c * SOFTMAX_ROWS:(c + 1) * SOFTMAX_ROWS, :] = (
                p.reshape(SOFTMAX_ROWS, blk).astype(BF16))
        m_ref[g] = mn

    def weighted_values(g, j, slot, first):
        pv = jnp.dot(vt_ref[0, g, j], p_ref[slot, g], preferred_element_type=F32)
        acc_ref[g] = pv if first else alpha_ref[slot, g] * acc_ref[g] + pv

    for g in range(hg):
        logits(g, n, 0, bdiag_ref[g])
    for g in range(hg):
        select(g)
    for g in range(hg):
        logits(g, jnp.maximum(n - 1, 0), 1, bsub_ref[g])
    for g in range(hg):
        softmax(g, 0, None)

    @pl.when(n == 0)
    def _():
        for g in range(hg):
            weighted_values(g, n, 0, True)

    @pl.when(n >= 1)
    def _():
        for g in range(hg):
            weighted_values(g, n, 0, True)
            softmax(g, 1, selb_ref[g, pl.ds(n - 1, 1), :])
            logits(g, 0, 0, None)

        def trip(j, slot):
            nxt = jnp.minimum(j + 1, n - 2)
            prev = jnp.where(j == 0, n - 1, j - 1)
            for g in range(hg):
                logits(g, nxt, 1 - slot, None)
            for g in range(hg):
                weighted_values(g, prev, 1 - slot, False)
            for g in range(hg):
                softmax(g, slot, selb_ref[g, pl.ds(j, 1), :])

        def far(i, carry):
            for r in range(FAR_UNROLL):
                trip(FAR_UNROLL * i + r, r & 1)
            return carry

        n_far = n - 1
        lax.fori_loop(0, n_far // FAR_UNROLL, far, 0)
        done = (n_far // FAR_UNROLL) * FAR_UNROLL
        piece = FAR_UNROLL // 2
        while piece >= 1:
            @pl.when(((n_far - done) & piece) != 0)
            def _(piece=piece, done=done):
                for r in range(piece):
                    trip(done + r, r & 1)
            done = done + ((n_far - done) & piece)
            piece //= 2

        last = jnp.where(n >= 2, n - 2, n - 1)
        for g in range(hg):
            weighted_values(g, last, n & 1, False)

    for g in range(hg):
        ot_ref[0, g * HEAD_DIM:(g + 1) * HEAD_DIM, :] = (
            acc_ref[g, 0:HEAD_DIM, :] / acc_ref[g, HEAD_DIM:HEAD_DIM + 1, :])


def _moba_prompt(rel_bias, qt, kh, vt, kmh):
    b, nh, nb, hd, blk = qt.shape
    t = nb * blk
    hg = HEAD_GROUP
    grp = lambda bi, hi, ni: (bi, hi, 0, 0, 0)
    return pl.pallas_call(
        functools.partial(_moba_prompt_body, nb=nb, hg=hg),
        grid=(b, nh // hg, nb),
        in_specs=[
            pl.BlockSpec(memory_space=pltpu.SMEM),
            pl.BlockSpec((1, hg, 1, hd, blk), lambda bi, hi, ni: (bi, hi, ni, 0, 0)),
            pl.BlockSpec((1, hg, nb, blk, hd), grp, pipeline_mode=pl.Buffered(1)),
            pl.BlockSpec((1, hg, nb, VT_ROWS, blk), grp, pipeline_mode=pl.Buffered(1)),
            pl.BlockSpec((1, hg, nb, hd), lambda bi, hi, ni: (bi, hi, 0, 0)),
        ],
        out_specs=pl.BlockSpec((1, hg * hd, blk), lambda bi, hi, ni: (bi, hi, ni)),
        out_shape=jax.ShapeDtypeStruct((b, nh * hd, t), F32),
        scratch_shapes=[
            pltpu.VMEM((hg, blk, blk), F32),
            pltpu.VMEM((hg, blk, blk), F32),
            pltpu.VMEM((hg, nb, blk), F32),
            pltpu.VMEM((hg, 1, blk), F32),
            pltpu.VMEM((2, hg, 1, blk), F32),
            pltpu.VMEM((hg, VT_ROWS, blk), F32),
            pltpu.VMEM((2, hg, blk, blk), F32),
            pltpu.VMEM((2, hg, blk, blk), BF16),
        ],
        compiler_params=_cparams(3),
        name="prompt_moba",
    )(rel_bias, qt, kh, vt, kmh)


def _prompt_mix_out_body(a_ref, halo_ref, cw_ref, cb_ref, lg_ref, lb_ref, ot_ref, x_ref, w_ref,
                         y_ref, sh_ref, conv_ref, wb_ref, *, tm):
    t = pl.program_id(1)
    rows = tm + HALO_A

    @pl.when((pl.program_id(0) == 0) & (t == 0))
    def _():
        for j in range(CONV_A):
            wb_ref[j] = jnp.broadcast_to(cw_ref[j:j + 1, :], (SUBLANES, D_A))

    sh_ref[0, 0:HALO_A] = jnp.where(t == 0, 0.0, halo_ref[0])
    sh_ref[0, HALO_A:rows] = a_ref[0]
    for s in range(1, SUBLANES):
        sh_ref[s, 0:rows - SUBLANES] = sh_ref[0, s:s + rows - SUBLANES]

    groups = CONV_ROWS // SUBLANES
    first_off = HALO_A - (CONV_A - 1)

    def chunk(c, carry):
        r0 = pl.multiple_of(c * CONV_ROWS, CONV_ROWS)
        for l0 in range(0, D_A, CONV_LANES):
            lanes = slice(l0, l0 + CONV_LANES)
            acc = [jnp.zeros((SUBLANES, CONV_LANES), F32) for _ in range(groups)]
            for s in range(SUBLANES):
                offs = [o for o in range(first_off, first_off + CONV_A) if o % SUBLANES == s]
                qs = [o // SUBLANES for o in offs]
                slab = {gq: sh_ref[s, pl.ds(r0 + gq * SUBLANES, SUBLANES), lanes]
                        for gq in range(min(qs), max(qs) + groups)}
                for o, q in zip(offs, qs):
                    w = wb_ref[o - first_off, :, lanes]
                    for g in range(groups):
                        acc[g] = acc[g] + w * slab[g + q]
            for g in range(groups):
                conv_ref[pl.ds(r0 + g * SUBLANES, SUBLANES), lanes] = acc[g]
        return carry

    lax.fori_loop(0, tm // CONV_ROWS, chunk, 0)

    ao = _silu(_layernorm(conv_ref[...] + cb_ref[...], lg_ref[...], lb_ref[...])).astype(BF16)
    o = ot_ref[0].T.astype(BF16)
    y = (jnp.dot(ao, w_ref[0:D_A, :], preferred_element_type=F32)
         + jnp.dot(o, w_ref[D_A:D_A + D_B, :], preferred_element_type=F32))
    y_ref[0] = x_ref[0] + y


def _prompt_mix_out(a, cw, cb, lg, lb, ot, x, w):
    b, t, d = x.shape
    tm = TILE_M
    nt = t // tm
    hpt = tm // HALO_A
    tok = lambda bi, ti: (bi, ti, 0)
    return pl.pallas_call(
        functools.partial(_prompt_mix_out_body, tm=tm),
        grid=(b, nt),
        in_specs=[
            pl.BlockSpec((1, tm, D_A), tok),
            pl.BlockSpec((1, HALO_A, D_A), lambda bi, ti: (bi, jnp.maximum(ti * hpt - 1, 0), 0)),
            _const_spec((CONV_A, D_A)),
            _const_spec((1, D_A)),
            _const_spec((1, D_A)),
            _const_spec((1, D_A)),
            pl.BlockSpec((1, D_B, tm), lambda bi, ti: (bi, 0, ti)),
            pl.BlockSpec((1, tm, d), tok),
            _const_spec((D_A + D_B, d)),
        ],
        out_specs=pl.BlockSpec((1, tm, d), tok),
        out_shape=jax.ShapeDtypeStruct((b, t, d), F32),
        scratch_shapes=[
            pltpu.VMEM((SUBLANES, tm + HALO_A, D_A), F32),
            pltpu.VMEM((tm, D_A), F32),
            pltpu.VMEM((CONV_A, SUBLANES, D_A), F32),
        ],
        compiler_params=_cparams(2),
        name="prompt_mix_out",
    )(a, a, cw, cb, lg, lb, ot, x, w)


def _conv3_chunk(u, ext_ref, carry_ref, cw_ref, c0, tm):
    ext_ref[0:SUBLANES] = carry_ref[:, c0:c0 + CH]
    ext_ref[SUBLANES:SUBLANES + tm] = u
    carry_ref[:, c0:c0 + CH] = u[tm - SUBLANES:tm]
    return (cw_ref[0:1, c0:c0 + CH] * ext_ref[SUBLANES - 2:SUBLANES - 2 + tm]
            + cw_ref[1:2, c0:c0 + CH] * ext_ref[SUBLANES - 1:SUBLANES - 1 + tm]
            + cw_ref[2:3, c0:c0 + CH] * u)


def _prompt_ffn_body(x_ref, g_ref, wu_ref, cw_ref, cb_ref, wd_ref, y_ref, st_ref,
                     carry_ref, extg_ref, extu_ref, act_ref, *, tm):
    @pl.when(pl.program_id(1) == 0)
    def _():
        carry_ref[...] = jnp.zeros_like(carry_ref)

    x = x_ref[0]
    h = _rmsnorm(x, g_ref[...]).astype(BF16)
    for c in range(D_FF // CH):
        cg = c * CH
        cu = D_FF + c * CH
        ug = jnp.dot(h, wu_ref[0, :, cg:cg + CH], preferred_element_type=F32)
        uu = jnp.dot(h, wu_ref[0, :, cu:cu + CH], preferred_element_type=F32)
        gg = _conv3_chunk(ug, extg_ref, carry_ref, cw_ref, cg, tm) + cb_ref[:, cg:cg + CH]
        gu = _conv3_chunk(uu, extu_ref, carry_ref, cw_ref, cu, tm) + cb_ref[:, cu:cu + CH]
        act_ref[:, cg:cg + CH] = (_silu(gg) * gu).astype(BF16)
    st_ref[0] = carry_ref[...]
    y_ref[0] = x + jnp.dot(act_ref[...], wd_ref[0], preferred_element_type=F32)


def _prompt_ffn(x, g, wu, cw, cb, wd, layer):
    b, t, d = x.shape
    tm = TILE_M
    nt = t // tm
    tok = lambda bi, ti: (bi, ti, 0)
    return pl.pallas_call(
        functools.partial(_prompt_ffn_body, tm=tm),
        grid=(b, nt),
        in_specs=[
            pl.BlockSpec((1, tm, d), tok),
            _const_spec((1, d)),
            _layer_spec((d, 2 * D_FF), layer),
            _const_spec((CONV_F, 2 * D_FF)),
            _const_spec((1, 2 * D_FF)),
            _layer_spec((D_FF, d), layer),
        ],
        out_specs=[
            pl.BlockSpec((1, tm, d), tok),
            pl.BlockSpec((1, SUBLANES, 2 * D_FF), lambda bi, ti: (bi, 0, 0)),
        ],
        out_shape=[
            jax.ShapeDtypeStruct((b, t, d), F32),
            jax.ShapeDtypeStruct((b, SUBLANES, 2 * D_FF), F32),
        ],
        scratch_shapes=[
            pltpu.VMEM((SUBLANES, 2 * D_FF), F32),
            pltpu.VMEM((tm + SUBLANES, CH), F32),
            pltpu.VMEM((tm + SUBLANES, CH), F32),
            pltpu.VMEM((tm, D_FF), BF16),
        ],
        compiler_params=_cparams(2),
        name="prompt_ffn",
    )(x, g, wu, cw, cb, wd)


def _prompt_mixc_body(x_ref, g_ref, wi_ref, cw_ref, wo_ref, y_ref, st_ref,
                      carry_ref, ext_ref, z_ref, *, tm):
    @pl.when(pl.program_id(1) == 0)
    def _():
        carry_ref[...] = jnp.zeros_like(carry_ref)

    x = x_ref[0]
    h = _rmsnorm(x, g_ref[...]).astype(BF16)
    for c in range(D_C // CH):
        c0 = c * CH
        gb = jnp.dot(h, wi_ref[:, c0:c0 + CH], preferred_element_type=F32)
        gc = jnp.dot(h, wi_ref[:, D_C + c0:D_C + c0 + CH], preferred_element_type=F32)
        u = jnp.dot(h, wi_ref[:, 2 * D_C + c0:2 * D_C + c0 + CH], preferred_element_type=F32)
        conv = _conv3_chunk(gc * u, ext_ref, carry_ref, cw_ref, c0, tm)
        z_ref[:, c0:c0 + CH] = (gb * conv).astype(BF16)
    st_ref[0] = carry_ref[...]
    y_ref[0] = x + jnp.dot(z_ref[...], wo_ref[...], preferred_element_type=F32)


def _prompt_mixc(x, g, wi, cw, wo):
    b, t, d = x.shape
    tm = TILE_M
    nt = t // tm
    tok = lambda bi, ti: (bi, ti, 0)
    return pl.pallas_call(
        functools.partial(_prompt_mixc_body, tm=tm),
        grid=(b, nt),
        in_specs=[
            pl.BlockSpec((1, tm, d), tok),
            _const_spec((1, d)),
            _const_spec((d, 3 * D_C)),
            _const_spec((CONV_C, D_C)),
            _const_spec((D_C, d)),
        ],
        out_specs=[
            pl.BlockSpec((1, tm, d), tok),
            pl.BlockSpec((1, SUBLANES, D_C), lambda bi, ti: (bi, 0, 0)),
        ],
        out_shape=[
            jax.ShapeDtypeStruct((b, t, d), F32),
            jax.ShapeDtypeStruct((b, SUBLANES, D_C), F32),
        ],
        scratch_shapes=[
            pltpu.VMEM((SUBLANES, D_C), F32),
            pltpu.VMEM((tm + SUBLANES, CH), F32),
            pltpu.VMEM((tm, D_C), BF16),
        ],
        compiler_params=_cparams(2),
        name="prompt_mixc",
    )(x, g, wi, cw, wo)


def _sample_even_body(x_ref, g_ref, w_ref, qg_ref, kg_ref, ho_ref, sa_ref, cw_ref, cb_ref, lg_ref, lb_ref,
                      ao_ref, san_ref, q_ref, k_ref, v_ref):
    a, q, k, v = _in_proj_even(x_ref[...], g_ref[...], w_ref[...], qg_ref[...], kg_ref[...], ho_ref[...])
    q_ref[...] = q
    k_ref[...] = k
    v_ref[...] = v
    hist = CONV_A - 1
    conv = cb_ref[...] + cw_ref[hist:hist + 1, :] * a
    for j in range(hist):
        conv = conv + cw_ref[j:j + 1, :] * sa_ref[j]
    ao_ref[...] = _silu(_layernorm(conv, lg_ref[...], lb_ref[...])).astype(BF16)
    san_ref[0:hist - 1] = sa_ref[1:hist]
    san_ref[hist - 1] = a


def _sample_even(x, g, w, qg, kg, head_ones, sa, cw, cb, lg, lb):
    m = x.shape[0]
    return pl.pallas_call(
        _sample_even_body,
        out_shape=[
            jax.ShapeDtypeStruct((m, D_A), BF16),
            jax.ShapeDtypeStruct(sa.shape, F32),
            jax.ShapeDtypeStruct((m, D_B), F32),
            jax.ShapeDtypeStruct((m, D_B), F32),
            jax.ShapeDtypeStruct((m, D_B), F32),
        ],
        compiler_params=pltpu.CompilerParams(vmem_limit_bytes=VMEM_LIMIT),
        name="sample_even",
    )(x, g, w, qg, kg, head_ones, sa, cw, cb, lg, lb)


def _moba_sample_keys_body(pt_ref, q_ref, qdh_ref, kn_ref, rbt_ref, *refs, n_pages, page):
    del pt_ref
    outs = refs[SEQ_PER_STEP * n_pages:]
    bias_ref = outs[4]
    past = n_pages * page

    @pl.when(pl.program_id(0) == 0)
    def _():
        pos = lax.broadcasted_iota(jnp.int32, (N_HEADS, past), 1)
        dist = past - pos
        out = jnp.zeros((N_HEADS, past), F32) + rbt_ref[:, 0:1]
        for i, th in enumerate(T5_THRESH):
            out = jnp.where(dist >= th, rbt_ref[:, i + 1:i + 2], out)
        bias_ref[...] = out * LOG2E

    for sq in range(SEQ_PER_STEP):
        _moba_sample_keys_one(sq, q_ref, qdh_ref, kn_ref, rbt_ref, refs[sq * n_pages:(sq + 1) * n_pages], *outs,
                              n_pages=n_pages, page=page)


def _moba_sample_keys_one(sq, q_ref, qdh_ref, kn_ref, rbt_ref, k_pages, pc_ref, pn_ref, l_ref, sel_ref,
                          bias_ref, qb_ref, s_ref, *, n_pages, page):
    past = n_pages * page
    n_past_blocks = past // MOBA_BLOCK
    n_sel = min(MOBA_TOPK, n_past_blocks)
    qb_ref = qb_ref.at[sq]
    s_ref = s_ref.at[sq]

    head_of_lane = lax.broadcasted_iota(jnp.int32, (N_HEADS, D_B), 1) // HEAD_DIM
    hmask = head_of_lane == lax.broadcasted_iota(jnp.int32, (N_HEADS, D_B), 0)
    qbd = jnp.where(hmask, q_ref[sq], 0.0)

    qdh = qdh_ref[sq]
    for h in range(N_HEADS):
        qb_ref[h] = jnp.broadcast_to(qdh[:, h:h + 1], (HEAD_DIM, page))

    for pg in range(n_pages):
        rows = [jnp.sum(k_pages[pg][0, h] * qb_ref[h], axis=0, keepdims=True) for h in range(N_HEADS)]
        s_ref[:, pg * page:(pg + 1) * page] = jnp.concatenate(rows, axis=0)

    gates = [jnp.sum(s_ref[:, j * MOBA_BLOCK:(j + 1) * MOBA_BLOCK], axis=-1, keepdims=True)
             for j in range(n_past_blocks)]
    ranks = []
    pieces = []
    for j in range(n_past_blocks):
        rank = jnp.zeros((N_HEADS, 1), jnp.int32)
        for i in range(n_past_blocks):
            if i == j:
                continue
            ahead = (gates[i] >= gates[j]) if i < j else (gates[i] > gates[j])
            rank = rank + jnp.where(ahead, 1, 0)
        ranks.append(rank)
        lanes = slice(j * MOBA_BLOCK, (j + 1) * MOBA_BLOCK)
        pieces.append(s_ref[:, lanes] + bias_ref[:, lanes] + jnp.where(rank < n_sel, 0.0, -jnp.inf))
    s = jnp.concatenate(pieces, axis=-1)

    s_new = jnp.sum(qbd * kn_ref[sq], axis=-1, keepdims=True) + rbt_ref[:, 0:1] * LOG2E
    m = jnp.maximum(jnp.max(s, axis=-1, keepdims=True), s_new)
    p = jnp.exp2(s - m)
    p_new = jnp.exp2(s_new - m)
    pn_ref[sq] = jnp.broadcast_to(p_new, (N_HEADS, 128))
    l_ref[sq] = jnp.broadcast_to(jnp.sum(p, axis=-1, keepdims=True) + p_new, (N_HEADS, 128))

    lane = lax.broadcasted_iota(jnp.int32, (N_HEADS, 128), 1)
    sel = jnp.zeros((N_HEADS, 128), jnp.int32)
    for r in range(n_sel):
        picked = jnp.zeros((N_HEADS, MOBA_BLOCK), F32)
        block_id = jnp.zeros((N_HEADS, 1), jnp.int32)
        for j in range(n_past_blocks):
            picked = picked + jnp.where(ranks[j] == r, p[:, j * MOBA_BLOCK:(j + 1) * MOBA_BLOCK], 0.0)
            block_id = block_id + jnp.where(ranks[j] == r, j, 0)
        pc_ref[sq, :, r * MOBA_BLOCK:(r + 1) * MOBA_BLOCK] = picked
        sel = jnp.where(lane == r, block_id, sel)
    sel_ref[sq] = sel


def _moba_sample_values_body(pt_ref, sel_ref, pc_ref, pn_ref, l_ref, vn_ref, v_hbm, o_ref, buf_ref, sem_ref,
                             *, n_sel, ppb, page):
    b = pl.program_id(0)
    n_tiles = n_sel * ppb

    def tile_copy(seq, slot, h, i):
        block = sel_ref[seq * (N_HEADS * n_sel) + h * n_sel + i // ppb]
        pool_page = pt_ref[seq, block * ppb + i % ppb]
        return pltpu.make_async_copy(v_hbm.at[pool_page, h], buf_ref.at[slot, h * n_tiles + i], sem_ref.at[slot])

    def for_all_tiles(seq, slot, act):
        for h in range(N_HEADS):
            for i in range(n_tiles):
                act(tile_copy(seq, slot, h, i))

    @pl.when(b == 0)
    def _():
        for_all_tiles(0, 0, lambda cp: cp.start())

    @pl.when(b + 1 < pl.num_programs(0))
    def _():
        for_all_tiles(b + 1, (b + 1) % 2, lambda cp: cp.start())

    slot = b % 2
    for_all_tiles(b, slot, lambda cp: cp.wait())

    head_of_lane = lax.broadcasted_iota(jnp.int32, (N_HEADS, D_B), 1) // HEAD_DIM
    hmask = head_of_lane == lax.broadcasted_iota(jnp.int32, (N_HEADS, D_B), 0)
    ones = jnp.ones((SUBLANES, page), BF16)
    lane_sum = lambda x: lax.dot_general(ones, x, (((1,), (1,)), ((), ())), preferred_element_type=F32)
    outs = []
    for h in range(N_HEADS):
        acc = jnp.zeros((HEAD_DIM, page), F32)
        for i in range(n_tiles):
            acc = acc + buf_ref[slot, h * n_tiles + i] * pc_ref[0, h:h + 1, i * page:(i + 1) * page]
        hi = acc.astype(BF16)
        lo = (acc - hi.astype(F32)).astype(BF16)
        outs.append((lane_sum(hi) + lane_sum(lo))[0:1, :])
    o_past = jnp.concatenate(outs, axis=-1)
    spread = lambda col: jnp.sum(jnp.where(hmask, col, 0.0), axis=0, keepdims=True)
    o_ref[0] = (o_past + spread(pn_ref[0, :, 0:1]) * vn_ref[0]) / spread(l_ref[0, :, 0:1])


def _moba_sample(page_table, q, k_new, v_new, rel_bias_t, cache_kt, cache_vt):
    m, n_pages = page_table.shape
    page = cache_kt.shape[-1]
    past = n_pages * page
    n_sel = min(MOBA_TOPK, past // MOBA_BLOCK)
    ppb = MOBA_BLOCK // page
    row = lambda bi, *_: (bi, 0, 0)

    sps = SEQ_PER_STEP
    assert m % sps == 0
    keys_spec = pltpu.PrefetchScalarGridSpec(
        num_scalar_prefetch=1,
        grid=(m // sps,),
        in_specs=[
            pl.BlockSpec((sps, 1, D_B), row),
            pl.BlockSpec((sps, HEAD_DIM, N_HEADS), row),
            pl.BlockSpec((sps, 1, D_B), row),
            pl.BlockSpec((N_HEADS, NUM_BUCKETS), lambda bi, pt: (0, 0)),
        ] + [pl.BlockSpec((1, N_HEADS, HEAD_DIM, page), lambda bi, pt, sq=sq, pg=pg: (pt[bi * sps + sq, pg], 0, 0, 0))
             for sq in range(sps) for pg in range(n_pages)],
        out_specs=[
            pl.BlockSpec((sps, N_HEADS, n_sel * MOBA_BLOCK), row),
            pl.BlockSpec((sps, N_HEADS, 128), row),
            pl.BlockSpec((sps, N_HEADS, 128), row),
            pl.BlockSpec((sps, N_HEADS, 128), row),
        ],
        scratch_shapes=[
            pltpu.VMEM((N_HEADS, past), F32),
            pltpu.VMEM((sps, N_HEADS, HEAD_DIM, page), F32),
            pltpu.VMEM((sps, N_HEADS, past), F32),
        ],
    )
    q_dh = q.reshape(m, N_HEADS, HEAD_DIM).transpose(0, 2, 1)
    pc, pn, l, sel = pl.pallas_call(
        functools.partial(_moba_sample_keys_body, n_pages=n_pages, page=page),
        grid_spec=keys_spec,
        out_shape=[
            jax.ShapeDtypeStruct((m, N_HEADS, n_sel * MOBA_BLOCK), F32),
            jax.ShapeDtypeStruct((m, N_HEADS, 128), F32),
            jax.ShapeDtypeStruct((m, N_HEADS, 128), F32),
            jax.ShapeDtypeStruct((m, N_HEADS, 128), jnp.int32),
        ],
        compiler_params=_cparams(1),
        name="sample_moba_keys",
    )(page_table, q.reshape(m, 1, D_B), q_dh, k_new.reshape(m, 1, D_B), rel_bias_t, *([cache_kt] * (sps * n_pages)))

    sel_flat = sel[:, :, :n_sel].reshape(m * N_HEADS * n_sel)
    values_spec = pltpu.PrefetchScalarGridSpec(
        num_scalar_prefetch=2,
        grid=(m,),
        in_specs=[
            pl.BlockSpec((1, N_HEADS, n_sel * MOBA_BLOCK), row),
            pl.BlockSpec((1, N_HEADS, 128), row),
            pl.BlockSpec((1, N_HEADS, 128), row),
            pl.BlockSpec((1, 1, D_B), row),
            pl.BlockSpec(memory_space=pl.ANY),
        ],
        out_specs=pl.BlockSpec((1, 1, D_B), row),
        scratch_shapes=[
            pltpu.VMEM((2, N_HEADS * n_sel * ppb, HEAD_DIM, page), F32),
            pltpu.SemaphoreType.DMA((2,)),
        ],
    )
    return pl.pallas_call(
        functools.partial(_moba_sample_values_body, n_sel=n_sel, ppb=ppb, page=page),
        grid_spec=values_spec,
        out_shape=jax.ShapeDtypeStruct((m, 1, D_B), F32),
        compiler_params=_cparams(1),
        name="sample_moba_values",
    )(page_table, sel_flat, pc, pn, l, v_new.reshape(m, 1, D_B), cache_vt)


def _sample_ffn(x, st_ref, g_ref, wu_ref, cw_ref, cb_ref, wd_ref, stn_ref):
    h = _rmsnorm(x, g_ref[...]).astype(BF16)
    up = jnp.dot(h, wu_ref[0], preferred_element_type=F32)
    older, newer = st_ref[0, :, 0, :], st_ref[0, :, 1, :]
    conv = cw_ref[0:1, :] * older + cw_ref[1:2, :] * newer + cw_ref[2:3, :] * up + cb_ref[...]
    act = (_silu(conv[:, :D_FF]) * conv[:, D_FF:]).astype(BF16)
    stn_ref[:, 0, :] = newer
    stn_ref[:, 1, :] = up
    return x + jnp.dot(act, wd_ref[0], preferred_element_type=F32)


def _sample_out_ffn_body(ao_ref, o_ref, x_ref, wo_ref, st_ref, g_ref, wu_ref, cw_ref, cb_ref, wd_ref,
                         y_ref, stn_ref):
    x1 = (x_ref[...]
          + jnp.dot(ao_ref[...], wo_ref[0:D_A, :], preferred_element_type=F32)
          + jnp.dot(o_ref[...].astype(BF16), wo_ref[D_A:D_A + D_B, :], preferred_element_type=F32))
    y_ref[...] = _sample_ffn(x1, st_ref, g_ref, wu_ref, cw_ref, cb_ref, wd_ref, stn_ref)


def _whole(a):
    return _const_spec(a.shape)


def _whole_out(shape):
    nd = len(shape)
    return pl.BlockSpec(tuple(shape), lambda *_: (0,) * nd)


def _sample_out_ffn(ao, o, x, wo, st, g, wu, cw, cb, wd, layer):
    st_shape = st.shape[1:]
    return pl.pallas_call(
        _sample_out_ffn_body,
        grid=(1,),
        in_specs=[_whole(ao), _whole(o), _whole(x), _whole(wo), _layer_spec(st_shape, layer), _whole(g),
                  _layer_spec(wu.shape[1:], layer), _whole(cw), _whole(cb), _layer_spec(wd.shape[1:], layer)],
        out_specs=[_whole_out(x.shape), _whole_out(st_shape)],
        out_shape=[jax.ShapeDtypeStruct(x.shape, F32), jax.ShapeDtypeStruct(st_shape, F32)],
        compiler_params=_cparams(1),
        name="sample_out_ffn",
    )(ao, o, x, wo, st, g, wu, cw, cb, wd)


def _sample_odd_body(x_ref, gm_ref, wi_ref, ccw_ref, sc_ref, wo_ref, st_ref, g_ref, wu_ref, cw_ref, cb_ref,
                     wd_ref, y_ref, scn_ref, stn_ref):
    x = x_ref[...]
    h = _rmsnorm(x, gm_ref[...]).astype(BF16)
    z = jnp.dot(h, wi_ref[...], preferred_element_type=F32)
    gcu = z[:, D_C:2 * D_C] * z[:, 2 * D_C:]
    older, newer = sc_ref[:, 0, :], sc_ref[:, 1, :]
    conv = ccw_ref[0:1, :] * older + ccw_ref[1:2, :] * newer + ccw_ref[2:3, :] * gcu
    scn_ref[:, 0, :] = newer
    scn_ref[:, 1, :] = gcu
    x1 = x + jnp.dot((z[:, :D_C] * conv).astype(BF16), wo_ref[...], preferred_element_type=F32)
    y_ref[...] = _sample_ffn(x1, st_ref, g_ref, wu_ref, cw_ref, cb_ref, wd_ref, stn_ref)


def _sample_odd(x, gm, wi, ccw, sc, wo, st, g, wu, cw, cb, wd, layer):
    st_shape = st.shape[1:]
    return pl.pallas_call(
        _sample_odd_body,
        grid=(1,),
        in_specs=[_whole(x), _whole(gm), _whole(wi), _whole(ccw), _whole(sc), _whole(wo),
                  _layer_spec(st_shape, layer), _whole(g), _layer_spec(wu.shape[1:], layer), _whole(cw), _whole(cb),
                  _layer_spec(wd.shape[1:], layer)],
        out_specs=[_whole_out(x.shape), _whole_out(sc.shape), _whole_out(st_shape)],
        out_shape=[jax.ShapeDtypeStruct(x.shape, F32), jax.ShapeDtypeStruct(sc.shape, F32),
                   jax.ShapeDtypeStruct(st_shape, F32)],
        compiler_params=_cparams(1),
        name="sample_odd",
    )(x, gm, wi, ccw, sc, wo, st, g, wu, cw, cb, wd)


def kernel(x_prompt, x_sample, cache_k, cache_v, state_conv_a, state_conv_c, state_ffn, page_table, rel_bias,
           norm_mix_e, w_in_e, conv_a_w, conv_a_b, ln_a_g, ln_a_b, q_norm_g, k_norm_g, w_out_e,
           norm_mix_o, w_in_o, conv_c_w, w_out_o, norm_ffn, w_up, conv_f_w, conv_f_b, w_down):
    b, t, d = x_prompt.shape
    m = x_sample.shape[0]
    n_pool, page = cache_k.shape[1], cache_k.shape[2]
    n_pages = page_table.shape[1]
    assert norm_mix_e.shape[0] == 1 and norm_mix_o.shape[0] == 1 and norm_ffn.shape[0] == 2
    assert x_sample.shape[1] == 1 and t % TILE_M == 0 and t % page == 0
    assert (n_pages * page) % MOBA_BLOCK == 0 and MOBA_BLOCK % page == 0

    row = lambda v: v.reshape(1, -1)
    w_in_e_b = w_in_e[0].astype(BF16)
    w_out_e_b = w_out_e[0].astype(BF16)
    w_in_o_b = w_in_o[0].astype(BF16)
    w_out_o_b = w_out_o[0].astype(BF16)
    w_up_b = w_up.astype(BF16)
    w_down_b = w_down.astype(BF16)
    qg = row(jnp.tile(q_norm_g[0], N_HEADS))
    kg = row(jnp.tile(k_norm_g[0], N_HEADS))
    lane_head = jnp.arange(D_B, dtype=jnp.int32) // HEAD_DIM
    head_ones = (lane_head[:, None] == lane_head[None, :]).astype(BF16)
    g_e, g_o = row(norm_mix_e[0]), row(norm_mix_o[0])
    cab, lag, lab = row(conv_a_b[0]), row(ln_a_g[0]), row(ln_a_b[0])

    a_p, k_p, v_p, qt, kh, vt, km = _prompt_inproj(x_prompt, g_e, w_in_e_b, qg, kg, head_ones, page)
    nb = t // MOBA_BLOCK
    kmh = km.reshape(b, nb, N_HEADS, HEAD_DIM).transpose(0, 2, 1, 3)
    ot = _moba_prompt(rel_bias, qt, kh, vt, kmh)
    x1 = _prompt_mix_out(a_p, conv_a_w[0], cab, lag, lab, ot, x_prompt, w_out_e_b)
    x2, f0 = _prompt_ffn(x1, row(norm_ffn[0]), w_up_b, conv_f_w[0], row(conv_f_b[0]), w_down_b, 0)
    x3, c_st = _prompt_mixc(x2, g_o, w_in_o_b, conv_c_w[0], w_out_o_b)
    y_prompt, f1 = _prompt_ffn(x3, row(norm_ffn[1]), w_up_b, conv_f_w[1], row(conv_f_b[1]), w_down_b, 1)

    k_prompt = k_p.transpose(0, 1, 4, 2, 3)[None]
    v_prompt = v_p.transpose(0, 1, 4, 2, 3)[None]
    a_prompt = a_p[:, t - (CONV_A - 1):, :][None]
    c_prompt = c_st[:, SUBLANES - (CONV_C - 1):, :][None]
    f_prompt = jnp.stack([f0[:, SUBLANES - (CONV_F - 1):, :], f1[:, SUBLANES - (CONV_F - 1):, :]])

    xs = x_sample.reshape(m, d)
    sa = state_conv_a[0].transpose(1, 0, 2)
    sc = state_conv_c[0]
    sf = state_ffn
    ao_s, sa_new, q_s, k_s, v_s = _sample_even(xs, g_e, w_in_e_b, qg, kg, head_ones, sa, conv_a_w[0], cab, lag, lab)
    o_s = _moba_sample(page_table, q_s, k_s, v_s, rel_bias.T,
                       cache_k[0].transpose(0, 2, 3, 1), cache_v[0].transpose(0, 2, 3, 1))
    xs1, sf0 = _sample_out_ffn(ao_s, o_s.reshape(m, D_B), xs, w_out_e_b, sf, row(norm_ffn[0]), w_up_b,
                               conv_f_w[0], row(conv_f_b[0]), w_down_b, 0)
    ys, sc_new, sf1 = _sample_odd(xs1, g_o, w_in_o_b, conv_c_w[0], sc, w_out_o_b, sf, row(norm_ffn[1]),
                                  w_up_b, conv_f_w[1], row(conv_f_b[1]), w_down_b, 1)

    y_sample = ys.reshape(m, 1, d)
    k_sample = k_s.reshape(1, m, 1, N_HEADS, HEAD_DIM)
    v_sample = v_s.reshape(1, m, 1, N_HEADS, HEAD_DIM)
    a_sample = sa_new.transpose(1, 0, 2)[None]
    c_sample = sc_new[None]
    f_sample = jnp.stack([sf0, sf1])
    return (y_prompt, y_sample, k_prompt, v_prompt, a_prompt, c_prompt, f_prompt,
            k_sample, v_sample, a_sample, c_sample, f_sample)
```

```python
import functools
import math

import jax
import jax.numpy as jnp
from jax import lax
from jax.experimental import pallas as pl
from jax.experimental.pallas import tpu as pltpu

F32 = jnp.float32
BF16 = jnp.bfloat16

EPS = 1e-6
D_MODEL = 1024
D_A = 512
CONV_A = 31
N_HEADS = 8
HEAD_DIM = 64
D_B = N_HEADS * HEAD_DIM
MOBA_BLOCK = 256
MOBA_TOPK = 3
NUM_BUCKETS = 32
MAX_DISTANCE = 128
D_C = 1024
CONV_C = 3
D_FF = 2816
CONV_F = 3
LOG2E = math.log2(math.e)
Q_SCALE = HEAD_DIM ** -0.5 * LOG2E
VT_ROWS = HEAD_DIM + 16
MAX_SHIFT_RANGE = 100.0

TILE_M = 512
CONV_ROWS = 64
CONV_LANES = 256
HALO_A = 32
HEAD_GROUP = 8
SOFTMAX_ROWS = 64
SEQ_PER_STEP = 2
FAR_UNROLL = 2
CH = 256
SUBLANES = 8
VMEM_LIMIT = 56 * 1024 * 1024


def _t5_thresholds():
    max_exact = NUM_BUCKETS // 2
    th = list(range(1, max_exact + 1))
    for k in range(1, NUM_BUCKETS - max_exact):
        th.append(math.ceil(max_exact * (MAX_DISTANCE / max_exact) ** (k / (NUM_BUCKETS - max_exact))))
    return tuple(th)


T5_THRESH = _t5_thresholds()


def _cparams(n_grid):
    return pltpu.CompilerParams(dimension_semantics=("arbitrary",) * n_grid, vmem_limit_bytes=VMEM_LIMIT)


def _const_spec(shape):
    nd = len(shape)
    return pl.BlockSpec(shape, lambda *_: (0,) * nd, pipeline_mode=pl.Buffered(1))


def _layer_spec(shape, layer):
    nd = len(shape)
    return pl.BlockSpec((1,) + tuple(shape), lambda *_: (layer,) + (0,) * nd, pipeline_mode=pl.Buffered(1))


def _rmsnorm(x, g):
    return x * lax.rsqrt(jnp.mean(x * x, axis=-1, keepdims=True) + EPS) * g


def _split_dot(x, w_bf16):
    hi = x.astype(BF16)
    lo = (x - hi.astype(F32)).astype(BF16)
    return (jnp.dot(hi, w_bf16, preferred_element_type=F32)
            + jnp.dot(lo, w_bf16, preferred_element_type=F32))


def _head_rmsnorm(x, g, head_ones):
    ss = _split_dot(x * x, head_ones)
    return x * lax.rsqrt(ss * (1.0 / HEAD_DIM) + EPS) * g


def _silu(x):
    return x * jax.nn.sigmoid(x)


def _layernorm(x, g, b):
    mu = jnp.mean(x, axis=-1, keepdims=True)
    xc = x - mu
    var = jnp.mean(xc * xc, axis=-1, keepdims=True)
    return xc * lax.rsqrt(var + EPS) * g + b


def _in_proj_even(x, g, w, qg, kg, head_ones):
    h = _rmsnorm(x, g).astype(BF16)
    z = jnp.dot(h, w, preferred_element_type=F32)
    a = z[:, :D_A] * jax.nn.sigmoid(z[:, D_A:2 * D_A])
    q = _head_rmsnorm(z[:, 2 * D_A:2 * D_A + D_B], qg, head_ones) * Q_SCALE
    k = _head_rmsnorm(z[:, 2 * D_A + D_B:2 * D_A + 2 * D_B], kg, head_ones)
    v = z[:, 2 * D_A + 2 * D_B:]
    return a, q, k, v


def _prompt_inproj_body(x_ref, g_ref, w_ref, qg_ref, kg_ref, ho_ref,
                        a_ref, kp_ref, vp_ref, qt_ref, kh_ref, vt_ref, km_ref, *, tm, page):
    a, q, k, v = _in_proj_even(x_ref[0], g_ref[...], w_ref[...], qg_ref[...], kg_ref[...], ho_ref[...])
    a_ref[0] = a
    qt = q.T
    kt = k.T
    vt = v.T
    for pg in range(tm // page):
        kp_ref[0, pg] = kt[:, pg * page:(pg + 1) * page].reshape(N_HEADS, HEAD_DIM, page)
        vp_ref[0, pg] = vt[:, pg * page:(pg + 1) * page].reshape(N_HEADS, HEAD_DIM, page)
    kb = k.astype(BF16)
    pad_row = lax.broadcasted_iota(jnp.int32, (N_HEADS, VT_ROWS - HEAD_DIM, MOBA_BLOCK), 1)
    ones_rows = jnp.where(pad_row == 0, 1.0, 0.0).astype(BF16)
    for i in range(tm // MOBA_BLOCK):
        r0 = i * MOBA_BLOCK
        qt_ref[0, :, i] = qt[:, r0:r0 + MOBA_BLOCK].reshape(N_HEADS, HEAD_DIM, MOBA_BLOCK).astype(BF16)
        vt_ref[0, :, i, 0:HEAD_DIM, :] = (
            vt[:, r0:r0 + MOBA_BLOCK].reshape(N_HEADS, HEAD_DIM, MOBA_BLOCK).astype(BF16))
        vt_ref[0, :, i, HEAD_DIM:VT_ROWS, :] = ones_rows
        for hh in range(N_HEADS):
            kh_ref[0, hh, i] = kb[r0:r0 + MOBA_BLOCK, hh * HEAD_DIM:(hh + 1) * HEAD_DIM]
        km_ref[0, i] = jnp.mean(k[r0:r0 + MOBA_BLOCK], axis=0, keepdims=True)


def _prompt_inproj(x, g, w, qg, kg, head_ones, page):
    b, t, d = x.shape
    tm = TILE_M
    nt = t // tm
    nb = t // MOBA_BLOCK
    bpt = tm // MOBA_BLOCK
    ppt = tm // page
    n_out = w.shape[1]
    tok = lambda bi, ti: (bi, ti, 0)
    blk5 = lambda bi, ti: (bi, 0, ti, 0, 0)
    pages = lambda bi, ti: (bi, ti, 0, 0, 0)
    return pl.pallas_call(
        functools.partial(_prompt_inproj_body, tm=tm, page=page),
        grid=(b, nt),
        in_specs=[
            pl.BlockSpec((1, tm, d), tok),
            _const_spec((1, d)),
            _const_spec((d, n_out)),
            _const_spec((1, D_B)),
            _const_spec((1, D_B)),
            _const_spec((D_B, D_B)),
        ],
        out_specs=[
            pl.BlockSpec((1, tm, D_A), tok),
            pl.BlockSpec((1, ppt, N_HEADS, HEAD_DIM, page), pages),
            pl.BlockSpec((1, ppt, N_HEADS, HEAD_DIM, page), pages),
            pl.BlockSpec((1, N_HEADS, bpt, HEAD_DIM, MOBA_BLOCK), blk5),
            pl.BlockSpec((1, N_HEADS, bpt, MOBA_BLOCK, HEAD_DIM), blk5),
            pl.BlockSpec((1, N_HEADS, bpt, VT_ROWS, MOBA_BLOCK), blk5),
            pl.BlockSpec((1, bpt, 1, D_B), lambda bi, ti: (bi, ti, 0, 0)),
        ],
        out_shape=[
            jax.ShapeDtypeStruct((b, t, D_A), F32),
            jax.ShapeDtypeStruct((b, t // page, N_HEADS, HEAD_DIM, page), F32),
            jax.ShapeDtypeStruct((b, t // page, N_HEADS, HEAD_DIM, page), F32),
            jax.ShapeDtypeStruct((b, N_HEADS, nb, HEAD_DIM, MOBA_BLOCK), BF16),
            jax.ShapeDtypeStruct((b, N_HEADS, nb, MOBA_BLOCK, HEAD_DIM), BF16),
            jax.ShapeDtypeStruct((b, N_HEADS, nb, VT_ROWS, MOBA_BLOCK), BF16),
            jax.ShapeDtypeStruct((b, nb, 1, D_B), F32),
        ],
        compiler_params=_cparams(2),
        name="prompt_inproj",
    )(x, g, w, qg, kg, head_ones)


def _t5_bias_scalar_table(dist, rb_ref, h):
    out = jnp.full(dist.shape, rb_ref[0, h] * LOG2E, F32)
    for i, th in enumerate(T5_THRESH):
        out = jnp.where(dist >= th, rb_ref[i + 1, h] * LOG2E, out)
    return out


def _moba_prompt_body(rb_ref, ctl_ref, qt_ref, kh_ref, vt_ref, km_ref, ot_ref,
                      bdiag_ref, bsub_ref, selb_ref, m_ref, alpha_ref, acc_ref, s_ref, p_ref, *, nb, hg):
    h0 = pl.program_id(1) * hg
    n = pl.program_id(2)
    blk = MOBA_BLOCK
    bounded = ctl_ref[1, 0] > 0.5
    shifts = [jnp.where(bounded, ctl_ref[0, h0 + g], 0.0) for g in range(hg)]

    @pl.when(n == 0)
    def _():
        ki = lax.broadcasted_iota(jnp.int32, (blk, blk), 0)
        qi = lax.broadcasted_iota(jnp.int32, (blk, blk), 1)
        d0 = qi - ki
        for g in range(hg):
            bdiag_ref[g] = jnp.where(d0 >= 0, _t5_bias_scalar_table(jnp.maximum(d0, 0), rb_ref, h0 + g),
                                     -jnp.inf) - shifts[g]
            bsub_ref[g] = _t5_bias_scalar_table(d0 + blk, rb_ref, h0 + g) - shifts[g]

    bi = lax.broadcasted_iota(jnp.int32, (nb, blk), 0)
    qts = [qt_ref[0, g, 0] for g in range(hg)]

    def select(g):
        km = km_ref[0, g]
        km_hi = km.astype(BF16)
        km_lo = (km - km_hi.astype(F32)).astype(BF16)
        gate = (jnp.dot(km_hi, qts[g], preferred_element_type=F32)
                + jnp.dot(km_lo, qts[g], preferred_element_type=F32))
        avail = jnp.where(bi < n, 1.0, 0.0)
        far_bias = rb_ref[NUM_BUCKETS - 1, h0 + g] * LOG2E - shifts[g]
        selb = jnp.full((nb, blk), -jnp.inf, F32)
        for _ in range(MOBA_TOPK):
            gm = jnp.where(avail > 0.0, gate, -jnp.inf)
            top = jnp.max(gm, axis=0, keepdims=True)
            first = jnp.where(avail > 0.0, jnp.where(gm == top, bi, nb), nb)
            pick = bi == jnp.min(first, axis=0, keepdims=True)
            selb = jnp.where(pick, jnp.where(bi == n - 1, 0.0, far_bias), selb)
            avail = jnp.where(pick, 0.0, avail)
        selb_ref[g] = selb

    n_chunks = blk // SOFTMAX_ROWS

    def logits(g, j, slot, bias):
        s = jnp.dot(kh_ref[0, g, j], qts[g], preferred_element_type=F32)
        s_ref[slot, g] = s if bias is None else s + bias

    def chunk(g, slot, c):
        return s_ref[slot, g, c * SOFTMAX_ROWS:(c + 1) * SOFTMAX_ROWS, :].reshape(
            SOFTMAX_ROWS // SUBLANES, SUBLANES, blk)

    def softmax(g, slot, row):
        m8 = jnp.max(chunk(g, slot, 0), axis=0)
        for c in range(1, n_chunks):
            m8 = jnp.maximum(m8, jnp.max(chunk(g, slot, c), axis=0))
        m_blk = jnp.max(m8, axis=0, keepdims=True)
        if row is None:
            mn = m_blk
            shift = mn
        else:
            m_old = m_ref[g]
            mn = jnp.maximum(m_old, m_blk + row)
            alpha_ref[slot, g] = jnp.exp2(m_old - mn)
            shift = mn - row
        for c in range(n_chunks):
            p = jnp.exp2(chunk(g, slot, c) - shift)
            p_ref[slot, g, c * SOFTMAX_ROWS:(c + 1) * SOFTMAX_ROWS, :] = (
                p.reshape(SOFTMAX_ROWS, blk).astype(BF16))
        m_ref[g] = mn

    def weighted_values(g, j, slot, first):
        pv = jnp.dot(vt_ref[0, g, j], p_ref[slot, g], preferred_element_type=F32)
        acc_ref[g] = pv if first else alpha_ref[slot, g] * acc_ref[g] + pv

    def far_trips(trip):
        def far(i, carry):
            for r in range(FAR_UNROLL):
                trip(FAR_UNROLL * i + r, r & 1)
            return carry

        n_far = n - 1
        lax.fori_loop(0, n_far // FAR_UNROLL, far, 0)
        done = (n_far // FAR_UNROLL) * FAR_UNROLL
        piece = FAR_UNROLL // 2
        while piece >= 1:
            @pl.when(((n_far - done) & piece) != 0)
            def _(piece=piece, done=done):
                for r in range(piece):
                    trip(done + r, r & 1)
            done = done + ((n_far - done) & piece)
            piece //= 2

    last = jnp.where(n >= 2, n - 2, n - 1)

    @pl.when(bounded)
    def _():
        def probabilities(g, j, slot, bias):
            s = jnp.dot(kh_ref[0, g, j], qts[g], preferred_element_type=F32)
            p_ref[slot, g] = jnp.exp2(s + bias).astype(BF16)

        def add_values(g, j, slot, first):
            pv = jnp.dot(vt_ref[0, g, j], p_ref[slot, g], preferred_element_type=F32)
            acc_ref[g] = pv if first else acc_ref[g] + pv

        for g in range(hg):
            probabilities(g, n, 0, bdiag_ref[g])
        for g in range(hg):
            select(g)

        @pl.when(n == 0)
        def _():
            for g in range(hg):
                add_values(g, n, 0, True)

        @pl.when(n >= 1)
        def _():
            for g in range(hg):
                probabilities(g, n - 1, 1, bsub_ref[g] + selb_ref[g, pl.ds(n - 1, 1), :])
            for g in range(hg):
                add_values(g, n, 0, True)

            def trip(j, slot):
                prev = jnp.where(j == 0, n - 1, j - 1)
                for g in range(hg):
                    probabilities(g, j, slot, selb_ref[g, pl.ds(j, 1), :])
                for g in range(hg):
                    add_values(g, prev, 1 - slot, False)

            far_trips(trip)
            for g in range(hg):
                add_values(g, last, n & 1, False)

    @pl.when(jnp.logical_not(bounded))
    def _():
        for g in range(hg):
            logits(g, n, 0, bdiag_ref[g])
        for g in range(hg):
            select(g)
        for g in range(hg):
            logits(g, jnp.maximum(n - 1, 0), 1, bsub_ref[g])
        for g in range(hg):
            softmax(g, 0, None)

        @pl.when(n == 0)
        def _():
            for g in range(hg):
                weighted_values(g, n, 0, True)

        @pl.when(n >= 1)
        def _():
            for g in range(hg):
                weighted_values(g, n, 0, True)
                softmax(g, 1, selb_ref[g, pl.ds(n - 1, 1), :])
                logits(g, 0, 0, None)

            def trip(j, slot):
                nxt = jnp.minimum(j + 1, n - 2)
                prev = jnp.where(j == 0, n - 1, j - 1)
                for g in range(hg):
                    logits(g, nxt, 1 - slot, None)
                for g in range(hg):
                    weighted_values(g, prev, 1 - slot, False)
                for g in range(hg):
                    softmax(g, slot, selb_ref[g, pl.ds(j, 1), :])

            far_trips(trip)
            for g in range(hg):
                weighted_values(g, last, n & 1, False)

    for g in range(hg):
        ot_ref[0, g * HEAD_DIM:(g + 1) * HEAD_DIM, :] = (
            acc_ref[g, 0:HEAD_DIM, :] / acc_ref[g, HEAD_DIM:HEAD_DIM + 1, :])


def _logit_bound(rel_bias, q_gain, k_gain):
    qk = 1.02 * HEAD_DIM * Q_SCALE * jnp.max(jnp.abs(q_gain)) * jnp.max(jnp.abs(k_gain))
    table = rel_bias * LOG2E
    top, low = jnp.max(table, axis=0), jnp.min(table, axis=0)
    usable = jnp.all(2.0 * qk + (top - low) <= MAX_SHIFT_RANGE)
    return jnp.stack([qk + top, jnp.broadcast_to(usable.astype(F32), top.shape)])


def _moba_prompt(rel_bias, ctl, qt, kh, vt, kmh):
    b, nh, nb, hd, blk = qt.shape
    t = nb * blk
    hg = HEAD_GROUP
    grp = lambda bi, hi, ni: (bi, hi, 0, 0, 0)
    return pl.pallas_call(
        functools.partial(_moba_prompt_body, nb=nb, hg=hg),
        grid=(b, nh // hg, nb),
        in_specs=[
            pl.BlockSpec(memory_space=pltpu.SMEM),
            pl.BlockSpec(memory_space=pltpu.SMEM),
            pl.BlockSpec((1, hg, 1, hd, blk), lambda bi, hi, ni: (bi, hi, ni, 0, 0)),
            pl.BlockSpec((1, hg, nb, blk, hd), grp, pipeline_mode=pl.Buffered(1)),
            pl.BlockSpec((1, hg, nb, VT_ROWS, blk), grp, pipeline_mode=pl.Buffered(1)),
            pl.BlockSpec((1, hg, nb, hd), lambda bi, hi, ni: (bi, hi, 0, 0)),
        ],
        out_specs=pl.BlockSpec((1, hg * hd, blk), lambda bi, hi, ni: (bi, hi, ni)),
        out_shape=jax.ShapeDtypeStruct((b, nh * hd, t), F32),
        scratch_shapes=[
            pltpu.VMEM((hg, blk, blk), F32),
            pltpu.VMEM((hg, blk, blk), F32),
            pltpu.VMEM((hg, nb, blk), F32),
            pltpu.VMEM((hg, 1, blk), F32),
            pltpu.VMEM((2, hg, 1, blk), F32),
            pltpu.VMEM((hg, VT_ROWS, blk), F32),
            pltpu.VMEM((2, hg, blk, blk), F32),
            pltpu.VMEM((2, hg, blk, blk), BF16),
        ],
        compiler_params=_cparams(3),
        name="prompt_moba",
    )(rel_bias, ctl, qt, kh, vt, kmh)


def _prompt_mix_out_body(a_ref, halo_ref, cw_ref, cb_ref, lg_ref, lb_ref, ot_ref, x_ref, w_ref,
                         y_ref, sh_ref, conv_ref, wb_ref, *, tm):
    t = pl.program_id(1)
    rows = tm + HALO_A

    @pl.when((pl.program_id(0) == 0) & (t == 0))
    def _():
        for j in range(CONV_A):
            wb_ref[j] = jnp.broadcast_to(cw_ref[j:j + 1, :], (SUBLANES, D_A))

    sh_ref[0, 0:HALO_A] = jnp.where(t == 0, 0.0, halo_ref[0])
    sh_ref[0, HALO_A:rows] = a_ref[0]
    for s in range(1, SUBLANES):
        sh_ref[s, 0:rows - SUBLANES] = sh_ref[0, s:s + rows - SUBLANES]

    groups = CONV_ROWS // SUBLANES
    first_off = HALO_A - (CONV_A - 1)

    def chunk(c, carry):
        r0 = pl.multiple_of(c * CONV_ROWS, CONV_ROWS)
        for l0 in range(0, D_A, CONV_LANES):
            lanes = slice(l0, l0 + CONV_LANES)
            acc = [jnp.zeros((SUBLANES, CONV_LANES), F32) for _ in range(groups)]
            for s in range(SUBLANES):
                offs = [o for o in range(first_off, first_off + CONV_A) if o % SUBLANES == s]
                qs = [o // SUBLANES for o in offs]
                slab = {gq: sh_ref[s, pl.ds(r0 + gq * SUBLANES, SUBLANES), lanes]
                        for gq in range(min(qs), max(qs) + groups)}
                for o, q in zip(offs, qs):
                    w = wb_ref[o - first_off, :, lanes]
                    for g in range(groups):
                        acc[g] = acc[g] + w * slab[g + q]
            for g in range(groups):
                conv_ref[pl.ds(r0 + g * SUBLANES, SUBLANES), lanes] = acc[g]
        return carry

    lax.fori_loop(0, tm // CONV_ROWS, chunk, 0)

    ao = _silu(_layernorm(conv_ref[...] + cb_ref[...], lg_ref[...], lb_ref[...])).astype(BF16)
    o = ot_ref[0].T.astype(BF16)
    y = (jnp.dot(ao, w_ref[0:D_A, :], preferred_element_type=F32)
         + jnp.dot(o, w_ref[D_A:D_A + D_B, :], preferred_element_type=F32))
    y_ref[0] = x_ref[0] + y


def _prompt_mix_out(a, cw, cb, lg, lb, ot, x, w):
    b, t, d = x.shape
    tm = TILE_M
    nt = t // tm
    hpt = tm // HALO_A
    tok = lambda bi, ti: (bi, ti, 0)
    return pl.pallas_call(
        functools.partial(_prompt_mix_out_body, tm=tm),
        grid=(b, nt),
        in_specs=[
            pl.BlockSpec((1, tm, D_A), tok),
            pl.BlockSpec((1, HALO_A, D_A), lambda bi, ti: (bi, jnp.maximum(ti * hpt - 1, 0), 0)),
            _const_spec((CONV_A, D_A)),
            _const_spec((1, D_A)),
            _const_spec((1, D_A)),
            _const_spec((1, D_A)),
            pl.BlockSpec((1, D_B, tm), lambda bi, ti: (bi, 0, ti)),
            pl.BlockSpec((1, tm, d), tok),
            _const_spec((D_A + D_B, d)),
        ],
        out_specs=pl.BlockSpec((1, tm, d), tok),
        out_shape=jax.ShapeDtypeStruct((b, t, d), F32),
        scratch_shapes=[
            pltpu.VMEM((SUBLANES, tm + HALO_A, D_A), F32),
            pltpu.VMEM((tm, D_A), F32),
            pltpu.VMEM((CONV_A, SUBLANES, D_A), F32),
        ],
        compiler_params=_cparams(2),
        name="prompt_mix_out",
    )(a, a, cw, cb, lg, lb, ot, x, w)


def _conv3_chunk(u, ext_ref, carry_ref, cw_ref, c0, tm):
    ext_ref[0:SUBLANES] = carry_ref[:, c0:c0 + CH]
    ext_ref[SUBLANES:SUBLANES + tm] = u
    carry_ref[:, c0:c0 + CH] = u[tm - SUBLANES:tm]
    return (cw_ref[0:1, c0:c0 + CH] * ext_ref[SUBLANES - 2:SUBLANES - 2 + tm]
            + cw_ref[1:2, c0:c0 + CH] * ext_ref[SUBLANES - 1:SUBLANES - 1 + tm]
            + cw_ref[2:3, c0:c0 + CH] * u)


def _prompt_ffn_body(x_ref, g_ref, wu_ref, cw_ref, cb_ref, wd_ref, y_ref, st_ref,
                     carry_ref, extg_ref, extu_ref, act_ref, *, tm):
    @pl.when(pl.program_id(1) == 0)
    def _():
        carry_ref[...] = jnp.zeros_like(carry_ref)

    x = x_ref[0]
    h = _rmsnorm(x, g_ref[...]).astype(BF16)
    for c in range(D_FF // CH):
        cg = c * CH
        cu = D_FF + c * CH
        ug = jnp.dot(h, wu_ref[0, :, cg:cg + CH], preferred_element_type=F32)
        uu = jnp.dot(h, wu_ref[0, :, cu:cu + CH], preferred_element_type=F32)
        gg = _conv3_chunk(ug, extg_ref, carry_ref, cw_ref, cg, tm) + cb_ref[:, cg:cg + CH]
        gu = _conv3_chunk(uu, extu_ref, carry_ref, cw_ref, cu, tm) + cb_ref[:, cu:cu + CH]
        act_ref[:, cg:cg + CH] = (_silu(gg) * gu).astype(BF16)
    st_ref[0] = carry_ref[...]
    y_ref[0] = x + jnp.dot(act_ref[...], wd_ref[0], preferred_element_type=F32)


def _prompt_ffn(x, g, wu, cw, cb, wd, layer):
    b, t, d = x.shape
    tm = TILE_M
    nt = t // tm
    tok = lambda bi, ti: (bi, ti, 0)
    return pl.pallas_call(
        functools.partial(_prompt_ffn_body, tm=tm),
        grid=(b, nt),
        in_specs=[
            pl.BlockSpec((1, tm, d), tok),
            _const_spec((1, d)),
            _layer_spec((d, 2 * D_FF), layer),
            _const_spec((CONV_F, 2 * D_FF)),
            _const_spec((1, 2 * D_FF)),
            _layer_spec((D_FF, d), layer),
        ],
        out_specs=[
            pl.BlockSpec((1, tm, d), tok),
            pl.BlockSpec((1, SUBLANES, 2 * D_FF), lambda bi, ti: (bi, 0, 0)),
        ],
        out_shape=[
            jax.ShapeDtypeStruct((b, t, d), F32),
            jax.ShapeDtypeStruct((b, SUBLANES, 2 * D_FF), F32),
        ],
        scratch_shapes=[
            pltpu.VMEM((SUBLANES, 2 * D_FF), F32),
            pltpu.VMEM((tm + SUBLANES, CH), F32),
            pltpu.VMEM((tm + SUBLANES, CH), F32),
            pltpu.VMEM((tm, D_FF), BF16),
        ],
        compiler_params=_cparams(2),
        name="prompt_ffn",
    )(x, g, wu, cw, cb, wd)


def _prompt_mixc_body(x_ref, g_ref, wi_ref, cw_ref, wo_ref, y_ref, st_ref,
                      carry_ref, ext_ref, z_ref, *, tm):
    @pl.when(pl.program_id(1) == 0)
    def _():
        carry_ref[...] = jnp.zeros_like(carry_ref)

    x = x_ref[0]
    h = _rmsnorm(x, g_ref[...]).astype(BF16)
    for c in range(D_C // CH):
        c0 = c * CH
        gb = jnp.dot(h, wi_ref[:, c0:c0 + CH], preferred_element_type=F32)
        gc = jnp.dot(h, wi_ref[:, D_C + c0:D_C + c0 + CH], preferred_element_type=F32)
        u = jnp.dot(h, wi_ref[:, 2 * D_C + c0:2 * D_C + c0 + CH], preferred_element_type=F32)
        conv = _conv3_chunk(gc * u, ext_ref, carry_ref, cw_ref, c0, tm)
        z_ref[:, c0:c0 + CH] = (gb * conv).astype(BF16)
    st_ref[0] = carry_ref[...]
    y_ref[0] = x + jnp.dot(z_ref[...], wo_ref[...], preferred_element_type=F32)


def _prompt_mixc(x, g, wi, cw, wo):
    b, t, d = x.shape
    tm = TILE_M
    nt = t // tm
    tok = lambda bi, ti: (bi, ti, 0)
    return pl.pallas_call(
        functools.partial(_prompt_mixc_body, tm=tm),
        grid=(b, nt),
        in_specs=[
            pl.BlockSpec((1, tm, d), tok),
            _const_spec((1, d)),
            _const_spec((d, 3 * D_C)),
            _const_spec((CONV_C, D_C)),
            _const_spec((D_C, d)),
        ],
        out_specs=[
            pl.BlockSpec((1, tm, d), tok),
            pl.BlockSpec((1, SUBLANES, D_C), lambda bi, ti: (bi, 0, 0)),
        ],
        out_shape=[
            jax.ShapeDtypeStruct((b, t, d), F32),
            jax.ShapeDtypeStruct((b, SUBLANES, D_C), F32),
        ],
        scratch_shapes=[
            pltpu.VMEM((SUBLANES, D_C), F32),
            pltpu.VMEM((tm + SUBLANES, CH), F32),
            pltpu.VMEM((tm, D_C), BF16),
        ],
        compiler_params=_cparams(2),
        name="prompt_mixc",
    )(x, g, wi, cw, wo)


def _sample_even_body(x_ref, g_ref, w_ref, qg_ref, kg_ref, ho_ref, sa_ref, cw_ref, cb_ref, lg_ref, lb_ref,
                      ao_ref, san_ref, q_ref, k_ref, v_ref):
    a, q, k, v = _in_proj_even(x_ref[...], g_ref[...], w_ref[...], qg_ref[...], kg_ref[...], ho_ref[...])
    q_ref[...] = q
    k_ref[...] = k
    v_ref[...] = v
    hist = CONV_A - 1
    conv = cb_ref[...] + cw_ref[hist:hist + 1, :] * a
    for j in range(hist):
        conv = conv + cw_ref[j:j + 1, :] * sa_ref[j]
    ao_ref[...] = _silu(_layernorm(conv, lg_ref[...], lb_ref[...])).astype(BF16)
    san_ref[0:hist - 1] = sa_ref[1:hist]
    san_ref[hist - 1] = a


def _sample_even(x, g, w, qg, kg, head_ones, sa, cw, cb, lg, lb):
    m = x.shape[0]
    return pl.pallas_call(
        _sample_even_body,
        out_shape=[
            jax.ShapeDtypeStruct((m, D_A), BF16),
            jax.ShapeDtypeStruct(sa.shape, F32),
            jax.ShapeDtypeStruct((m, D_B), F32),
            jax.ShapeDtypeStruct((m, D_B), F32),
            jax.ShapeDtypeStruct((m, D_B), F32),
        ],
        compiler_params=pltpu.CompilerParams(vmem_limit_bytes=VMEM_LIMIT),
        name="sample_even",
    )(x, g, w, qg, kg, head_ones, sa, cw, cb, lg, lb)


def _moba_sample_keys_body(pt_ref, q_ref, qdh_ref, kn_ref, rbt_ref, *refs, n_pages, page):
    del pt_ref
    outs = refs[SEQ_PER_STEP * n_pages:]
    bias_ref = outs[4]
    past = n_pages * page

    @pl.when(pl.program_id(0) == 0)
    def _():
        pos = lax.broadcasted_iota(jnp.int32, (N_HEADS, past), 1)
        dist = past - pos
        out = jnp.zeros((N_HEADS, past), F32) + rbt_ref[:, 0:1]
        for i, th in enumerate(T5_THRESH):
            out = jnp.where(dist >= th, rbt_ref[:, i + 1:i + 2], out)
        bias_ref[...] = out * LOG2E

    for sq in range(SEQ_PER_STEP):
        _moba_sample_keys_one(sq, q_ref, qdh_ref, kn_ref, rbt_ref, refs[sq * n_pages:(sq + 1) * n_pages], *outs,
                              n_pages=n_pages, page=page)


def _moba_sample_keys_one(sq, q_ref, qdh_ref, kn_ref, rbt_ref, k_pages, pc_ref, pn_ref, l_ref, sel_ref,
                          bias_ref, qb_ref, s_ref, *, n_pages, page):
    past = n_pages * page
    n_past_blocks = past // MOBA_BLOCK
    n_sel = min(MOBA_TOPK, n_past_blocks)
    qb_ref = qb_ref.at[sq]
    s_ref = s_ref.at[sq]

    head_of_lane = lax.broadcasted_iota(jnp.int32, (N_HEADS, D_B), 1) // HEAD_DIM
    hmask = head_of_lane == lax.broadcasted_iota(jnp.int32, (N_HEADS, D_B), 0)
    qbd = jnp.where(hmask, q_ref[sq], 0.0)

    qdh = qdh_ref[sq]
    for h in range(N_HEADS):
        qb_ref[h] = jnp.broadcast_to(qdh[:, h:h + 1], (HEAD_DIM, page))

    for pg in range(n_pages):
        rows = [jnp.sum(k_pages[pg][0, h] * qb_ref[h], axis=0, keepdims=True) for h in range(N_HEADS)]
        s_ref[:, pg * page:(pg + 1) * page] = jnp.concatenate(rows, axis=0)

    gates = [jnp.sum(s_ref[:, j * MOBA_BLOCK:(j + 1) * MOBA_BLOCK], axis=-1, keepdims=True)
             for j in range(n_past_blocks)]
    ranks = []
    pieces = []
    for j in range(n_past_blocks):
        rank = jnp.zeros((N_HEADS, 1), jnp.int32)
        for i in range(n_past_blocks):
            if i == j:
                continue
            ahead = (gates[i] >= gates[j]) if i < j else (gates[i] > gates[j])
            rank = rank + jnp.where(ahead, 1, 0)
        ranks.append(rank)
        lanes = slice(j * MOBA_BLOCK, (j + 1) * MOBA_BLOCK)
        pieces.append(s_ref[:, lanes] + bias_ref[:, lanes] + jnp.where(rank < n_sel, 0.0, -jnp.inf))
    s = jnp.concatenate(pieces, axis=-1)

    s_new = jnp.sum(qbd * kn_ref[sq], axis=-1, keepdims=True) + rbt_ref[:, 0:1] * LOG2E
    m = jnp.maximum(jnp.max(s, axis=-1, keepdims=True), s_new)
    p = jnp.exp2(s - m)
    p_new = jnp.exp2(s_new - m)
    pn_ref[sq] = jnp.broadcast_to(p_new, (N_HEADS, 128))
    l_ref[sq] = jnp.broadcast_to(jnp.sum(p, axis=-1, keepdims=True) + p_new, (N_HEADS, 128))

    lane = lax.broadcasted_iota(jnp.int32, (N_HEADS, 128), 1)
    sel = jnp.zeros((N_HEADS, 128), jnp.int32)
    for r in range(n_sel):
        picked = jnp.zeros((N_HEADS, MOBA_BLOCK), F32)
        block_id = jnp.zeros((N_HEADS, 1), jnp.int32)
        for j in range(n_past_blocks):
            picked = picked + jnp.where(ranks[j] == r, p[:, j * MOBA_BLOCK:(j + 1) * MOBA_BLOCK], 0.0)
            block_id = block_id + jnp.where(ranks[j] == r, j, 0)
        pc_ref[sq, :, r * MOBA_BLOCK:(r + 1) * MOBA_BLOCK] = picked
        sel = jnp.where(lane == r, block_id, sel)
    sel_ref[sq] = sel


def _moba_sample_values_body(pt_ref, sel_ref, pc_ref, pn_ref, l_ref, vn_ref, v_hbm, o_ref, buf_ref, sem_ref,
                             *, n_sel, ppb, page):
    b = pl.program_id(0)
    n_tiles = n_sel * ppb

    def tile_copy(seq, slot, h, i):
        block = sel_ref[seq * (N_HEADS * n_sel) + h * n_sel + i // ppb]
        pool_page = pt_ref[seq, block * ppb + i % ppb]
        return pltpu.make_async_copy(v_hbm.at[pool_page, h], buf_ref.at[slot, h * n_tiles + i], sem_ref.at[slot])

    def for_all_tiles(seq, slot, act):
        for h in range(N_HEADS):
            for i in range(n_tiles):
                act(tile_copy(seq, slot, h, i))

    @pl.when(b == 0)
    def _():
        for_all_tiles(0, 0, lambda cp: cp.start())

    @pl.when(b + 1 < pl.num_programs(0))
    def _():
        for_all_tiles(b + 1, (b + 1) % 2, lambda cp: cp.start())

    slot = b % 2
    for_all_tiles(b, slot, lambda cp: cp.wait())

    head_of_lane = lax.broadcasted_iota(jnp.int32, (N_HEADS, D_B), 1) // HEAD_DIM
    hmask = head_of_lane == lax.broadcasted_iota(jnp.int32, (N_HEADS, D_B), 0)
    ones = jnp.ones((SUBLANES, page), BF16)
    lane_sum = lambda x: lax.dot_general(ones, x, (((1,), (1,)), ((), ())), preferred_element_type=F32)
    outs = []
    for h in range(N_HEADS):
        acc = jnp.zeros((HEAD_DIM, page), F32)
        for i in range(n_tiles):
            acc = acc + buf_ref[slot, h * n_tiles + i] * pc_ref[0, h:h + 1, i * page:(i + 1) * page]
        hi = acc.astype(BF16)
        lo = (acc - hi.astype(F32)).astype(BF16)
        outs.append((lane_sum(hi) + lane_sum(lo))[0:1, :])
    o_past = jnp.concatenate(outs, axis=-1)
    spread = lambda col: jnp.sum(jnp.where(hmask, col, 0.0), axis=0, keepdims=True)
    o_ref[0] = (o_past + spread(pn_ref[0, :, 0:1]) * vn_ref[0]) / spread(l_ref[0, :, 0:1])


def _moba_sample(page_table, q, k_new, v_new, rel_bias_t, cache_kt, cache_vt):
    m, n_pages = page_table.shape
    page = cache_kt.shape[-1]
    past = n_pages * page
    n_sel = min(MOBA_TOPK, past // MOBA_BLOCK)
    ppb = MOBA_BLOCK // page
    row = lambda bi, *_: (bi, 0, 0)

    sps = SEQ_PER_STEP
    assert m % sps == 0
    keys_spec = pltpu.PrefetchScalarGridSpec(
        num_scalar_prefetch=1,
        grid=(m // sps,),
        in_specs=[
            pl.BlockSpec((sps, 1, D_B), row),
            pl.BlockSpec((sps, HEAD_DIM, N_HEADS), row),
            pl.BlockSpec((sps, 1, D_B), row),
            pl.BlockSpec((N_HEADS, NUM_BUCKETS), lambda bi, pt: (0, 0)),
        ] + [pl.BlockSpec((1, N_HEADS, HEAD_DIM, page), lambda bi, pt, sq=sq, pg=pg: (pt[bi * sps + sq, pg], 0, 0, 0))
             for sq in range(sps) for pg in range(n_pages)],
        out_specs=[
            pl.BlockSpec((sps, N_HEADS, n_sel * MOBA_BLOCK), row),
            pl.BlockSpec((sps, N_HEADS, 128), row),
            pl.BlockSpec((sps, N_HEADS, 128), row),
            pl.BlockSpec((sps, N_HEADS, 128), row),
        ],
        scratch_shapes=[
            pltpu.VMEM((N_HEADS, past), F32),
            pltpu.VMEM((sps, N_HEADS, HEAD_DIM, page), F32),
            pltpu.VMEM((sps, N_HEADS, past), F32),
        ],
    )
    q_dh = q.reshape(m, N_HEADS, HEAD_DIM).transpose(0, 2, 1)
    pc, pn, l, sel = pl.pallas_call(
        functools.partial(_moba_sample_keys_body, n_pages=n_pages, page=page),
        grid_spec=keys_spec,
        out_shape=[
            jax.ShapeDtypeStruct((m, N_HEADS, n_sel * MOBA_BLOCK), F32),
            jax.ShapeDtypeStruct((m, N_HEADS, 128), F32),
            jax.ShapeDtypeStruct((m, N_HEADS, 128), F32),
            jax.ShapeDtypeStruct((m, N_HEADS, 128), jnp.int32),
        ],
        compiler_params=_cparams(1),
        name="sample_moba_keys",
    )(page_table, q.reshape(m, 1, D_B), q_dh, k_new.reshape(m, 1, D_B), rel_bias_t, *([cache_kt] * (sps * n_pages)))

    sel_flat = sel[:, :, :n_sel].reshape(m * N_HEADS * n_sel)
    values_spec = pltpu.PrefetchScalarGridSpec(
        num_scalar_prefetch=2,
        grid=(m,),
        in_specs=[
            pl.BlockSpec((1, N_HEADS, n_sel * MOBA_BLOCK), row),
            pl.BlockSpec((1, N_HEADS, 128), row),
            pl.BlockSpec((1, N_HEADS, 128), row),
            pl.BlockSpec((1, 1, D_B), row),
            pl.BlockSpec(memory_space=pl.ANY),
        ],
        out_specs=pl.BlockSpec((1, 1, D_B), row),
        scratch_shapes=[
            pltpu.VMEM((2, N_HEADS * n_sel * ppb, HEAD_DIM, page), F32),
            pltpu.SemaphoreType.DMA((2,)),
        ],
    )
    return pl.pallas_call(
        functools.partial(_moba_sample_values_body, n_sel=n_sel, ppb=ppb, page=page),
        grid_spec=values_spec,
        out_shape=jax.ShapeDtypeStruct((m, 1, D_B), F32),
        compiler_params=_cparams(1),
        name="sample_moba_values",
    )(page_table, sel_flat, pc, pn, l, v_new.reshape(m, 1, D_B), cache_vt)


def _sample_ffn(x, st_ref, g_ref, wu_ref, cw_ref, cb_ref, wd_ref, stn_ref):
    h = _rmsnorm(x, g_ref[...]).astype(BF16)
    up = jnp.dot(h, wu_ref[0], preferred_element_type=F32)
    older, newer = st_ref[0, :, 0, :], st_ref[0, :, 1, :]
    conv = cw_ref[0:1, :] * older + cw_ref[1:2, :] * newer + cw_ref[2:3, :] * up + cb_ref[...]
    act = (_silu(conv[:, :D_FF]) * conv[:, D_FF:]).astype(BF16)
    stn_ref[:, 0, :] = newer
    stn_ref[:, 1, :] = up
    return x + jnp.dot(act, wd_ref[0], preferred_element_type=F32)


def _sample_out_ffn_body(ao_ref, o_ref, x_ref, wo_ref, st_ref, g_ref, wu_ref, cw_ref, cb_ref, wd_ref,
                         y_ref, stn_ref):
    x1 = (x_ref[...]
          + jnp.dot(ao_ref[...], wo_ref[0:D_A, :], preferred_element_type=F32)
          + jnp.dot(o_ref[...].astype(BF16), wo_ref[D_A:D_A + D_B, :], preferred_element_type=F32))
    y_ref[...] = _sample_ffn(x1, st_ref, g_ref, wu_ref, cw_ref, cb_ref, wd_ref, stn_ref)


def _whole(a):
    return _const_spec(a.shape)


def _whole_out(shape):
    nd = len(shape)
    return pl.BlockSpec(tuple(shape), lambda *_: (0,) * nd)


def _sample_out_ffn(ao, o, x, wo, st, g, wu, cw, cb, wd, layer):
    st_shape = st.shape[1:]
    return pl.pallas_call(
        _sample_out_ffn_body,
        grid=(1,),
        in_specs=[_whole(ao), _whole(o), _whole(x), _whole(wo), _layer_spec(st_shape, layer), _whole(g),
                  _layer_spec(wu.shape[1:], layer), _whole(cw), _whole(cb), _layer_spec(wd.shape[1:], layer)],
        out_specs=[_whole_out(x.shape), _whole_out(st_shape)],
        out_shape=[jax.ShapeDtypeStruct(x.shape, F32), jax.ShapeDtypeStruct(st_shape, F32)],
        compiler_params=_cparams(1),
        name="sample_out_ffn",
    )(ao, o, x, wo, st, g, wu, cw, cb, wd)


def _sample_odd_body(x_ref, gm_ref, wi_ref, ccw_ref, sc_ref, wo_ref, st_ref, g_ref, wu_ref, cw_ref, cb_ref,
                     wd_ref, y_ref, scn_ref, stn_ref):
    x = x_ref[...]
    h = _rmsnorm(x, gm_ref[...]).astype(BF16)
    z = jnp.dot(h, wi_ref[...], preferred_element_type=F32)
    gcu = z[:, D_C:2 * D_C] * z[:, 2 * D_C:]
    older, newer = sc_ref[:, 0, :], sc_ref[:, 1, :]
    conv = ccw_ref[0:1, :] * older + ccw_ref[1:2, :] * newer + ccw_ref[2:3, :] * gcu
    scn_ref[:, 0, :] = newer
    scn_ref[:, 1, :] = gcu
    x1 = x + jnp.dot((z[:, :D_C] * conv).astype(BF16), wo_ref[...], preferred_element_type=F32)
    y_ref[...] = _sample_ffn(x1, st_ref, g_ref, wu_ref, cw_ref, cb_ref, wd_ref, stn_ref)


def _sample_odd(x, gm, wi, ccw, sc, wo, st, g, wu, cw, cb, wd, layer):
    st_shape = st.shape[1:]
    return pl.pallas_call(
        _sample_odd_body,
        grid=(1,),
        in_specs=[_whole(x), _whole(gm), _whole(wi), _whole(ccw), _whole(sc), _whole(wo),
                  _layer_spec(st_shape, layer), _whole(g), _layer_spec(wu.shape[1:], layer), _whole(cw), _whole(cb),
                  _layer_spec(wd.shape[1:], layer)],
        out_specs=[_whole_out(x.shape), _whole_out(sc.shape), _whole_out(st_shape)],
        out_shape=[jax.ShapeDtypeStruct(x.shape, F32), jax.ShapeDtypeStruct(sc.shape, F32),
                   jax.ShapeDtypeStruct(st_shape, F32)],
        compiler_params=_cparams(1),
        name="sample_odd",
    )(x, gm, wi, ccw, sc, wo, st, g, wu, cw, cb, wd)


def kernel(x_prompt, x_sample, cache_k, cache_v, state_conv_a, state_conv_c, state_ffn, page_table, rel_bias,
           norm_mix_e, w_in_e, conv_a_w, conv_a_b, ln_a_g, ln_a_b, q_norm_g, k_norm_g, w_out_e,
           norm_mix_o, w_in_o, conv_c_w, w_out_o, norm_ffn, w_up, conv_f_w, conv_f_b, w_down):
    b, t, d = x_prompt.shape
    m = x_sample.shape[0]
    n_pool, page = cache_k.shape[1], cache_k.shape[2]
    n_pages = page_table.shape[1]
    assert norm_mix_e.shape[0] == 1 and norm_mix_o.shape[0] == 1 and norm_ffn.shape[0] == 2
    assert x_sample.shape[1] == 1 and t % TILE_M == 0 and t % page == 0
    assert (n_pages * page) % MOBA_BLOCK == 0 and MOBA_BLOCK % page == 0

    row = lambda v: v.reshape(1, -1)
    w_in_e_b = w_in_e[0].astype(BF16)
    w_out_e_b = w_out_e[0].astype(BF16)
    w_in_o_b = w_in_o[0].astype(BF16)
    w_out_o_b = w_out_o[0].astype(BF16)
    w_up_b = w_up.astype(BF16)
    w_down_b = w_down.astype(BF16)
    qg = row(jnp.tile(q_norm_g[0], N_HEADS))
    kg = row(jnp.tile(k_norm_g[0], N_HEADS))
    lane_head = jnp.arange(D_B, dtype=jnp.int32) // HEAD_DIM
    head_ones = (lane_head[:, None] == lane_head[None, :]).astype(BF16)
    g_e, g_o = row(norm_mix_e[0]), row(norm_mix_o[0])
    cab, lag, lab = row(conv_a_b[0]), row(ln_a_g[0]), row(ln_a_b[0])

    a_p, k_p, v_p, qt, kh, vt, km = _prompt_inproj(x_prompt, g_e, w_in_e_b, qg, kg, head_ones, page)
    nb = t // MOBA_BLOCK
    kmh = km.reshape(b, nb, N_HEADS, HEAD_DIM).transpose(0, 2, 1, 3)
    ot = _moba_prompt(rel_bias, _logit_bound(rel_bias, q_norm_g[0], k_norm_g[0]), qt, kh, vt, kmh)
    x1 = _prompt_mix_out(a_p, conv_a_w[0], cab, lag, lab, ot, x_prompt, w_out_e_b)
    x2, f0 = _prompt_ffn(x1, row(norm_ffn[0]), w_up_b, conv_f_w[0], row(conv_f_b[0]), w_down_b, 0)
    x3, c_st = _prompt_mixc(x2, g_o, w_in_o_b, conv_c_w[0], w_out_o_b)
    y_prompt, f1 = _prompt_ffn(x3, row(norm_ffn[1]), w_up_b, conv_f_w[1], row(conv_f_b[1]), w_down_b, 1)

    k_prompt = k_p.transpose(0, 1, 4, 2, 3)[None]
    v_prompt = v_p.transpose(0, 1, 4, 2, 3)[None]
    a_prompt = a_p[:, t - (CONV_A - 1):, :][None]
    c_prompt = c_st[:, SUBLANES - (CONV_C - 1):, :][None]
    f_prompt = jnp.stack([f0[:, SUBLANES - (CONV_F - 1):, :], f1[:, SUBLANES - (CONV_F - 1):, :]])

    xs = x_sample.reshape(m, d)
    sa = state_conv_a[0].transpose(1, 0, 2)
    sc = state_conv_c[0]
    sf = state_ffn
    ao_s, sa_new, q_s, k_s, v_s = _sample_even(xs, g_e, w_in_e_b, qg, kg, head_ones, sa, conv_a_w[0], cab, lag, lab)
    o_s = _moba_sample(page_table, q_s, k_s, v_s, rel_bias.T,
                       cache_k[0].transpose(0, 2, 3, 1), cache_v[0].transpose(0, 2, 3, 1))
    xs1, sf0 = _sample_out_ffn(ao_s, o_s.reshape(m, D_B), xs, w_out_e_b, sf, row(norm_ffn[0]), w_up_b,
                               conv_f_w[0], row(conv_f_b[0]), w_down_b, 0)
    ys, sc_new, sf1 = _sample_odd(xs1, g_o, w_in_o_b, conv_c_w[0], sc, w_out_o_b, sf, row(norm_ffn[1]),
                                  w_up_b, conv_f_w[1], row(conv_f_b[1]), w_down_b, 1)

    y_sample = ys.reshape(m, 1, d)
    k_sample = k_s.reshape(1, m, 1, N_HEADS, HEAD_DIM)
    v_sample = v_s.reshape(1, m, 1, N_HEADS, HEAD_DIM)
    a_sample = sa_new.transpose(1, 0, 2)[None]
    c_sample = sc_new[None]
    f_sample = jnp.stack([sf0, sf1])
    return (y_prompt, y_sample, k_prompt, v_prompt, a_prompt, c_prompt, f_prompt,
            k_sample, v_sample, a_sample, c_sample, f_sample)
```

```python
import functools
import math

import jax
import jax.numpy as jnp
from jax import lax
from jax.experimental import pallas as pl
from jax.experimental.pallas import tpu as pltpu

F32 = jnp.float32
BF16 = jnp.bfloat16

EPS = 1e-6
D_MODEL = 1024
D_A = 512
CONV_A = 31
N_HEADS = 8
HEAD_DIM = 64
D_B = N_HEADS * HEAD_DIM
MOBA_BLOCK = 256
MOBA_TOPK = 3
NUM_BUCKETS = 32
MAX_DISTANCE = 128
D_C = 1024
CONV_C = 3
D_FF = 2816
CONV_F = 3
LOG2E = math.log2(math.e)
Q_SCALE = HEAD_DIM ** -0.5 * LOG2E
VT_ROWS = HEAD_DIM + 16
MAX_SHIFT_RANGE = 100.0

TILE_M = 512
CONV_ROWS = 64
CONV_LANES = 256
HALO_A = 32
HEAD_GROUP = 8
SOFTMAX_ROWS = 64
SEQ_PER_STEP = 4
FAR_UNROLL = 2
BOUNDED_UNROLL = 4
CH = 256
SUBLANES = 8
VMEM_LIMIT = 56 * 1024 * 1024


def _t5_thresholds():
    max_exact = NUM_BUCKETS // 2
    th = list(range(1, max_exact + 1))
    for k in range(1, NUM_BUCKETS - max_exact):
        th.append(math.ceil(max_exact * (MAX_DISTANCE / max_exact) ** (k / (NUM_BUCKETS - max_exact))))
    return tuple(th)


T5_THRESH = _t5_thresholds()


def _cparams(n_grid):
    return pltpu.CompilerParams(dimension_semantics=("arbitrary",) * n_grid, vmem_limit_bytes=VMEM_LIMIT)


def _const_spec(shape):
    nd = len(shape)
    return pl.BlockSpec(shape, lambda *_: (0,) * nd, pipeline_mode=pl.Buffered(1))


def _layer_spec(shape, layer):
    nd = len(shape)
    return pl.BlockSpec((1,) + tuple(shape), lambda *_: (layer,) + (0,) * nd, pipeline_mode=pl.Buffered(1))


def _rmsnorm(x, g):
    return x * lax.rsqrt(jnp.mean(x * x, axis=-1, keepdims=True) + EPS) * g


def _split_dot(x, w_bf16):
    hi = x.astype(BF16)
    lo = (x - hi.astype(F32)).astype(BF16)
    return (jnp.dot(hi, w_bf16, preferred_element_type=F32)
            + jnp.dot(lo, w_bf16, preferred_element_type=F32))


def _head_rmsnorm(x, g, head_ones):
    ss = _split_dot(x * x, head_ones)
    return x * lax.rsqrt(ss * (1.0 / HEAD_DIM) + EPS) * g


def _silu(x):
    return x * jax.nn.sigmoid(x)


def _layernorm(x, g, b):
    mu = jnp.mean(x, axis=-1, keepdims=True)
    xc = x - mu
    var = jnp.mean(xc * xc, axis=-1, keepdims=True)
    return xc * lax.rsqrt(var + EPS) * g + b


def _in_proj_even(x, g, w, qg, kg, head_ones):
    h = _rmsnorm(x, g).astype(BF16)
    z = jnp.dot(h, w, preferred_element_type=F32)
    a = z[:, :D_A] * jax.nn.sigmoid(z[:, D_A:2 * D_A])
    q = _head_rmsnorm(z[:, 2 * D_A:2 * D_A + D_B], qg, head_ones) * Q_SCALE
    k = _head_rmsnorm(z[:, 2 * D_A + D_B:2 * D_A + 2 * D_B], kg, head_ones)
    v = z[:, 2 * D_A + 2 * D_B:]
    return a, q, k, v


def _prompt_inproj_body(x_ref, g_ref, w_ref, qg_ref, kg_ref, ho_ref,
                        a_ref, kp_ref, vp_ref, qt_ref, kh_ref, vt_ref, km_ref, *, tm, page):
    a, q, k, v = _in_proj_even(x_ref[0], g_ref[...], w_ref[...], qg_ref[...], kg_ref[...], ho_ref[...])
    a_ref[0] = a
    qt = q.T
    kt = k.T
    vt = v.T
    for pg in range(tm // page):
        kp_ref[0, pg] = kt[:, pg * page:(pg + 1) * page].reshape(N_HEADS, HEAD_DIM, page)
        vp_ref[0, pg] = vt[:, pg * page:(pg + 1) * page].reshape(N_HEADS, HEAD_DIM, page)
    kb = k.astype(BF16)
    pad_row = lax.broadcasted_iota(jnp.int32, (N_HEADS, VT_ROWS - HEAD_DIM, MOBA_BLOCK), 1)
    ones_rows = jnp.where(pad_row == 0, 1.0, 0.0).astype(BF16)
    for i in range(tm // MOBA_BLOCK):
        r0 = i * MOBA_BLOCK
        qt_ref[0, :, i] = qt[:, r0:r0 + MOBA_BLOCK].reshape(N_HEADS, HEAD_DIM, MOBA_BLOCK).astype(BF16)
        vt_ref[0, :, i, 0:HEAD_DIM, :] = (
            vt[:, r0:r0 + MOBA_BLOCK].reshape(N_HEADS, HEAD_DIM, MOBA_BLOCK).astype(BF16))
        vt_ref[0, :, i, HEAD_DIM:VT_ROWS, :] = ones_rows
        for hh in range(N_HEADS):
            kh_ref[0, hh, i] = kb[r0:r0 + MOBA_BLOCK, hh * HEAD_DIM:(hh + 1) * HEAD_DIM]
        km_ref[0, i] = jnp.mean(k[r0:r0 + MOBA_BLOCK], axis=0, keepdims=True)


def _prompt_inproj(x, g, w, qg, kg, head_ones, page):
    b, t, d = x.shape
    tm = TILE_M
    nt = t // tm
    nb = t // MOBA_BLOCK
    bpt = tm // MOBA_BLOCK
    ppt = tm // page
    n_out = w.shape[1]
    tok = lambda bi, ti: (bi, ti, 0)
    blk5 = lambda bi, ti: (bi, 0, ti, 0, 0)
    pages = lambda bi, ti: (bi, ti, 0, 0, 0)
    return pl.pallas_call(
        functools.partial(_prompt_inproj_body, tm=tm, page=page),
        grid=(b, nt),
        in_specs=[
            pl.BlockSpec((1, tm, d), tok),
            _const_spec((1, d)),
            _const_spec((d, n_out)),
            _const_spec((1, D_B)),
            _const_spec((1, D_B)),
            _const_spec((D_B, D_B)),
        ],
        out_specs=[
            pl.BlockSpec((1, tm, D_A), tok),
            pl.BlockSpec((1, ppt, N_HEADS, HEAD_DIM, page), pages),
            pl.BlockSpec((1, ppt, N_HEADS, HEAD_DIM, page), pages),
            pl.BlockSpec((1, N_HEADS, bpt, HEAD_DIM, MOBA_BLOCK), blk5),
            pl.BlockSpec((1, N_HEADS, bpt, MOBA_BLOCK, HEAD_DIM), blk5),
            pl.BlockSpec((1, N_HEADS, bpt, VT_ROWS, MOBA_BLOCK), blk5),
            pl.BlockSpec((1, bpt, 1, D_B), lambda bi, ti: (bi, ti, 0, 0)),
        ],
        out_shape=[
            jax.ShapeDtypeStruct((b, t, D_A), F32),
            jax.ShapeDtypeStruct((b, t // page, N_HEADS, HEAD_DIM, page), F32),
            jax.ShapeDtypeStruct((b, t // page, N_HEADS, HEAD_DIM, page), F32),
            jax.ShapeDtypeStruct((b, N_HEADS, nb, HEAD_DIM, MOBA_BLOCK), BF16),
            jax.ShapeDtypeStruct((b, N_HEADS, nb, MOBA_BLOCK, HEAD_DIM), BF16),
            jax.ShapeDtypeStruct((b, N_HEADS, nb, VT_ROWS, MOBA_BLOCK), BF16),
            jax.ShapeDtypeStruct((b, nb, 1, D_B), F32),
        ],
        compiler_params=_cparams(2),
        name="prompt_inproj",
    )(x, g, w, qg, kg, head_ones)


def _t5_bias_scalar_table(dist, rb_ref, h):
    out = jnp.full(dist.shape, rb_ref[0, h] * LOG2E, F32)
    for i, th in enumerate(T5_THRESH):
        out = jnp.where(dist >= th, rb_ref[i + 1, h] * LOG2E, out)
    return out


def _moba_prompt_body(rb_ref, ctl_ref, qt_ref, kh_ref, vt_ref, km_ref, ot_ref,
                      bdiag_ref, bsub_ref, selb_ref, m_ref, alpha_ref, acc_ref, s_ref, p_ref, *, nb, hg):
    h0 = pl.program_id(1) * hg
    n = pl.program_id(2)
    blk = MOBA_BLOCK
    bounded = ctl_ref[1, 0] > 0.5
    shifts = [jnp.where(bounded, ctl_ref[0, h0 + g], 0.0) for g in range(hg)]

    @pl.when(n == 0)
    def _():
        ki = lax.broadcasted_iota(jnp.int32, (blk, blk), 0)
        qi = lax.broadcasted_iota(jnp.int32, (blk, blk), 1)
        d0 = qi - ki
        for g in range(hg):
            bdiag_ref[g] = jnp.where(d0 >= 0, _t5_bias_scalar_table(jnp.maximum(d0, 0), rb_ref, h0 + g),
                                     -jnp.inf) - shifts[g]
            bsub_ref[g] = _t5_bias_scalar_table(d0 + blk, rb_ref, h0 + g) - shifts[g]

    bi = lax.broadcasted_iota(jnp.int32, (nb, blk), 0)
    qts = [qt_ref[0, g, 0] for g in range(hg)]

    def select(g):
        km = km_ref[0, g]
        km_hi = km.astype(BF16)
        km_lo = (km - km_hi.astype(F32)).astype(BF16)
        gate = (jnp.dot(km_hi, qts[g], preferred_element_type=F32)
                + jnp.dot(km_lo, qts[g], preferred_element_type=F32))
        avail = jnp.where(bi < n, 1.0, 0.0)
        far_bias = rb_ref[NUM_BUCKETS - 1, h0 + g] * LOG2E - shifts[g]
        selb = jnp.full((nb, blk), -jnp.inf, F32)
        for _ in range(MOBA_TOPK):
            gm = jnp.where(avail > 0.0, gate, -jnp.inf)
            top = jnp.max(gm, axis=0, keepdims=True)
            first = jnp.where(avail > 0.0, jnp.where(gm == top, bi, nb), nb)
            pick = bi == jnp.min(first, axis=0, keepdims=True)
            selb = jnp.where(pick, jnp.where(bi == n - 1, 0.0, far_bias), selb)
            avail = jnp.where(pick, 0.0, avail)
        selb_ref[g] = selb

    n_chunks = blk // SOFTMAX_ROWS

    def logits(g, j, slot, bias):
        s = jnp.dot(kh_ref[0, g, j], qts[g], preferred_element_type=F32)
        s_ref[slot, g] = s if bias is None else s + bias

    def chunk(g, slot, c):
        return s_ref[slot, g, c * SOFTMAX_ROWS:(c + 1) * SOFTMAX_ROWS, :].reshape(
            SOFTMAX_ROWS // SUBLANES, SUBLANES, blk)

    def softmax(g, slot, row):
        m8 = jnp.max(chunk(g, slot, 0), axis=0)
        for c in range(1, n_chunks):
            m8 = jnp.maximum(m8, jnp.max(chunk(g, slot, c), axis=0))
        m_blk = jnp.max(m8, axis=0, keepdims=True)
        if row is None:
            mn = m_blk
            shift = mn
        else:
            m_old = m_ref[g]
            mn = jnp.maximum(m_old, m_blk + row)
            alpha_ref[slot, g] = jnp.exp2(m_old - mn)
            shift = mn - row
        for c in range(n_chunks):
            p = jnp.exp2(chunk(g, slot, c) - shift)
            p_ref[slot, g, c * SOFTMAX_ROWS:(c + 1) * SOFTMAX_ROWS, :] = (
                p.reshape(SOFTMAX_ROWS, blk).astype(BF16))
        m_ref[g] = mn

    def weighted_values(g, j, slot, first):
        pv = jnp.dot(vt_ref[0, g, j], p_ref[slot, g], preferred_element_type=F32)
        acc_ref[g] = pv if first else alpha_ref[slot, g] * acc_ref[g] + pv

    def far_trips(trip, unroll):
        def far(i, carry):
            for r in range(unroll):
                trip(unroll * i + r, r & 1)
            return carry

        n_far = n - 1
        lax.fori_loop(0, n_far // unroll, far, 0)
        done = (n_far // unroll) * unroll
        piece = unroll // 2
        while piece >= 1:
            @pl.when(((n_far - done) & piece) != 0)
            def _(piece=piece, done=done):
                for r in range(piece):
                    trip(done + r, r & 1)
            done = done + ((n_far - done) & piece)
            piece //= 2

    last = jnp.where(n >= 2, n - 2, n - 1)

    @pl.when(bounded)
    def _():
        def probabilities(g, j, slot, bias):
            s = jnp.dot(kh_ref[0, g, j], qts[g], preferred_element_type=F32)
            p_ref[slot, g] = jnp.exp2(s + bias).astype(BF16)

        def add_values(g, j, slot, first):
            pv = jnp.dot(vt_ref[0, g, j], p_ref[slot, g], preferred_element_type=F32)
            acc_ref[g] = pv if first else acc_ref[g] + pv

        for g in range(hg):
            probabilities(g, n, 0, bdiag_ref[g])
        for g in range(hg):
            select(g)

        @pl.when(n == 0)
        def _():
            for g in range(hg):
                add_values(g, n, 0, True)

        @pl.when(n >= 1)
        def _():
            for g in range(hg):
                probabilities(g, n - 1, 1, bsub_ref[g] + selb_ref[g, pl.ds(n - 1, 1), :])
            for g in range(hg):
                add_values(g, n, 0, True)

            def trip(j, slot):
                prev = jnp.where(j == 0, n - 1, j - 1)
                for g in range(hg):
                    probabilities(g, j, slot, selb_ref[g, pl.ds(j, 1), :])
                for g in range(hg):
                    add_values(g, prev, 1 - slot, False)

            far_trips(trip, BOUNDED_UNROLL)
            for g in range(hg):
                add_values(g, last, n & 1, False)

    @pl.when(jnp.logical_not(bounded))
    def _():
        for g in range(hg):
            logits(g, n, 0, bdiag_ref[g])
        for g in range(hg):
            select(g)
        for g in range(hg):
            logits(g, jnp.maximum(n - 1, 0), 1, bsub_ref[g])
        for g in range(hg):
            softmax(g, 0, None)

        @pl.when(n == 0)
        def _():
            for g in range(hg):
                weighted_values(g, n, 0, True)

        @pl.when(n >= 1)
        def _():
            for g in range(hg):
                weighted_values(g, n, 0, True)
                softmax(g, 1, selb_ref[g, pl.ds(n - 1, 1), :])
                logits(g, 0, 0, None)

            def trip(j, slot):
                nxt = jnp.minimum(j + 1, n - 2)
                prev = jnp.where(j == 0, n - 1, j - 1)
                for g in range(hg):
                    logits(g, nxt, 1 - slot, None)
                for g in range(hg):
                    weighted_values(g, prev, 1 - slot, False)
                for g in range(hg):
                    softmax(g, slot, selb_ref[g, pl.ds(j, 1), :])

            far_trips(trip, FAR_UNROLL)
            for g in range(hg):
                weighted_values(g, last, n & 1, False)

    for g in range(hg):
        ot_ref[0, g * HEAD_DIM:(g + 1) * HEAD_DIM, :] = (
            acc_ref[g, 0:HEAD_DIM, :] / acc_ref[g, HEAD_DIM:HEAD_DIM + 1, :])


def _logit_bound(rel_bias, q_gain, k_gain):
    qk = 1.02 * HEAD_DIM * Q_SCALE * jnp.max(jnp.abs(q_gain)) * jnp.max(jnp.abs(k_gain))
    table = rel_bias * LOG2E
    top, low = jnp.max(table, axis=0), jnp.min(table, axis=0)
    usable = jnp.all(2.0 * qk + (top - low) <= MAX_SHIFT_RANGE)
    return jnp.stack([qk + top, jnp.broadcast_to(usable.astype(F32), top.shape)])


def _moba_prompt(rel_bias, ctl, qt, kh, vt, kmh):
    b, nh, nb, hd, blk = qt.shape
    t = nb * blk
    hg = HEAD_GROUP
    grp = lambda bi, hi, ni: (bi, hi, 0, 0, 0)
    return pl.pallas_call(
        functools.partial(_moba_prompt_body, nb=nb, hg=hg),
        grid=(b, nh // hg, nb),
        in_specs=[
            pl.BlockSpec(memory_space=pltpu.SMEM),
            pl.BlockSpec(memory_space=pltpu.SMEM),
            pl.BlockSpec((1, hg, 1, hd, blk), lambda bi, hi, ni: (bi, hi, ni, 0, 0)),
            pl.BlockSpec((1, hg, nb, blk, hd), grp, pipeline_mode=pl.Buffered(1)),
            pl.BlockSpec((1, hg, nb, VT_ROWS, blk), grp, pipeline_mode=pl.Buffered(1)),
            pl.BlockSpec((1, hg, nb, hd), lambda bi, hi, ni: (bi, hi, 0, 0)),
        ],
        out_specs=pl.BlockSpec((1, hg * hd, blk), lambda bi, hi, ni: (bi, hi, ni)),
        out_shape=jax.ShapeDtypeStruct((b, nh * hd, t), F32),
        scratch_shapes=[
            pltpu.VMEM((hg, blk, blk), F32),
            pltpu.VMEM((hg, blk, blk), F32),
            pltpu.VMEM((hg, nb, blk), F32),
            pltpu.VMEM((hg, 1, blk), F32),
            pltpu.VMEM((2, hg, 1, blk), F32),
            pltpu.VMEM((hg, VT_ROWS, blk), F32),
            pltpu.VMEM((2, hg, blk, blk), F32),
            pltpu.VMEM((2, hg, blk, blk), BF16),
        ],
        compiler_params=_cparams(3),
        name="prompt_moba",
    )(rel_bias, ctl, qt, kh, vt, kmh)


def _prompt_mix_out_body(a_ref, halo_ref, cw_ref, cb_ref, lg_ref, lb_ref, ot_ref, x_ref, w_ref,
                         y_ref, sh_ref, conv_ref, wb_ref, *, tm):
    t = pl.program_id(1)
    rows = tm + HALO_A

    @pl.when((pl.program_id(0) == 0) & (t == 0))
    def _():
        for j in range(CONV_A):
            wb_ref[j] = jnp.broadcast_to(cw_ref[j:j + 1, :], (SUBLANES, D_A))

    sh_ref[0, 0:HALO_A] = jnp.where(t == 0, 0.0, halo_ref[0])
    sh_ref[0, HALO_A:rows] = a_ref[0]
    for s in range(1, SUBLANES):
        sh_ref[s, 0:rows - SUBLANES] = sh_ref[0, s:s + rows - SUBLANES]

    groups = CONV_ROWS // SUBLANES
    first_off = HALO_A - (CONV_A - 1)

    def chunk(c, carry):
        r0 = pl.multiple_of(c * CONV_ROWS, CONV_ROWS)
        for l0 in range(0, D_A, CONV_LANES):
            lanes = slice(l0, l0 + CONV_LANES)
            acc = [jnp.zeros((SUBLANES, CONV_LANES), F32) for _ in range(groups)]
            for s in range(SUBLANES):
                offs = [o for o in range(first_off, first_off + CONV_A) if o % SUBLANES == s]
                qs = [o // SUBLANES for o in offs]
                slab = {gq: sh_ref[s, pl.ds(r0 + gq * SUBLANES, SUBLANES), lanes]
                        for gq in range(min(qs), max(qs) + groups)}
                for o, q in zip(offs, qs):
                    w = wb_ref[o - first_off, :, lanes]
                    for g in range(groups):
                        acc[g] = acc[g] + w * slab[g + q]
            for g in range(groups):
                conv_ref[pl.ds(r0 + g * SUBLANES, SUBLANES), lanes] = acc[g]
        return carry

    lax.fori_loop(0, tm // CONV_ROWS, chunk, 0)

    ao = _silu(_layernorm(conv_ref[...] + cb_ref[...], lg_ref[...], lb_ref[...])).astype(BF16)
    o = ot_ref[0].T.astype(BF16)
    y = (jnp.dot(ao, w_ref[0:D_A, :], preferred_element_type=F32)
         + jnp.dot(o, w_ref[D_A:D_A + D_B, :], preferred_element_type=F32))
    y_ref[0] = x_ref[0] + y


def _prompt_mix_out(a, cw, cb, lg, lb, ot, x, w):
    b, t, d = x.shape
    tm = TILE_M
    nt = t // tm
    hpt = tm // HALO_A
    tok = lambda bi, ti: (bi, ti, 0)
    return pl.pallas_call(
        functools.partial(_prompt_mix_out_body, tm=tm),
        grid=(b, nt),
        in_specs=[
            pl.BlockSpec((1, tm, D_A), tok),
            pl.BlockSpec((1, HALO_A, D_A), lambda bi, ti: (bi, jnp.maximum(ti * hpt - 1, 0), 0)),
            _const_spec((CONV_A, D_A)),
            _const_spec((1, D_A)),
            _const_spec((1, D_A)),
            _const_spec((1, D_A)),
            pl.BlockSpec((1, D_B, tm), lambda bi, ti: (bi, 0, ti)),
            pl.BlockSpec((1, tm, d), tok),
            _const_spec((D_A + D_B, d)),
        ],
        out_specs=pl.BlockSpec((1, tm, d), tok),
        out_shape=jax.ShapeDtypeStruct((b, t, d), F32),
        scratch_shapes=[
            pltpu.VMEM((SUBLANES, tm + HALO_A, D_A), F32),
            pltpu.VMEM((tm, D_A), F32),
            pltpu.VMEM((CONV_A, SUBLANES, D_A), F32),
        ],
        compiler_params=_cparams(2),
        name="prompt_mix_out",
    )(a, a, cw, cb, lg, lb, ot, x, w)


def _conv3_chunk(u, ext_ref, carry_ref, cw_ref, c0, tm):
    ext_ref[0:SUBLANES] = carry_ref[:, c0:c0 + CH]
    ext_ref[SUBLANES:SUBLANES + tm] = u
    carry_ref[:, c0:c0 + CH] = u[tm - SUBLANES:tm]
    return (cw_ref[0:1, c0:c0 + CH] * ext_ref[SUBLANES - 2:SUBLANES - 2 + tm]
            + cw_ref[1:2, c0:c0 + CH] * ext_ref[SUBLANES - 1:SUBLANES - 1 + tm]
            + cw_ref[2:3, c0:c0 + CH] * u)


def _prompt_ffn_body(x_ref, g_ref, wu_ref, cw_ref, cb_ref, wd_ref, y_ref, st_ref,
                     carry_ref, extg_ref, extu_ref, act_ref, *, tm):
    @pl.when(pl.program_id(1) == 0)
    def _():
        carry_ref[...] = jnp.zeros_like(carry_ref)

    x = x_ref[0]
    h = _rmsnorm(x, g_ref[...]).astype(BF16)
    for c in range(D_FF // CH):
        cg = c * CH
        cu = D_FF + c * CH
        ug = jnp.dot(h, wu_ref[0, :, cg:cg + CH], preferred_element_type=F32)
        uu = jnp.dot(h, wu_ref[0, :, cu:cu + CH], preferred_element_type=F32)
        gg = _conv3_chunk(ug, extg_ref, carry_ref, cw_ref, cg, tm) + cb_ref[:, cg:cg + CH]
        gu = _conv3_chunk(uu, extu_ref, carry_ref, cw_ref, cu, tm) + cb_ref[:, cu:cu + CH]
        act_ref[:, cg:cg + CH] = (_silu(gg) * gu).astype(BF16)
    st_ref[0] = carry_ref[...]
    y_ref[0] = x + jnp.dot(act_ref[...], wd_ref[0], preferred_element_type=F32)


def _prompt_ffn(x, g, wu, cw, cb, wd, layer):
    b, t, d = x.shape
    tm = TILE_M
    nt = t // tm
    tok = lambda bi, ti: (bi, ti, 0)
    return pl.pallas_call(
        functools.partial(_prompt_ffn_body, tm=tm),
        grid=(b, nt),
        in_specs=[
            pl.BlockSpec((1, tm, d), tok),
            _const_spec((1, d)),
            _layer_spec((d, 2 * D_FF), layer),
            _const_spec((CONV_F, 2 * D_FF)),
            _const_spec((1, 2 * D_FF)),
            _layer_spec((D_FF, d), layer),
        ],
        out_specs=[
            pl.BlockSpec((1, tm, d), tok),
            pl.BlockSpec((1, SUBLANES, 2 * D_FF), lambda bi, ti: (bi, 0, 0)),
        ],
        out_shape=[
            jax.ShapeDtypeStruct((b, t, d), F32),
            jax.ShapeDtypeStruct((b, SUBLANES, 2 * D_FF), F32),
        ],
        scratch_shapes=[
            pltpu.VMEM((SUBLANES, 2 * D_FF), F32),
            pltpu.VMEM((tm + SUBLANES, CH), F32),
            pltpu.VMEM((tm + SUBLANES, CH), F32),
            pltpu.VMEM((tm, D_FF), BF16),
        ],
        compiler_params=_cparams(2),
        name="prompt_ffn",
    )(x, g, wu, cw, cb, wd)


def _prompt_mixc_body(x_ref, g_ref, wi_ref, cw_ref, wo_ref, y_ref, st_ref,
                      carry_ref, ext_ref, z_ref, *, tm):
    @pl.when(pl.program_id(1) == 0)
    def _():
        carry_ref[...] = jnp.zeros_like(carry_ref)

    x = x_ref[0]
    h = _rmsnorm(x, g_ref[...]).astype(BF16)
    for c in range(D_C // CH):
        c0 = c * CH
        gb = jnp.dot(h, wi_ref[:, c0:c0 + CH], preferred_element_type=F32)
        gc = jnp.dot(h, wi_ref[:, D_C + c0:D_C + c0 + CH], preferred_element_type=F32)
        u = jnp.dot(h, wi_ref[:, 2 * D_C + c0:2 * D_C + c0 + CH], preferred_element_type=F32)
        conv = _conv3_chunk(gc * u, ext_ref, carry_ref, cw_ref, c0, tm)
        z_ref[:, c0:c0 + CH] = (gb * conv).astype(BF16)
    st_ref[0] = carry_ref[...]
    y_ref[0] = x + jnp.dot(z_ref[...], wo_ref[...], preferred_element_type=F32)


def _prompt_mixc(x, g, wi, cw, wo):
    b, t, d = x.shape
    tm = TILE_M
    nt = t // tm
    tok = lambda bi, ti: (bi, ti, 0)
    return pl.pallas_call(
        functools.partial(_prompt_mixc_body, tm=tm),
        grid=(b, nt),
        in_specs=[
            pl.BlockSpec((1, tm, d), tok),
            _const_spec((1, d)),
            _const_spec((d, 3 * D_C)),
            _const_spec((CONV_C, D_C)),
            _const_spec((D_C, d)),
        ],
        out_specs=[
            pl.BlockSpec((1, tm, d), tok),
            pl.BlockSpec((1, SUBLANES, D_C), lambda bi, ti: (bi, 0, 0)),
        ],
        out_shape=[
            jax.ShapeDtypeStruct((b, t, d), F32),
            jax.ShapeDtypeStruct((b, SUBLANES, D_C), F32),
        ],
        scratch_shapes=[
            pltpu.VMEM((SUBLANES, D_C), F32),
            pltpu.VMEM((tm + SUBLANES, CH), F32),
            pltpu.VMEM((tm, D_C), BF16),
        ],
        compiler_params=_cparams(2),
        name="prompt_mixc",
    )(x, g, wi, cw, wo)


def _sample_even_body(x_ref, g_ref, w_ref, qg_ref, kg_ref, ho_ref, sa_ref, cw_ref, cb_ref, lg_ref, lb_ref,
                      ao_ref, san_ref, q_ref, k_ref, v_ref):
    a, q, k, v = _in_proj_even(x_ref[...], g_ref[...], w_ref[...], qg_ref[...], kg_ref[...], ho_ref[...])
    q_ref[...] = q
    k_ref[...] = k
    v_ref[...] = v
    hist = CONV_A - 1
    conv = cb_ref[...] + cw_ref[hist:hist + 1, :] * a
    for j in range(hist):
        conv = conv + cw_ref[j:j + 1, :] * sa_ref[j]
    ao_ref[...] = _silu(_layernorm(conv, lg_ref[...], lb_ref[...])).astype(BF16)
    san_ref[0:hist - 1] = sa_ref[1:hist]
    san_ref[hist - 1] = a


def _sample_even(x, g, w, qg, kg, head_ones, sa, cw, cb, lg, lb):
    m = x.shape[0]
    return pl.pallas_call(
        _sample_even_body,
        out_shape=[
            jax.ShapeDtypeStruct((m, D_A), BF16),
            jax.ShapeDtypeStruct(sa.shape, F32),
            jax.ShapeDtypeStruct((m, D_B), F32),
            jax.ShapeDtypeStruct((m, D_B), F32),
            jax.ShapeDtypeStruct((m, D_B), F32),
        ],
        compiler_params=pltpu.CompilerParams(vmem_limit_bytes=VMEM_LIMIT),
        name="sample_even",
    )(x, g, w, qg, kg, head_ones, sa, cw, cb, lg, lb)


def _moba_sample_keys_body(pt_ref, q_ref, qdh_ref, kn_ref, rbt_ref, *refs, n_pages, page):
    del pt_ref
    outs = refs[SEQ_PER_STEP * n_pages:]
    bias_ref = outs[4]
    past = n_pages * page

    @pl.when(pl.program_id(0) == 0)
    def _():
        pos = lax.broadcasted_iota(jnp.int32, (N_HEADS, past), 1)
        dist = past - pos
        out = jnp.zeros((N_HEADS, past), F32) + rbt_ref[:, 0:1]
        for i, th in enumerate(T5_THRESH):
            out = jnp.where(dist >= th, rbt_ref[:, i + 1:i + 2], out)
        bias_ref[...] = out * LOG2E

    for sq in range(SEQ_PER_STEP):
        _moba_sample_keys_one(sq, q_ref, qdh_ref, kn_ref, rbt_ref, refs[sq * n_pages:(sq + 1) * n_pages], *outs,
                              n_pages=n_pages, page=page)


def _moba_sample_keys_one(sq, q_ref, qdh_ref, kn_ref, rbt_ref, k_pages, pc_ref, pn_ref, l_ref, sel_ref,
                          bias_ref, qb_ref, s_ref, *, n_pages, page):
    past = n_pages * page
    n_past_blocks = past // MOBA_BLOCK
    n_sel = min(MOBA_TOPK, n_past_blocks)
    qb_ref = qb_ref.at[sq]
    s_ref = s_ref.at[sq]

    head_of_lane = lax.broadcasted_iota(jnp.int32, (N_HEADS, D_B), 1) // HEAD_DIM
    hmask = head_of_lane == lax.broadcasted_iota(jnp.int32, (N_HEADS, D_B), 0)
    qbd = jnp.where(hmask, q_ref[sq], 0.0)

    qdh = qdh_ref[sq]
    for h in range(N_HEADS):
        qb_ref[h] = jnp.broadcast_to(qdh[:, h:h + 1], (HEAD_DIM, page))

    for pg in range(n_pages):
        rows = [jnp.sum(k_pages[pg][0, h] * qb_ref[h], axis=0, keepdims=True) for h in range(N_HEADS)]
        s_ref[:, pg * page:(pg + 1) * page] = jnp.concatenate(rows, axis=0)

    gates = [jnp.sum(s_ref[:, j * MOBA_BLOCK:(j + 1) * MOBA_BLOCK], axis=-1, keepdims=True)
             for j in range(n_past_blocks)]
    ranks = []
    pieces = []
    for j in range(n_past_blocks):
        rank = jnp.zeros((N_HEADS, 1), jnp.int32)
        for i in range(n_past_blocks):
            if i == j:
                continue
            ahead = (gates[i] >= gates[j]) if i < j else (gates[i] > gates[j])
            rank = rank + jnp.where(ahead, 1, 0)
        ranks.append(rank)
        lanes = slice(j * MOBA_BLOCK, (j + 1) * MOBA_BLOCK)
        pieces.append(s_ref[:, lanes] + bias_ref[:, lanes] + jnp.where(rank < n_sel, 0.0, -jnp.inf))
    s = jnp.concatenate(pieces, axis=-1)

    s_new = jnp.sum(qbd * kn_ref[sq], axis=-1, keepdims=True) + rbt_ref[:, 0:1] * LOG2E
    m = jnp.maximum(jnp.max(s, axis=-1, keepdims=True), s_new)
    p = jnp.exp2(s - m)
    p_new = jnp.exp2(s_new - m)
    pn_ref[sq] = jnp.broadcast_to(p_new, (N_HEADS, 128))
    l_ref[sq] = jnp.broadcast_to(jnp.sum(p, axis=-1, keepdims=True) + p_new, (N_HEADS, 128))

    lane = lax.broadcasted_iota(jnp.int32, (N_HEADS, 128), 1)
    sel = jnp.zeros((N_HEADS, 128), jnp.int32)
    for r in range(n_sel):
        picked = jnp.zeros((N_HEADS, MOBA_BLOCK), F32)
        block_id = jnp.zeros((N_HEADS, 1), jnp.int32)
        for j in range(n_past_blocks):
            picked = picked + jnp.where(ranks[j] == r, p[:, j * MOBA_BLOCK:(j + 1) * MOBA_BLOCK], 0.0)
            block_id = block_id + jnp.where(ranks[j] == r, j, 0)
        pc_ref[sq, :, r * MOBA_BLOCK:(r + 1) * MOBA_BLOCK] = picked
        sel = jnp.where(lane == r, block_id, sel)
    sel_ref[sq] = sel


def _moba_sample_values_body(pt_ref, sel_ref, pc_ref, pn_ref, l_ref, vn_ref, v_hbm, o_ref, buf_ref, sem_ref,
                             *, n_sel, ppb, page):
    b = pl.program_id(0)
    n_tiles = n_sel * ppb

    def tile_copy(seq, slot, h, i):
        block = sel_ref[seq * (N_HEADS * n_sel) + h * n_sel + i // ppb]
        pool_page = pt_ref[seq, block * ppb + i % ppb]
        return pltpu.make_async_copy(v_hbm.at[pool_page, h], buf_ref.at[slot, h * n_tiles + i], sem_ref.at[slot])

    def for_all_tiles(seq, slot, act):
        for h in range(N_HEADS):
            for i in range(n_tiles):
                act(tile_copy(seq, slot, h, i))

    @pl.when(b == 0)
    def _():
        for_all_tiles(0, 0, lambda cp: cp.start())

    @pl.when(b + 1 < pl.num_programs(0))
    def _():
        for_all_tiles(b + 1, (b + 1) % 2, lambda cp: cp.start())

    slot = b % 2
    for_all_tiles(b, slot, lambda cp: cp.wait())

    head_of_lane = lax.broadcasted_iota(jnp.int32, (N_HEADS, D_B), 1) // HEAD_DIM
    hmask = head_of_lane == lax.broadcasted_iota(jnp.int32, (N_HEADS, D_B), 0)
    ones = jnp.ones((SUBLANES, page), BF16)
    lane_sum = lambda x: lax.dot_general(ones, x, (((1,), (1,)), ((), ())), preferred_element_type=F32)
    outs = []
    for h in range(N_HEADS):
        acc = jnp.zeros((HEAD_DIM, page), F32)
        for i in range(n_tiles):
            acc = acc + buf_ref[slot, h * n_tiles + i] * pc_ref[0, h:h + 1, i * page:(i + 1) * page]
        hi = acc.astype(BF16)
        lo = (acc - hi.astype(F32)).astype(BF16)
        outs.append((lane_sum(hi) + lane_sum(lo))[0:1, :])
    o_past = jnp.concatenate(outs, axis=-1)
    spread = lambda col: jnp.sum(jnp.where(hmask, col, 0.0), axis=0, keepdims=True)
    o_ref[0] = (o_past + spread(pn_ref[0, :, 0:1]) * vn_ref[0]) / spread(l_ref[0, :, 0:1])


def _moba_sample(page_table, q, k_new, v_new, rel_bias_t, cache_kt, cache_vt):
    m, n_pages = page_table.shape
    page = cache_kt.shape[-1]
    past = n_pages * page
    n_sel = min(MOBA_TOPK, past // MOBA_BLOCK)
    ppb = MOBA_BLOCK // page
    row = lambda bi, *_: (bi, 0, 0)

    sps = SEQ_PER_STEP
    assert m % sps == 0
    keys_spec = pltpu.PrefetchScalarGridSpec(
        num_scalar_prefetch=1,
        grid=(m // sps,),
        in_specs=[
            pl.BlockSpec((sps, 1, D_B), row),
            pl.BlockSpec((sps, HEAD_DIM, N_HEADS), row),
            pl.BlockSpec((sps, 1, D_B), row),
            pl.BlockSpec((N_HEADS, NUM_BUCKETS), lambda bi, pt: (0, 0)),
        ] + [pl.BlockSpec((1, N_HEADS, HEAD_DIM, page), lambda bi, pt, sq=sq, pg=pg: (pt[bi * sps + sq, pg], 0, 0, 0))
             for sq in range(sps) for pg in range(n_pages)],
        out_specs=[
            pl.BlockSpec((sps, N_HEADS, n_sel * MOBA_BLOCK), row),
            pl.BlockSpec((sps, N_HEADS, 128), row),
            pl.BlockSpec((sps, N_HEADS, 128), row),
            pl.BlockSpec((sps, N_HEADS, 128), row),
        ],
        scratch_shapes=[
            pltpu.VMEM((N_HEADS, past), F32),
            pltpu.VMEM((sps, N_HEADS, HEAD_DIM, page), F32),
            pltpu.VMEM((sps, N_HEADS, past), F32),
        ],
    )
    q_dh = q.reshape(m, N_HEADS, HEAD_DIM).transpose(0, 2, 1)
    pc, pn, l, sel = pl.pallas_call(
        functools.partial(_moba_sample_keys_body, n_pages=n_pages, page=page),
        grid_spec=keys_spec,
        out_shape=[
            jax.ShapeDtypeStruct((m, N_HEADS, n_sel * MOBA_BLOCK), F32),
            jax.ShapeDtypeStruct((m, N_HEADS, 128), F32),
            jax.ShapeDtypeStruct((m, N_HEADS, 128), F32),
            jax.ShapeDtypeStruct((m, N_HEADS, 128), jnp.int32),
        ],
        compiler_params=_cparams(1),
        name="sample_moba_keys",
    )(page_table, q.reshape(m, 1, D_B), q_dh, k_new.reshape(m, 1, D_B), rel_bias_t, *([cache_kt] * (sps * n_pages)))

    sel_flat = sel[:, :, :n_sel].reshape(m * N_HEADS * n_sel)
    values_spec = pltpu.PrefetchScalarGridSpec(
        num_scalar_prefetch=2,
        grid=(m,),
        in_specs=[
            pl.BlockSpec((1, N_HEADS, n_sel * MOBA_BLOCK), row),
            pl.BlockSpec((1, N_HEADS, 128), row),
            pl.BlockSpec((1, N_HEADS, 128), row),
            pl.BlockSpec((1, 1, D_B), row),
            pl.BlockSpec(memory_space=pl.ANY),
        ],
        out_specs=pl.BlockSpec((1, 1, D_B), row),
        scratch_shapes=[
            pltpu.VMEM((2, N_HEADS * n_sel * ppb, HEAD_DIM, page), F32),
            pltpu.SemaphoreType.DMA((2,)),
        ],
    )
    return pl.pallas_call(
        functools.partial(_moba_sample_values_body, n_sel=n_sel, ppb=ppb, page=page),
        grid_spec=values_spec,
        out_shape=jax.ShapeDtypeStruct((m, 1, D_B), F32),
        compiler_params=_cparams(1),
        name="sample_moba_values",
    )(page_table, sel_flat, pc, pn, l, v_new.reshape(m, 1, D_B), cache_vt)


def _sample_ffn(x, st_ref, g_ref, wu_ref, cw_ref, cb_ref, wd_ref, stn_ref):
    h = _rmsnorm(x, g_ref[...]).astype(BF16)
    up = jnp.dot(h, wu_ref[0], preferred_element_type=F32)
    older, newer = st_ref[0, :, 0, :], st_ref[0, :, 1, :]
    conv = cw_ref[0:1, :] * older + cw_ref[1:2, :] * newer + cw_ref[2:3, :] * up + cb_ref[...]
    act = (_silu(conv[:, :D_FF]) * conv[:, D_FF:]).astype(BF16)
    stn_ref[:, 0, :] = newer
    stn_ref[:, 1, :] = up
    return x + jnp.dot(act, wd_ref[0], preferred_element_type=F32)


def _sample_out_ffn_body(ao_ref, o_ref, x_ref, wo_ref, st_ref, g_ref, wu_ref, cw_ref, cb_ref, wd_ref,
                         y_ref, stn_ref):
    x1 = (x_ref[...]
          + jnp.dot(ao_ref[...], wo_ref[0:D_A, :], preferred_element_type=F32)
          + jnp.dot(o_ref[...].astype(BF16), wo_ref[D_A:D_A + D_B, :], preferred_element_type=F32))
    y_ref[...] = _sample_ffn(x1, st_ref, g_ref, wu_ref, cw_ref, cb_ref, wd_ref, stn_ref)


def _whole(a):
    return _const_spec(a.shape)


def _whole_out(shape):
    nd = len(shape)
    return pl.BlockSpec(tuple(shape), lambda *_: (0,) * nd)


def _sample_out_ffn(ao, o, x, wo, st, g, wu, cw, cb, wd, layer):
    st_shape = st.shape[1:]
    return pl.pallas_call(
        _sample_out_ffn_body,
        grid=(1,),
        in_specs=[_whole(ao), _whole(o), _whole(x), _whole(wo), _layer_spec(st_shape, layer), _whole(g),
                  _layer_spec(wu.shape[1:], layer), _whole(cw), _whole(cb), _layer_spec(wd.shape[1:], layer)],
        out_specs=[_whole_out(x.shape), _whole_out(st_shape)],
        out_shape=[jax.ShapeDtypeStruct(x.shape, F32), jax.ShapeDtypeStruct(st_shape, F32)],
        compiler_params=_cparams(1),
        name="sample_out_ffn",
    )(ao, o, x, wo, st, g, wu, cw, cb, wd)


def _sample_odd_body(x_ref, gm_ref, wi_ref, ccw_ref, sc_ref, wo_ref, st_ref, g_ref, wu_ref, cw_ref, cb_ref,
                     wd_ref, y_ref, scn_ref, stn_ref):
    x = x_ref[...]
    h = _rmsnorm(x, gm_ref[...]).astype(BF16)
    z = jnp.dot(h, wi_ref[...], preferred_element_type=F32)
    gcu = z[:, D_C:2 * D_C] * z[:, 2 * D_C:]
    older, newer = sc_ref[:, 0, :], sc_ref[:, 1, :]
    conv = ccw_ref[0:1, :] * older + ccw_ref[1:2, :] * newer + ccw_ref[2:3, :] * gcu
    scn_ref[:, 0, :] = newer
    scn_ref[:, 1, :] = gcu
    x1 = x + jnp.dot((z[:, :D_C] * conv).astype(BF16), wo_ref[...], preferred_element_type=F32)
    y_ref[...] = _sample_ffn(x1, st_ref, g_ref, wu_ref, cw_ref, cb_ref, wd_ref, stn_ref)


def _sample_odd(x, gm, wi, ccw, sc, wo, st, g, wu, cw, cb, wd, layer):
    st_shape = st.shape[1:]
    return pl.pallas_call(
        _sample_odd_body,
        grid=(1,),
        in_specs=[_whole(x), _whole(gm), _whole(wi), _whole(ccw), _whole(sc), _whole(wo),
                  _layer_spec(st_shape, layer), _whole(g), _layer_spec(wu.shape[1:], layer), _whole(cw), _whole(cb),
                  _layer_spec(wd.shape[1:], layer)],
        out_specs=[_whole_out(x.shape), _whole_out(sc.shape), _whole_out(st_shape)],
        out_shape=[jax.ShapeDtypeStruct(x.shape, F32), jax.ShapeDtypeStruct(sc.shape, F32),
                   jax.ShapeDtypeStruct(st_shape, F32)],
        compiler_params=_cparams(1),
        name="sample_odd",
    )(x, gm, wi, ccw, sc, wo, st, g, wu, cw, cb, wd)


def kernel(x_prompt, x_sample, cache_k, cache_v, state_conv_a, state_conv_c, state_ffn, page_table, rel_bias,
           norm_mix_e, w_in_e, conv_a_w, conv_a_b, ln_a_g, ln_a_b, q_norm_g, k_norm_g, w_out_e,
           norm_mix_o, w_in_o, conv_c_w, w_out_o, norm_ffn, w_up, conv_f_w, conv_f_b, w_down):
    b, t, d = x_prompt.shape
    m = x_sample.shape[0]
    n_pool, page = cache_k.shape[1], cache_k.shape[2]
    n_pages = page_table.shape[1]
    assert norm_mix_e.shape[0] == 1 and norm_mix_o.shape[0] == 1 and norm_ffn.shape[0] == 2
    assert x_sample.shape[1] == 1 and t % TILE_M == 0 and t % page == 0
    assert (n_pages * page) % MOBA_BLOCK == 0 and MOBA_BLOCK % page == 0

    row = lambda v: v.reshape(1, -1)
    w_in_e_b = w_in_e[0].astype(BF16)
    w_out_e_b = w_out_e[0].astype(BF16)
    w_in_o_b = w_in_o[0].astype(BF16)
    w_out_o_b = w_out_o[0].astype(BF16)
    w_up_b = w_up.astype(BF16)
    w_down_b = w_down.astype(BF16)
    qg = row(jnp.tile(q_norm_g[0], N_HEADS))
    kg = row(jnp.tile(k_norm_g[0], N_HEADS))
    lane_head = jnp.arange(D_B, dtype=jnp.int32) // HEAD_DIM
    head_ones = (lane_head[:, None] == lane_head[None, :]).astype(BF16)
    g_e, g_o = row(norm_mix_e[0]), row(norm_mix_o[0])
    cab, lag, lab = row(conv_a_b[0]), row(ln_a_g[0]), row(ln_a_b[0])

    a_p, k_p, v_p, qt, kh, vt, km = _prompt_inproj(x_prompt, g_e, w_in_e_b, qg, kg, head_ones, page)
    nb = t // MOBA_BLOCK
    kmh = km.reshape(b, nb, N_HEADS, HEAD_DIM).transpose(0, 2, 1, 3)
    ot = _moba_prompt(rel_bias, _logit_bound(rel_bias, q_norm_g[0], k_norm_g[0]), qt, kh, vt, kmh)
    x1 = _prompt_mix_out(a_p, conv_a_w[0], cab, lag, lab, ot, x_prompt, w_out_e_b)
    x2, f0 = _prompt_ffn(x1, row(norm_ffn[0]), w_up_b, conv_f_w[0], row(conv_f_b[0]), w_down_b, 0)
    x3, c_st = _prompt_mixc(x2, g_o, w_in_o_b, conv_c_w[0], w_out_o_b)
    y_prompt, f1 = _prompt_ffn(x3, row(norm_ffn[1]), w_up_b, conv_f_w[1], row(conv_f_b[1]), w_down_b, 1)

    k_prompt = k_p.transpose(0, 1, 4, 2, 3)[None]
    v_prompt = v_p.transpose(0, 1, 4, 2, 3)[None]
    a_prompt = a_p[:, t - (CONV_A - 1):, :][None]
    c_prompt = c_st[:, SUBLANES - (CONV_C - 1):, :][None]
    f_prompt = jnp.stack([f0[:, SUBLANES - (CONV_F - 1):, :], f1[:, SUBLANES - (CONV_F - 1):, :]])

    xs = x_sample.reshape(m, d)
    sa = state_conv_a[0].transpose(1, 0, 2)
    sc = state_conv_c[0]
    sf = state_ffn
    ao_s, sa_new, q_s, k_s, v_s = _sample_even(xs, g_e, w_in_e_b, qg, kg, head_ones, sa, conv_a_w[0], cab, lag, lab)
    o_s = _moba_sample(page_table, q_s, k_s, v_s, rel_bias.T,
                       cache_k[0].transpose(0, 2, 3, 1), cache_v[0].transpose(0, 2, 3, 1))
    xs1, sf0 = _sample_out_ffn(ao_s, o_s.reshape(m, D_B), xs, w_out_e_b, sf, row(norm_ffn[0]), w_up_b,
                               conv_f_w[0], row(conv_f_b[0]), w_down_b, 0)
    ys, sc_new, sf1 = _sample_odd(xs1, g_o, w_in_o_b, conv_c_w[0], sc, w_out_o_b, sf, row(norm_ffn[1]),
                                  w_up_b, conv_f_w[1], row(conv_f_b[1]), w_down_b, 1)

    y_sample = ys.reshape(m, 1, d)
    k_sample = k_s.reshape(1, m, 1, N_HEADS, HEAD_DIM)
    v_sample = v_s.reshape(1, m, 1, N_HEADS, HEAD_DIM)
    a_sample = sa_new.transpose(1, 0, 2)[None]
    c_sample = sc_new[None]
    f_sample = jnp.stack([sf0, sf1])
    return (y_prompt, y_sample, k_prompt, v_prompt, a_prompt, c_prompt, f_prompt,
            k_sample, v_sample, a_sample, c_sample, f_sample)
```

```python
import functools
import math

import jax
import jax.numpy as jnp
from jax import lax
from jax.experimental import pallas as pl
from jax.experimental.pallas import tpu as pltpu

F32 = jnp.float32
BF16 = jnp.bfloat16

EPS = 1e-6
D_MODEL = 1024
D_A = 512
CONV_A = 31
N_HEADS = 8
HEAD_DIM = 64
D_B = N_HEADS * HEAD_DIM
MOBA_BLOCK = 256
MOBA_TOPK = 3
NUM_BUCKETS = 32
MAX_DISTANCE = 128
D_C = 1024
CONV_C = 3
D_FF = 2816
CONV_F = 3
LOG2E = math.log2(math.e)
Q_SCALE = HEAD_DIM ** -0.5 * LOG2E
VT_ROWS = HEAD_DIM + 16
MAX_SHIFT_RANGE = 100.0

TILE_M = 512
CONV_ROWS = 64
CONV_LANES = 256
HALO_A = 32
HEAD_GROUP = 8
SOFTMAX_ROWS = 64
SEQ_PER_STEP = 4
FAR_UNROLL = 2
BOUNDED_UNROLL = 4
CH = 256
SUBLANES = 8
VMEM_LIMIT = 56 * 1024 * 1024


def _t5_thresholds():
    max_exact = NUM_BUCKETS // 2
    th = list(range(1, max_exact + 1))
    for k in range(1, NUM_BUCKETS - max_exact):
        th.append(math.ceil(max_exact * (MAX_DISTANCE / max_exact) ** (k / (NUM_BUCKETS - max_exact))))
    return tuple(th)


T5_THRESH = _t5_thresholds()


def _cparams(n_grid):
    return pltpu.CompilerParams(dimension_semantics=("arbitrary",) * n_grid, vmem_limit_bytes=VMEM_LIMIT)


def _const_spec(shape):
    nd = len(shape)
    return pl.BlockSpec(shape, lambda *_: (0,) * nd, pipeline_mode=pl.Buffered(1))


def _layer_spec(shape, layer):
    nd = len(shape)
    return pl.BlockSpec((1,) + tuple(shape), lambda *_: (layer,) + (0,) * nd, pipeline_mode=pl.Buffered(1))


def _rmsnorm(x, g):
    return x * lax.rsqrt(jnp.mean(x * x, axis=-1, keepdims=True) + EPS) * g


def _split_dot(x, w_bf16):
    hi = x.astype(BF16)
    lo = (x - hi.astype(F32)).astype(BF16)
    return (jnp.dot(hi, w_bf16, preferred_element_type=F32)
            + jnp.dot(lo, w_bf16, preferred_element_type=F32))


def _head_rmsnorm(x, g, head_ones):
    ss = _split_dot(x * x, head_ones)
    return x * lax.rsqrt(ss * (1.0 / HEAD_DIM) + EPS) * g


def _silu(x):
    return x * jax.nn.sigmoid(x)


def _layernorm(x, g, b):
    mu = jnp.mean(x, axis=-1, keepdims=True)
    xc = x - mu
    var = jnp.mean(xc * xc, axis=-1, keepdims=True)
    return xc * lax.rsqrt(var + EPS) * g + b


def _in_proj_even(x, g, w, qg, kg, head_ones):
    h = _rmsnorm(x, g).astype(BF16)
    z = jnp.dot(h, w, preferred_element_type=F32)
    a = z[:, :D_A] * jax.nn.sigmoid(z[:, D_A:2 * D_A])
    q = _head_rmsnorm(z[:, 2 * D_A:2 * D_A + D_B], qg, head_ones) * Q_SCALE
    k = _head_rmsnorm(z[:, 2 * D_A + D_B:2 * D_A + 2 * D_B], kg, head_ones)
    v = z[:, 2 * D_A + 2 * D_B:]
    return a, q, k, v


def _prompt_inproj_body(x_ref, g_ref, w_ref, qg_ref, kg_ref, ho_ref,
                        a_ref, kp_ref, vp_ref, qt_ref, kh_ref, vt_ref, km_ref, *, tm, page):
    a, q, k, v = _in_proj_even(x_ref[0], g_ref[...], w_ref[...], qg_ref[...], kg_ref[...], ho_ref[...])
    a_ref[0] = a
    qt = q.T
    kt = k.T
    vt = v.T
    for pg in range(tm // page):
        kp_ref[0, pg] = kt[:, pg * page:(pg + 1) * page].reshape(N_HEADS, HEAD_DIM, page)
        vp_ref[0, pg] = vt[:, pg * page:(pg + 1) * page].reshape(N_HEADS, HEAD_DIM, page)
    kb = k.astype(BF16)
    pad_row = lax.broadcasted_iota(jnp.int32, (N_HEADS, VT_ROWS - HEAD_DIM, MOBA_BLOCK), 1)
    ones_rows = jnp.where(pad_row == 0, 1.0, 0.0).astype(BF16)
    for i in range(tm // MOBA_BLOCK):
        r0 = i * MOBA_BLOCK
        qt_ref[0, :, i] = qt[:, r0:r0 + MOBA_BLOCK].reshape(N_HEADS, HEAD_DIM, MOBA_BLOCK).astype(BF16)
        vt_ref[0, :, i, 0:HEAD_DIM, :] = (
            vt[:, r0:r0 + MOBA_BLOCK].reshape(N_HEADS, HEAD_DIM, MOBA_BLOCK).astype(BF16))
        vt_ref[0, :, i, HEAD_DIM:VT_ROWS, :] = ones_rows
        for hh in range(N_HEADS):
            kh_ref[0, hh, i] = kb[r0:r0 + MOBA_BLOCK, hh * HEAD_DIM:(hh + 1) * HEAD_DIM]
        km_ref[0, i] = jnp.mean(k[r0:r0 + MOBA_BLOCK], axis=0, keepdims=True)


def _prompt_inproj(x, g, w, qg, kg, head_ones, page):
    b, t, d = x.shape
    tm = TILE_M
    nt = t // tm
    nb = t // MOBA_BLOCK
    bpt = tm // MOBA_BLOCK
    ppt = tm // page
    n_out = w.shape[1]
    tok = lambda bi, ti: (bi, ti, 0)
    blk5 = lambda bi, ti: (bi, 0, ti, 0, 0)
    pages = lambda bi, ti: (bi, ti, 0, 0, 0)
    return pl.pallas_call(
        functools.partial(_prompt_inproj_body, tm=tm, page=page),
        grid=(b, nt),
        in_specs=[
            pl.BlockSpec((1, tm, d), tok),
            _const_spec((1, d)),
            _const_spec((d, n_out)),
            _const_spec((1, D_B)),
            _const_spec((1, D_B)),
            _const_spec((D_B, D_B)),
        ],
        out_specs=[
            pl.BlockSpec((1, tm, D_A), tok),
            pl.BlockSpec((1, ppt, N_HEADS, HEAD_DIM, page), pages),
            pl.BlockSpec((1, ppt, N_HEADS, HEAD_DIM, page), pages),
            pl.BlockSpec((1, N_HEADS, bpt, HEAD_DIM, MOBA_BLOCK), blk5),
            pl.BlockSpec((1, N_HEADS, bpt, MOBA_BLOCK, HEAD_DIM), blk5),
            pl.BlockSpec((1, N_HEADS, bpt, VT_ROWS, MOBA_BLOCK), blk5),
            pl.BlockSpec((1, bpt, 1, D_B), lambda bi, ti: (bi, ti, 0, 0)),
        ],
        out_shape=[
            jax.ShapeDtypeStruct((b, t, D_A), F32),
            jax.ShapeDtypeStruct((b, t // page, N_HEADS, HEAD_DIM, page), F32),
            jax.ShapeDtypeStruct((b, t // page, N_HEADS, HEAD_DIM, page), F32),
            jax.ShapeDtypeStruct((b, N_HEADS, nb, HEAD_DIM, MOBA_BLOCK), BF16),
            jax.ShapeDtypeStruct((b, N_HEADS, nb, MOBA_BLOCK, HEAD_DIM), BF16),
            jax.ShapeDtypeStruct((b, N_HEADS, nb, VT_ROWS, MOBA_BLOCK), BF16),
            jax.ShapeDtypeStruct((b, nb, 1, D_B), F32),
        ],
        compiler_params=_cparams(2),
        name="prompt_inproj",
    )(x, g, w, qg, kg, head_ones)


def _t5_bias_scalar_table(dist, rb_ref, h):
    out = jnp.full(dist.shape, rb_ref[0, h] * LOG2E, F32)
    for i, th in enumerate(T5_THRESH):
        out = jnp.where(dist >= th, rb_ref[i + 1, h] * LOG2E, out)
    return out


def _moba_prompt_body(rb_ref, ctl_ref, qt_ref, kh_ref, vt_ref, km_ref, ot_ref,
                      bdiag_ref, bsub_ref, selb_ref, m_ref, alpha_ref, acc_ref, s_ref, p_ref, *, nb, hg):
    h0 = pl.program_id(1) * hg
    n = pl.program_id(2)
    blk = MOBA_BLOCK
    bounded = ctl_ref[1, 0] > 0.5
    shifts = [jnp.where(bounded, ctl_ref[0, h0 + g], 0.0) for g in range(hg)]

    @pl.when(n == 0)
    def _():
        ki = lax.broadcasted_iota(jnp.int32, (blk, blk), 0)
        qi = lax.broadcasted_iota(jnp.int32, (blk, blk), 1)
        d0 = qi - ki
        for g in range(hg):
            bdiag_ref[g] = jnp.where(d0 >= 0, _t5_bias_scalar_table(jnp.maximum(d0, 0), rb_ref, h0 + g),
                                     -jnp.inf) - shifts[g]
            bsub_ref[g] = _t5_bias_scalar_table(d0 + blk, rb_ref, h0 + g) - shifts[g]

    bi = lax.broadcasted_iota(jnp.int32, (nb, blk), 0)
    qts = [qt_ref[0, g, 0] for g in range(hg)]

    def select(g):
        km = km_ref[0, g]
        km_hi = km.astype(BF16)
        km_lo = (km - km_hi.astype(F32)).astype(BF16)
        gate = (jnp.dot(km_hi, qts[g], preferred_element_type=F32)
                + jnp.dot(km_lo, qts[g], preferred_element_type=F32))
        avail = jnp.where(bi < n, 1.0, 0.0)
        far_bias = rb_ref[NUM_BUCKETS - 1, h0 + g] * LOG2E - shifts[g]
        selb = jnp.full((nb, blk), -jnp.inf, F32)
        for _ in range(MOBA_TOPK):
            gm = jnp.where(avail > 0.0, gate, -jnp.inf)
            top = jnp.max(gm, axis=0, keepdims=True)
            first = jnp.where(avail > 0.0, jnp.where(gm == top, bi, nb), nb)
            pick = bi == jnp.min(first, axis=0, keepdims=True)
            selb = jnp.where(pick, jnp.where(bi == n - 1, 0.0, far_bias), selb)
            avail = jnp.where(pick, 0.0, avail)
        selb_ref[g] = selb

    n_chunks = blk // SOFTMAX_ROWS

    def logits(g, j, slot, bias):
        s = jnp.dot(kh_ref[0, g, j], qts[g], preferred_element_type=F32)
        s_ref[slot, g] = s if bias is None else s + bias

    def chunk(g, slot, c):
        return s_ref[slot, g, c * SOFTMAX_ROWS:(c + 1) * SOFTMAX_ROWS, :].reshape(
            SOFTMAX_ROWS // SUBLANES, SUBLANES, blk)

    def softmax(g, slot, row):
        m8 = jnp.max(chunk(g, slot, 0), axis=0)
        for c in range(1, n_chunks):
            m8 = jnp.maximum(m8, jnp.max(chunk(g, slot, c), axis=0))
        m_blk = jnp.max(m8, axis=0, keepdims=True)
        if row is None:
            mn = m_blk
            shift = mn
        else:
            m_old = m_ref[g]
            mn = jnp.maximum(m_old, m_blk + row)
            alpha_ref[slot, g] = jnp.exp2(m_old - mn)
            shift = mn - row
        for c in range(n_chunks):
            p = jnp.exp2(chunk(g, slot, c) - shift)
            p_ref[slot, g, c * SOFTMAX_ROWS:(c + 1) * SOFTMAX_ROWS, :] = (
                p.reshape(SOFTMAX_ROWS, blk).astype(BF16))
        m_ref[g] = mn

    def weighted_values(g, j, slot, first):
        pv = jnp.dot(vt_ref[0, g, j], p_ref[slot, g], preferred_element_type=F32)
        acc_ref[g] = pv if first else alpha_ref[slot, g] * acc_ref[g] + pv

    def far_trips(trip, unroll):
        def far(i, carry):
            for r in range(unroll):
                trip(unroll * i + r, r & 1)
            return carry

        n_far = n - 1
        lax.fori_loop(0, n_far // unroll, far, 0)
        done = (n_far // unroll) * unroll
        piece = unroll // 2
        while piece >= 1:
            @pl.when(((n_far - done) & piece) != 0)
            def _(piece=piece, done=done):
                for r in range(piece):
                    trip(done + r, r & 1)
            done = done + ((n_far - done) & piece)
            piece //= 2

    last = jnp.where(n >= 2, n - 2, n - 1)

    @pl.when(bounded)
    def _():
        def probabilities(g, j, slot, bias):
            s = jnp.dot(kh_ref[0, g, j], qts[g], preferred_element_type=F32)
            p_ref[slot, g] = jnp.exp2(s + bias).astype(BF16)

        def add_values(g, j, slot, first):
            pv = jnp.dot(vt_ref[0, g, j], p_ref[slot, g], preferred_element_type=F32)
            acc_ref[g] = pv if first else acc_ref[g] + pv

        for g in range(hg):
            probabilities(g, n, 0, bdiag_ref[g])
        for g in range(hg):
            select(g)

        @pl.when(n == 0)
        def _():
            for g in range(hg):
                add_values(g, n, 0, True)

        @pl.when(n >= 1)
        def _():
            for g in range(hg):
                probabilities(g, n - 1, 1, bsub_ref[g] + selb_ref[g, pl.ds(n - 1, 1), :])
            for g in range(hg):
                add_values(g, n, 0, True)

            def trip(j, slot):
                prev = jnp.where(j == 0, n - 1, j - 1)
                for g in range(hg):
                    probabilities(g, j, slot, selb_ref[g, pl.ds(j, 1), :])
                for g in range(hg):
                    add_values(g, prev, 1 - slot, False)

            far_trips(trip, BOUNDED_UNROLL)
            for g in range(hg):
                add_values(g, last, n & 1, False)

    @pl.when(jnp.logical_not(bounded))
    def _():
        for g in range(hg):
            logits(g, n, 0, bdiag_ref[g])
        for g in range(hg):
            select(g)
        for g in range(hg):
            logits(g, jnp.maximum(n - 1, 0), 1, bsub_ref[g])
        for g in range(hg):
            softmax(g, 0, None)

        @pl.when(n == 0)
        def _():
            for g in range(hg):
                weighted_values(g, n, 0, True)

        @pl.when(n >= 1)
        def _():
            for g in range(hg):
                weighted_values(g, n, 0, True)
                softmax(g, 1, selb_ref[g, pl.ds(n - 1, 1), :])
                logits(g, 0, 0, None)

            def trip(j, slot):
                nxt = jnp.minimum(j + 1, n - 2)
                prev = jnp.where(j == 0, n - 1, j - 1)
                for g in range(hg):
                    logits(g, nxt, 1 - slot, None)
                for g in range(hg):
                    weighted_values(g, prev, 1 - slot, False)
                for g in range(hg):
                    softmax(g, slot, selb_ref[g, pl.ds(j, 1), :])

            far_trips(trip, FAR_UNROLL)
            for g in range(hg):
                weighted_values(g, last, n & 1, False)

    for g in range(hg):
        ot_ref[0, g * HEAD_DIM:(g + 1) * HEAD_DIM, :] = (
            acc_ref[g, 0:HEAD_DIM, :] / acc_ref[g, HEAD_DIM:HEAD_DIM + 1, :])


def _logit_bound(rel_bias, q_gain, k_gain):
    qk = 1.02 * HEAD_DIM * Q_SCALE * jnp.max(jnp.abs(q_gain)) * jnp.max(jnp.abs(k_gain))
    table = rel_bias * LOG2E
    top, low = jnp.max(table, axis=0), jnp.min(table, axis=0)
    usable = jnp.all(2.0 * qk + (top - low) <= MAX_SHIFT_RANGE)
    return jnp.stack([qk + top, jnp.broadcast_to(usable.astype(F32), top.shape)])


def _moba_prompt(rel_bias, ctl, qt, kh, vt, kmh):
    b, nh, nb, hd, blk = qt.shape
    t = nb * blk
    hg = HEAD_GROUP
    grp = lambda bi, hi, ni: (bi, hi, 0, 0, 0)
    return pl.pallas_call(
        functools.partial(_moba_prompt_body, nb=nb, hg=hg),
        grid=(b, nh // hg, nb),
        in_specs=[
            pl.BlockSpec(memory_space=pltpu.SMEM),
            pl.BlockSpec(memory_space=pltpu.SMEM),
            pl.BlockSpec((1, hg, 1, hd, blk), lambda bi, hi, ni: (bi, hi, ni, 0, 0)),
            pl.BlockSpec((1, hg, nb, blk, hd), grp, pipeline_mode=pl.Buffered(1)),
            pl.BlockSpec((1, hg, nb, VT_ROWS, blk), grp, pipeline_mode=pl.Buffered(1)),
            pl.BlockSpec((1, hg, nb, hd), lambda bi, hi, ni: (bi, hi, 0, 0)),
        ],
        out_specs=pl.BlockSpec((1, hg * hd, blk), lambda bi, hi, ni: (bi, hi, ni)),
        out_shape=jax.ShapeDtypeStruct((b, nh * hd, t), F32),
        scratch_shapes=[
            pltpu.VMEM((hg, blk, blk), F32),
            pltpu.VMEM((hg, blk, blk), F32),
            pltpu.VMEM((hg, nb, blk), F32),
            pltpu.VMEM((hg, 1, blk), F32),
            pltpu.VMEM((2, hg, 1, blk), F32),
            pltpu.VMEM((hg, VT_ROWS, blk), F32),
            pltpu.VMEM((2, hg, blk, blk), F32),
            pltpu.VMEM((2, hg, blk, blk), BF16),
        ],
        compiler_params=_cparams(3),
        name="prompt_moba",
    )(rel_bias, ctl, qt, kh, vt, kmh)


def _prompt_mix_out_body(a_ref, halo_ref, cw_ref, cb_ref, lg_ref, lb_ref, ot_ref, x_ref, w_ref,
                         y_ref, sh_ref, conv_ref, wb_ref, *, tm):
    t = pl.program_id(1)
    rows = tm + HALO_A

    @pl.when((pl.program_id(0) == 0) & (t == 0))
    def _():
        for j in range(CONV_A):
            wb_ref[j] = jnp.broadcast_to(cw_ref[j:j + 1, :], (SUBLANES, D_A))

    sh_ref[0, 0:HALO_A] = jnp.where(t == 0, 0.0, halo_ref[0])
    sh_ref[0, HALO_A:rows] = a_ref[0]
    for s in range(1, SUBLANES):
        sh_ref[s, 0:rows - SUBLANES] = sh_ref[0, s:s + rows - SUBLANES]

    groups = CONV_ROWS // SUBLANES
    first_off = HALO_A - (CONV_A - 1)

    def chunk(c, carry):
        r0 = pl.multiple_of(c * CONV_ROWS, CONV_ROWS)
        for l0 in range(0, D_A, CONV_LANES):
            lanes = slice(l0, l0 + CONV_LANES)
            acc = [jnp.zeros((SUBLANES, CONV_LANES), F32) for _ in range(groups)]
            for s in range(SUBLANES):
                offs = [o for o in range(first_off, first_off + CONV_A) if o % SUBLANES == s]
                qs = [o // SUBLANES for o in offs]
                slab = {gq: sh_ref[s, pl.ds(r0 + gq * SUBLANES, SUBLANES), lanes]
                        for gq in range(min(qs), max(qs) + groups)}
                for o, q in zip(offs, qs):
                    w = wb_ref[o - first_off, :, lanes]
                    for g in range(groups):
                        acc[g] = acc[g] + w * slab[g + q]
            for g in range(groups):
                conv_ref[pl.ds(r0 + g * SUBLANES, SUBLANES), lanes] = acc[g]
        return carry

    lax.fori_loop(0, tm // CONV_ROWS, chunk, 0)

    ao = _silu(_layernorm(conv_ref[...] + cb_ref[...], lg_ref[...], lb_ref[...])).astype(BF16)
    o = ot_ref[0].T.astype(BF16)
    y = (jnp.dot(ao, w_ref[0:D_A, :], preferred_element_type=F32)
         + jnp.dot(o, w_ref[D_A:D_A + D_B, :], preferred_element_type=F32))
    y_ref[0] = x_ref[0] + y


def _prompt_mix_out(a, cw, cb, lg, lb, ot, x, w):
    b, t, d = x.shape
    tm = TILE_M
    nt = t // tm
    hpt = tm // HALO_A
    tok = lambda bi, ti: (bi, ti, 0)
    return pl.pallas_call(
        functools.partial(_prompt_mix_out_body, tm=tm),
        grid=(b, nt),
        in_specs=[
            pl.BlockSpec((1, tm, D_A), tok),
            pl.BlockSpec((1, HALO_A, D_A), lambda bi, ti: (bi, jnp.maximum(ti * hpt - 1, 0), 0)),
            _const_spec((CONV_A, D_A)),
            _const_spec((1, D_A)),
            _const_spec((1, D_A)),
            _const_spec((1, D_A)),
            pl.BlockSpec((1, D_B, tm), lambda bi, ti: (bi, 0, ti)),
            pl.BlockSpec((1, tm, d), tok),
            _const_spec((D_A + D_B, d)),
        ],
        out_specs=pl.BlockSpec((1, tm, d), tok),
        out_shape=jax.ShapeDtypeStruct((b, t, d), F32),
        scratch_shapes=[
            pltpu.VMEM((SUBLANES, tm + HALO_A, D_A), F32),
            pltpu.VMEM((tm, D_A), F32),
            pltpu.VMEM((CONV_A, SUBLANES, D_A), F32),
        ],
        compiler_params=_cparams(2),
        name="prompt_mix_out",
    )(a, a, cw, cb, lg, lb, ot, x, w)


def _conv3_chunk(u, ext_ref, carry_ref, cw_ref, c0, tm):
    ext_ref[0:SUBLANES] = carry_ref[:, c0:c0 + CH]
    ext_ref[SUBLANES:SUBLANES + tm] = u
    carry_ref[:, c0:c0 + CH] = u[tm - SUBLANES:tm]
    return (cw_ref[0:1, c0:c0 + CH] * ext_ref[SUBLANES - 2:SUBLANES - 2 + tm]
            + cw_ref[1:2, c0:c0 + CH] * ext_ref[SUBLANES - 1:SUBLANES - 1 + tm]
            + cw_ref[2:3, c0:c0 + CH] * u)


def _prompt_ffn_body(x_ref, g_ref, wu_ref, cw_ref, cb_ref, wd_ref, y_ref, st_ref,
                     carry_ref, extg_ref, extu_ref, act_ref, *, tm):
    @pl.when(pl.program_id(1) == 0)
    def _():
        carry_ref[...] = jnp.zeros_like(carry_ref)

    x = x_ref[0]
    h = _rmsnorm(x, g_ref[...]).astype(BF16)
    for c in range(D_FF // CH):
        cg = c * CH
        cu = D_FF + c * CH
        ug = jnp.dot(h, wu_ref[0, :, cg:cg + CH], preferred_element_type=F32)
        uu = jnp.dot(h, wu_ref[0, :, cu:cu + CH], preferred_element_type=F32)
        gg = _conv3_chunk(ug, extg_ref, carry_ref, cw_ref, cg, tm) + cb_ref[:, cg:cg + CH]
        gu = _conv3_chunk(uu, extu_ref, carry_ref, cw_ref, cu, tm) + cb_ref[:, cu:cu + CH]
        act_ref[:, cg:cg + CH] = (_silu(gg) * gu).astype(BF16)
    st_ref[0] = carry_ref[...]
    y_ref[0] = x + jnp.dot(act_ref[...], wd_ref[0], preferred_element_type=F32)


def _prompt_ffn(x, g, wu, cw, cb, wd, layer):
    b, t, d = x.shape
    tm = TILE_M
    nt = t // tm
    tok = lambda bi, ti: (bi, ti, 0)
    return pl.pallas_call(
        functools.partial(_prompt_ffn_body, tm=tm),
        grid=(b, nt),
        in_specs=[
            pl.BlockSpec((1, tm, d), tok),
            _const_spec((1, d)),
            _layer_spec((d, 2 * D_FF), layer),
            _const_spec((CONV_F, 2 * D_FF)),
            _const_spec((1, 2 * D_FF)),
            _layer_spec((D_FF, d), layer),
        ],
        out_specs=[
            pl.BlockSpec((1, tm, d), tok),
            pl.BlockSpec((1, SUBLANES, 2 * D_FF), lambda bi, ti: (bi, 0, 0)),
        ],
        out_shape=[
            jax.ShapeDtypeStruct((b, t, d), F32),
            jax.ShapeDtypeStruct((b, SUBLANES, 2 * D_FF), F32),
        ],
        scratch_shapes=[
            pltpu.VMEM((SUBLANES, 2 * D_FF), F32),
            pltpu.VMEM((tm + SUBLANES, CH), F32),
            pltpu.VMEM((tm + SUBLANES, CH), F32),
            pltpu.VMEM((tm, D_FF), BF16),
        ],
        compiler_params=_cparams(2),
        name="prompt_ffn",
    )(x, g, wu, cw, cb, wd)


def _prompt_mixc_body(x_ref, g_ref, wi_ref, cw_ref, wo_ref, y_ref, st_ref,
                      carry_ref, ext_ref, z_ref, *, tm):
    @pl.when(pl.program_id(1) == 0)
    def _():
        carry_ref[...] = jnp.zeros_like(carry_ref)

    x = x_ref[0]
    h = _rmsnorm(x, g_ref[...]).astype(BF16)
    for c in range(D_C // CH):
        c0 = c * CH
        gb = jnp.dot(h, wi_ref[:, c0:c0 + CH], preferred_element_type=F32)
        gc = jnp.dot(h, wi_ref[:, D_C + c0:D_C + c0 + CH], preferred_element_type=F32)
        u = jnp.dot(h, wi_ref[:, 2 * D_C + c0:2 * D_C + c0 + CH], preferred_element_type=F32)
        conv = _conv3_chunk(gc * u, ext_ref, carry_ref, cw_ref, c0, tm)
        z_ref[:, c0:c0 + CH] = (gb * conv).astype(BF16)
    st_ref[0] = carry_ref[...]
    y_ref[0] = x + jnp.dot(z_ref[...], wo_ref[...], preferred_element_type=F32)


def _prompt_mixc(x, g, wi, cw, wo):
    b, t, d = x.shape
    tm = TILE_M
    nt = t // tm
    tok = lambda bi, ti: (bi, ti, 0)
    return pl.pallas_call(
        functools.partial(_prompt_mixc_body, tm=tm),
        grid=(b, nt),
        in_specs=[
            pl.BlockSpec((1, tm, d), tok),
            _const_spec((1, d)),
            _const_spec((d, 3 * D_C)),
            _const_spec((CONV_C, D_C)),
            _const_spec((D_C, d)),
        ],
        out_specs=[
            pl.BlockSpec((1, tm, d), tok),
            pl.BlockSpec((1, SUBLANES, D_C), lambda bi, ti: (bi, 0, 0)),
        ],
        out_shape=[
            jax.ShapeDtypeStruct((b, t, d), F32),
            jax.ShapeDtypeStruct((b, SUBLANES, D_C), F32),
        ],
        scratch_shapes=[
            pltpu.VMEM((SUBLANES, D_C), F32),
            pltpu.VMEM((tm + SUBLANES, CH), F32),
            pltpu.VMEM((tm, D_C), BF16),
        ],
        compiler_params=_cparams(2),
        name="prompt_mixc",
    )(x, g, wi, cw, wo)


def _sample_even_body(x_ref, g_ref, w_ref, qg_ref, kg_ref, ho_ref, sa_ref, cw_ref, cb_ref, lg_ref, lb_ref,
                      ao_ref, san_ref, q_ref, k_ref, v_ref):
    a, q, k, v = _in_proj_even(x_ref[...], g_ref[...], w_ref[...], qg_ref[...], kg_ref[...], ho_ref[...])
    q_ref[...] = q
    k_ref[...] = k
    v_ref[...] = v
    hist = CONV_A - 1
    conv = cb_ref[...] + cw_ref[hist:hist + 1, :] * a
    for j in range(hist):
        conv = conv + cw_ref[j:j + 1, :] * sa_ref[j]
    ao_ref[...] = _silu(_layernorm(conv, lg_ref[...], lb_ref[...])).astype(BF16)
    san_ref[0:hist - 1] = sa_ref[1:hist]
    san_ref[hist - 1] = a


def _sample_even(x, g, w, qg, kg, head_ones, sa, cw, cb, lg, lb):
    m = x.shape[0]
    return pl.pallas_call(
        _sample_even_body,
        out_shape=[
            jax.ShapeDtypeStruct((m, D_A), BF16),
            jax.ShapeDtypeStruct(sa.shape, F32),
            jax.ShapeDtypeStruct((m, D_B), F32),
            jax.ShapeDtypeStruct((m, D_B), F32),
            jax.ShapeDtypeStruct((m, D_B), F32),
        ],
        compiler_params=pltpu.CompilerParams(vmem_limit_bytes=VMEM_LIMIT),
        name="sample_even",
    )(x, g, w, qg, kg, head_ones, sa, cw, cb, lg, lb)


def _moba_sample_keys_body(pt_ref, q_ref, qdh_ref, kn_ref, rbt_ref, *refs, n_pages, page):
    del pt_ref
    outs = refs[SEQ_PER_STEP * n_pages:]
    bias_ref = outs[4]
    past = n_pages * page

    @pl.when(pl.program_id(0) == 0)
    def _():
        pos = lax.broadcasted_iota(jnp.int32, (N_HEADS, past), 1)
        dist = past - pos
        out = jnp.zeros((N_HEADS, past), F32) + rbt_ref[:, 0:1]
        for i, th in enumerate(T5_THRESH):
            out = jnp.where(dist >= th, rbt_ref[:, i + 1:i + 2], out)
        bias_ref[...] = out * LOG2E

    for sq in range(SEQ_PER_STEP):
        _moba_sample_keys_one(sq, q_ref, qdh_ref, kn_ref, rbt_ref, refs[sq * n_pages:(sq + 1) * n_pages], *outs,
                              n_pages=n_pages, page=page)


def _moba_sample_keys_one(sq, q_ref, qdh_ref, kn_ref, rbt_ref, k_pages, pc_ref, pn_ref, l_ref, sel_ref,
                          bias_ref, qb_ref, s_ref, *, n_pages, page):
    past = n_pages * page
    n_past_blocks = past // MOBA_BLOCK
    n_sel = min(MOBA_TOPK, n_past_blocks)
    qb_ref = qb_ref.at[sq]
    s_ref = s_ref.at[sq]

    head_of_lane = lax.broadcasted_iota(jnp.int32, (N_HEADS, D_B), 1) // HEAD_DIM
    hmask = head_of_lane == lax.broadcasted_iota(jnp.int32, (N_HEADS, D_B), 0)
    qbd = jnp.where(hmask, q_ref[sq], 0.0)

    qdh = qdh_ref[sq]
    for h in range(N_HEADS):
        qb_ref[h] = jnp.broadcast_to(qdh[:, h:h + 1], (HEAD_DIM, page))

    for pg in range(n_pages):
        rows = [jnp.sum(k_pages[pg][0, h] * qb_ref[h], axis=0, keepdims=True) for h in range(N_HEADS)]
        s_ref[:, pg * page:(pg + 1) * page] = jnp.concatenate(rows, axis=0)

    gates = [jnp.sum(s_ref[:, j * MOBA_BLOCK:(j + 1) * MOBA_BLOCK], axis=-1, keepdims=True)
             for j in range(n_past_blocks)]
    ranks = []
    pieces = []
    for j in range(n_past_blocks):
        rank = jnp.zeros((N_HEADS, 1), jnp.int32)
        for i in range(n_past_blocks):
            if i == j:
                continue
            ahead = (gates[i] >= gates[j]) if i < j else (gates[i] > gates[j])
            rank = rank + jnp.where(ahead, 1, 0)
        ranks.append(rank)
        lanes = slice(j * MOBA_BLOCK, (j + 1) * MOBA_BLOCK)
        pieces.append(s_ref[:, lanes] + bias_ref[:, lanes] + jnp.where(rank < n_sel, 0.0, -jnp.inf))
    s = jnp.concatenate(pieces, axis=-1)

    s_new = jnp.sum(qbd * kn_ref[sq], axis=-1, keepdims=True) + rbt_ref[:, 0:1] * LOG2E
    m = jnp.maximum(jnp.max(s, axis=-1, keepdims=True), s_new)
    p = jnp.exp2(s - m)
    p_new = jnp.exp2(s_new - m)
    pn_ref[sq] = jnp.broadcast_to(p_new, (N_HEADS, 128))
    l_ref[sq] = jnp.broadcast_to(jnp.sum(p, axis=-1, keepdims=True) + p_new, (N_HEADS, 128))

    lane = lax.broadcasted_iota(jnp.int32, (N_HEADS, 128), 1)
    sel = jnp.zeros((N_HEADS, 128), jnp.int32)
    for r in range(n_sel):
        picked = jnp.zeros((N_HEADS, MOBA_BLOCK), F32)
        block_id = jnp.zeros((N_HEADS, 1), jnp.int32)
        for j in range(n_past_blocks):
            picked = picked + jnp.where(ranks[j] == r, p[:, j * MOBA_BLOCK:(j + 1) * MOBA_BLOCK], 0.0)
            block_id = block_id + jnp.where(ranks[j] == r, j, 0)
        pc_ref[sq, :, r * MOBA_BLOCK:(r + 1) * MOBA_BLOCK] = picked
        sel = jnp.where(lane == r, block_id, sel)
    sel_ref[sq] = sel


def _moba_sample_values_body(tile_page_ref, pc_ref, pn_ref, l_ref, vn_ref, v_hbm, o_ref, buf_ref, sem_ref,
                             *, n_tiles, page):
    b = pl.program_id(0)

    def tile_copy(seq, slot, h, i):
        pool_page = tile_page_ref[seq * (N_HEADS * n_tiles) + h * n_tiles + i]
        return pltpu.make_async_copy(v_hbm.at[pool_page, h], buf_ref.at[slot, h * n_tiles + i], sem_ref.at[slot])

    def for_all_tiles(seq, slot, act):
        for h in range(N_HEADS):
            for i in range(n_tiles):
                act(tile_copy(seq, slot, h, i))

    @pl.when(b == 0)
    def _():
        for_all_tiles(0, 0, lambda cp: cp.start())

    @pl.when(b + 1 < pl.num_programs(0))
    def _():
        for_all_tiles(b + 1, (b + 1) % 2, lambda cp: cp.start())

    slot = b % 2
    for_all_tiles(b, slot, lambda cp: cp.wait())

    head_of_lane = lax.broadcasted_iota(jnp.int32, (N_HEADS, D_B), 1) // HEAD_DIM
    hmask = head_of_lane == lax.broadcasted_iota(jnp.int32, (N_HEADS, D_B), 0)
    ones = jnp.ones((SUBLANES, page), BF16)
    lane_sum = lambda x: lax.dot_general(ones, x, (((1,), (1,)), ((), ())), preferred_element_type=F32)
    outs = []
    for h in range(N_HEADS):
        acc = jnp.zeros((HEAD_DIM, page), F32)
        for i in range(n_tiles):
            acc = acc + buf_ref[slot, h * n_tiles + i] * pc_ref[0, h:h + 1, i * page:(i + 1) * page]
        hi = acc.astype(BF16)
        lo = (acc - hi.astype(F32)).astype(BF16)
        outs.append((lane_sum(hi) + lane_sum(lo))[0:1, :])
    o_past = jnp.concatenate(outs, axis=-1)
    spread = lambda col: jnp.sum(jnp.where(hmask, col, 0.0), axis=0, keepdims=True)
    o_ref[0] = (o_past + spread(pn_ref[0, :, 0:1]) * vn_ref[0]) / spread(l_ref[0, :, 0:1])


def _moba_sample(page_table, q, k_new, v_new, rel_bias_t, cache_kt, cache_vt):
    m, n_pages = page_table.shape
    page = cache_kt.shape[-1]
    past = n_pages * page
    n_sel = min(MOBA_TOPK, past // MOBA_BLOCK)
    ppb = MOBA_BLOCK // page
    row = lambda bi, *_: (bi, 0, 0)

    sps = SEQ_PER_STEP
    assert m % sps == 0
    keys_spec = pltpu.PrefetchScalarGridSpec(
        num_scalar_prefetch=1,
        grid=(m // sps,),
        in_specs=[
            pl.BlockSpec((sps, 1, D_B), row),
            pl.BlockSpec((sps, HEAD_DIM, N_HEADS), row),
            pl.BlockSpec((sps, 1, D_B), row),
            pl.BlockSpec((N_HEADS, NUM_BUCKETS), lambda bi, pt: (0, 0)),
        ] + [pl.BlockSpec((1, N_HEADS, HEAD_DIM, page), lambda bi, pt, sq=sq, pg=pg: (pt[bi * sps + sq, pg], 0, 0, 0))
             for sq in range(sps) for pg in range(n_pages)],
        out_specs=[
            pl.BlockSpec((sps, N_HEADS, n_sel * MOBA_BLOCK), row),
            pl.BlockSpec((sps, N_HEADS, 128), row),
            pl.BlockSpec((sps, N_HEADS, 128), row),
            pl.BlockSpec((sps, N_HEADS, 128), row),
        ],
        scratch_shapes=[
            pltpu.VMEM((N_HEADS, past), F32),
            pltpu.VMEM((sps, N_HEADS, HEAD_DIM, page), F32),
            pltpu.VMEM((sps, N_HEADS, past), F32),
        ],
    )
    q_dh = q.reshape(m, N_HEADS, HEAD_DIM).transpose(0, 2, 1)
    pc, pn, l, sel = pl.pallas_call(
        functools.partial(_moba_sample_keys_body, n_pages=n_pages, page=page),
        grid_spec=keys_spec,
        out_shape=[
            jax.ShapeDtypeStruct((m, N_HEADS, n_sel * MOBA_BLOCK), F32),
            jax.ShapeDtypeStruct((m, N_HEADS, 128), F32),
            jax.ShapeDtypeStruct((m, N_HEADS, 128), F32),
            jax.ShapeDtypeStruct((m, N_HEADS, 128), jnp.int32),
        ],
        compiler_params=_cparams(1),
        name="sample_moba_keys",
    )(page_table, q.reshape(m, 1, D_B), q_dh, k_new.reshape(m, 1, D_B), rel_bias_t, *([cache_kt] * (sps * n_pages)))

    blocks = sel[:, :, :n_sel, None] * ppb + jnp.arange(ppb, dtype=jnp.int32)
    tile_pages = jnp.take_along_axis(page_table, blocks.reshape(m, -1), axis=1).reshape(-1)
    values_spec = pltpu.PrefetchScalarGridSpec(
        num_scalar_prefetch=1,
        grid=(m,),
        in_specs=[
            pl.BlockSpec((1, N_HEADS, n_sel * MOBA_BLOCK), row),
            pl.BlockSpec((1, N_HEADS, 128), row),
            pl.BlockSpec((1, N_HEADS, 128), row),
            pl.BlockSpec((1, 1, D_B), row),
            pl.BlockSpec(memory_space=pl.ANY),
        ],
        out_specs=pl.BlockSpec((1, 1, D_B), row),
        scratch_shapes=[
            pltpu.VMEM((2, N_HEADS * n_sel * ppb, HEAD_DIM, page), F32),
            pltpu.SemaphoreType.DMA((2,)),
        ],
    )
    return pl.pallas_call(
        functools.partial(_moba_sample_values_body, n_tiles=n_sel * ppb, page=page),
        grid_spec=values_spec,
        out_shape=jax.ShapeDtypeStruct((m, 1, D_B), F32),
        compiler_params=_cparams(1),
        name="sample_moba_values",
    )(tile_pages, pc, pn, l, v_new.reshape(m, 1, D_B), cache_vt)


def _sample_ffn(x, st_ref, g_ref, wu_ref, cw_ref, cb_ref, wd_ref, stn_ref):
    h = _rmsnorm(x, g_ref[...]).astype(BF16)
    up = jnp.dot(h, wu_ref[0], preferred_element_type=F32)
    older, newer = st_ref[0, :, 0, :], st_ref[0, :, 1, :]
    conv = cw_ref[0:1, :] * older + cw_ref[1:2, :] * newer + cw_ref[2:3, :] * up + cb_ref[...]
    act = (_silu(conv[:, :D_FF]) * conv[:, D_FF:]).astype(BF16)
    stn_ref[:, 0, :] = newer
    stn_ref[:, 1, :] = up
    return x + jnp.dot(act, wd_ref[0], preferred_element_type=F32)


def _sample_out_ffn_body(ao_ref, o_ref, x_ref, wo_ref, st_ref, g_ref, wu_ref, cw_ref, cb_ref, wd_ref,
                         y_ref, stn_ref):
    x1 = (x_ref[...]
          + jnp.dot(ao_ref[...], wo_ref[0:D_A, :], preferred_element_type=F32)
          + jnp.dot(o_ref[...].astype(BF16), wo_ref[D_A:D_A + D_B, :], preferred_element_type=F32))
    y_ref[...] = _sample_ffn(x1, st_ref, g_ref, wu_ref, cw_ref, cb_ref, wd_ref, stn_ref)


def _whole(a):
    return _const_spec(a.shape)


def _whole_out(shape):
    nd = len(shape)
    return pl.BlockSpec(tuple(shape), lambda *_: (0,) * nd)


def _sample_out_ffn(ao, o, x, wo, st, g, wu, cw, cb, wd, layer):
    st_shape = st.shape[1:]
    return pl.pallas_call(
        _sample_out_ffn_body,
        grid=(1,),
        in_specs=[_whole(ao), _whole(o), _whole(x), _whole(wo), _layer_spec(st_shape, layer), _whole(g),
                  _layer_spec(wu.shape[1:], layer), _whole(cw), _whole(cb), _layer_spec(wd.shape[1:], layer)],
        out_specs=[_whole_out(x.shape), _whole_out(st_shape)],
        out_shape=[jax.ShapeDtypeStruct(x.shape, F32), jax.ShapeDtypeStruct(st_shape, F32)],
        compiler_params=_cparams(1),
        name="sample_out_ffn",
    )(ao, o, x, wo, st, g, wu, cw, cb, wd)


def _sample_odd_body(x_ref, gm_ref, wi_ref, ccw_ref, sc_ref, wo_ref, st_ref, g_ref, wu_ref, cw_ref, cb_ref,
                     wd_ref, y_ref, scn_ref, stn_ref):
    x = x_ref[...]
    h = _rmsnorm(x, gm_ref[...]).astype(BF16)
    z = jnp.dot(h, wi_ref[...], preferred_element_type=F32)
    gcu = z[:, D_C:2 * D_C] * z[:, 2 * D_C:]
    older, newer = sc_ref[:, 0, :], sc_ref[:, 1, :]
    conv = ccw_ref[0:1, :] * older + ccw_ref[1:2, :] * newer + ccw_ref[2:3, :] * gcu
    scn_ref[:, 0, :] = newer
    scn_ref[:, 1, :] = gcu
    x1 = x + jnp.dot((z[:, :D_C] * conv).astype(BF16), wo_ref[...], preferred_element_type=F32)
    y_ref[...] = _sample_ffn(x1, st_ref, g_ref, wu_ref, cw_ref, cb_ref, wd_ref, stn_ref)


def _sample_odd(x, gm, wi, ccw, sc, wo, st, g, wu, cw, cb, wd, layer):
    st_shape = st.shape[1:]
    return pl.pallas_call(
        _sample_odd_body,
        grid=(1,),
        in_specs=[_whole(x), _whole(gm), _whole(wi), _whole(ccw), _whole(sc), _whole(wo),
                  _layer_spec(st_shape, layer), _whole(g), _layer_spec(wu.shape[1:], layer), _whole(cw), _whole(cb),
                  _layer_spec(wd.shape[1:], layer)],
        out_specs=[_whole_out(x.shape), _whole_out(sc.shape), _whole_out(st_shape)],
        out_shape=[jax.ShapeDtypeStruct(x.shape, F32), jax.ShapeDtypeStruct(sc.shape, F32),
                   jax.ShapeDtypeStruct(st_shape, F32)],
        compiler_params=_cparams(1),
        name="sample_odd",
    )(x, gm, wi, ccw, sc, wo, st, g, wu, cw, cb, wd)


def kernel(x_prompt, x_sample, cache_k, cache_v, state_conv_a, state_conv_c, state_ffn, page_table, rel_bias,
           norm_mix_e, w_in_e, conv_a_w, conv_a_b, ln_a_g, ln_a_b, q_norm_g, k_norm_g, w_out_e,
           norm_mix_o, w_in_o, conv_c_w, w_out_o, norm_ffn, w_up, conv_f_w, conv_f_b, w_down):
    b, t, d = x_prompt.shape
    m = x_sample.shape[0]
    n_pool, page = cache_k.shape[1], cache_k.shape[2]
    n_pages = page_table.shape[1]
    assert norm_mix_e.shape[0] == 1 and norm_mix_o.shape[0] == 1 and norm_ffn.shape[0] == 2
    assert x_sample.shape[1] == 1 and t % TILE_M == 0 and t % page == 0
    assert (n_pages * page) % MOBA_BLOCK == 0 and MOBA_BLOCK % page == 0

    row = lambda v: v.reshape(1, -1)
    w_in_e_b = w_in_e[0].astype(BF16)
    w_out_e_b = w_out_e[0].astype(BF16)
    w_in_o_b = w_in_o[0].astype(BF16)
    w_out_o_b = w_out_o[0].astype(BF16)
    w_up_b = w_up.astype(BF16)
    w_down_b = w_down.astype(BF16)
    qg = row(jnp.tile(q_norm_g[0], N_HEADS))
    kg = row(jnp.tile(k_norm_g[0], N_HEADS))
    lane_head = jnp.arange(D_B, dtype=jnp.int32) // HEAD_DIM
    head_ones = (lane_head[:, None] == lane_head[None, :]).astype(BF16)
    g_e, g_o = row(norm_mix_e[0]), row(norm_mix_o[0])
    cab, lag, lab = row(conv_a_b[0]), row(ln_a_g[0]), row(ln_a_b[0])

    a_p, k_p, v_p, qt, kh, vt, km = _prompt_inproj(x_prompt, g_e, w_in_e_b, qg, kg, head_ones, page)
    nb = t // MOBA_BLOCK
    kmh = km.reshape(b, nb, N_HEADS, HEAD_DIM).transpose(0, 2, 1, 3)
    ot = _moba_prompt(rel_bias, _logit_bound(rel_bias, q_norm_g[0], k_norm_g[0]), qt, kh, vt, kmh)
    x1 = _prompt_mix_out(a_p, conv_a_w[0], cab, lag, lab, ot, x_prompt, w_out_e_b)
    x2, f0 = _prompt_ffn(x1, row(norm_ffn[0]), w_up_b, conv_f_w[0], row(conv_f_b[0]), w_down_b, 0)
    x3, c_st = _prompt_mixc(x2, g_o, w_in_o_b, conv_c_w[0], w_out_o_b)
    y_prompt, f1 = _prompt_ffn(x3, row(norm_ffn[1]), w_up_b, conv_f_w[1], row(conv_f_b[1]), w_down_b, 1)

    k_prompt = k_p.transpose(0, 1, 4, 2, 3)[None]
    v_prompt = v_p.transpose(0, 1, 4, 2, 3)[None]
    a_prompt = a_p[:, t - (CONV_A - 1):, :][None]
    c_prompt = c_st[:, SUBLANES - (CONV_C - 1):, :][None]
    f_prompt = jnp.stack([f0[:, SUBLANES - (CONV_F - 1):, :], f1[:, SUBLANES - (CONV_F - 1):, :]])

    xs = x_sample.reshape(m, d)
    sa = state_conv_a[0].transpose(1, 0, 2)
    sc = state_conv_c[0]
    sf = state_ffn
    ao_s, sa_new, q_s, k_s, v_s = _sample_even(xs, g_e, w_in_e_b, qg, kg, head_ones, sa, conv_a_w[0], cab, lag, lab)
    o_s = _moba_sample(page_table, q_s, k_s, v_s, rel_bias.T,
                       cache_k[0].transpose(0, 2, 3, 1), cache_v[0].transpose(0, 2, 3, 1))
    xs1, sf0 = _sample_out_ffn(ao_s, o_s.reshape(m, D_B), xs, w_out_e_b, sf, row(norm_ffn[0]), w_up_b,
                               conv_f_w[0], row(conv_f_b[0]), w_down_b, 0)
    ys, sc_new, sf1 = _sample_odd(xs1, g_o, w_in_o_b, conv_c_w[0], sc, w_out_o_b, sf, row(norm_ffn[1]),
                                  w_up_b, conv_f_w[1], row(conv_f_b[1]), w_down_b, 1)

    y_sample = ys.reshape(m, 1, d)
    k_sample = k_s.reshape(1, m, 1, N_HEADS, HEAD_DIM)
    v_sample = v_s.reshape(1, m, 1, N_HEADS, HEAD_DIM)
    a_sample = sa_new.transpose(1, 0, 2)[None]
    c_sample = sc_new[None]
    f_sample = jnp.stack([sf0, sf1])
    return (y_prompt, y_sample, k_prompt, v_prompt, a_prompt, c_prompt, f_prompt,
            k_sample, v_sample, a_sample, c_sample, f_sample)
```

```python
import functools
import math

import jax
import jax.numpy as jnp
from jax import lax
from jax.experimental import pallas as pl
from jax.experimental.pallas import tpu as pltpu

F32 = jnp.float32
BF16 = jnp.bfloat16

EPS = 1e-6
D_MODEL = 1024
D_A = 512
CONV_A = 31
N_HEADS = 8
HEAD_DIM = 64
D_B = N_HEADS * HEAD_DIM
MOBA_BLOCK = 256
MOBA_TOPK = 3
NUM_BUCKETS = 32
MAX_DISTANCE = 128
D_C = 1024
CONV_C = 3
D_FF = 2816
CONV_F = 3
LOG2E = math.log2(math.e)
Q_SCALE = HEAD_DIM ** -0.5 * LOG2E
VT_ROWS = HEAD_DIM + 16
MAX_SHIFT_RANGE = 100.0

TILE_M = 512
CONV_ROWS = 64
CONV_LANES = 256
HALO_A = 32
HEAD_GROUP = 8
SOFTMAX_ROWS = 64
SEQ_PER_STEP = 4
FAR_UNROLL = 2
BOUNDED_UNROLL = 4
CH = 256
SUBLANES = 8
VMEM_LIMIT = 56 * 1024 * 1024


def _t5_thresholds():
    max_exact = NUM_BUCKETS // 2
    th = list(range(1, max_exact + 1))
    for k in range(1, NUM_BUCKETS - max_exact):
        th.append(math.ceil(max_exact * (MAX_DISTANCE / max_exact) ** (k / (NUM_BUCKETS - max_exact))))
    return tuple(th)


T5_THRESH = _t5_thresholds()


def _cparams(n_grid):
    return pltpu.CompilerParams(dimension_semantics=("arbitrary",) * n_grid, vmem_limit_bytes=VMEM_LIMIT)


def _const_spec(shape):
    nd = len(shape)
    return pl.BlockSpec(shape, lambda *_: (0,) * nd, pipeline_mode=pl.Buffered(1))


def _layer_spec(shape, layer):
    nd = len(shape)
    return pl.BlockSpec((1,) + tuple(shape), lambda *_: (layer,) + (0,) * nd, pipeline_mode=pl.Buffered(1))


def _rmsnorm(x, g):
    return x * lax.rsqrt(jnp.mean(x * x, axis=-1, keepdims=True) + EPS) * g


def _split_dot(x, w_bf16):
    hi = x.astype(BF16)
    lo = (x - hi.astype(F32)).astype(BF16)
    return (jnp.dot(hi, w_bf16, preferred_element_type=F32)
            + jnp.dot(lo, w_bf16, preferred_element_type=F32))


def _head_rmsnorm(x, g, head_ones):
    ss = _split_dot(x * x, head_ones)
    return x * lax.rsqrt(ss * (1.0 / HEAD_DIM) + EPS) * g


def _silu(x):
    return x * jax.nn.sigmoid(x)


def _layernorm(x, g, b):
    mu = jnp.mean(x, axis=-1, keepdims=True)
    xc = x - mu
    var = jnp.mean(xc * xc, axis=-1, keepdims=True)
    return xc * lax.rsqrt(var + EPS) * g + b


def _in_proj_even(x, g, w, qg, kg, head_ones):
    h = _rmsnorm(x, g).astype(BF16)
    z = jnp.dot(h, w, preferred_element_type=F32)
    a = z[:, :D_A] * jax.nn.sigmoid(z[:, D_A:2 * D_A])
    q = _head_rmsnorm(z[:, 2 * D_A:2 * D_A + D_B], qg, head_ones) * Q_SCALE
    k = _head_rmsnorm(z[:, 2 * D_A + D_B:2 * D_A + 2 * D_B], kg, head_ones)
    v = z[:, 2 * D_A + 2 * D_B:]
    return a, q, k, v


def _prompt_inproj_body(x_ref, g_ref, w_ref, qg_ref, kg_ref, ho_ref,
                        a_ref, kp_ref, vp_ref, qt_ref, kh_ref, vt_ref, km_ref, *, tm, page):
    a, q, k, v = _in_proj_even(x_ref[0], g_ref[...], w_ref[...], qg_ref[...], kg_ref[...], ho_ref[...])
    a_ref[0] = a
    qt = q.T
    kt = k.T
    vt = v.T
    for pg in range(tm // page):
        kp_ref[0, pg] = kt[:, pg * page:(pg + 1) * page].reshape(N_HEADS, HEAD_DIM, page)
        vp_ref[0, pg] = vt[:, pg * page:(pg + 1) * page].reshape(N_HEADS, HEAD_DIM, page)
    kb = k.astype(BF16)
    pad_row = lax.broadcasted_iota(jnp.int32, (N_HEADS, VT_ROWS - HEAD_DIM, MOBA_BLOCK), 1)
    ones_rows = jnp.where(pad_row == 0, 1.0, 0.0).astype(BF16)
    for i in range(tm // MOBA_BLOCK):
        r0 = i * MOBA_BLOCK
        qt_ref[0, :, i] = qt[:, r0:r0 + MOBA_BLOCK].reshape(N_HEADS, HEAD_DIM, MOBA_BLOCK).astype(BF16)
        vt_ref[0, :, i, 0:HEAD_DIM, :] = (
            vt[:, r0:r0 + MOBA_BLOCK].reshape(N_HEADS, HEAD_DIM, MOBA_BLOCK).astype(BF16))
        vt_ref[0, :, i, HEAD_DIM:VT_ROWS, :] = ones_rows
        for hh in range(N_HEADS):
            kh_ref[0, hh, i] = kb[r0:r0 + MOBA_BLOCK, hh * HEAD_DIM:(hh + 1) * HEAD_DIM]
        km_ref[0, i] = jnp.mean(k[r0:r0 + MOBA_BLOCK], axis=0, keepdims=True)


def _prompt_inproj(x, g, w, qg, kg, head_ones, page):
    b, t, d = x.shape
    tm = TILE_M
    nt = t // tm
    nb = t // MOBA_BLOCK
    bpt = tm // MOBA_BLOCK
    ppt = tm // page
    n_out = w.shape[1]
    tok = lambda bi, ti: (bi, ti, 0)
    blk5 = lambda bi, ti: (bi, 0, ti, 0, 0)
    pages = lambda bi, ti: (bi, ti, 0, 0, 0)
    return pl.pallas_call(
        functools.partial(_prompt_inproj_body, tm=tm, page=page),
        grid=(b, nt),
        in_specs=[
            pl.BlockSpec((1, tm, d), tok),
            _const_spec((1, d)),
            _const_spec((d, n_out)),
            _const_spec((1, D_B)),
            _const_spec((1, D_B)),
            _const_spec((D_B, D_B)),
        ],
        out_specs=[
            pl.BlockSpec((1, tm, D_A), tok),
            pl.BlockSpec((1, ppt, N_HEADS, HEAD_DIM, page), pages),
            pl.BlockSpec((1, ppt, N_HEADS, HEAD_DIM, page), pages),
            pl.BlockSpec((1, N_HEADS, bpt, HEAD_DIM, MOBA_BLOCK), blk5),
            pl.BlockSpec((1, N_HEADS, bpt, MOBA_BLOCK, HEAD_DIM), blk5),
            pl.BlockSpec((1, N_HEADS, bpt, VT_ROWS, MOBA_BLOCK), blk5),
            pl.BlockSpec((1, bpt, 1, D_B), lambda bi, ti: (bi, ti, 0, 0)),
        ],
        out_shape=[
            jax.ShapeDtypeStruct((b, t, D_A), F32),
            jax.ShapeDtypeStruct((b, t // page, N_HEADS, HEAD_DIM, page), F32),
            jax.ShapeDtypeStruct((b, t // page, N_HEADS, HEAD_DIM, page), F32),
            jax.ShapeDtypeStruct((b, N_HEADS, nb, HEAD_DIM, MOBA_BLOCK), BF16),
            jax.ShapeDtypeStruct((b, N_HEADS, nb, MOBA_BLOCK, HEAD_DIM), BF16),
            jax.ShapeDtypeStruct((b, N_HEADS, nb, VT_ROWS, MOBA_BLOCK), BF16),
            jax.ShapeDtypeStruct((b, nb, 1, D_B), F32),
        ],
        compiler_params=_cparams(2),
        name="prompt_inproj",
    )(x, g, w, qg, kg, head_ones)


def _t5_bias_scalar_table(dist, rb_ref, h):
    out = jnp.full(dist.shape, rb_ref[0, h] * LOG2E, F32)
    for i, th in enumerate(T5_THRESH):
        out = jnp.where(dist >= th, rb_ref[i + 1, h] * LOG2E, out)
    return out


def _moba_prompt_body(rb_ref, ctl_ref, qt_ref, kh_ref, vt_ref, km_ref, ot_ref,
                      bdiag_ref, bsub_ref, selb_ref, m_ref, alpha_ref, acc_ref, s_ref, p_ref, *, nb, hg):
    h0 = pl.program_id(1) * hg
    n = pl.program_id(2)
    blk = MOBA_BLOCK
    bounded = ctl_ref[1, 0] > 0.5
    shifts = [jnp.where(bounded, ctl_ref[0, h0 + g], 0.0) for g in range(hg)]

    @pl.when(n == 0)
    def _():
        ki = lax.broadcasted_iota(jnp.int32, (blk, blk), 0)
        qi = lax.broadcasted_iota(jnp.int32, (blk, blk), 1)
        d0 = qi - ki
        for g in range(hg):
            bdiag_ref[g] = jnp.where(d0 >= 0, _t5_bias_scalar_table(jnp.maximum(d0, 0), rb_ref, h0 + g),
                                     -jnp.inf) - shifts[g]
            bsub_ref[g] = _t5_bias_scalar_table(d0 + blk, rb_ref, h0 + g) - shifts[g]

    bi = lax.broadcasted_iota(jnp.int32, (nb, blk), 0)
    qts = [qt_ref[0, g, 0] for g in range(hg)]

    def select(g):
        km = km_ref[0, g]
        km_hi = km.astype(BF16)
        km_lo = (km - km_hi.astype(F32)).astype(BF16)
        gate = (jnp.dot(km_hi, qts[g], preferred_element_type=F32)
                + jnp.dot(km_lo, qts[g], preferred_element_type=F32))
        avail = jnp.where(bi < n, 1.0, 0.0)
        far_bias = rb_ref[NUM_BUCKETS - 1, h0 + g] * LOG2E - shifts[g]
        selb = jnp.full((nb, blk), -jnp.inf, F32)
        for _ in range(MOBA_TOPK):
            gm = jnp.where(avail > 0.0, gate, -jnp.inf)
            top = jnp.max(gm, axis=0, keepdims=True)
            first = jnp.where(avail > 0.0, jnp.where(gm == top, bi, nb), nb)
            pick = bi == jnp.min(first, axis=0, keepdims=True)
            selb = jnp.where(pick, jnp.where(bi == n - 1, 0.0, far_bias), selb)
            avail = jnp.where(pick, 0.0, avail)
        selb_ref[g] = selb

    n_chunks = blk // SOFTMAX_ROWS

    def logits(g, j, slot, bias):
        s = jnp.dot(kh_ref[0, g, j], qts[g], preferred_element_type=F32)
        s_ref[slot, g] = s if bias is None else s + bias

    def chunk(g, slot, c):
        return s_ref[slot, g, c * SOFTMAX_ROWS:(c + 1) * SOFTMAX_ROWS, :].reshape(
            SOFTMAX_ROWS // SUBLANES, SUBLANES, blk)

    def softmax(g, slot, row):
        m8 = jnp.max(chunk(g, slot, 0), axis=0)
        for c in range(1, n_chunks):
            m8 = jnp.maximum(m8, jnp.max(chunk(g, slot, c), axis=0))
        m_blk = jnp.max(m8, axis=0, keepdims=True)
        if row is None:
            mn = m_blk
            shift = mn
        else:
            m_old = m_ref[g]
            mn = jnp.maximum(m_old, m_blk + row)
            alpha_ref[slot, g] = jnp.exp2(m_old - mn)
            shift = mn - row
        for c in range(n_chunks):
            p = jnp.exp2(chunk(g, slot, c) - shift)
            p_ref[slot, g, c * SOFTMAX_ROWS:(c + 1) * SOFTMAX_ROWS, :] = (
                p.reshape(SOFTMAX_ROWS, blk).astype(BF16))
        m_ref[g] = mn

    def weighted_values(g, j, slot, first):
        pv = jnp.dot(vt_ref[0, g, j], p_ref[slot, g], preferred_element_type=F32)
        acc_ref[g] = pv if first else alpha_ref[slot, g] * acc_ref[g] + pv

    def far_trips(trip, unroll):
        def far(i, carry):
            for r in range(unroll):
                trip(unroll * i + r, r & 1)
            return carry

        n_far = n - 1
        lax.fori_loop(0, n_far // unroll, far, 0)
        done = (n_far // unroll) * unroll
        piece = unroll // 2
        while piece >= 1:
            @pl.when(((n_far - done) & piece) != 0)
            def _(piece=piece, done=done):
                for r in range(piece):
                    trip(done + r, r & 1)
            done = done + ((n_far - done) & piece)
            piece //= 2

    last = jnp.where(n >= 2, n - 2, n - 1)

    @pl.when(bounded)
    def _():
        def probabilities(g, j, slot, bias):
            s = jnp.dot(kh_ref[0, g, j], qts[g], preferred_element_type=F32)
            p_ref[slot, g] = jnp.exp2(s + bias).astype(BF16)

        def add_values(g, j, slot, first):
            pv = jnp.dot(vt_ref[0, g, j], p_ref[slot, g], preferred_element_type=F32)
            acc_ref[g] = pv if first else acc_ref[g] + pv

        for g in range(hg):
            probabilities(g, n, 0, bdiag_ref[g])
        for g in range(hg):
            select(g)

        @pl.when(n == 0)
        def _():
            for g in range(hg):
                add_values(g, n, 0, True)

        @pl.when(n >= 1)
        def _():
            for g in range(hg):
                probabilities(g, n - 1, 1, bsub_ref[g] + selb_ref[g, pl.ds(n - 1, 1), :])
            for g in range(hg):
                add_values(g, n, 0, True)

            def trip(j, slot):
                prev = jnp.where(j == 0, n - 1, j - 1)
                for g in range(hg):
                    probabilities(g, j, slot, selb_ref[g, pl.ds(j, 1), :])
                for g in range(hg):
                    add_values(g, prev, 1 - slot, False)

            far_trips(trip, BOUNDED_UNROLL)
            for g in range(hg):
                add_values(g, last, n & 1, False)

    @pl.when(jnp.logical_not(bounded))
    def _():
        for g in range(hg):
            logits(g, n, 0, bdiag_ref[g])
        for g in range(hg):
            select(g)
        for g in range(hg):
            logits(g, jnp.maximum(n - 1, 0), 1, bsub_ref[g])
        for g in range(hg):
            softmax(g, 0, None)

        @pl.when(n == 0)
        def _():
            for g in range(hg):
                weighted_values(g, n, 0, True)

        @pl.when(n >= 1)
        def _():
            for g in range(hg):
                weighted_values(g, n, 0, True)
                softmax(g, 1, selb_ref[g, pl.ds(n - 1, 1), :])
                logits(g, 0, 0, None)

            def trip(j, slot):
                nxt = jnp.minimum(j + 1, n - 2)
                prev = jnp.where(j == 0, n - 1, j - 1)
                for g in range(hg):
                    logits(g, nxt, 1 - slot, None)
                for g in range(hg):
                    weighted_values(g, prev, 1 - slot, False)
                for g in range(hg):
                    softmax(g, slot, selb_ref[g, pl.ds(j, 1), :])

            far_trips(trip, FAR_UNROLL)
            for g in range(hg):
                weighted_values(g, last, n & 1, False)

    for g in range(hg):
        ot_ref[0, g * HEAD_DIM:(g + 1) * HEAD_DIM, :] = (
            acc_ref[g, 0:HEAD_DIM, :] / acc_ref[g, HEAD_DIM:HEAD_DIM + 1, :])


def _logit_bound(rel_bias, q_gain, k_gain):
    qk = 1.02 * HEAD_DIM * Q_SCALE * jnp.max(jnp.abs(q_gain)) * jnp.max(jnp.abs(k_gain))
    table = rel_bias * LOG2E
    top, low = jnp.max(table, axis=0), jnp.min(table, axis=0)
    usable = jnp.all(2.0 * qk + (top - low) <= MAX_SHIFT_RANGE)
    return jnp.stack([qk + top, jnp.broadcast_to(usable.astype(F32), top.shape)])


def _moba_prompt(rel_bias, ctl, qt, kh, vt, kmh):
    b, nh, nb, hd, blk = qt.shape
    t = nb * blk
    hg = HEAD_GROUP
    grp = lambda bi, hi, ni: (bi, hi, 0, 0, 0)
    return pl.pallas_call(
        functools.partial(_moba_prompt_body, nb=nb, hg=hg),
        grid=(b, nh // hg, nb),
        in_specs=[
            pl.BlockSpec(memory_space=pltpu.SMEM),
            pl.BlockSpec(memory_space=pltpu.SMEM),
            pl.BlockSpec((1, hg, 1, hd, blk), lambda bi, hi, ni: (bi, hi, ni, 0, 0)),
            pl.BlockSpec((1, hg, nb, blk, hd), grp, pipeline_mode=pl.Buffered(1)),
            pl.BlockSpec((1, hg, nb, VT_ROWS, blk), grp, pipeline_mode=pl.Buffered(1)),
            pl.BlockSpec((1, hg, nb, hd), lambda bi, hi, ni: (bi, hi, 0, 0)),
        ],
        out_specs=pl.BlockSpec((1, hg * hd, blk), lambda bi, hi, ni: (bi, hi, ni)),
        out_shape=jax.ShapeDtypeStruct((b, nh * hd, t), F32),
        scratch_shapes=[
            pltpu.VMEM((hg, blk, blk), F32),
            pltpu.VMEM((hg, blk, blk), F32),
            pltpu.VMEM((hg, nb, blk), F32),
            pltpu.VMEM((hg, 1, blk), F32),
            pltpu.VMEM((2, hg, 1, blk), F32),
            pltpu.VMEM((hg, VT_ROWS, blk), F32),
            pltpu.VMEM((2, hg, blk, blk), F32),
            pltpu.VMEM((2, hg, blk, blk), BF16),
        ],
        compiler_params=_cparams(3),
        name="prompt_moba",
    )(rel_bias, ctl, qt, kh, vt, kmh)


def _prompt_mix_out_body(a_ref, halo_ref, cw_ref, cb_ref, lg_ref, lb_ref, ot_ref, x_ref, w_ref,
                         y_ref, sh_ref, conv_ref, wb_ref, *, tm):
    t = pl.program_id(1)
    rows = tm + HALO_A

    @pl.when((pl.program_id(0) == 0) & (t == 0))
    def _():
        for j in range(CONV_A):
            wb_ref[j] = jnp.broadcast_to(cw_ref[j:j + 1, :], (SUBLANES, D_A))

    sh_ref[0, 0:HALO_A] = jnp.where(t == 0, 0.0, halo_ref[0])
    sh_ref[0, HALO_A:rows] = a_ref[0]
    for s in range(1, SUBLANES):
        sh_ref[s, 0:rows - SUBLANES] = sh_ref[0, s:s + rows - SUBLANES]

    groups = CONV_ROWS // SUBLANES
    first_off = HALO_A - (CONV_A - 1)

    def chunk(c, carry):
        r0 = pl.multiple_of(c * CONV_ROWS, CONV_ROWS)
        for l0 in range(0, D_A, CONV_LANES):
            lanes = slice(l0, l0 + CONV_LANES)
            acc = [jnp.zeros((SUBLANES, CONV_LANES), F32) for _ in range(groups)]
            for s in range(SUBLANES):
                offs = [o for o in range(first_off, first_off + CONV_A) if o % SUBLANES == s]
                qs = [o // SUBLANES for o in offs]
                slab = {gq: sh_ref[s, pl.ds(r0 + gq * SUBLANES, SUBLANES), lanes]
                        for gq in range(min(qs), max(qs) + groups)}
                for o, q in zip(offs, qs):
                    w = wb_ref[o - first_off, :, lanes]
                    for g in range(groups):
                        acc[g] = acc[g] + w * slab[g + q]
            for g in range(groups):
                conv_ref[pl.ds(r0 + g * SUBLANES, SUBLANES), lanes] = acc[g]
        return carry

    lax.fori_loop(0, tm // CONV_ROWS, chunk, 0)

    ao = _silu(_layernorm(conv_ref[...] + cb_ref[...], lg_ref[...], lb_ref[...])).astype(BF16)
    o = ot_ref[0].T.astype(BF16)
    y = (jnp.dot(ao, w_ref[0:D_A, :], preferred_element_type=F32)
         + jnp.dot(o, w_ref[D_A:D_A + D_B, :], preferred_element_type=F32))
    y_ref[0] = x_ref[0] + y


def _prompt_mix_out(a, cw, cb, lg, lb, ot, x, w):
    b, t, d = x.shape
    tm = TILE_M
    nt = t // tm
    hpt = tm // HALO_A
    tok = lambda bi, ti: (bi, ti, 0)
    return pl.pallas_call(
        functools.partial(_prompt_mix_out_body, tm=tm),
        grid=(b, nt),
        in_specs=[
            pl.BlockSpec((1, tm, D_A), tok),
            pl.BlockSpec((1, HALO_A, D_A), lambda bi, ti: (bi, jnp.maximum(ti * hpt - 1, 0), 0)),
            _const_spec((CONV_A, D_A)),
            _const_spec((1, D_A)),
            _const_spec((1, D_A)),
            _const_spec((1, D_A)),
            pl.BlockSpec((1, D_B, tm), lambda bi, ti: (bi, 0, ti)),
            pl.BlockSpec((1, tm, d), tok),
            _const_spec((D_A + D_B, d)),
        ],
        out_specs=pl.BlockSpec((1, tm, d), tok),
        out_shape=jax.ShapeDtypeStruct((b, t, d), F32),
        scratch_shapes=[
            pltpu.VMEM((SUBLANES, tm + HALO_A, D_A), F32),
            pltpu.VMEM((tm, D_A), F32),
            pltpu.VMEM((CONV_A, SUBLANES, D_A), F32),
        ],
        compiler_params=_cparams(2),
        name="prompt_mix_out",
    )(a, a, cw, cb, lg, lb, ot, x, w)


def _conv3_chunk(u, ext_ref, carry_ref, cw_ref, c0, tm):
    lo = SUBLANES
    carry = carry_ref[:, c0:c0 + CH]
    for k in range(1, 3):
        ext_ref[k - 1, lo:lo + k] = carry[SUBLANES - k:SUBLANES]
        ext_ref[k - 1, lo + k:lo + k + tm] = u
    carry_ref[:, c0:c0 + CH] = u[tm - SUBLANES:tm]
    return (cw_ref[0:1, c0:c0 + CH] * ext_ref[1, lo:lo + tm]
            + cw_ref[1:2, c0:c0 + CH] * ext_ref[0, lo:lo + tm]
            + cw_ref[2:3, c0:c0 + CH] * u)


def _prompt_ffn_body(x_ref, g_ref, wu_ref, cw_ref, cb_ref, wd_ref, y_ref, st_ref,
                     carry_ref, extg_ref, extu_ref, act_ref, *, tm):
    @pl.when(pl.program_id(1) == 0)
    def _():
        carry_ref[...] = jnp.zeros_like(carry_ref)

    x = x_ref[0]
    h = _rmsnorm(x, g_ref[...]).astype(BF16)
    for c in range(D_FF // CH):
        cg = c * CH
        cu = D_FF + c * CH
        ug = jnp.dot(h, wu_ref[0, :, cg:cg + CH], preferred_element_type=F32)
        uu = jnp.dot(h, wu_ref[0, :, cu:cu + CH], preferred_element_type=F32)
        gg = _conv3_chunk(ug, extg_ref, carry_ref, cw_ref, cg, tm) + cb_ref[:, cg:cg + CH]
        gu = _conv3_chunk(uu, extu_ref, carry_ref, cw_ref, cu, tm) + cb_ref[:, cu:cu + CH]
        act_ref[:, cg:cg + CH] = (_silu(gg) * gu).astype(BF16)
    st_ref[0] = carry_ref[...]
    y_ref[0] = x + jnp.dot(act_ref[...], wd_ref[0], preferred_element_type=F32)


def _prompt_ffn(x, g, wu, cw, cb, wd, layer):
    b, t, d = x.shape
    tm = TILE_M
    nt = t // tm
    tok = lambda bi, ti: (bi, ti, 0)
    return pl.pallas_call(
        functools.partial(_prompt_ffn_body, tm=tm),
        grid=(b, nt),
        in_specs=[
            pl.BlockSpec((1, tm, d), tok),
            _const_spec((1, d)),
            _layer_spec((d, 2 * D_FF), layer),
            _const_spec((CONV_F, 2 * D_FF)),
            _const_spec((1, 2 * D_FF)),
            _layer_spec((D_FF, d), layer),
        ],
        out_specs=[
            pl.BlockSpec((1, tm, d), tok),
            pl.BlockSpec((1, SUBLANES, 2 * D_FF), lambda bi, ti: (bi, 0, 0)),
        ],
        out_shape=[
            jax.ShapeDtypeStruct((b, t, d), F32),
            jax.ShapeDtypeStruct((b, SUBLANES, 2 * D_FF), F32),
        ],
        scratch_shapes=[
            pltpu.VMEM((SUBLANES, 2 * D_FF), F32),
            pltpu.VMEM((2, tm + 2 * SUBLANES, CH), F32),
            pltpu.VMEM((2, tm + 2 * SUBLANES, CH), F32),
            pltpu.VMEM((tm, D_FF), BF16),
        ],
        compiler_params=_cparams(2),
        name="prompt_ffn",
    )(x, g, wu, cw, cb, wd)


def _prompt_mixc_body(x_ref, g_ref, wi_ref, cw_ref, wo_ref, y_ref, st_ref,
                      carry_ref, ext_ref, z_ref, *, tm):
    @pl.when(pl.program_id(1) == 0)
    def _():
        carry_ref[...] = jnp.zeros_like(carry_ref)

    x = x_ref[0]
    h = _rmsnorm(x, g_ref[...]).astype(BF16)
    for c in range(D_C // CH):
        c0 = c * CH
        gb = jnp.dot(h, wi_ref[:, c0:c0 + CH], preferred_element_type=F32)
        gc = jnp.dot(h, wi_ref[:, D_C + c0:D_C + c0 + CH], preferred_element_type=F32)
        u = jnp.dot(h, wi_ref[:, 2 * D_C + c0:2 * D_C + c0 + CH], preferred_element_type=F32)
        conv = _conv3_chunk(gc * u, ext_ref, carry_ref, cw_ref, c0, tm)
        z_ref[:, c0:c0 + CH] = (gb * conv).astype(BF16)
    st_ref[0] = carry_ref[...]
    y_ref[0] = x + jnp.dot(z_ref[...], wo_ref[...], preferred_element_type=F32)


def _prompt_mixc(x, g, wi, cw, wo):
    b, t, d = x.shape
    tm = TILE_M
    nt = t // tm
    tok = lambda bi, ti: (bi, ti, 0)
    return pl.pallas_call(
        functools.partial(_prompt_mixc_body, tm=tm),
        grid=(b, nt),
        in_specs=[
            pl.BlockSpec((1, tm, d), tok),
            _const_spec((1, d)),
            _const_spec((d, 3 * D_C)),
            _const_spec((CONV_C, D_C)),
            _const_spec((D_C, d)),
        ],
        out_specs=[
            pl.BlockSpec((1, tm, d), tok),
            pl.BlockSpec((1, SUBLANES, D_C), lambda bi, ti: (bi, 0, 0)),
        ],
        out_shape=[
            jax.ShapeDtypeStruct((b, t, d), F32),
            jax.ShapeDtypeStruct((b, SUBLANES, D_C), F32),
        ],
        scratch_shapes=[
            pltpu.VMEM((SUBLANES, D_C), F32),
            pltpu.VMEM((2, tm + 2 * SUBLANES, CH), F32),
            pltpu.VMEM((tm, D_C), BF16),
        ],
        compiler_params=_cparams(2),
        name="prompt_mixc",
    )(x, g, wi, cw, wo)


def _sample_even_body(x_ref, g_ref, w_ref, qg_ref, kg_ref, ho_ref, sa_ref, cw_ref, cb_ref, lg_ref, lb_ref,
                      ao_ref, san_ref, q_ref, k_ref, v_ref):
    a, q, k, v = _in_proj_even(x_ref[...], g_ref[...], w_ref[...], qg_ref[...], kg_ref[...], ho_ref[...])
    q_ref[...] = q
    k_ref[...] = k
    v_ref[...] = v
    hist = CONV_A - 1
    conv = cb_ref[...] + cw_ref[hist:hist + 1, :] * a
    for j in range(hist):
        conv = conv + cw_ref[j:j + 1, :] * sa_ref[j]
    ao_ref[...] = _silu(_layernorm(conv, lg_ref[...], lb_ref[...])).astype(BF16)
    san_ref[0:hist - 1] = sa_ref[1:hist]
    san_ref[hist - 1] = a


def _sample_even(x, g, w, qg, kg, head_ones, sa, cw, cb, lg, lb):
    m = x.shape[0]
    return pl.pallas_call(
        _sample_even_body,
        out_shape=[
            jax.ShapeDtypeStruct((m, D_A), BF16),
            jax.ShapeDtypeStruct(sa.shape, F32),
            jax.ShapeDtypeStruct((m, D_B), F32),
            jax.ShapeDtypeStruct((m, D_B), F32),
            jax.ShapeDtypeStruct((m, D_B), F32),
        ],
        compiler_params=pltpu.CompilerParams(vmem_limit_bytes=VMEM_LIMIT),
        name="sample_even",
    )(x, g, w, qg, kg, head_ones, sa, cw, cb, lg, lb)


def _moba_sample_keys_body(pt_ref, q_ref, qdh_ref, kn_ref, rbt_ref, *refs, n_pages, page):
    del pt_ref
    outs = refs[SEQ_PER_STEP * n_pages:]
    bias_ref = outs[4]
    past = n_pages * page

    @pl.when(pl.program_id(0) == 0)
    def _():
        pos = lax.broadcasted_iota(jnp.int32, (N_HEADS, past), 1)
        dist = past - pos
        out = jnp.zeros((N_HEADS, past), F32) + rbt_ref[:, 0:1]
        for i, th in enumerate(T5_THRESH):
            out = jnp.where(dist >= th, rbt_ref[:, i + 1:i + 2], out)
        bias_ref[...] = out * LOG2E

    for sq in range(SEQ_PER_STEP):
        _moba_sample_keys_one(sq, q_ref, qdh_ref, kn_ref, rbt_ref, refs[sq * n_pages:(sq + 1) * n_pages], *outs,
                              n_pages=n_pages, page=page)


def _moba_sample_keys_one(sq, q_ref, qdh_ref, kn_ref, rbt_ref, k_pages, pc_ref, pn_ref, l_ref, sel_ref,
                          bias_ref, qb_ref, s_ref, *, n_pages, page):
    past = n_pages * page
    n_past_blocks = past // MOBA_BLOCK
    n_sel = min(MOBA_TOPK, n_past_blocks)
    qb_ref = qb_ref.at[sq]
    s_ref = s_ref.at[sq]

    head_of_lane = lax.broadcasted_iota(jnp.int32, (N_HEADS, D_B), 1) // HEAD_DIM
    hmask = head_of_lane == lax.broadcasted_iota(jnp.int32, (N_HEADS, D_B), 0)
    qbd = jnp.where(hmask, q_ref[sq], 0.0)

    qdh = qdh_ref[sq]
    for h in range(N_HEADS):
        qb_ref[h] = jnp.broadcast_to(qdh[:, h:h + 1], (HEAD_DIM, page))

    for pg in range(n_pages):
        rows = [jnp.sum(k_pages[pg][0, h] * qb_ref[h], axis=0, keepdims=True) for h in range(N_HEADS)]
        s_ref[:, pg * page:(pg + 1) * page] = jnp.concatenate(rows, axis=0)

    gates = [jnp.sum(s_ref[:, j * MOBA_BLOCK:(j + 1) * MOBA_BLOCK], axis=-1, keepdims=True)
             for j in range(n_past_blocks)]
    ranks = []
    pieces = []
    for j in range(n_past_blocks):
        rank = jnp.zeros((N_HEADS, 1), jnp.int32)
        for i in range(n_past_blocks):
            if i == j:
                continue
            ahead = (gates[i] >= gates[j]) if i < j else (gates[i] > gates[j])
            rank = rank + jnp.where(ahead, 1, 0)
        ranks.append(rank)
        lanes = slice(j * MOBA_BLOCK, (j + 1) * MOBA_BLOCK)
        pieces.append(s_ref[:, lanes] + bias_ref[:, lanes] + jnp.where(rank < n_sel, 0.0, -jnp.inf))
    s = jnp.concatenate(pieces, axis=-1)

    s_new = jnp.sum(qbd * kn_ref[sq], axis=-1, keepdims=True) + rbt_ref[:, 0:1] * LOG2E
    m = jnp.maximum(jnp.max(s, axis=-1, keepdims=True), s_new)
    p = jnp.exp2(s - m)
    p_new = jnp.exp2(s_new - m)
    pn_ref[sq] = jnp.broadcast_to(p_new, (N_HEADS, 128))
    l_ref[sq] = jnp.broadcast_to(jnp.sum(p, axis=-1, keepdims=True) + p_new, (N_HEADS, 128))

    lane = lax.broadcasted_iota(jnp.int32, (N_HEADS, 128), 1)
    sel = jnp.zeros((N_HEADS, 128), jnp.int32)
    for r in range(n_sel):
        picked = jnp.zeros((N_HEADS, MOBA_BLOCK), F32)
        block_id = jnp.zeros((N_HEADS, 1), jnp.int32)
        for j in range(n_past_blocks):
            picked = picked + jnp.where(ranks[j] == r, p[:, j * MOBA_BLOCK:(j + 1) * MOBA_BLOCK], 0.0)
            block_id = block_id + jnp.where(ranks[j] == r, j, 0)
        pc_ref[sq, :, r * MOBA_BLOCK:(r + 1) * MOBA_BLOCK] = picked
        sel = jnp.where(lane == r, block_id, sel)
    sel_ref[sq] = sel


def _moba_sample_values_body(pt_ref, sel_ref, pc_ref, pn_ref, l_ref, vn_ref, v_hbm, o_ref, buf_ref, sem_ref,
                             *, n_sel, ppb, page):
    b = pl.program_id(0)
    n_tiles = n_sel * ppb

    def tile_copy(seq, slot, h, i):
        block = sel_ref[seq * (N_HEADS * n_sel) + h * n_sel + i // ppb]
        pool_page = pt_ref[seq, block * ppb + i % ppb]
        return pltpu.make_async_copy(v_hbm.at[pool_page, h], buf_ref.at[slot, h * n_tiles + i], sem_ref.at[slot])

    def for_all_tiles(seq, slot, act):
        for h in range(N_HEADS):
            for i in range(n_tiles):
                act(tile_copy(seq, slot, h, i))

    @pl.when(b == 0)
    def _():
        for_all_tiles(0, 0, lambda cp: cp.start())

    @pl.when(b + 1 < pl.num_programs(0))
    def _():
        for_all_tiles(b + 1, (b + 1) % 2, lambda cp: cp.start())

    slot = b % 2
    for_all_tiles(b, slot, lambda cp: cp.wait())

    head_of_lane = lax.broadcasted_iota(jnp.int32, (N_HEADS, D_B), 1) // HEAD_DIM
    hmask = head_of_lane == lax.broadcasted_iota(jnp.int32, (N_HEADS, D_B), 0)
    ones = jnp.ones((SUBLANES, page), BF16)
    lane_sum = lambda x: lax.dot_general(ones, x, (((1,), (1,)), ((), ())), preferred_element_type=F32)
    outs = []
    for h in range(N_HEADS):
        acc = jnp.zeros((HEAD_DIM, page), F32)
        for i in range(n_tiles):
            acc = acc + buf_ref[slot, h * n_tiles + i] * pc_ref[0, h:h + 1, i * page:(i + 1) * page]
        hi = acc.astype(BF16)
        lo = (acc - hi.astype(F32)).astype(BF16)
        outs.append((lane_sum(hi) + lane_sum(lo))[0:1, :])
    o_past = jnp.concatenate(outs, axis=-1)
    spread = lambda col: jnp.sum(jnp.where(hmask, col, 0.0), axis=0, keepdims=True)
    o_ref[0] = (o_past + spread(pn_ref[0, :, 0:1]) * vn_ref[0]) / spread(l_ref[0, :, 0:1])


def _moba_sample(page_table, q, k_new, v_new, rel_bias_t, cache_kt, cache_vt):
    m, n_pages = page_table.shape
    page = cache_kt.shape[-1]
    past = n_pages * page
    n_sel = min(MOBA_TOPK, past // MOBA_BLOCK)
    ppb = MOBA_BLOCK // page
    row = lambda bi, *_: (bi, 0, 0)

    sps = SEQ_PER_STEP
    assert m % sps == 0
    keys_spec = pltpu.PrefetchScalarGridSpec(
        num_scalar_prefetch=1,
        grid=(m // sps,),
        in_specs=[
            pl.BlockSpec((sps, 1, D_B), row),
            pl.BlockSpec((sps, HEAD_DIM, N_HEADS), row),
            pl.BlockSpec((sps, 1, D_B), row),
            pl.BlockSpec((N_HEADS, NUM_BUCKETS), lambda bi, pt: (0, 0)),
        ] + [pl.BlockSpec((1, N_HEADS, HEAD_DIM, page), lambda bi, pt, sq=sq, pg=pg: (pt[bi * sps + sq, pg], 0, 0, 0))
             for sq in range(sps) for pg in range(n_pages)],
        out_specs=[
            pl.BlockSpec((sps, N_HEADS, n_sel * MOBA_BLOCK), row),
            pl.BlockSpec((sps, N_HEADS, 128), row),
            pl.BlockSpec((sps, N_HEADS, 128), row),
            pl.BlockSpec((sps, N_HEADS, 128), row),
        ],
        scratch_shapes=[
            pltpu.VMEM((N_HEADS, past), F32),
            pltpu.VMEM((sps, N_HEADS, HEAD_DIM, page), F32),
            pltpu.VMEM((sps, N_HEADS, past), F32),
        ],
    )
    q_dh = q.reshape(m, N_HEADS, HEAD_DIM).transpose(0, 2, 1)
    pc, pn, l, sel = pl.pallas_call(
        functools.partial(_moba_sample_keys_body, n_pages=n_pages, page=page),
        grid_spec=keys_spec,
        out_shape=[
            jax.ShapeDtypeStruct((m, N_HEADS, n_sel * MOBA_BLOCK), F32),
            jax.ShapeDtypeStruct((m, N_HEADS, 128), F32),
            jax.ShapeDtypeStruct((m, N_HEADS, 128), F32),
            jax.ShapeDtypeStruct((m, N_HEADS, 128), jnp.int32),
        ],
        compiler_params=_cparams(1),
        name="sample_moba_keys",
    )(page_table, q.reshape(m, 1, D_B), q_dh, k_new.reshape(m, 1, D_B), rel_bias_t, *([cache_kt] * (sps * n_pages)))

    sel_flat = sel[:, :, :n_sel].reshape(m * N_HEADS * n_sel)
    values_spec = pltpu.PrefetchScalarGridSpec(
        num_scalar_prefetch=2,
        grid=(m,),
        in_specs=[
            pl.BlockSpec((1, N_HEADS, n_sel * MOBA_BLOCK), row),
            pl.BlockSpec((1, N_HEADS, 128), row),
            pl.BlockSpec((1, N_HEADS, 128), row),
            pl.BlockSpec((1, 1, D_B), row),
            pl.BlockSpec(memory_space=pl.ANY),
        ],
        out_specs=pl.BlockSpec((1, 1, D_B), row),
        scratch_shapes=[
            pltpu.VMEM((2, N_HEADS * n_sel * ppb, HEAD_DIM, page), F32),
            pltpu.SemaphoreType.DMA((2,)),
        ],
    )
    return pl.pallas_call(
        functools.partial(_moba_sample_values_body, n_sel=n_sel, ppb=ppb, page=page),
        grid_spec=values_spec,
        out_shape=jax.ShapeDtypeStruct((m, 1, D_B), F32),
        compiler_params=_cparams(1),
        name="sample_moba_values",
    )(page_table, sel_flat, pc, pn, l, v_new.reshape(m, 1, D_B), cache_vt)


def _sample_ffn(x, st_ref, g_ref, wu_ref, cw_ref, cb_ref, wd_ref, stn_ref):
    h = _rmsnorm(x, g_ref[...]).astype(BF16)
    up = jnp.dot(h, wu_ref[0], preferred_element_type=F32)
    older, newer = st_ref[0, :, 0, :], st_ref[0, :, 1, :]
    conv = cw_ref[0:1, :] * older + cw_ref[1:2, :] * newer + cw_ref[2:3, :] * up + cb_ref[...]
    act = (_silu(conv[:, :D_FF]) * conv[:, D_FF:]).astype(BF16)
    stn_ref[:, 0, :] = newer
    stn_ref[:, 1, :] = up
    return x + jnp.dot(act, wd_ref[0], preferred_element_type=F32)


def _sample_out_ffn_body(ao_ref, o_ref, x_ref, wo_ref, st_ref, g_ref, wu_ref, cw_ref, cb_ref, wd_ref,
                         y_ref, stn_ref):
    x1 = (x_ref[...]
          + jnp.dot(ao_ref[...], wo_ref[0:D_A, :], preferred_element_type=F32)
          + jnp.dot(o_ref[...].astype(BF16), wo_ref[D_A:D_A + D_B, :], preferred_element_type=F32))
    y_ref[...] = _sample_ffn(x1, st_ref, g_ref, wu_ref, cw_ref, cb_ref, wd_ref, stn_ref)


def _whole(a):
    return _const_spec(a.shape)


def _whole_out(shape):
    nd = len(shape)
    return pl.BlockSpec(tuple(shape), lambda *_: (0,) * nd)


def _sample_out_ffn(ao, o, x, wo, st, g, wu, cw, cb, wd, layer):
    st_shape = st.shape[1:]
    return pl.pallas_call(
        _sample_out_ffn_body,
        grid=(1,),
        in_specs=[_whole(ao), _whole(o), _whole(x), _whole(wo), _layer_spec(st_shape, layer), _whole(g),
                  _layer_spec(wu.shape[1:], layer), _whole(cw), _whole(cb), _layer_spec(wd.shape[1:], layer)],
        out_specs=[_whole_out(x.shape), _whole_out(st_shape)],
        out_shape=[jax.ShapeDtypeStruct(x.shape, F32), jax.ShapeDtypeStruct(st_shape, F32)],
        compiler_params=_cparams(1),
        name="sample_out_ffn",
    )(ao, o, x, wo, st, g, wu, cw, cb, wd)


def _sample_odd_body(x_ref, gm_ref, wi_ref, ccw_ref, sc_ref, wo_ref, st_ref, g_ref, wu_ref, cw_ref, cb_ref,
                     wd_ref, y_ref, scn_ref, stn_ref):
    x = x_ref[...]
    h = _rmsnorm(x, gm_ref[...]).astype(BF16)
    z = jnp.dot(h, wi_ref[...], preferred_element_type=F32)
    gcu = z[:, D_C:2 * D_C] * z[:, 2 * D_C:]
    older, newer = sc_ref[:, 0, :], sc_ref[:, 1, :]
    conv = ccw_ref[0:1, :] * older + ccw_ref[1:2, :] * newer + ccw_ref[2:3, :] * gcu
    scn_ref[:, 0, :] = newer
    scn_ref[:, 1, :] = gcu
    x1 = x + jnp.dot((z[:, :D_C] * conv).astype(BF16), wo_ref[...], preferred_element_type=F32)
    y_ref[...] = _sample_ffn(x1, st_ref, g_ref, wu_ref, cw_ref, cb_ref, wd_ref, stn_ref)


def _sample_odd(x, gm, wi, ccw, sc, wo, st, g, wu, cw, cb, wd, layer):
    st_shape = st.shape[1:]
    return pl.pallas_call(
        _sample_odd_body,
        grid=(1,),
        in_specs=[_whole(x), _whole(gm), _whole(wi), _whole(ccw), _whole(sc), _whole(wo),
                  _layer_spec(st_shape, layer), _whole(g), _layer_spec(wu.shape[1:], layer), _whole(cw), _whole(cb),
                  _layer_spec(wd.shape[1:], layer)],
        out_specs=[_whole_out(x.shape), _whole_out(sc.shape), _whole_out(st_shape)],
        out_shape=[jax.ShapeDtypeStruct(x.shape, F32), jax.ShapeDtypeStruct(sc.shape, F32),
                   jax.ShapeDtypeStruct(st_shape, F32)],
        compiler_params=_cparams(1),
        name="sample_odd",
    )(x, gm, wi, ccw, sc, wo, st, g, wu, cw, cb, wd)


def kernel(x_prompt, x_sample, cache_k, cache_v, state_conv_a, state_conv_c, state_ffn, page_table, rel_bias,
           norm_mix_e, w_in_e, conv_a_w, conv_a_b, ln_a_g, ln_a_b, q_norm_g, k_norm_g, w_out_e,
           norm_mix_o, w_in_o, conv_c_w, w_out_o, norm_ffn, w_up, conv_f_w, conv_f_b, w_down):
    b, t, d = x_prompt.shape
    m = x_sample.shape[0]
    n_pool, page = cache_k.shape[1], cache_k.shape[2]
    n_pages = page_table.shape[1]
    assert norm_mix_e.shape[0] == 1 and norm_mix_o.shape[0] == 1 and norm_ffn.shape[0] == 2
    assert x_sample.shape[1] == 1 and t % TILE_M == 0 and t % page == 0
    assert (n_pages * page) % MOBA_BLOCK == 0 and MOBA_BLOCK % page == 0

    row = lambda v: v.reshape(1, -1)
    w_in_e_b = w_in_e[0].astype(BF16)
    w_out_e_b = w_out_e[0].astype(BF16)
    w_in_o_b = w_in_o[0].astype(BF16)
    w_out_o_b = w_out_o[0].astype(BF16)
    w_up_b = w_up.astype(BF16)
    w_down_b = w_down.astype(BF16)
    qg = row(jnp.tile(q_norm_g[0], N_HEADS))
    kg = row(jnp.tile(k_norm_g[0], N_HEADS))
    lane_head = jnp.arange(D_B, dtype=jnp.int32) // HEAD_DIM
    head_ones = (lane_head[:, None] == lane_head[None, :]).astype(BF16)
    g_e, g_o = row(norm_mix_e[0]), row(norm_mix_o[0])
    cab, lag, lab = row(conv_a_b[0]), row(ln_a_g[0]), row(ln_a_b[0])

    a_p, k_p, v_p, qt, kh, vt, km = _prompt_inproj(x_prompt, g_e, w_in_e_b, qg, kg, head_ones, page)
    nb = t // MOBA_BLOCK
    kmh = km.reshape(b, nb, N_HEADS, HEAD_DIM).transpose(0, 2, 1, 3)
    ot = _moba_prompt(rel_bias, _logit_bound(rel_bias, q_norm_g[0], k_norm_g[0]), qt, kh, vt, kmh)
    x1 = _prompt_mix_out(a_p, conv_a_w[0], cab, lag, lab, ot, x_prompt, w_out_e_b)
    x2, f0 = _prompt_ffn(x1, row(norm_ffn[0]), w_up_b, conv_f_w[0], row(conv_f_b[0]), w_down_b, 0)
    x3, c_st = _prompt_mixc(x2, g_o, w_in_o_b, conv_c_w[0], w_out_o_b)
    y_prompt, f1 = _prompt_ffn(x3, row(norm_ffn[1]), w_up_b, conv_f_w[1], row(conv_f_b[1]), w_down_b, 1)

    k_prompt = k_p.transpose(0, 1, 4, 2, 3)[None]
    v_prompt = v_p.transpose(0, 1, 4, 2, 3)[None]
    a_prompt = a_p[:, t - (CONV_A - 1):, :][None]
    c_prompt = c_st[:, SUBLANES - (CONV_C - 1):, :][None]
    f_prompt = jnp.stack([f0[:, SUBLANES - (CONV_F - 1):, :], f1[:, SUBLANES - (CONV_F - 1):, :]])

    xs = x_sample.reshape(m, d)
    sa = state_conv_a[0].transpose(1, 0, 2)
    sc = state_conv_c[0]
    sf = state_ffn
    ao_s, sa_new, q_s, k_s, v_s = _sample_even(xs, g_e, w_in_e_b, qg, kg, head_ones, sa, conv_a_w[0], cab, lag, lab)
    o_s = _moba_sample(page_table, q_s, k_s, v_s, rel_bias.T,
                       cache_k[0].transpose(0, 2, 3, 1), cache_v[0].transpose(0, 2, 3, 1))
    xs1, sf0 = _sample_out_ffn(ao_s, o_s.reshape(m, D_B), xs, w_out_e_b, sf, row(norm_ffn[0]), w_up_b,
                               conv_f_w[0], row(conv_f_b[0]), w_down_b, 0)
    ys, sc_new, sf1 = _sample_odd(xs1, g_o, w_in_o_b, conv_c_w[0], sc, w_out_o_b, sf, row(norm_ffn[1]),
                                  w_up_b, conv_f_w[1], row(conv_f_b[1]), w_down_b, 1)

    y_sample = ys.reshape(m, 1, d)
    k_sample = k_s.reshape(1, m, 1, N_HEADS, HEAD_DIM)
    v_sample = v_s.reshape(1, m, 1, N_HEADS, HEAD_DIM)
    a_sample = sa_new.transpose(1, 0, 2)[None]
    c_sample = sc_new[None]
    f_sample = jnp.stack([sf0, sf1])
    return (y_prompt, y_sample, k_prompt, v_prompt, a_prompt, c_prompt, f_prompt,
            k_sample, v_sample, a_sample, c_sample, f_sample)
```

```python
import functools
import math

import jax
import jax.numpy as jnp
from jax import lax
from jax.experimental import pallas as pl
from jax.experimental.pallas import tpu as pltpu

F32 = jnp.float32
BF16 = jnp.bfloat16

EPS = 1e-6
D_MODEL = 1024
D_A = 512
CONV_A = 31
N_HEADS = 8
HEAD_DIM = 64
D_B = N_HEADS * HEAD_DIM
MOBA_BLOCK = 256
MOBA_TOPK = 3
NUM_BUCKETS = 32
MAX_DISTANCE = 128
D_C = 1024
CONV_C = 3
D_FF = 2816
CONV_F = 3
LOG2E = math.log2(math.e)
Q_SCALE = HEAD_DIM ** -0.5 * LOG2E
VT_ROWS = HEAD_DIM + 16
MAX_SHIFT_RANGE = 100.0

TILE_M = 512
FFN_TILE_M = 1024
CONV_ROWS = 64
CONV_LANES = 256
HALO_A = 32
HEAD_GROUP = 8
SOFTMAX_ROWS = 64
SEQ_PER_STEP = 4
FAR_UNROLL = 2
BOUNDED_UNROLL = 4
CH = 256
SUBLANES = 8
VMEM_LIMIT = 56 * 1024 * 1024


def _t5_thresholds():
    max_exact = NUM_BUCKETS // 2
    th = list(range(1, max_exact + 1))
    for k in range(1, NUM_BUCKETS - max_exact):
        th.append(math.ceil(max_exact * (MAX_DISTANCE / max_exact) ** (k / (NUM_BUCKETS - max_exact))))
    return tuple(th)


T5_THRESH = _t5_thresholds()


def _cparams(n_grid):
    return pltpu.CompilerParams(dimension_semantics=("arbitrary",) * n_grid, vmem_limit_bytes=VMEM_LIMIT)


def _const_spec(shape):
    nd = len(shape)
    return pl.BlockSpec(shape, lambda *_: (0,) * nd, pipeline_mode=pl.Buffered(1))


def _layer_spec(shape, layer):
    nd = len(shape)
    return pl.BlockSpec((1,) + tuple(shape), lambda *_: (layer,) + (0,) * nd, pipeline_mode=pl.Buffered(1))


def _rmsnorm(x, g):
    return x * lax.rsqrt(jnp.mean(x * x, axis=-1, keepdims=True) + EPS) * g


def _split_dot(x, w_bf16):
    hi = x.astype(BF16)
    lo = (x - hi.astype(F32)).astype(BF16)
    return (jnp.dot(hi, w_bf16, preferred_element_type=F32)
            + jnp.dot(lo, w_bf16, preferred_element_type=F32))


def _head_rmsnorm(x, g, head_ones):
    ss = _split_dot(x * x, head_ones)
    return x * lax.rsqrt(ss * (1.0 / HEAD_DIM) + EPS) * g


def _silu(x):
    return x * jax.nn.sigmoid(x)


def _layernorm(x, g, b):
    mu = jnp.mean(x, axis=-1, keepdims=True)
    xc = x - mu
    var = jnp.mean(xc * xc, axis=-1, keepdims=True)
    return xc * lax.rsqrt(var + EPS) * g + b


def _in_proj_even(x, g, w, qg, kg, head_ones):
    h = _rmsnorm(x, g).astype(BF16)
    z = jnp.dot(h, w, preferred_element_type=F32)
    a = z[:, :D_A] * jax.nn.sigmoid(z[:, D_A:2 * D_A])
    q = _head_rmsnorm(z[:, 2 * D_A:2 * D_A + D_B], qg, head_ones) * Q_SCALE
    k = _head_rmsnorm(z[:, 2 * D_A + D_B:2 * D_A + 2 * D_B], kg, head_ones)
    v = z[:, 2 * D_A + 2 * D_B:]
    return a, q, k, v


def _prompt_inproj_body(x_ref, g_ref, w_ref, qg_ref, kg_ref, ho_ref,
                        a_ref, kp_ref, vp_ref, qt_ref, kh_ref, vt_ref, km_ref, *, tm, page):
    a, q, k, v = _in_proj_even(x_ref[0], g_ref[...], w_ref[...], qg_ref[...], kg_ref[...], ho_ref[...])
    a_ref[0] = a
    qt = q.T
    kt = k.T
    vt = v.T
    for pg in range(tm // page):
        kp_ref[0, pg] = kt[:, pg * page:(pg + 1) * page].reshape(N_HEADS, HEAD_DIM, page)
        vp_ref[0, pg] = vt[:, pg * page:(pg + 1) * page].reshape(N_HEADS, HEAD_DIM, page)
    kb = k.astype(BF16)
    pad_row = lax.broadcasted_iota(jnp.int32, (N_HEADS, VT_ROWS - HEAD_DIM, MOBA_BLOCK), 1)
    ones_rows = jnp.where(pad_row == 0, 1.0, 0.0).astype(BF16)
    for i in range(tm // MOBA_BLOCK):
        r0 = i * MOBA_BLOCK
        qt_ref[0, :, i] = qt[:, r0:r0 + MOBA_BLOCK].reshape(N_HEADS, HEAD_DIM, MOBA_BLOCK).astype(BF16)
        vt_ref[0, :, i, 0:HEAD_DIM, :] = (
            vt[:, r0:r0 + MOBA_BLOCK].reshape(N_HEADS, HEAD_DIM, MOBA_BLOCK).astype(BF16))
        vt_ref[0, :, i, HEAD_DIM:VT_ROWS, :] = ones_rows
        for hh in range(N_HEADS):
            kh_ref[0, hh, i] = kb[r0:r0 + MOBA_BLOCK, hh * HEAD_DIM:(hh + 1) * HEAD_DIM]
        km_ref[0, i] = jnp.mean(k[r0:r0 + MOBA_BLOCK], axis=0, keepdims=True)


def _prompt_inproj(x, g, w, qg, kg, head_ones, page):
    b, t, d = x.shape
    tm = TILE_M
    nt = t // tm
    nb = t // MOBA_BLOCK
    bpt = tm // MOBA_BLOCK
    ppt = tm // page
    n_out = w.shape[1]
    tok = lambda bi, ti: (bi, ti, 0)
    blk5 = lambda bi, ti: (bi, 0, ti, 0, 0)
    pages = lambda bi, ti: (bi, ti, 0, 0, 0)
    return pl.pallas_call(
        functools.partial(_prompt_inproj_body, tm=tm, page=page),
        grid=(b, nt),
        in_specs=[
            pl.BlockSpec((1, tm, d), tok),
            _const_spec((1, d)),
            _const_spec((d, n_out)),
            _const_spec((1, D_B)),
            _const_spec((1, D_B)),
            _const_spec((D_B, D_B)),
        ],
        out_specs=[
            pl.BlockSpec((1, tm, D_A), tok),
            pl.BlockSpec((1, ppt, N_HEADS, HEAD_DIM, page), pages),
            pl.BlockSpec((1, ppt, N_HEADS, HEAD_DIM, page), pages),
            pl.BlockSpec((1, N_HEADS, bpt, HEAD_DIM, MOBA_BLOCK), blk5),
            pl.BlockSpec((1, N_HEADS, bpt, MOBA_BLOCK, HEAD_DIM), blk5),
            pl.BlockSpec((1, N_HEADS, bpt, VT_ROWS, MOBA_BLOCK), blk5),
            pl.BlockSpec((1, bpt, 1, D_B), lambda bi, ti: (bi, ti, 0, 0)),
        ],
        out_shape=[
            jax.ShapeDtypeStruct((b, t, D_A), F32),
            jax.ShapeDtypeStruct((b, t // page, N_HEADS, HEAD_DIM, page), F32),
            jax.ShapeDtypeStruct((b, t // page, N_HEADS, HEAD_DIM, page), F32),
            jax.ShapeDtypeStruct((b, N_HEADS, nb, HEAD_DIM, MOBA_BLOCK), BF16),
            jax.ShapeDtypeStruct((b, N_HEADS, nb, MOBA_BLOCK, HEAD_DIM), BF16),
            jax.ShapeDtypeStruct((b, N_HEADS, nb, VT_ROWS, MOBA_BLOCK), BF16),
            jax.ShapeDtypeStruct((b, nb, 1, D_B), F32),
        ],
        compiler_params=_cparams(2),
        name="prompt_inproj",
    )(x, g, w, qg, kg, head_ones)


def _t5_bias_scalar_table(dist, rb_ref, h):
    out = jnp.full(dist.shape, rb_ref[0, h] * LOG2E, F32)
    for i, th in enumerate(T5_THRESH):
        out = jnp.where(dist >= th, rb_ref[i + 1, h] * LOG2E, out)
    return out


def _moba_prompt_body(rb_ref, ctl_ref, qt_ref, kh_ref, vt_ref, km_ref, ot_ref,
                      bdiag_ref, bsub_ref, selb_ref, m_ref, alpha_ref, acc_ref, s_ref, p_ref, *, nb, hg):
    h0 = pl.program_id(1) * hg
    n = pl.program_id(2)
    blk = MOBA_BLOCK
    bounded = ctl_ref[1, 0] > 0.5
    shifts = [jnp.where(bounded, ctl_ref[0, h0 + g], 0.0) for g in range(hg)]

    @pl.when(n == 0)
    def _():
        ki = lax.broadcasted_iota(jnp.int32, (blk, blk), 0)
        qi = lax.broadcasted_iota(jnp.int32, (blk, blk), 1)
        d0 = qi - ki
        for g in range(hg):
            bdiag_ref[g] = jnp.where(d0 >= 0, _t5_bias_scalar_table(jnp.maximum(d0, 0), rb_ref, h0 + g),
                                     -jnp.inf) - shifts[g]
            bsub_ref[g] = _t5_bias_scalar_table(d0 + blk, rb_ref, h0 + g) - shifts[g]

    bi = lax.broadcasted_iota(jnp.int32, (nb, blk), 0)
    qts = [qt_ref[0, g, 0] for g in range(hg)]

    def select(g):
        km = km_ref[0, g]
        km_hi = km.astype(BF16)
        km_lo = (km - km_hi.astype(F32)).astype(BF16)
        gate = (jnp.dot(km_hi, qts[g], preferred_element_type=F32)
                + jnp.dot(km_lo, qts[g], preferred_element_type=F32))
        avail = jnp.where(bi < n, 1.0, 0.0)
        far_bias = rb_ref[NUM_BUCKETS - 1, h0 + g] * LOG2E - shifts[g]
        selb = jnp.full((nb, blk), -jnp.inf, F32)
        for _ in range(MOBA_TOPK):
            gm = jnp.where(avail > 0.0, gate, -jnp.inf)
            top = jnp.max(gm, axis=0, keepdims=True)
            first = jnp.where(avail > 0.0, jnp.where(gm == top, bi, nb), nb)
            pick = bi == jnp.min(first, axis=0, keepdims=True)
            selb = jnp.where(pick, jnp.where(bi == n - 1, 0.0, far_bias), selb)
            avail = jnp.where(pick, 0.0, avail)
        selb_ref[g] = selb

    n_chunks = blk // SOFTMAX_ROWS

    def logits(g, j, slot, bias):
        s = jnp.dot(kh_ref[0, g, j], qts[g], preferred_element_type=F32)
        s_ref[slot, g] = s if bias is None else s + bias

    def chunk(g, slot, c):
        return s_ref[slot, g, c * SOFTMAX_ROWS:(c + 1) * SOFTMAX_ROWS, :].reshape(
            SOFTMAX_ROWS // SUBLANES, SUBLANES, blk)

    def softmax(g, slot, row):
        m8 = jnp.max(chunk(g, slot, 0), axis=0)
        for c in range(1, n_chunks):
            m8 = jnp.maximum(m8, jnp.max(chunk(g, slot, c), axis=0))
        m_blk = jnp.max(m8, axis=0, keepdims=True)
        if row is None:
            mn = m_blk
            shift = mn
        else:
            m_old = m_ref[g]
            mn = jnp.maximum(m_old, m_blk + row)
            alpha_ref[slot, g] = jnp.exp2(m_old - mn)
            shift = mn - row
        for c in range(n_chunks):
            p = jnp.exp2(chunk(g, slot, c) - shift)
            p_ref[slot, g, c * SOFTMAX_ROWS:(c + 1) * SOFTMAX_ROWS, :] = (
                p.reshape(SOFTMAX_ROWS, blk).astype(BF16))
        m_ref[g] = mn

    def weighted_values(g, j, slot, first):
        pv = jnp.dot(vt_ref[0, g, j], p_ref[slot, g], preferred_element_type=F32)
        acc_ref[g] = pv if first else alpha_ref[slot, g] * acc_ref[g] + pv

    def far_trips(trip, unroll):
        def far(i, carry):
            for r in range(unroll):
                trip(unroll * i + r, r & 1)
            return carry

        n_far = jnp.maximum(n - 1, 0)
        lax.fori_loop(0, n_far // unroll, far, 0)
        done = (n_far // unroll) * unroll
        piece = unroll // 2
        while piece >= 1:
            @pl.when(((n_far - done) & piece) != 0)
            def _(piece=piece, done=done):
                for r in range(piece):
                    trip(done + r, r & 1)
            done = done + ((n_far - done) & piece)
            piece //= 2

    last = jnp.where(n >= 2, n - 2, n - 1)

    @pl.when(bounded)
    def _():
        def probabilities(g, j, slot, bias):
            s = jnp.dot(kh_ref[0, g, j], qts[g], preferred_element_type=F32)
            p_ref[slot, g] = jnp.exp2(s + bias).astype(BF16)

        def add_values(g, j, slot, first):
            pv = jnp.dot(vt_ref[0, g, j], p_ref[slot, g], preferred_element_type=F32)
            acc_ref[g] = pv if first else acc_ref[g] + pv

        before = jnp.maximum(n - 1, 0)
        for g in range(hg):
            probabilities(g, n, 0, bdiag_ref[g])
        for g in range(hg):
            select(g)
        for g in range(hg):
            probabilities(g, before, 1, bsub_ref[g] + selb_ref[g, pl.ds(before, 1), :])
        for g in range(hg):
            add_values(g, n, 0, True)

        def trip(j, slot):
            prev = jnp.where(j == 0, before, j - 1)
            for g in range(hg):
                probabilities(g, j, slot, selb_ref[g, pl.ds(j, 1), :])
            for g in range(hg):
                add_values(g, prev, 1 - slot, False)

        far_trips(trip, BOUNDED_UNROLL)
        last_block = jnp.where(n >= 2, n - 2, before)
        last_slot = jnp.where(n >= 2, n & 1, 1)
        for g in range(hg):
            add_values(g, last_block, last_slot, False)

    @pl.when(jnp.logical_not(bounded))
    def _():
        for g in range(hg):
            logits(g, n, 0, bdiag_ref[g])
        for g in range(hg):
            select(g)
        for g in range(hg):
            logits(g, jnp.maximum(n - 1, 0), 1, bsub_ref[g])
        for g in range(hg):
            softmax(g, 0, None)

        @pl.when(n == 0)
        def _():
            for g in range(hg):
                weighted_values(g, n, 0, True)

        @pl.when(n >= 1)
        def _():
            for g in range(hg):
                weighted_values(g, n, 0, True)
                softmax(g, 1, selb_ref[g, pl.ds(n - 1, 1), :])
                logits(g, 0, 0, None)

            def trip(j, slot):
                nxt = jnp.minimum(j + 1, n - 2)
                prev = jnp.where(j == 0, n - 1, j - 1)
                for g in range(hg):
                    logits(g, nxt, 1 - slot, None)
                for g in range(hg):
                    weighted_values(g, prev, 1 - slot, False)
                for g in range(hg):
                    softmax(g, slot, selb_ref[g, pl.ds(j, 1), :])

            far_trips(trip, FAR_UNROLL)
            for g in range(hg):
                weighted_values(g, last, n & 1, False)

    for g in range(hg):
        ot_ref[0, g * HEAD_DIM:(g + 1) * HEAD_DIM, :] = (
            acc_ref[g, 0:HEAD_DIM, :] / acc_ref[g, HEAD_DIM:HEAD_DIM + 1, :])


def _logit_bound(rel_bias, q_gain, k_gain):
    qk = 1.02 * HEAD_DIM * Q_SCALE * jnp.max(jnp.abs(q_gain)) * jnp.max(jnp.abs(k_gain))
    table = rel_bias * LOG2E
    top, low = jnp.max(table, axis=0), jnp.min(table, axis=0)
    usable = jnp.all(2.0 * qk + (top - low) <= MAX_SHIFT_RANGE)
    return jnp.stack([qk + top, jnp.broadcast_to(usable.astype(F32), top.shape)])


def _moba_prompt(rel_bias, ctl, qt, kh, vt, kmh):
    b, nh, nb, hd, blk = qt.shape
    t = nb * blk
    hg = HEAD_GROUP
    grp = lambda bi, hi, ni: (bi, hi, 0, 0, 0)
    return pl.pallas_call(
        functools.partial(_moba_prompt_body, nb=nb, hg=hg),
        grid=(b, nh // hg, nb),
        in_specs=[
            pl.BlockSpec(memory_space=pltpu.SMEM),
            pl.BlockSpec(memory_space=pltpu.SMEM),
            pl.BlockSpec((1, hg, 1, hd, blk), lambda bi, hi, ni: (bi, hi, ni, 0, 0)),
            pl.BlockSpec((1, hg, nb, blk, hd), grp, pipeline_mode=pl.Buffered(1)),
            pl.BlockSpec((1, hg, nb, VT_ROWS, blk), grp, pipeline_mode=pl.Buffered(1)),
            pl.BlockSpec((1, hg, nb, hd), lambda bi, hi, ni: (bi, hi, 0, 0)),
        ],
        out_specs=pl.BlockSpec((1, hg * hd, blk), lambda bi, hi, ni: (bi, hi, ni)),
        out_shape=jax.ShapeDtypeStruct((b, nh * hd, t), F32),
        scratch_shapes=[
            pltpu.VMEM((hg, blk, blk), F32),
            pltpu.VMEM((hg, blk, blk), F32),
            pltpu.VMEM((hg, nb, blk), F32),
            pltpu.VMEM((hg, 1, blk), F32),
            pltpu.VMEM((2, hg, 1, blk), F32),
            pltpu.VMEM((hg, VT_ROWS, blk), F32),
            pltpu.VMEM((2, hg, blk, blk), F32),
            pltpu.VMEM((2, hg, blk, blk), BF16),
        ],
        compiler_params=_cparams(3),
        name="prompt_moba",
    )(rel_bias, ctl, qt, kh, vt, kmh)


def _prompt_mix_out_body(a_ref, halo_ref, cw_ref, cb_ref, lg_ref, lb_ref, ot_ref, x_ref, w_ref,
                         y_ref, sh_ref, conv_ref, wb_ref, *, tm):
    t = pl.program_id(1)
    rows = tm + HALO_A

    @pl.when((pl.program_id(0) == 0) & (t == 0))
    def _():
        for j in range(CONV_A):
            wb_ref[j] = jnp.broadcast_to(cw_ref[j:j + 1, :], (SUBLANES, D_A))

    sh_ref[0, 0:HALO_A] = jnp.where(t == 0, 0.0, halo_ref[0])
    sh_ref[0, HALO_A:rows] = a_ref[0]
    for s in range(1, SUBLANES):
        sh_ref[s, 0:rows - SUBLANES] = sh_ref[0, s:s + rows - SUBLANES]

    groups = CONV_ROWS // SUBLANES
    first_off = HALO_A - (CONV_A - 1)

    def chunk(c, carry):
        r0 = pl.multiple_of(c * CONV_ROWS, CONV_ROWS)
        for l0 in range(0, D_A, CONV_LANES):
            lanes = slice(l0, l0 + CONV_LANES)
            acc = [jnp.zeros((SUBLANES, CONV_LANES), F32) for _ in range(groups)]
            for s in range(SUBLANES):
                offs = [o for o in range(first_off, first_off + CONV_A) if o % SUBLANES == s]
                qs = [o // SUBLANES for o in offs]
                slab = {gq: sh_ref[s, pl.ds(r0 + gq * SUBLANES, SUBLANES), lanes]
                        for gq in range(min(qs), max(qs) + groups)}
                for o, q in zip(offs, qs):
                    w = wb_ref[o - first_off, :, lanes]
                    for g in range(groups):
                        acc[g] = acc[g] + w * slab[g + q]
            for g in range(groups):
                conv_ref[pl.ds(r0 + g * SUBLANES, SUBLANES), lanes] = acc[g]
        return carry

    lax.fori_loop(0, tm // CONV_ROWS, chunk, 0)

    ao = _silu(_layernorm(conv_ref[...] + cb_ref[...], lg_ref[...], lb_ref[...])).astype(BF16)
    o = ot_ref[0].T.astype(BF16)
    y = (jnp.dot(ao, w_ref[0:D_A, :], preferred_element_type=F32)
         + jnp.dot(o, w_ref[D_A:D_A + D_B, :], preferred_element_type=F32))
    y_ref[0] = x_ref[0] + y


def _prompt_mix_out(a, cw, cb, lg, lb, ot, x, w):
    b, t, d = x.shape
    tm = TILE_M
    nt = t // tm
    hpt = tm // HALO_A
    tok = lambda bi, ti: (bi, ti, 0)
    return pl.pallas_call(
        functools.partial(_prompt_mix_out_body, tm=tm),
        grid=(b, nt),
        in_specs=[
            pl.BlockSpec((1, tm, D_A), tok),
            pl.BlockSpec((1, HALO_A, D_A), lambda bi, ti: (bi, jnp.maximum(ti * hpt - 1, 0), 0)),
            _const_spec((CONV_A, D_A)),
            _const_spec((1, D_A)),
            _const_spec((1, D_A)),
            _const_spec((1, D_A)),
            pl.BlockSpec((1, D_B, tm), lambda bi, ti: (bi, 0, ti)),
            pl.BlockSpec((1, tm, d), tok),
            _const_spec((D_A + D_B, d)),
        ],
        out_specs=pl.BlockSpec((1, tm, d), tok),
        out_shape=jax.ShapeDtypeStruct((b, t, d), F32),
        scratch_shapes=[
            pltpu.VMEM((SUBLANES, tm + HALO_A, D_A), F32),
            pltpu.VMEM((tm, D_A), F32),
            pltpu.VMEM((CONV_A, SUBLANES, D_A), F32),
        ],
        compiler_params=_cparams(2),
        name="prompt_mix_out",
    )(a, a, cw, cb, lg, lb, ot, x, w)


def _conv3_chunk(u, ext_ref, carry_ref, cw_ref, c0, tm):
    lo = SUBLANES
    carry = carry_ref[:, c0:c0 + CH]
    for k in range(1, 3):
        ext_ref[k - 1, lo:lo + k] = carry[SUBLANES - k:SUBLANES]
        ext_ref[k - 1, lo + k:lo + k + tm] = u
    carry_ref[:, c0:c0 + CH] = u[tm - SUBLANES:tm]
    return (cw_ref[0:1, c0:c0 + CH] * ext_ref[1, lo:lo + tm]
            + cw_ref[1:2, c0:c0 + CH] * ext_ref[0, lo:lo + tm]
            + cw_ref[2:3, c0:c0 + CH] * u)


def _prompt_ffn_body(x_ref, g_ref, wu_ref, cw_ref, cb_ref, wd_ref, y_ref, st_ref,
                     carry_ref, extg_ref, extu_ref, act_ref, *, tm):
    @pl.when(pl.program_id(1) == 0)
    def _():
        carry_ref[...] = jnp.zeros_like(carry_ref)

    x = x_ref[0]
    h = _rmsnorm(x, g_ref[...]).astype(BF16)
    for c in range(D_FF // CH):
        cg = c * CH
        cu = D_FF + c * CH
        ug = jnp.dot(h, wu_ref[0, :, cg:cg + CH], preferred_element_type=F32)
        uu = jnp.dot(h, wu_ref[0, :, cu:cu + CH], preferred_element_type=F32)
        gg = _conv3_chunk(ug, extg_ref, carry_ref, cw_ref, cg, tm) + cb_ref[:, cg:cg + CH]
        gu = _conv3_chunk(uu, extu_ref, carry_ref, cw_ref, cu, tm) + cb_ref[:, cu:cu + CH]
        act_ref[:, cg:cg + CH] = (_silu(gg) * gu).astype(BF16)
    st_ref[0] = carry_ref[...]
    y_ref[0] = x + jnp.dot(act_ref[...], wd_ref[0], preferred_element_type=F32)


def _prompt_ffn(x, g, wu, cw, cb, wd, layer):
    b, t, d = x.shape
    tm = FFN_TILE_M
    nt = t // tm
    tok = lambda bi, ti: (bi, ti, 0)
    return pl.pallas_call(
        functools.partial(_prompt_ffn_body, tm=tm),
        grid=(b, nt),
        in_specs=[
            pl.BlockSpec((1, tm, d), tok),
            _const_spec((1, d)),
            _layer_spec((d, 2 * D_FF), layer),
            _const_spec((CONV_F, 2 * D_FF)),
            _const_spec((1, 2 * D_FF)),
            _layer_spec((D_FF, d), layer),
        ],
        out_specs=[
            pl.BlockSpec((1, tm, d), tok),
            pl.BlockSpec((1, SUBLANES, 2 * D_FF), lambda bi, ti: (bi, 0, 0)),
        ],
        out_shape=[
            jax.ShapeDtypeStruct((b, t, d), F32),
            jax.ShapeDtypeStruct((b, SUBLANES, 2 * D_FF), F32),
        ],
        scratch_shapes=[
            pltpu.VMEM((SUBLANES, 2 * D_FF), F32),
            pltpu.VMEM((2, tm + 2 * SUBLANES, CH), F32),
            pltpu.VMEM((2, tm + 2 * SUBLANES, CH), F32),
            pltpu.VMEM((tm, D_FF), BF16),
        ],
        compiler_params=_cparams(2),
        name="prompt_ffn",
    )(x, g, wu, cw, cb, wd)


def _prompt_mixc_body(x_ref, g_ref, wi_ref, cw_ref, wo_ref, y_ref, st_ref,
                      carry_ref, ext_ref, z_ref, *, tm):
    @pl.when(pl.program_id(1) == 0)
    def _():
        carry_ref[...] = jnp.zeros_like(carry_ref)

    x = x_ref[0]
    h = _rmsnorm(x, g_ref[...]).astype(BF16)
    for c in range(D_C // CH):
        c0 = c * CH
        gb = jnp.dot(h, wi_ref[:, c0:c0 + CH], preferred_element_type=F32)
        gc = jnp.dot(h, wi_ref[:, D_C + c0:D_C + c0 + CH], preferred_element_type=F32)
        u = jnp.dot(h, wi_ref[:, 2 * D_C + c0:2 * D_C + c0 + CH], preferred_element_type=F32)
        conv = _conv3_chunk(gc * u, ext_ref, carry_ref, cw_ref, c0, tm)
        z_ref[:, c0:c0 + CH] = (gb * conv).astype(BF16)
    st_ref[0] = carry_ref[...]
    y_ref[0] = x + jnp.dot(z_ref[...], wo_ref[...], preferred_element_type=F32)


def _prompt_mixc(x, g, wi, cw, wo):
    b, t, d = x.shape
    tm = TILE_M
    nt = t // tm
    tok = lambda bi, ti: (bi, ti, 0)
    return pl.pallas_call(
        functools.partial(_prompt_mixc_body, tm=tm),
        grid=(b, nt),
        in_specs=[
            pl.BlockSpec((1, tm, d), tok),
            _const_spec((1, d)),
            _const_spec((d, 3 * D_C)),
            _const_spec((CONV_C, D_C)),
            _const_spec((D_C, d)),
        ],
        out_specs=[
            pl.BlockSpec((1, tm, d), tok),
            pl.BlockSpec((1, SUBLANES, D_C), lambda bi, ti: (bi, 0, 0)),
        ],
        out_shape=[
            jax.ShapeDtypeStruct((b, t, d), F32),
            jax.ShapeDtypeStruct((b, SUBLANES, D_C), F32),
        ],
        scratch_shapes=[
            pltpu.VMEM((SUBLANES, D_C), F32),
            pltpu.VMEM((2, tm + 2 * SUBLANES, CH), F32),
            pltpu.VMEM((tm, D_C), BF16),
        ],
        compiler_params=_cparams(2),
        name="prompt_mixc",
    )(x, g, wi, cw, wo)


def _sample_even_body(x_ref, g_ref, w_ref, qg_ref, kg_ref, ho_ref, sa_ref, cw_ref, cb_ref, lg_ref, lb_ref,
                      ao_ref, san_ref, q_ref, k_ref, v_ref):
    a, q, k, v = _in_proj_even(x_ref[...], g_ref[...], w_ref[...], qg_ref[...], kg_ref[...], ho_ref[...])
    q_ref[...] = q
    k_ref[...] = k
    v_ref[...] = v
    hist = CONV_A - 1
    conv = cb_ref[...] + cw_ref[hist:hist + 1, :] * a
    for j in range(hist):
        conv = conv + cw_ref[j:j + 1, :] * sa_ref[j]
    ao_ref[...] = _silu(_layernorm(conv, lg_ref[...], lb_ref[...])).astype(BF16)
    san_ref[0:hist - 1] = sa_ref[1:hist]
    san_ref[hist - 1] = a


def _sample_even(x, g, w, qg, kg, head_ones, sa, cw, cb, lg, lb):
    m = x.shape[0]
    return pl.pallas_call(
        _sample_even_body,
        out_shape=[
            jax.ShapeDtypeStruct((m, D_A), BF16),
            jax.ShapeDtypeStruct(sa.shape, F32),
            jax.ShapeDtypeStruct((m, D_B), F32),
            jax.ShapeDtypeStruct((m, D_B), F32),
            jax.ShapeDtypeStruct((m, D_B), F32),
        ],
        compiler_params=pltpu.CompilerParams(vmem_limit_bytes=VMEM_LIMIT),
        name="sample_even",
    )(x, g, w, qg, kg, head_ones, sa, cw, cb, lg, lb)


def _moba_sample_keys_body(pt_ref, q_ref, qdh_ref, kn_ref, rbt_ref, *refs, n_pages, page):
    del pt_ref
    outs = refs[SEQ_PER_STEP * n_pages:]
    bias_ref = outs[4]
    past = n_pages * page

    @pl.when(pl.program_id(0) == 0)
    def _():
        pos = lax.broadcasted_iota(jnp.int32, (N_HEADS, past), 1)
        dist = past - pos
        out = jnp.zeros((N_HEADS, past), F32) + rbt_ref[:, 0:1]
        for i, th in enumerate(T5_THRESH):
            out = jnp.where(dist >= th, rbt_ref[:, i + 1:i + 2], out)
        bias_ref[...] = out * LOG2E

    for sq in range(SEQ_PER_STEP):
        _moba_sample_keys_one(sq, q_ref, qdh_ref, kn_ref, rbt_ref, refs[sq * n_pages:(sq + 1) * n_pages], *outs,
                              n_pages=n_pages, page=page)


def _moba_sample_keys_one(sq, q_ref, qdh_ref, kn_ref, rbt_ref, k_pages, pc_ref, pn_ref, l_ref, sel_ref,
                          bias_ref, qb_ref, s_ref, *, n_pages, page):
    past = n_pages * page
    n_past_blocks = past // MOBA_BLOCK
    n_sel = min(MOBA_TOPK, n_past_blocks)
    qb_ref = qb_ref.at[sq]
    s_ref = s_ref.at[sq]

    head_of_lane = lax.broadcasted_iota(jnp.int32, (N_HEADS, D_B), 1) // HEAD_DIM
    hmask = head_of_lane == lax.broadcasted_iota(jnp.int32, (N_HEADS, D_B), 0)
    qbd = jnp.where(hmask, q_ref[sq], 0.0)

    qdh = qdh_ref[sq]
    for h in range(N_HEADS):
        qb_ref[h] = jnp.broadcast_to(qdh[:, h:h + 1], (HEAD_DIM, page))

    for pg in range(n_pages):
        rows = [jnp.sum(k_pages[pg][0, h] * qb_ref[h], axis=0, keepdims=True) for h in range(N_HEADS)]
        s_ref[:, pg * page:(pg + 1) * page] = jnp.concatenate(rows, axis=0)

    gates = [jnp.sum(s_ref[:, j * MOBA_BLOCK:(j + 1) * MOBA_BLOCK], axis=-1, keepdims=True)
             for j in range(n_past_blocks)]
    ranks = []
    pieces = []
    for j in range(n_past_blocks):
        rank = jnp.zeros((N_HEADS, 1), jnp.int32)
        for i in range(n_past_blocks):
            if i == j:
                continue
            ahead = (gates[i] >= gates[j]) if i < j else (gates[i] > gates[j])
            rank = rank + jnp.where(ahead, 1, 0)
        ranks.append(rank)
        lanes = slice(j * MOBA_BLOCK, (j + 1) * MOBA_BLOCK)
        pieces.append(s_ref[:, lanes] + bias_ref[:, lanes] + jnp.where(rank < n_sel, 0.0, -jnp.inf))
    s = jnp.concatenate(pieces, axis=-1)

    s_new = jnp.sum(qbd * kn_ref[sq], axis=-1, keepdims=True) + rbt_ref[:, 0:1] * LOG2E
    m = jnp.maximum(jnp.max(s, axis=-1, keepdims=True), s_new)
    p = jnp.exp2(s - m)
    p_new = jnp.exp2(s_new - m)
    pn_ref[sq] = jnp.broadcast_to(p_new, (N_HEADS, 128))
    l_ref[sq] = jnp.broadcast_to(jnp.sum(p, axis=-1, keepdims=True) + p_new, (N_HEADS, 128))

    lane = lax.broadcasted_iota(jnp.int32, (N_HEADS, 128), 1)
    sel = jnp.zeros((N_HEADS, 128), jnp.int32)
    for r in range(n_sel):
        picked = jnp.zeros((N_HEADS, MOBA_BLOCK), F32)
        block_id = jnp.zeros((N_HEADS, 1), jnp.int32)
        for j in range(n_past_blocks):
            picked = picked + jnp.where(ranks[j] == r, p[:, j * MOBA_BLOCK:(j + 1) * MOBA_BLOCK], 0.0)
            block_id = block_id + jnp.where(ranks[j] == r, j, 0)
        pc_ref[sq, :, r * MOBA_BLOCK:(r + 1) * MOBA_BLOCK] = picked
        sel = jnp.where(lane == r, block_id, sel)
    sel_ref[sq] = sel


def _moba_sample_values_body(pt_ref, sel_ref, pc_ref, pn_ref, l_ref, vn_ref, v_hbm, o_ref, buf_ref, sem_ref,
                             *, n_sel, ppb, page):
    b = pl.program_id(0)
    n_tiles = n_sel * ppb

    def tile_copy(seq, slot, h, i):
        block = sel_ref[seq * (N_HEADS * n_sel) + h * n_sel + i // ppb]
        pool_page = pt_ref[seq, block * ppb + i % ppb]
        return pltpu.make_async_copy(v_hbm.at[pool_page, h], buf_ref.at[slot, h * n_tiles + i], sem_ref.at[slot])

    def for_all_tiles(seq, slot, act):
        for h in range(N_HEADS):
            for i in range(n_tiles):
                act(tile_copy(seq, slot, h, i))

    @pl.when(b == 0)
    def _():
        for_all_tiles(0, 0, lambda cp: cp.start())

    @pl.when(b + 1 < pl.num_programs(0))
    def _():
        for_all_tiles(b + 1, (b + 1) % 2, lambda cp: cp.start())

    slot = b % 2
    for_all_tiles(b, slot, lambda cp: cp.wait())

    head_of_lane = lax.broadcasted_iota(jnp.int32, (N_HEADS, D_B), 1) // HEAD_DIM
    hmask = head_of_lane == lax.broadcasted_iota(jnp.int32, (N_HEADS, D_B), 0)
    ones = jnp.ones((SUBLANES, page), BF16)
    lane_sum = lambda x: lax.dot_general(ones, x, (((1,), (1,)), ((), ())), preferred_element_type=F32)
    outs = []
    for h in range(N_HEADS):
        acc = jnp.zeros((HEAD_DIM, page), F32)
        for i in range(n_tiles):
            acc = acc + buf_ref[slot, h * n_tiles + i] * pc_ref[0, h:h + 1, i * page:(i + 1) * page]
        hi = acc.astype(BF16)
        lo = (acc - hi.astype(F32)).astype(BF16)
        outs.append((lane_sum(hi) + lane_sum(lo))[0:1, :])
    o_past = jnp.concatenate(outs, axis=-1)
    spread = lambda col: jnp.sum(jnp.where(hmask, col, 0.0), axis=0, keepdims=True)
    o_ref[0] = (o_past + spread(pn_ref[0, :, 0:1]) * vn_ref[0]) / spread(l_ref[0, :, 0:1])


def _moba_sample(page_table, q, k_new, v_new, rel_bias_t, cache_kt, cache_vt):
    m, n_pages = page_table.shape
    page = cache_kt.shape[-1]
    past = n_pages * page
    n_sel = min(MOBA_TOPK, past // MOBA_BLOCK)
    ppb = MOBA_BLOCK // page
    row = lambda bi, *_: (bi, 0, 0)

    sps = SEQ_PER_STEP
    assert m % sps == 0
    keys_spec = pltpu.PrefetchScalarGridSpec(
        num_scalar_prefetch=1,
        grid=(m // sps,),
        in_specs=[
            pl.BlockSpec((sps, 1, D_B), row),
            pl.BlockSpec((sps, HEAD_DIM, N_HEADS), row),
            pl.BlockSpec((sps, 1, D_B), row),
            pl.BlockSpec((N_HEADS, NUM_BUCKETS), lambda bi, pt: (0, 0)),
        ] + [pl.BlockSpec((1, N_HEADS, HEAD_DIM, page), lambda bi, pt, sq=sq, pg=pg: (pt[bi * sps + sq, pg], 0, 0, 0))
             for sq in range(sps) for pg in range(n_pages)],
        out_specs=[
            pl.BlockSpec((sps, N_HEADS, n_sel * MOBA_BLOCK), row),
            pl.BlockSpec((sps, N_HEADS, 128), row),
            pl.BlockSpec((sps, N_HEADS, 128), row),
            pl.BlockSpec((sps, N_HEADS, 128), row),
        ],
        scratch_shapes=[
            pltpu.VMEM((N_HEADS, past), F32),
            pltpu.VMEM((sps, N_HEADS, HEAD_DIM, page), F32),
            pltpu.VMEM((sps, N_HEADS, past), F32),
        ],
    )
    q_dh = q.reshape(m, N_HEADS, HEAD_DIM).transpose(0, 2, 1)
    pc, pn, l, sel = pl.pallas_call(
        functools.partial(_moba_sample_keys_body, n_pages=n_pages, page=page),
        grid_spec=keys_spec,
        out_shape=[
            jax.ShapeDtypeStruct((m, N_HEADS, n_sel * MOBA_BLOCK), F32),
            jax.ShapeDtypeStruct((m, N_HEADS, 128), F32),
            jax.ShapeDtypeStruct((m, N_HEADS, 128), F32),
            jax.ShapeDtypeStruct((m, N_HEADS, 128), jnp.int32),
        ],
        compiler_params=_cparams(1),
        name="sample_moba_keys",
    )(page_table, q.reshape(m, 1, D_B), q_dh, k_new.reshape(m, 1, D_B), rel_bias_t, *([cache_kt] * (sps * n_pages)))

    sel_flat = sel[:, :, :n_sel].reshape(m * N_HEADS * n_sel)
    values_spec = pltpu.PrefetchScalarGridSpec(
        num_scalar_prefetch=2,
        grid=(m,),
        in_specs=[
            pl.BlockSpec((1, N_HEADS, n_sel * MOBA_BLOCK), row),
            pl.BlockSpec((1, N_HEADS, 128), row),
            pl.BlockSpec((1, N_HEADS, 128), row),
            pl.BlockSpec((1, 1, D_B), row),
            pl.BlockSpec(memory_space=pl.ANY),
        ],
        out_specs=pl.BlockSpec((1, 1, D_B), row),
        scratch_shapes=[
            pltpu.VMEM((2, N_HEADS * n_sel * ppb, HEAD_DIM, page), F32),
            pltpu.SemaphoreType.DMA((2,)),
        ],
    )
    return pl.pallas_call(
        functools.partial(_moba_sample_values_body, n_sel=n_sel, ppb=ppb, page=page),
        grid_spec=values_spec,
        out_shape=jax.ShapeDtypeStruct((m, 1, D_B), F32),
        compiler_params=_cparams(1),
        name="sample_moba_values",
    )(page_table, sel_flat, pc, pn, l, v_new.reshape(m, 1, D_B), cache_vt)


def _sample_ffn(x, st_ref, g_ref, wu_ref, cw_ref, cb_ref, wd_ref, stn_ref):
    h = _rmsnorm(x, g_ref[...]).astype(BF16)
    up = jnp.dot(h, wu_ref[0], preferred_element_type=F32)
    older, newer = st_ref[0, :, 0, :], st_ref[0, :, 1, :]
    conv = cw_ref[0:1, :] * older + cw_ref[1:2, :] * newer + cw_ref[2:3, :] * up + cb_ref[...]
    act = (_silu(conv[:, :D_FF]) * conv[:, D_FF:]).astype(BF16)
    stn_ref[:, 0, :] = newer
    stn_ref[:, 1, :] = up
    return x + jnp.dot(act, wd_ref[0], preferred_element_type=F32)


def _sample_out_ffn_body(ao_ref, o_ref, x_ref, wo_ref, st_ref, g_ref, wu_ref, cw_ref, cb_ref, wd_ref,
                         y_ref, stn_ref):
    x1 = (x_ref[...]
          + jnp.dot(ao_ref[...], wo_ref[0:D_A, :], preferred_element_type=F32)
          + jnp.dot(o_ref[...].astype(BF16), wo_ref[D_A:D_A + D_B, :], preferred_element_type=F32))
    y_ref[...] = _sample_ffn(x1, st_ref, g_ref, wu_ref, cw_ref, cb_ref, wd_ref, stn_ref)


def _whole(a):
    return _const_spec(a.shape)


def _whole_out(shape):
    nd = len(shape)
    return pl.BlockSpec(tuple(shape), lambda *_: (0,) * nd)


def _sample_out_ffn(ao, o, x, wo, st, g, wu, cw, cb, wd, layer):
    st_shape = st.shape[1:]
    return pl.pallas_call(
        _sample_out_ffn_body,
        grid=(1,),
        in_specs=[_whole(ao), _whole(o), _whole(x), _whole(wo), _layer_spec(st_shape, layer), _whole(g),
                  _layer_spec(wu.shape[1:], layer), _whole(cw), _whole(cb), _layer_spec(wd.shape[1:], layer)],
        out_specs=[_whole_out(x.shape), _whole_out(st_shape)],
        out_shape=[jax.ShapeDtypeStruct(x.shape, F32), jax.ShapeDtypeStruct(st_shape, F32)],
        compiler_params=_cparams(1),
        name="sample_out_ffn",
    )(ao, o, x, wo, st, g, wu, cw, cb, wd)


def _sample_odd_body(x_ref, gm_ref, wi_ref, ccw_ref, sc_ref, wo_ref, st_ref, g_ref, wu_ref, cw_ref, cb_ref,
                     wd_ref, y_ref, scn_ref, stn_ref):
    x = x_ref[...]
    h = _rmsnorm(x, gm_ref[...]).astype(BF16)
    z = jnp.dot(h, wi_ref[...], preferred_element_type=F32)
    gcu = z[:, D_C:2 * D_C] * z[:, 2 * D_C:]
    older, newer = sc_ref[:, 0, :], sc_ref[:, 1, :]
    conv = ccw_ref[0:1, :] * older + ccw_ref[1:2, :] * newer + ccw_ref[2:3, :] * gcu
    scn_ref[:, 0, :] = newer
    scn_ref[:, 1, :] = gcu
    x1 = x + jnp.dot((z[:, :D_C] * conv).astype(BF16), wo_ref[...], preferred_element_type=F32)
    y_ref[...] = _sample_ffn(x1, st_ref, g_ref, wu_ref, cw_ref, cb_ref, wd_ref, stn_ref)


def _sample_odd(x, gm, wi, ccw, sc, wo, st, g, wu, cw, cb, wd, layer):
    st_shape = st.shape[1:]
    return pl.pallas_call(
        _sample_odd_body,
        grid=(1,),
        in_specs=[_whole(x), _whole(gm), _whole(wi), _whole(ccw), _whole(sc), _whole(wo),
                  _layer_spec(st_shape, layer), _whole(g), _layer_spec(wu.shape[1:], layer), _whole(cw), _whole(cb),
                  _layer_spec(wd.shape[1:], layer)],
        out_specs=[_whole_out(x.shape), _whole_out(sc.shape), _whole_out(st_shape)],
        out_shape=[jax.ShapeDtypeStruct(x.shape, F32), jax.ShapeDtypeStruct(sc.shape, F32),
                   jax.ShapeDtypeStruct(st_shape, F32)],
        compiler_params=_cparams(1),
        name="sample_odd",
    )(x, gm, wi, ccw, sc, wo, st, g, wu, cw, cb, wd)


def kernel(x_prompt, x_sample, cache_k, cache_v, state_conv_a, state_conv_c, state_ffn, page_table, rel_bias,
           norm_mix_e, w_in_e, conv_a_w, conv_a_b, ln_a_g, ln_a_b, q_norm_g, k_norm_g, w_out_e,
           norm_mix_o, w_in_o, conv_c_w, w_out_o, norm_ffn, w_up, conv_f_w, conv_f_b, w_down):
    b, t, d = x_prompt.shape
    m = x_sample.shape[0]
    n_pool, page = cache_k.shape[1], cache_k.shape[2]
    n_pages = page_table.shape[1]
    assert norm_mix_e.shape[0] == 1 and norm_mix_o.shape[0] == 1 and norm_ffn.shape[0] == 2
    assert x_sample.shape[1] == 1 and t % TILE_M == 0 and t % FFN_TILE_M == 0 and t % page == 0
    assert (n_pages * page) % MOBA_BLOCK == 0 and MOBA_BLOCK % page == 0

    row = lambda v: v.reshape(1, -1)
    w_in_e_b = w_in_e[0].astype(BF16)
    w_out_e_b = w_out_e[0].astype(BF16)
    w_in_o_b = w_in_o[0].astype(BF16)
    w_out_o_b = w_out_o[0].astype(BF16)
    w_up_b = w_up.astype(BF16)
    w_down_b = w_down.astype(BF16)
    qg = row(jnp.tile(q_norm_g[0], N_HEADS))
    kg = row(jnp.tile(k_norm_g[0], N_HEADS))
    lane_head = jnp.arange(D_B, dtype=jnp.int32) // HEAD_DIM
    head_ones = (lane_head[:, None] == lane_head[None, :]).astype(BF16)
    g_e, g_o = row(norm_mix_e[0]), row(norm_mix_o[0])
    cab, lag, lab = row(conv_a_b[0]), row(ln_a_g[0]), row(ln_a_b[0])

    a_p, k_p, v_p, qt, kh, vt, km = _prompt_inproj(x_prompt, g_e, w_in_e_b, qg, kg, head_ones, page)
    nb = t // MOBA_BLOCK
    kmh = km.reshape(b, nb, N_HEADS, HEAD_DIM).transpose(0, 2, 1, 3)
    ot = _moba_prompt(rel_bias, _logit_bound(rel_bias, q_norm_g[0], k_norm_g[0]), qt, kh, vt, kmh)
    x1 = _prompt_mix_out(a_p, conv_a_w[0], cab, lag, lab, ot, x_prompt, w_out_e_b)
    x2, f0 = _prompt_ffn(x1, row(norm_ffn[0]), w_up_b, conv_f_w[0], row(conv_f_b[0]), w_down_b, 0)
    x3, c_st = _prompt_mixc(x2, g_o, w_in_o_b, conv_c_w[0], w_out_o_b)
    y_prompt, f1 = _prompt_ffn(x3, row(norm_ffn[1]), w_up_b, conv_f_w[1], row(conv_f_b[1]), w_down_b, 1)

    k_prompt = k_p.transpose(0, 1, 4, 2, 3)[None]
    v_prompt = v_p.transpose(0, 1, 4, 2, 3)[None]
    a_prompt = a_p[:, t - (CONV_A - 1):, :][None]
    c_prompt = c_st[:, SUBLANES - (CONV_C - 1):, :][None]
    f_prompt = jnp.stack([f0[:, SUBLANES - (CONV_F - 1):, :], f1[:, SUBLANES - (CONV_F - 1):, :]])

    xs = x_sample.reshape(m, d)
    sa = state_conv_a[0].transpose(1, 0, 2)
    sc = state_conv_c[0]
    sf = state_ffn
    ao_s, sa_new, q_s, k_s, v_s = _sample_even(xs, g_e, w_in_e_b, qg, kg, head_ones, sa, conv_a_w[0], cab, lag, lab)
    o_s = _moba_sample(page_table, q_s, k_s, v_s, rel_bias.T,
                       cache_k[0].transpose(0, 2, 3, 1), cache_v[0].transpose(0, 2, 3, 1))
    xs1, sf0 = _sample_out_ffn(ao_s, o_s.reshape(m, D_B), xs, w_out_e_b, sf, row(norm_ffn[0]), w_up_b,
                               conv_f_w[0], row(conv_f_b[0]), w_down_b, 0)
    ys, sc_new, sf1 = _sample_odd(xs1, g_o, w_in_o_b, conv_c_w[0], sc, w_out_o_b, sf, row(norm_ffn[1]),
                                  w_up_b, conv_f_w[1], row(conv_f_b[1]), w_down_b, 1)

    y_sample = ys.reshape(m, 1, d)
    k_sample = k_s.reshape(1, m, 1, N_HEADS, HEAD_DIM)
    v_sample = v_s.reshape(1, m, 1, N_HEADS, HEAD_DIM)
    a_sample = sa_new.transpose(1, 0, 2)[None]
    c_sample = sc_new[None]
    f_sample = jnp.stack([sf0, sf1])
    return (y_prompt, y_sample, k_prompt, v_prompt, a_prompt, c_prompt, f_prompt,
            k_sample, v_sample, a_sample, c_sample, f_sample)
```

```python
import functools
import math

import jax
import jax.numpy as jnp
from jax import lax
from jax.experimental import pallas as pl
from jax.experimental.pallas import tpu as pltpu

F32 = jnp.float32
BF16 = jnp.bfloat16

EPS = 1e-6
D_MODEL = 1024
D_A = 512
CONV_A = 31
N_HEADS = 8
HEAD_DIM = 64
D_B = N_HEADS * HEAD_DIM
MOBA_BLOCK = 256
MOBA_TOPK = 3
NUM_BUCKETS = 32
MAX_DISTANCE = 128
D_C = 1024
CONV_C = 3
D_FF = 2816
CONV_F = 3
LOG2E = math.log2(math.e)
Q_SCALE = HEAD_DIM ** -0.5 * LOG2E
VT_ROWS = HEAD_DIM + 16
MAX_SHIFT_RANGE = 100.0

TILE_M = 512
FFN_TILE_M = 1024
CONV_ROWS = 64
CONV_LANES = 256
HALO_A = 32
HEAD_GROUP = 8
SOFTMAX_ROWS = 64
SEQ_PER_STEP = 4
FAR_UNROLL = 2
BOUNDED_UNROLL = 4
CH = 256
SUBLANES = 8
VMEM_LIMIT = 56 * 1024 * 1024


def _t5_thresholds():
    max_exact = NUM_BUCKETS // 2
    th = list(range(1, max_exact + 1))
    for k in range(1, NUM_BUCKETS - max_exact):
        th.append(math.ceil(max_exact * (MAX_DISTANCE / max_exact) ** (k / (NUM_BUCKETS - max_exact))))
    return tuple(th)


T5_THRESH = _t5_thresholds()


def _cparams(n_grid):
    return pltpu.CompilerParams(dimension_semantics=("arbitrary",) * n_grid, vmem_limit_bytes=VMEM_LIMIT)


def _const_spec(shape):
    nd = len(shape)
    return pl.BlockSpec(shape, lambda *_: (0,) * nd, pipeline_mode=pl.Buffered(1))


def _layer_spec(shape, layer):
    nd = len(shape)
    return pl.BlockSpec((1,) + tuple(shape), lambda *_: (layer,) + (0,) * nd, pipeline_mode=pl.Buffered(1))


def _rmsnorm(x, g):
    return x * lax.rsqrt(jnp.mean(x * x, axis=-1, keepdims=True) + EPS) * g


def _split_dot(x, w_bf16):
    hi = x.astype(BF16)
    lo = (x - hi.astype(F32)).astype(BF16)
    return (jnp.dot(hi, w_bf16, preferred_element_type=F32)
            + jnp.dot(lo, w_bf16, preferred_element_type=F32))


def _head_rmsnorm(x, g, head_ones):
    ss = _split_dot(x * x, head_ones)
    return x * lax.rsqrt(ss * (1.0 / HEAD_DIM) + EPS) * g


def _silu(x):
    return x * jax.nn.sigmoid(x)


def _layernorm(x, g, b):
    mu = jnp.mean(x, axis=-1, keepdims=True)
    xc = x - mu
    var = jnp.mean(xc * xc, axis=-1, keepdims=True)
    return xc * lax.rsqrt(var + EPS) * g + b


def _in_proj_even(x, g, w, qg, kg, head_ones):
    h = _rmsnorm(x, g).astype(BF16)
    z = jnp.dot(h, w, preferred_element_type=F32)
    a = z[:, :D_A] * jax.nn.sigmoid(z[:, D_A:2 * D_A])
    q = _head_rmsnorm(z[:, 2 * D_A:2 * D_A + D_B], qg, head_ones) * Q_SCALE
    k = _head_rmsnorm(z[:, 2 * D_A + D_B:2 * D_A + 2 * D_B], kg, head_ones)
    v = z[:, 2 * D_A + 2 * D_B:]
    return a, q, k, v


def _prompt_inproj_body(x_ref, g_ref, w_ref, qg_ref, kg_ref, ho_ref,
                        a_ref, kp_ref, vp_ref, qt_ref, kh_ref, vt_ref, km_ref, *, tm, page):
    a, q, k, v = _in_proj_even(x_ref[0], g_ref[...], w_ref[...], qg_ref[...], kg_ref[...], ho_ref[...])
    a_ref[0] = a
    qt = q.T
    kt = k.T
    vt = v.T
    for pg in range(tm // page):
        kp_ref[0, pg] = kt[:, pg * page:(pg + 1) * page].reshape(N_HEADS, HEAD_DIM, page)
        vp_ref[0, pg] = vt[:, pg * page:(pg + 1) * page].reshape(N_HEADS, HEAD_DIM, page)
    kb = k.astype(BF16)
    pad_row = lax.broadcasted_iota(jnp.int32, (N_HEADS, VT_ROWS - HEAD_DIM, MOBA_BLOCK), 1)
    ones_rows = jnp.where(pad_row == 0, 1.0, 0.0).astype(BF16)
    for i in range(tm // MOBA_BLOCK):
        r0 = i * MOBA_BLOCK
        qt_ref[0, :, i] = qt[:, r0:r0 + MOBA_BLOCK].reshape(N_HEADS, HEAD_DIM, MOBA_BLOCK).astype(BF16)
        vt_ref[0, :, i, 0:HEAD_DIM, :] = (
            vt[:, r0:r0 + MOBA_BLOCK].reshape(N_HEADS, HEAD_DIM, MOBA_BLOCK).astype(BF16))
        vt_ref[0, :, i, HEAD_DIM:VT_ROWS, :] = ones_rows
        for hh in range(N_HEADS):
            kh_ref[0, hh, i] = kb[r0:r0 + MOBA_BLOCK, hh * HEAD_DIM:(hh + 1) * HEAD_DIM]
        km_ref[0, i] = jnp.mean(k[r0:r0 + MOBA_BLOCK], axis=0, keepdims=True)


def _prompt_inproj(x, g, w, qg, kg, head_ones, page):
    b, t, d = x.shape
    tm = TILE_M
    nt = t // tm
    nb = t // MOBA_BLOCK
    bpt = tm // MOBA_BLOCK
    ppt = tm // page
    n_out = w.shape[1]
    tok = lambda bi, ti: (bi, ti, 0)
    blk5 = lambda bi, ti: (bi, 0, ti, 0, 0)
    pages = lambda bi, ti: (bi, ti, 0, 0, 0)
    return pl.pallas_call(
        functools.partial(_prompt_inproj_body, tm=tm, page=page),
        grid=(b, nt),
        in_specs=[
            pl.BlockSpec((1, tm, d), tok),
            _const_spec((1, d)),
            _const_spec((d, n_out)),
            _const_spec((1, D_B)),
            _const_spec((1, D_B)),
            _const_spec((D_B, D_B)),
        ],
        out_specs=[
            pl.BlockSpec((1, tm, D_A), tok),
            pl.BlockSpec((1, ppt, N_HEADS, HEAD_DIM, page), pages),
            pl.BlockSpec((1, ppt, N_HEADS, HEAD_DIM, page), pages),
            pl.BlockSpec((1, N_HEADS, bpt, HEAD_DIM, MOBA_BLOCK), blk5),
            pl.BlockSpec((1, N_HEADS, bpt, MOBA_BLOCK, HEAD_DIM), blk5),
            pl.BlockSpec((1, N_HEADS, bpt, VT_ROWS, MOBA_BLOCK), blk5),
            pl.BlockSpec((1, bpt, 1, D_B), lambda bi, ti: (bi, ti, 0, 0)),
        ],
        out_shape=[
            jax.ShapeDtypeStruct((b, t, D_A), F32),
            jax.ShapeDtypeStruct((b, t // page, N_HEADS, HEAD_DIM, page), F32),
            jax.ShapeDtypeStruct((b, t // page, N_HEADS, HEAD_DIM, page), F32),
            jax.ShapeDtypeStruct((b, N_HEADS, nb, HEAD_DIM, MOBA_BLOCK), BF16),
            jax.ShapeDtypeStruct((b, N_HEADS, nb, MOBA_BLOCK, HEAD_DIM), BF16),
            jax.ShapeDtypeStruct((b, N_HEADS, nb, VT_ROWS, MOBA_BLOCK), BF16),
            jax.ShapeDtypeStruct((b, nb, 1, D_B), F32),
        ],
        compiler_params=_cparams(2),
        name="prompt_inproj",
    )(x, g, w, qg, kg, head_ones)


def _t5_bias_scalar_table(dist, rb_ref, h):
    out = jnp.full(dist.shape, rb_ref[0, h] * LOG2E, F32)
    for i, th in enumerate(T5_THRESH):
        out = jnp.where(dist >= th, rb_ref[i + 1, h] * LOG2E, out)
    return out


def _moba_prompt_body(rb_ref, ctl_ref, qt_ref, kh_ref, vt_ref, km_ref, ot_ref,
                      bdiag_ref, bsub_ref, selb_ref, m_ref, alpha_ref, acc_ref, s_ref, p_ref, *, nb, hg):
    h0 = pl.program_id(1) * hg
    n = pl.program_id(2)
    blk = MOBA_BLOCK
    bounded = ctl_ref[1, 0] > 0.5
    shifts = [jnp.where(bounded, ctl_ref[0, h0 + g], 0.0) for g in range(hg)]

    @pl.when(n == 0)
    def _():
        ki = lax.broadcasted_iota(jnp.int32, (blk, blk), 0)
        qi = lax.broadcasted_iota(jnp.int32, (blk, blk), 1)
        d0 = qi - ki
        for g in range(hg):
            bdiag_ref[g] = jnp.where(d0 >= 0, _t5_bias_scalar_table(jnp.maximum(d0, 0), rb_ref, h0 + g),
                                     -jnp.inf) - shifts[g]
            bsub_ref[g] = _t5_bias_scalar_table(d0 + blk, rb_ref, h0 + g) - shifts[g]

    bi = lax.broadcasted_iota(jnp.int32, (nb, blk), 0)
    qts = [qt_ref[0, g, 0] for g in range(hg)]

    def select(g):
        km = km_ref[0, g]
        km_hi = km.astype(BF16)
        km_lo = (km - km_hi.astype(F32)).astype(BF16)
        gate = (jnp.dot(km_hi, qts[g], preferred_element_type=F32)
                + jnp.dot(km_lo, qts[g], preferred_element_type=F32))
        avail = jnp.where(bi < n, 1.0, 0.0)
        far_bias = rb_ref[NUM_BUCKETS - 1, h0 + g] * LOG2E - shifts[g]
        selb = jnp.full((nb, blk), -jnp.inf, F32)
        for _ in range(MOBA_TOPK):
            gm = jnp.where(avail > 0.0, gate, -jnp.inf)
            top = jnp.max(gm, axis=0, keepdims=True)
            first = jnp.where(avail > 0.0, jnp.where(gm == top, bi, nb), nb)
            pick = bi == jnp.min(first, axis=0, keepdims=True)
            selb = jnp.where(pick, jnp.where(bi == n - 1, 0.0, far_bias), selb)
            avail = jnp.where(pick, 0.0, avail)
        selb_ref[g] = selb

    n_chunks = blk // SOFTMAX_ROWS

    def logits(g, j, slot, bias):
        s = jnp.dot(kh_ref[0, g, j], qts[g], preferred_element_type=F32)
        s_ref[slot, g] = s if bias is None else s + bias

    def chunk(g, slot, c):
        return s_ref[slot, g, c * SOFTMAX_ROWS:(c + 1) * SOFTMAX_ROWS, :].reshape(
            SOFTMAX_ROWS // SUBLANES, SUBLANES, blk)

    def softmax(g, slot, row):
        m8 = jnp.max(chunk(g, slot, 0), axis=0)
        for c in range(1, n_chunks):
            m8 = jnp.maximum(m8, jnp.max(chunk(g, slot, c), axis=0))
        m_blk = jnp.max(m8, axis=0, keepdims=True)
        if row is None:
            mn = m_blk
            shift = mn
        else:
            m_old = m_ref[g]
            mn = jnp.maximum(m_old, m_blk + row)
            alpha_ref[slot, g] = jnp.exp2(m_old - mn)
            shift = mn - row
        for c in range(n_chunks):
            p = jnp.exp2(chunk(g, slot, c) - shift)
            p_ref[slot, g, c * SOFTMAX_ROWS:(c + 1) * SOFTMAX_ROWS, :] = (
                p.reshape(SOFTMAX_ROWS, blk).astype(BF16))
        m_ref[g] = mn

    def weighted_values(g, j, slot, first):
        pv = jnp.dot(vt_ref[0, g, j], p_ref[slot, g], preferred_element_type=F32)
        acc_ref[g] = pv if first else alpha_ref[slot, g] * acc_ref[g] + pv

    def far_trips(trip, unroll):
        def far(i, carry):
            for r in range(unroll):
                trip(unroll * i + r, r & 1)
            return carry

        n_far = jnp.maximum(n - 1, 0)
        lax.fori_loop(0, n_far // unroll, far, 0)
        done = (n_far // unroll) * unroll
        piece = unroll // 2
        while piece >= 1:
            @pl.when(((n_far - done) & piece) != 0)
            def _(piece=piece, done=done):
                for r in range(piece):
                    trip(done + r, r & 1)
            done = done + ((n_far - done) & piece)
            piece //= 2

    last = jnp.where(n >= 2, n - 2, n - 1)

    @pl.when(bounded)
    def _():
        def probabilities(g, j, slot, bias):
            s = jnp.dot(kh_ref[0, g, j], qts[g], preferred_element_type=F32)
            p_ref[slot, g] = jnp.exp2(s + bias).astype(BF16)

        def add_values(g, j, slot, first):
            pv = jnp.dot(vt_ref[0, g, j], p_ref[slot, g], preferred_element_type=F32)
            acc_ref[g] = pv if first else acc_ref[g] + pv

        before = jnp.maximum(n - 1, 0)
        for g in range(hg):
            probabilities(g, n, 0, bdiag_ref[g])
        for g in range(hg):
            select(g)
        for g in range(hg):
            probabilities(g, before, 1, bsub_ref[g] + selb_ref[g, pl.ds(before, 1), :])
            add_values(g, n, 0, True)

        def trip(j, slot):
            prev = jnp.where(j == 0, before, j - 1)
            for g in range(hg):
                probabilities(g, j, slot, selb_ref[g, pl.ds(j, 1), :])
                add_values(g, prev, 1 - slot, False)

        far_trips(trip, BOUNDED_UNROLL)
        last_block = jnp.where(n >= 2, n - 2, before)
        last_slot = jnp.where(n >= 2, n & 1, 1)
        for g in range(hg):
            add_values(g, last_block, last_slot, False)

    @pl.when(jnp.logical_not(bounded))
    def _():
        for g in range(hg):
            logits(g, n, 0, bdiag_ref[g])
        for g in range(hg):
            select(g)
        for g in range(hg):
            logits(g, jnp.maximum(n - 1, 0), 1, bsub_ref[g])
        for g in range(hg):
            softmax(g, 0, None)

        @pl.when(n == 0)
        def _():
            for g in range(hg):
                weighted_values(g, n, 0, True)

        @pl.when(n >= 1)
        def _():
            for g in range(hg):
                weighted_values(g, n, 0, True)
                softmax(g, 1, selb_ref[g, pl.ds(n - 1, 1), :])
                logits(g, 0, 0, None)

            def trip(j, slot):
                nxt = jnp.minimum(j + 1, n - 2)
                prev = jnp.where(j == 0, n - 1, j - 1)
                for g in range(hg):
                    logits(g, nxt, 1 - slot, None)
                for g in range(hg):
                    weighted_values(g, prev, 1 - slot, False)
                for g in range(hg):
                    softmax(g, slot, selb_ref[g, pl.ds(j, 1), :])

            far_trips(trip, FAR_UNROLL)
            for g in range(hg):
                weighted_values(g, last, n & 1, False)

    for g in range(hg):
        ot_ref[0, g * HEAD_DIM:(g + 1) * HEAD_DIM, :] = (
            acc_ref[g, 0:HEAD_DIM, :] / acc_ref[g, HEAD_DIM:HEAD_DIM + 1, :])


def _logit_bound(rel_bias, q_gain, k_gain):
    qk = 1.02 * HEAD_DIM * Q_SCALE * jnp.max(jnp.abs(q_gain)) * jnp.max(jnp.abs(k_gain))
    table = rel_bias * LOG2E
    top, low = jnp.max(table, axis=0), jnp.min(table, axis=0)
    usable = jnp.all(2.0 * qk + (top - low) <= MAX_SHIFT_RANGE)
    return jnp.stack([qk + top, jnp.broadcast_to(usable.astype(F32), top.shape)])


def _moba_prompt(rel_bias, ctl, qt, kh, vt, kmh):
    b, nh, nb, hd, blk = qt.shape
    t = nb * blk
    hg = HEAD_GROUP
    grp = lambda bi, hi, ni: (bi, hi, 0, 0, 0)
    return pl.pallas_call(
        functools.partial(_moba_prompt_body, nb=nb, hg=hg),
        grid=(b, nh // hg, nb),
        in_specs=[
            pl.BlockSpec(memory_space=pltpu.SMEM),
            pl.BlockSpec(memory_space=pltpu.SMEM),
            pl.BlockSpec((1, hg, 1, hd, blk), lambda bi, hi, ni: (bi, hi, ni, 0, 0)),
            pl.BlockSpec((1, hg, nb, blk, hd), grp, pipeline_mode=pl.Buffered(1)),
            pl.BlockSpec((1, hg, nb, VT_ROWS, blk), grp, pipeline_mode=pl.Buffered(1)),
            pl.BlockSpec((1, hg, nb, hd), lambda bi, hi, ni: (bi, hi, 0, 0)),
        ],
        out_specs=pl.BlockSpec((1, hg * hd, blk), lambda bi, hi, ni: (bi, hi, ni)),
        out_shape=jax.ShapeDtypeStruct((b, nh * hd, t), F32),
        scratch_shapes=[
            pltpu.VMEM((hg, blk, blk), F32),
            pltpu.VMEM((hg, blk, blk), F32),
            pltpu.VMEM((hg, nb, blk), F32),
            pltpu.VMEM((hg, 1, blk), F32),
            pltpu.VMEM((2, hg, 1, blk), F32),
            pltpu.VMEM((hg, VT_ROWS, blk), F32),
            pltpu.VMEM((2, hg, blk, blk), F32),
            pltpu.VMEM((2, hg, blk, blk), BF16),
        ],
        compiler_params=_cparams(3),
        name="prompt_moba",
    )(rel_bias, ctl, qt, kh, vt, kmh)


def _prompt_mix_out_body(a_ref, halo_ref, cw_ref, cb_ref, lg_ref, lb_ref, ot_ref, x_ref, w_ref,
                         y_ref, sh_ref, conv_ref, wb_ref, *, tm):
    t = pl.program_id(1)
    rows = tm + HALO_A

    @pl.when((pl.program_id(0) == 0) & (t == 0))
    def _():
        for j in range(CONV_A):
            wb_ref[j] = jnp.broadcast_to(cw_ref[j:j + 1, :], (SUBLANES, D_A))

    sh_ref[0, 0:HALO_A] = jnp.where(t == 0, 0.0, halo_ref[0])
    sh_ref[0, HALO_A:rows] = a_ref[0]
    for s in range(1, SUBLANES):
        sh_ref[s, 0:rows - SUBLANES] = sh_ref[0, s:s + rows - SUBLANES]

    groups = CONV_ROWS // SUBLANES
    first_off = HALO_A - (CONV_A - 1)

    def chunk(c, carry):
        r0 = pl.multiple_of(c * CONV_ROWS, CONV_ROWS)
        for l0 in range(0, D_A, CONV_LANES):
            lanes = slice(l0, l0 + CONV_LANES)
            acc = [jnp.zeros((SUBLANES, CONV_LANES), F32) for _ in range(groups)]
            for s in range(SUBLANES):
                offs = [o for o in range(first_off, first_off + CONV_A) if o % SUBLANES == s]
                qs = [o // SUBLANES for o in offs]
                slab = {gq: sh_ref[s, pl.ds(r0 + gq * SUBLANES, SUBLANES), lanes]
                        for gq in range(min(qs), max(qs) + groups)}
                for o, q in zip(offs, qs):
                    w = wb_ref[o - first_off, :, lanes]
                    for g in range(groups):
                        acc[g] = acc[g] + w * slab[g + q]
            for g in range(groups):
                conv_ref[pl.ds(r0 + g * SUBLANES, SUBLANES), lanes] = acc[g]
        return carry

    lax.fori_loop(0, tm // CONV_ROWS, chunk, 0)

    ao = _silu(_layernorm(conv_ref[...] + cb_ref[...], lg_ref[...], lb_ref[...])).astype(BF16)
    o = ot_ref[0].T.astype(BF16)
    y = (jnp.dot(ao, w_ref[0:D_A, :], preferred_element_type=F32)
         + jnp.dot(o, w_ref[D_A:D_A + D_B, :], preferred_element_type=F32))
    y_ref[0] = x_ref[0] + y


def _prompt_mix_out(a, cw, cb, lg, lb, ot, x, w):
    b, t, d = x.shape
    tm = TILE_M
    nt = t // tm
    hpt = tm // HALO_A
    tok = lambda bi, ti: (bi, ti, 0)
    return pl.pallas_call(
        functools.partial(_prompt_mix_out_body, tm=tm),
        grid=(b, nt),
        in_specs=[
            pl.BlockSpec((1, tm, D_A), tok),
            pl.BlockSpec((1, HALO_A, D_A), lambda bi, ti: (bi, jnp.maximum(ti * hpt - 1, 0), 0)),
            _const_spec((CONV_A, D_A)),
            _const_spec((1, D_A)),
            _const_spec((1, D_A)),
            _const_spec((1, D_A)),
            pl.BlockSpec((1, D_B, tm), lambda bi, ti: (bi, 0, ti)),
            pl.BlockSpec((1, tm, d), tok),
            _const_spec((D_A + D_B, d)),
        ],
        out_specs=pl.BlockSpec((1, tm, d), tok),
        out_shape=jax.ShapeDtypeStruct((b, t, d), F32),
        scratch_shapes=[
            pltpu.VMEM((SUBLANES, tm + HALO_A, D_A), F32),
            pltpu.VMEM((tm, D_A), F32),
            pltpu.VMEM((CONV_A, SUBLANES, D_A), F32),
        ],
        compiler_params=_cparams(2),
        name="prompt_mix_out",
    )(a, a, cw, cb, lg, lb, ot, x, w)


def _conv3_chunk(u, ext_ref, carry_ref, cw_ref, c0, tm):
    lo = SUBLANES
    carry = carry_ref[:, c0:c0 + CH]
    for k in range(1, 3):
        ext_ref[k - 1, lo:lo + k] = carry[SUBLANES - k:SUBLANES]
        ext_ref[k - 1, lo + k:lo + k + tm] = u
    carry_ref[:, c0:c0 + CH] = u[tm - SUBLANES:tm]
    return (cw_ref[0:1, c0:c0 + CH] * ext_ref[1, lo:lo + tm]
            + cw_ref[1:2, c0:c0 + CH] * ext_ref[0, lo:lo + tm]
            + cw_ref[2:3, c0:c0 + CH] * u)


def _prompt_ffn_body(x_ref, g_ref, wu_ref, cw_ref, cb_ref, wd_ref, y_ref, st_ref,
                     carry_ref, extg_ref, extu_ref, act_ref, *, tm):
    @pl.when(pl.program_id(1) == 0)
    def _():
        carry_ref[...] = jnp.zeros_like(carry_ref)

    x = x_ref[0]
    h = _rmsnorm(x, g_ref[...]).astype(BF16)
    for c in range(D_FF // CH):
        cg = c * CH
        cu = D_FF + c * CH
        ug = jnp.dot(h, wu_ref[0, :, cg:cg + CH], preferred_element_type=F32)
        uu = jnp.dot(h, wu_ref[0, :, cu:cu + CH], preferred_element_type=F32)
        gg = _conv3_chunk(ug, extg_ref, carry_ref, cw_ref, cg, tm) + cb_ref[:, cg:cg + CH]
        gu = _conv3_chunk(uu, extu_ref, carry_ref, cw_ref, cu, tm) + cb_ref[:, cu:cu + CH]
        act_ref[:, cg:cg + CH] = (_silu(gg) * gu).astype(BF16)
    st_ref[0] = carry_ref[...]
    y_ref[0] = x + jnp.dot(act_ref[...], wd_ref[0], preferred_element_type=F32)


def _prompt_ffn(x, g, wu, cw, cb, wd, layer):
    b, t, d = x.shape
    tm = FFN_TILE_M
    nt = t // tm
    tok = lambda bi, ti: (bi, ti, 0)
    return pl.pallas_call(
        functools.partial(_prompt_ffn_body, tm=tm),
        grid=(b, nt),
        in_specs=[
            pl.BlockSpec((1, tm, d), tok),
            _const_spec((1, d)),
            _layer_spec((d, 2 * D_FF), layer),
            _const_spec((CONV_F, 2 * D_FF)),
            _const_spec((1, 2 * D_FF)),
            _layer_spec((D_FF, d), layer),
        ],
        out_specs=[
            pl.BlockSpec((1, tm, d), tok),
            pl.BlockSpec((1, SUBLANES, 2 * D_FF), lambda bi, ti: (bi, 0, 0)),
        ],
        out_shape=[
            jax.ShapeDtypeStruct((b, t, d), F32),
            jax.ShapeDtypeStruct((b, SUBLANES, 2 * D_FF), F32),
        ],
        scratch_shapes=[
            pltpu.VMEM((SUBLANES, 2 * D_FF), F32),
            pltpu.VMEM((2, tm + 2 * SUBLANES, CH), F32),
            pltpu.VMEM((2, tm + 2 * SUBLANES, CH), F32),
            pltpu.VMEM((tm, D_FF), BF16),
        ],
        compiler_params=_cparams(2),
        name="prompt_ffn",
    )(x, g, wu, cw, cb, wd)


def _prompt_mixc_body(x_ref, g_ref, wi_ref, cw_ref, wo_ref, y_ref, st_ref,
                      carry_ref, ext_ref, z_ref, *, tm):
    @pl.when(pl.program_id(1) == 0)
    def _():
        carry_ref[...] = jnp.zeros_like(carry_ref)

    x = x_ref[0]
    h = _rmsnorm(x, g_ref[...]).astype(BF16)
    for c in range(D_C // CH):
        c0 = c * CH
        gb = jnp.dot(h, wi_ref[:, c0:c0 + CH], preferred_element_type=F32)
        gc = jnp.dot(h, wi_ref[:, D_C + c0:D_C + c0 + CH], preferred_element_type=F32)
        u = jnp.dot(h, wi_ref[:, 2 * D_C + c0:2 * D_C + c0 + CH], preferred_element_type=F32)
        conv = _conv3_chunk(gc * u, ext_ref, carry_ref, cw_ref, c0, tm)
        z_ref[:, c0:c0 + CH] = (gb * conv).astype(BF16)
    st_ref[0] = carry_ref[...]
    y_ref[0] = x + jnp.dot(z_ref[...], wo_ref[...], preferred_element_type=F32)


def _prompt_mixc(x, g, wi, cw, wo):
    b, t, d = x.shape
    tm = TILE_M
    nt = t // tm
    tok = lambda bi, ti: (bi, ti, 0)
    return pl.pallas_call(
        functools.partial(_prompt_mixc_body, tm=tm),
        grid=(b, nt),
        in_specs=[
            pl.BlockSpec((1, tm, d), tok),
            _const_spec((1, d)),
            _const_spec((d, 3 * D_C)),
            _const_spec((CONV_C, D_C)),
            _const_spec((D_C, d)),
        ],
        out_specs=[
            pl.BlockSpec((1, tm, d), tok),
            pl.BlockSpec((1, SUBLANES, D_C), lambda bi, ti: (bi, 0, 0)),
        ],
        out_shape=[
            jax.ShapeDtypeStruct((b, t, d), F32),
            jax.ShapeDtypeStruct((b, SUBLANES, D_C), F32),
        ],
        scratch_shapes=[
            pltpu.VMEM((SUBLANES, D_C), F32),
            pltpu.VMEM((2, tm + 2 * SUBLANES, CH), F32),
            pltpu.VMEM((tm, D_C), BF16),
        ],
        compiler_params=_cparams(2),
        name="prompt_mixc",
    )(x, g, wi, cw, wo)


def _sample_even_body(x_ref, g_ref, w_ref, qg_ref, kg_ref, ho_ref, sa_ref, cw_ref, cb_ref, lg_ref, lb_ref,
                      ao_ref, san_ref, q_ref, k_ref, v_ref):
    a, q, k, v = _in_proj_even(x_ref[...], g_ref[...], w_ref[...], qg_ref[...], kg_ref[...], ho_ref[...])
    q_ref[...] = q
    k_ref[...] = k
    v_ref[...] = v
    hist = CONV_A - 1
    conv = cb_ref[...] + cw_ref[hist:hist + 1, :] * a
    for j in range(hist):
        conv = conv + cw_ref[j:j + 1, :] * sa_ref[j]
    ao_ref[...] = _silu(_layernorm(conv, lg_ref[...], lb_ref[...])).astype(BF16)
    san_ref[0:hist - 1] = sa_ref[1:hist]
    san_ref[hist - 1] = a


def _sample_even(x, g, w, qg, kg, head_ones, sa, cw, cb, lg, lb):
    m = x.shape[0]
    return pl.pallas_call(
        _sample_even_body,
        out_shape=[
            jax.ShapeDtypeStruct((m, D_A), BF16),
            jax.ShapeDtypeStruct(sa.shape, F32),
            jax.ShapeDtypeStruct((m, D_B), F32),
            jax.ShapeDtypeStruct((m, D_B), F32),
            jax.ShapeDtypeStruct((m, D_B), F32),
        ],
        compiler_params=pltpu.CompilerParams(vmem_limit_bytes=VMEM_LIMIT),
        name="sample_even",
    )(x, g, w, qg, kg, head_ones, sa, cw, cb, lg, lb)


def _moba_sample_keys_body(pt_ref, q_ref, qdh_ref, kn_ref, rbt_ref, *refs, n_pages, page):
    del pt_ref
    outs = refs[SEQ_PER_STEP * n_pages:]
    bias_ref = outs[4]
    past = n_pages * page

    @pl.when(pl.program_id(0) == 0)
    def _():
        pos = lax.broadcasted_iota(jnp.int32, (N_HEADS, past), 1)
        dist = past - pos
        out = jnp.zeros((N_HEADS, past), F32) + rbt_ref[:, 0:1]
        for i, th in enumerate(T5_THRESH):
            out = jnp.where(dist >= th, rbt_ref[:, i + 1:i + 2], out)
        bias_ref[...] = out * LOG2E

    for sq in range(SEQ_PER_STEP):
        _moba_sample_keys_one(sq, q_ref, qdh_ref, kn_ref, rbt_ref, refs[sq * n_pages:(sq + 1) * n_pages], *outs,
                              n_pages=n_pages, page=page)


def _moba_sample_keys_one(sq, q_ref, qdh_ref, kn_ref, rbt_ref, k_pages, pc_ref, pn_ref, l_ref, sel_ref,
                          bias_ref, qb_ref, s_ref, *, n_pages, page):
    past = n_pages * page
    n_past_blocks = past // MOBA_BLOCK
    n_sel = min(MOBA_TOPK, n_past_blocks)
    qb_ref = qb_ref.at[sq]
    s_ref = s_ref.at[sq]

    head_of_lane = lax.broadcasted_iota(jnp.int32, (N_HEADS, D_B), 1) // HEAD_DIM
    hmask = head_of_lane == lax.broadcasted_iota(jnp.int32, (N_HEADS, D_B), 0)
    qbd = jnp.where(hmask, q_ref[sq], 0.0)

    qdh = qdh_ref[sq]
    for h in range(N_HEADS):
        qb_ref[h] = jnp.broadcast_to(qdh[:, h:h + 1], (HEAD_DIM, page))

    for pg in range(n_pages):
        rows = [jnp.sum(k_pages[pg][0, h] * qb_ref[h], axis=0, keepdims=True) for h in range(N_HEADS)]
        s_ref[:, pg * page:(pg + 1) * page] = jnp.concatenate(rows, axis=0)

    gates = [jnp.sum(s_ref[:, j * MOBA_BLOCK:(j + 1) * MOBA_BLOCK], axis=-1, keepdims=True)
             for j in range(n_past_blocks)]
    ranks = []
    pieces = []
    for j in range(n_past_blocks):
        rank = jnp.zeros((N_HEADS, 1), jnp.int32)
        for i in range(n_past_blocks):
            if i == j:
                continue
            ahead = (gates[i] >= gates[j]) if i < j else (gates[i] > gates[j])
            rank = rank + jnp.where(ahead, 1, 0)
        ranks.append(rank)
        lanes = slice(j * MOBA_BLOCK, (j + 1) * MOBA_BLOCK)
        pieces.append(s_ref[:, lanes] + bias_ref[:, lanes] + jnp.where(rank < n_sel, 0.0, -jnp.inf))
    s = jnp.concatenate(pieces, axis=-1)

    s_new = jnp.sum(qbd * kn_ref[sq], axis=-1, keepdims=True) + rbt_ref[:, 0:1] * LOG2E
    m = jnp.maximum(jnp.max(s, axis=-1, keepdims=True), s_new)
    p = jnp.exp2(s - m)
    p_new = jnp.exp2(s_new - m)
    pn_ref[sq] = jnp.broadcast_to(p_new, (N_HEADS, 128))
    l_ref[sq] = jnp.broadcast_to(jnp.sum(p, axis=-1, keepdims=True) + p_new, (N_HEADS, 128))

    lane = lax.broadcasted_iota(jnp.int32, (N_HEADS, 128), 1)
    sel = jnp.zeros((N_HEADS, 128), jnp.int32)
    for r in range(n_sel):
        picked = jnp.zeros((N_HEADS, MOBA_BLOCK), F32)
        block_id = jnp.zeros((N_HEADS, 1), jnp.int32)
        for j in range(n_past_blocks):
            picked = picked + jnp.where(ranks[j] == r, p[:, j * MOBA_BLOCK:(j + 1) * MOBA_BLOCK], 0.0)
            block_id = block_id + jnp.where(ranks[j] == r, j, 0)
        pc_ref[sq, :, r * MOBA_BLOCK:(r + 1) * MOBA_BLOCK] = picked
        sel = jnp.where(lane == r, block_id, sel)
    sel_ref[sq] = sel


def _moba_sample_values_body(pt_ref, sel_ref, pc_ref, pn_ref, l_ref, vn_ref, v_hbm, o_ref, buf_ref, sem_ref,
                             *, n_sel, ppb, page):
    b = pl.program_id(0)
    n_tiles = n_sel * ppb

    def tile_copy(seq, slot, h, i):
        block = sel_ref[seq * (N_HEADS * n_sel) + h * n_sel + i // ppb]
        pool_page = pt_ref[seq, block * ppb + i % ppb]
        return pltpu.make_async_copy(v_hbm.at[pool_page, h], buf_ref.at[slot, h * n_tiles + i], sem_ref.at[slot])

    def for_all_tiles(seq, slot, act):
        for h in range(N_HEADS):
            for i in range(n_tiles):
                act(tile_copy(seq, slot, h, i))

    @pl.when(b == 0)
    def _():
        for_all_tiles(0, 0, lambda cp: cp.start())

    @pl.when(b + 1 < pl.num_programs(0))
    def _():
        for_all_tiles(b + 1, (b + 1) % 2, lambda cp: cp.start())

    slot = b % 2
    for_all_tiles(b, slot, lambda cp: cp.wait())

    head_of_lane = lax.broadcasted_iota(jnp.int32, (N_HEADS, D_B), 1) // HEAD_DIM
    hmask = head_of_lane == lax.broadcasted_iota(jnp.int32, (N_HEADS, D_B), 0)
    ones = jnp.ones((SUBLANES, page), BF16)
    lane_sum = lambda x: lax.dot_general(ones, x, (((1,), (1,)), ((), ())), preferred_element_type=F32)
    outs = []
    for h in range(N_HEADS):
        acc = jnp.zeros((HEAD_DIM, page), F32)
        for i in range(n_tiles):
            acc = acc + buf_ref[slot, h * n_tiles + i] * pc_ref[0, h:h + 1, i * page:(i + 1) * page]
        hi = acc.astype(BF16)
        lo = (acc - hi.astype(F32)).astype(BF16)
        outs.append((lane_sum(hi) + lane_sum(lo))[0:1, :])
    o_past = jnp.concatenate(outs, axis=-1)
    spread = lambda col: jnp.sum(jnp.where(hmask, col, 0.0), axis=0, keepdims=True)
    o_ref[0] = (o_past + spread(pn_ref[0, :, 0:1]) * vn_ref[0]) / spread(l_ref[0, :, 0:1])


def _moba_sample(page_table, q, k_new, v_new, rel_bias_t, cache_kt, cache_vt):
    m, n_pages = page_table.shape
    page = cache_kt.shape[-1]
    past = n_pages * page
    n_sel = min(MOBA_TOPK, past // MOBA_BLOCK)
    ppb = MOBA_BLOCK // page
    row = lambda bi, *_: (bi, 0, 0)

    sps = SEQ_PER_STEP
    assert m % sps == 0
    keys_spec = pltpu.PrefetchScalarGridSpec(
        num_scalar_prefetch=1,
        grid=(m // sps,),
        in_specs=[
            pl.BlockSpec((sps, 1, D_B), row),
            pl.BlockSpec((sps, HEAD_DIM, N_HEADS), row),
            pl.BlockSpec((sps, 1, D_B), row),
            pl.BlockSpec((N_HEADS, NUM_BUCKETS), lambda bi, pt: (0, 0)),
        ] + [pl.BlockSpec((1, N_HEADS, HEAD_DIM, page), lambda bi, pt, sq=sq, pg=pg: (pt[bi * sps + sq, pg], 0, 0, 0))
             for sq in range(sps) for pg in range(n_pages)],
        out_specs=[
            pl.BlockSpec((sps, N_HEADS, n_sel * MOBA_BLOCK), row),
            pl.BlockSpec((sps, N_HEADS, 128), row),
            pl.BlockSpec((sps, N_HEADS, 128), row),
            pl.BlockSpec((sps, N_HEADS, 128), row),
        ],
        scratch_shapes=[
            pltpu.VMEM((N_HEADS, past), F32),
            pltpu.VMEM((sps, N_HEADS, HEAD_DIM, page), F32),
            pltpu.VMEM((sps, N_HEADS, past), F32),
        ],
    )
    q_dh = q.reshape(m, N_HEADS, HEAD_DIM).transpose(0, 2, 1)
    pc, pn, l, sel = pl.pallas_call(
        functools.partial(_moba_sample_keys_body, n_pages=n_pages, page=page),
        grid_spec=keys_spec,
        out_shape=[
            jax.ShapeDtypeStruct((m, N_HEADS, n_sel * MOBA_BLOCK), F32),
            jax.ShapeDtypeStruct((m, N_HEADS, 128), F32),
            jax.ShapeDtypeStruct((m, N_HEADS, 128), F32),
            jax.ShapeDtypeStruct((m, N_HEADS, 128), jnp.int32),
        ],
        compiler_params=_cparams(1),
        name="sample_moba_keys",
    )(page_table, q.reshape(m, 1, D_B), q_dh, k_new.reshape(m, 1, D_B), rel_bias_t, *([cache_kt] * (sps * n_pages)))

    sel_flat = sel[:, :, :n_sel].reshape(m * N_HEADS * n_sel)
    values_spec = pltpu.PrefetchScalarGridSpec(
        num_scalar_prefetch=2,
        grid=(m,),
        in_specs=[
            pl.BlockSpec((1, N_HEADS, n_sel * MOBA_BLOCK), row),
            pl.BlockSpec((1, N_HEADS, 128), row),
            pl.BlockSpec((1, N_HEADS, 128), row),
            pl.BlockSpec((1, 1, D_B), row),
            pl.BlockSpec(memory_space=pl.ANY),
        ],
        out_specs=pl.BlockSpec((1, 1, D_B), row),
        scratch_shapes=[
            pltpu.VMEM((2, N_HEADS * n_sel * ppb, HEAD_DIM, page), F32),
            pltpu.SemaphoreType.DMA((2,)),
        ],
    )
    return pl.pallas_call(
        functools.partial(_moba_sample_values_body, n_sel=n_sel, ppb=ppb, page=page),
        grid_spec=values_spec,
        out_shape=jax.ShapeDtypeStruct((m, 1, D_B), F32),
        compiler_params=_cparams(1),
        name="sample_moba_values",
    )(page_table, sel_flat, pc, pn, l, v_new.reshape(m, 1, D_B), cache_vt)


def _sample_ffn(x, st_ref, g_ref, wu_ref, cw_ref, cb_ref, wd_ref, stn_ref):
    h = _rmsnorm(x, g_ref[...]).astype(BF16)
    up = jnp.dot(h, wu_ref[0], preferred_element_type=F32)
    older, newer = st_ref[0, :, 0, :], st_ref[0, :, 1, :]
    conv = cw_ref[0:1, :] * older + cw_ref[1:2, :] * newer + cw_ref[2:3, :] * up + cb_ref[...]
    act = (_silu(conv[:, :D_FF]) * conv[:, D_FF:]).astype(BF16)
    stn_ref[:, 0, :] = newer
    stn_ref[:, 1, :] = up
    return x + jnp.dot(act, wd_ref[0], preferred_element_type=F32)


def _sample_out_ffn_body(ao_ref, o_ref, x_ref, wo_ref, st_ref, g_ref, wu_ref, cw_ref, cb_ref, wd_ref,
                         y_ref, stn_ref):
    x1 = (x_ref[...]
          + jnp.dot(ao_ref[...], wo_ref[0:D_A, :], preferred_element_type=F32)
          + jnp.dot(o_ref[...].astype(BF16), wo_ref[D_A:D_A + D_B, :], preferred_element_type=F32))
    y_ref[...] = _sample_ffn(x1, st_ref, g_ref, wu_ref, cw_ref, cb_ref, wd_ref, stn_ref)


def _whole(a):
    return _const_spec(a.shape)


def _whole_out(shape):
    nd = len(shape)
    return pl.BlockSpec(tuple(shape), lambda *_: (0,) * nd)


def _sample_out_ffn(ao, o, x, wo, st, g, wu, cw, cb, wd, layer):
    st_shape = st.shape[1:]
    return pl.pallas_call(
        _sample_out_ffn_body,
        grid=(1,),
        in_specs=[_whole(ao), _whole(o), _whole(x), _whole(wo), _layer_spec(st_shape, layer), _whole(g),
                  _layer_spec(wu.shape[1:], layer), _whole(cw), _whole(cb), _layer_spec(wd.shape[1:], layer)],
        out_specs=[_whole_out(x.shape), _whole_out(st_shape)],
        out_shape=[jax.ShapeDtypeStruct(x.shape, F32), jax.ShapeDtypeStruct(st_shape, F32)],
        compiler_params=_cparams(1),
        name="sample_out_ffn",
    )(ao, o, x, wo, st, g, wu, cw, cb, wd)


def _sample_odd_body(x_ref, gm_ref, wi_ref, ccw_ref, sc_ref, wo_ref, st_ref, g_ref, wu_ref, cw_ref, cb_ref,
                     wd_ref, y_ref, scn_ref, stn_ref):
    x = x_ref[...]
    h = _rmsnorm(x, gm_ref[...]).astype(BF16)
    z = jnp.dot(h, wi_ref[...], preferred_element_type=F32)
    gcu = z[:, D_C:2 * D_C] * z[:, 2 * D_C:]
    older, newer = sc_ref[:, 0, :], sc_ref[:, 1, :]
    conv = ccw_ref[0:1, :] * older + ccw_ref[1:2, :] * newer + ccw_ref[2:3, :] * gcu
    scn_ref[:, 0, :] = newer
    scn_ref[:, 1, :] = gcu
    x1 = x + jnp.dot((z[:, :D_C] * conv).astype(BF16), wo_ref[...], preferred_element_type=F32)
    y_ref[...] = _sample_ffn(x1, st_ref, g_ref, wu_ref, cw_ref, cb_ref, wd_ref, stn_ref)


def _sample_odd(x, gm, wi, ccw, sc, wo, st, g, wu, cw, cb, wd, layer):
    st_shape = st.shape[1:]
    return pl.pallas_call(
        _sample_odd_body,
        grid=(1,),
        in_specs=[_whole(x), _whole(gm), _whole(wi), _whole(ccw), _whole(sc), _whole(wo),
                  _layer_spec(st_shape, layer), _whole(g), _layer_spec(wu.shape[1:], layer), _whole(cw), _whole(cb),
                  _layer_spec(wd.shape[1:], layer)],
        out_specs=[_whole_out(x.shape), _whole_out(sc.shape), _whole_out(st_shape)],
        out_shape=[jax.ShapeDtypeStruct(x.shape, F32), jax.ShapeDtypeStruct(sc.shape, F32),
                   jax.ShapeDtypeStruct(st_shape, F32)],
        compiler_params=_cparams(1),
        name="sample_odd",
    )(x, gm, wi, ccw, sc, wo, st, g, wu, cw, cb, wd)


def kernel(x_prompt, x_sample, cache_k, cache_v, state_conv_a, state_conv_c, state_ffn, page_table, rel_bias,
           norm_mix_e, w_in_e, conv_a_w, conv_a_b, ln_a_g, ln_a_b, q_norm_g, k_norm_g, w_out_e,
           norm_mix_o, w_in_o, conv_c_w, w_out_o, norm_ffn, w_up, conv_f_w, conv_f_b, w_down):
    b, t, d = x_prompt.shape
    m = x_sample.shape[0]
    n_pool, page = cache_k.shape[1], cache_k.shape[2]
    n_pages = page_table.shape[1]
    assert norm_mix_e.shape[0] == 1 and norm_mix_o.shape[0] == 1 and norm_ffn.shape[0] == 2
    assert x_sample.shape[1] == 1 and t % TILE_M == 0 and t % FFN_TILE_M == 0 and t % page == 0
    assert (n_pages * page) % MOBA_BLOCK == 0 and MOBA_BLOCK % page == 0

    row = lambda v: v.reshape(1, -1)
    w_in_e_b = w_in_e[0].astype(BF16)
    w_out_e_b = w_out_e[0].astype(BF16)
    w_in_o_b = w_in_o[0].astype(BF16)
    w_out_o_b = w_out_o[0].astype(BF16)
    w_up_b = w_up.astype(BF16)
    w_down_b = w_down.astype(BF16)
    qg = row(jnp.tile(q_norm_g[0], N_HEADS))
    kg = row(jnp.tile(k_norm_g[0], N_HEADS))
    lane_head = jnp.arange(D_B, dtype=jnp.int32) // HEAD_DIM
    head_ones = (lane_head[:, None] == lane_head[None, :]).astype(BF16)
    g_e, g_o = row(norm_mix_e[0]), row(norm_mix_o[0])
    cab, lag, lab = row(conv_a_b[0]), row(ln_a_g[0]), row(ln_a_b[0])

    a_p, k_p, v_p, qt, kh, vt, km = _prompt_inproj(x_prompt, g_e, w_in_e_b, qg, kg, head_ones, page)
    nb = t // MOBA_BLOCK
    kmh = km.reshape(b, nb, N_HEADS, HEAD_DIM).transpose(0, 2, 1, 3)
    ot = _moba_prompt(rel_bias, _logit_bound(rel_bias, q_norm_g[0], k_norm_g[0]), qt, kh, vt, kmh)
    x1 = _prompt_mix_out(a_p, conv_a_w[0], cab, lag, lab, ot, x_prompt, w_out_e_b)
    x2, f0 = _prompt_ffn(x1, row(norm_ffn[0]), w_up_b, conv_f_w[0], row(conv_f_b[0]), w_down_b, 0)
    x3, c_st = _prompt_mixc(x2, g_o, w_in_o_b, conv_c_w[0], w_out_o_b)
    y_prompt, f1 = _prompt_ffn(x3, row(norm_ffn[1]), w_up_b, conv_f_w[1], row(conv_f_b[1]), w_down_b, 1)

    k_prompt = k_p.transpose(0, 1, 4, 2, 3)[None]
    v_prompt = v_p.transpose(0, 1, 4, 2, 3)[None]
    a_prompt = a_p[:, t - (CONV_A - 1):, :][None]
    c_prompt = c_st[:, SUBLANES - (CONV_C - 1):, :][None]
    f_prompt = jnp.stack([f0[:, SUBLANES - (CONV_F - 1):, :], f1[:, SUBLANES - (CONV_F - 1):, :]])

    xs = x_sample.reshape(m, d)
    sa = state_conv_a[0].transpose(1, 0, 2)
    sc = state_conv_c[0]
    sf = state_ffn
    ao_s, sa_new, q_s, k_s, v_s = _sample_even(xs, g_e, w_in_e_b, qg, kg, head_ones, sa, conv_a_w[0], cab, lag, lab)
    o_s = _moba_sample(page_table, q_s, k_s, v_s, rel_bias.T,
                       cache_k[0].transpose(0, 2, 3, 1), cache_v[0].transpose(0, 2, 3, 1))
    xs1, sf0 = _sample_out_ffn(ao_s, o_s.reshape(m, D_B), xs, w_out_e_b, sf, row(norm_ffn[0]), w_up_b,
                               conv_f_w[0], row(conv_f_b[0]), w_down_b, 0)
    ys, sc_new, sf1 = _sample_odd(xs1, g_o, w_in_o_b, conv_c_w[0], sc, w_out_o_b, sf, row(norm_ffn[1]),
                                  w_up_b, conv_f_w[1], row(conv_f_b[1]), w_down_b, 1)

    y_sample = ys.reshape(m, 1, d)
    k_sample = k_s.reshape(1, m, 1, N_HEADS, HEAD_DIM)
    v_sample = v_s.reshape(1, m, 1, N_HEADS, HEAD_DIM)
    a_sample = sa_new.transpose(1, 0, 2)[None]
    c_sample = sc_new[None]
    f_sample = jnp.stack([sf0, sf1])
    return (y_prompt, y_sample, k_prompt, v_prompt, a_prompt, c_prompt, f_prompt,
            k_sample, v_sample, a_sample, c_sample, f_sample)
```

```python
import functools
import math

import jax
import jax.numpy as jnp
from jax import lax
from jax.experimental import pallas as pl
from jax.experimental.pallas import tpu as pltpu

F32 = jnp.float32
BF16 = jnp.bfloat16

EPS = 1e-6
D_MODEL = 1024
D_A = 512
CONV_A = 31
N_HEADS = 8
HEAD_DIM = 64
D_B = N_HEADS * HEAD_DIM
MOBA_BLOCK = 256
MOBA_TOPK = 3
NUM_BUCKETS = 32
MAX_DISTANCE = 128
D_C = 1024
CONV_C = 3
D_FF = 2816
CONV_F = 3
LOG2E = math.log2(math.e)
Q_SCALE = HEAD_DIM ** -0.5 * LOG2E
VT_ROWS = HEAD_DIM + 16
MAX_SHIFT_RANGE = 100.0

TILE_M = 512
FFN_TILE_M = 1024
CONV_ROWS = 64
CONV_LANES = 256
HALO_A = 32
HEAD_GROUP = 8
SOFTMAX_ROWS = 64
SEQ_PER_STEP = 4
FAR_UNROLL = 2
BOUNDED_UNROLL = 8
CH = 256
SUBLANES = 8
VMEM_LIMIT = 56 * 1024 * 1024


def _t5_thresholds():
    max_exact = NUM_BUCKETS // 2
    th = list(range(1, max_exact + 1))
    for k in range(1, NUM_BUCKETS - max_exact):
        th.append(math.ceil(max_exact * (MAX_DISTANCE / max_exact) ** (k / (NUM_BUCKETS - max_exact))))
    return tuple(th)


T5_THRESH = _t5_thresholds()


def _cparams(n_grid):
    return pltpu.CompilerParams(dimension_semantics=("arbitrary",) * n_grid, vmem_limit_bytes=VMEM_LIMIT)


def _const_spec(shape):
    nd = len(shape)
    return pl.BlockSpec(shape, lambda *_: (0,) * nd, pipeline_mode=pl.Buffered(1))


def _layer_spec(shape, layer):
    nd = len(shape)
    return pl.BlockSpec((1,) + tuple(shape), lambda *_: (layer,) + (0,) * nd, pipeline_mode=pl.Buffered(1))


def _rmsnorm(x, g):
    return x * lax.rsqrt(jnp.mean(x * x, axis=-1, keepdims=True) + EPS) * g


def _split_dot(x, w_bf16):
    hi = x.astype(BF16)
    lo = (x - hi.astype(F32)).astype(BF16)
    return (jnp.dot(hi, w_bf16, preferred_element_type=F32)
            + jnp.dot(lo, w_bf16, preferred_element_type=F32))


def _head_rmsnorm(x, g, head_ones):
    ss = _split_dot(x * x, head_ones)
    return x * lax.rsqrt(ss * (1.0 / HEAD_DIM) + EPS) * g


def _silu(x):
    return x * jax.nn.sigmoid(x)


def _layernorm(x, g, b):
    mu = jnp.mean(x, axis=-1, keepdims=True)
    xc = x - mu
    var = jnp.mean(xc * xc, axis=-1, keepdims=True)
    return xc * lax.rsqrt(var + EPS) * g + b


def _in_proj_even(x, g, w, qg, kg, head_ones):
    h = _rmsnorm(x, g).astype(BF16)
    z = jnp.dot(h, w, preferred_element_type=F32)
    a = z[:, :D_A] * jax.nn.sigmoid(z[:, D_A:2 * D_A])
    q = _head_rmsnorm(z[:, 2 * D_A:2 * D_A + D_B], qg, head_ones) * Q_SCALE
    k = _head_rmsnorm(z[:, 2 * D_A + D_B:2 * D_A + 2 * D_B], kg, head_ones)
    v = z[:, 2 * D_A + 2 * D_B:]
    return a, q, k, v


def _prompt_inproj_body(x_ref, g_ref, w_ref, qg_ref, kg_ref, ho_ref,
                        a_ref, kp_ref, vp_ref, qt_ref, kh_ref, vt_ref, km_ref, *, tm, page):
    a, q, k, v = _in_proj_even(x_ref[0], g_ref[...], w_ref[...], qg_ref[...], kg_ref[...], ho_ref[...])
    a_ref[0] = a
    qt = q.T
    kt = k.T
    vt = v.T
    for pg in range(tm // page):
        kp_ref[0, pg] = kt[:, pg * page:(pg + 1) * page].reshape(N_HEADS, HEAD_DIM, page)
        vp_ref[0, pg] = vt[:, pg * page:(pg + 1) * page].reshape(N_HEADS, HEAD_DIM, page)
    kb = k.astype(BF16)
    pad_row = lax.broadcasted_iota(jnp.int32, (N_HEADS, VT_ROWS - HEAD_DIM, MOBA_BLOCK), 1)
    ones_rows = jnp.where(pad_row == 0, 1.0, 0.0).astype(BF16)
    for i in range(tm // MOBA_BLOCK):
        r0 = i * MOBA_BLOCK
        qt_ref[0, :, i] = qt[:, r0:r0 + MOBA_BLOCK].reshape(N_HEADS, HEAD_DIM, MOBA_BLOCK).astype(BF16)
        vt_ref[0, :, i, 0:HEAD_DIM, :] = (
            vt[:, r0:r0 + MOBA_BLOCK].reshape(N_HEADS, HEAD_DIM, MOBA_BLOCK).astype(BF16))
        vt_ref[0, :, i, HEAD_DIM:VT_ROWS, :] = ones_rows
        for hh in range(N_HEADS):
            kh_ref[0, hh, i] = kb[r0:r0 + MOBA_BLOCK, hh * HEAD_DIM:(hh + 1) * HEAD_DIM]
        km_ref[0, i] = jnp.mean(k[r0:r0 + MOBA_BLOCK], axis=0, keepdims=True)


def _prompt_inproj(x, g, w, qg, kg, head_ones, page):
    b, t, d = x.shape
    tm = TILE_M
    nt = t // tm
    nb = t // MOBA_BLOCK
    bpt = tm // MOBA_BLOCK
    ppt = tm // page
    n_out = w.shape[1]
    tok = lambda bi, ti: (bi, ti, 0)
    blk5 = lambda bi, ti: (bi, 0, ti, 0, 0)
    pages = lambda bi, ti: (bi, ti, 0, 0, 0)
    return pl.pallas_call(
        functools.partial(_prompt_inproj_body, tm=tm, page=page),
        grid=(b, nt),
        in_specs=[
            pl.BlockSpec((1, tm, d), tok),
            _const_spec((1, d)),
            _const_spec((d, n_out)),
            _const_spec((1, D_B)),
            _const_spec((1, D_B)),
            _const_spec((D_B, D_B)),
        ],
        out_specs=[
            pl.BlockSpec((1, tm, D_A), tok),
            pl.BlockSpec((1, ppt, N_HEADS, HEAD_DIM, page), pages),
            pl.BlockSpec((1, ppt, N_HEADS, HEAD_DIM, page), pages),
            pl.BlockSpec((1, N_HEADS, bpt, HEAD_DIM, MOBA_BLOCK), blk5),
            pl.BlockSpec((1, N_HEADS, bpt, MOBA_BLOCK, HEAD_DIM), blk5),
            pl.BlockSpec((1, N_HEADS, bpt, VT_ROWS, MOBA_BLOCK), blk5),
            pl.BlockSpec((1, bpt, 1, D_B), lambda bi, ti: (bi, ti, 0, 0)),
        ],
        out_shape=[
            jax.ShapeDtypeStruct((b, t, D_A), F32),
            jax.ShapeDtypeStruct((b, t // page, N_HEADS, HEAD_DIM, page), F32),
            jax.ShapeDtypeStruct((b, t // page, N_HEADS, HEAD_DIM, page), F32),
            jax.ShapeDtypeStruct((b, N_HEADS, nb, HEAD_DIM, MOBA_BLOCK), BF16),
            jax.ShapeDtypeStruct((b, N_HEADS, nb, MOBA_BLOCK, HEAD_DIM), BF16),
            jax.ShapeDtypeStruct((b, N_HEADS, nb, VT_ROWS, MOBA_BLOCK), BF16),
            jax.ShapeDtypeStruct((b, nb, 1, D_B), F32),
        ],
        compiler_params=_cparams(2),
        name="prompt_inproj",
    )(x, g, w, qg, kg, head_ones)


def _t5_bias_scalar_table(dist, rb_ref, h):
    out = jnp.full(dist.shape, rb_ref[0, h] * LOG2E, F32)
    for i, th in enumerate(T5_THRESH):
        out = jnp.where(dist >= th, rb_ref[i + 1, h] * LOG2E, out)
    return out


def _moba_prompt_body(rb_ref, ctl_ref, qt_ref, kh_ref, vt_ref, km_ref, ot_ref,
                      bdiag_ref, bsub_ref, selb_ref, m_ref, alpha_ref, acc_ref, s_ref, p_ref, *, nb, hg):
    h0 = pl.program_id(1) * hg
    n = pl.program_id(2)
    blk = MOBA_BLOCK
    bounded = ctl_ref[1, 0] > 0.5
    shifts = [jnp.where(bounded, ctl_ref[0, h0 + g], 0.0) for g in range(hg)]

    @pl.when(n == 0)
    def _():
        ki = lax.broadcasted_iota(jnp.int32, (blk, blk), 0)
        qi = lax.broadcasted_iota(jnp.int32, (blk, blk), 1)
        d0 = qi - ki
        for g in range(hg):
            bdiag_ref[g] = jnp.where(d0 >= 0, _t5_bias_scalar_table(jnp.maximum(d0, 0), rb_ref, h0 + g),
                                     -jnp.inf) - shifts[g]
            bsub_ref[g] = _t5_bias_scalar_table(d0 + blk, rb_ref, h0 + g) - shifts[g]

    bi = lax.broadcasted_iota(jnp.int32, (nb, blk), 0)
    qts = [qt_ref[0, g, 0] for g in range(hg)]

    def select(g):
        km = km_ref[0, g]
        km_hi = km.astype(BF16)
        km_lo = (km - km_hi.astype(F32)).astype(BF16)
        gate = (jnp.dot(km_hi, qts[g], preferred_element_type=F32)
                + jnp.dot(km_lo, qts[g], preferred_element_type=F32))
        avail = jnp.where(bi < n, 1.0, 0.0)
        far_bias = rb_ref[NUM_BUCKETS - 1, h0 + g] * LOG2E - shifts[g]
        selb = jnp.full((nb, blk), -jnp.inf, F32)
        for _ in range(MOBA_TOPK):
            gm = jnp.where(avail > 0.0, gate, -jnp.inf)
            top = jnp.max(gm, axis=0, keepdims=True)
            first = jnp.where(avail > 0.0, jnp.where(gm == top, bi, nb), nb)
            pick = bi == jnp.min(first, axis=0, keepdims=True)
            selb = jnp.where(pick, jnp.where(bi == n - 1, 0.0, far_bias), selb)
            avail = jnp.where(pick, 0.0, avail)
        selb_ref[g] = selb

    n_chunks = blk // SOFTMAX_ROWS

    def logits(g, j, slot, bias):
        s = jnp.dot(kh_ref[0, g, j], qts[g], preferred_element_type=F32)
        s_ref[slot, g] = s if bias is None else s + bias

    def chunk(g, slot, c):
        return s_ref[slot, g, c * SOFTMAX_ROWS:(c + 1) * SOFTMAX_ROWS, :].reshape(
            SOFTMAX_ROWS // SUBLANES, SUBLANES, blk)

    def softmax(g, slot, row):
        m8 = jnp.max(chunk(g, slot, 0), axis=0)
        for c in range(1, n_chunks):
            m8 = jnp.maximum(m8, jnp.max(chunk(g, slot, c), axis=0))
        m_blk = jnp.max(m8, axis=0, keepdims=True)
        if row is None:
            mn = m_blk
            shift = mn
        else:
            m_old = m_ref[g]
            mn = jnp.maximum(m_old, m_blk + row)
            alpha_ref[slot, g] = jnp.exp2(m_old - mn)
            shift = mn - row
        for c in range(n_chunks):
            p = jnp.exp2(chunk(g, slot, c) - shift)
            p_ref[slot, g, c * SOFTMAX_ROWS:(c + 1) * SOFTMAX_ROWS, :] = (
                p.reshape(SOFTMAX_ROWS, blk).astype(BF16))
        m_ref[g] = mn

    def weighted_values(g, j, slot, first):
        pv = jnp.dot(vt_ref[0, g, j], p_ref[slot, g], preferred_element_type=F32)
        acc_ref[g] = pv if first else alpha_ref[slot, g] * acc_ref[g] + pv

    def far_trips(trip, unroll):
        def far(i, carry):
            for r in range(unroll):
                trip(unroll * i + r, r & 1)
            return carry

        n_far = jnp.maximum(n - 1, 0)
        lax.fori_loop(0, n_far // unroll, far, 0)
        done = (n_far // unroll) * unroll
        piece = unroll // 2
        while piece >= 1:
            @pl.when(((n_far - done) & piece) != 0)
            def _(piece=piece, done=done):
                for r in range(piece):
                    trip(done + r, r & 1)
            done = done + ((n_far - done) & piece)
            piece //= 2

    last = jnp.where(n >= 2, n - 2, n - 1)

    @pl.when(bounded)
    def _():
        def probabilities(g, j, slot, bias):
            s = jnp.dot(kh_ref[0, g, j], qts[g], preferred_element_type=F32)
            p_ref[slot, g] = jnp.exp2(s + bias).astype(BF16)

        def add_values(g, j, slot, first):
            pv = jnp.dot(vt_ref[0, g, j], p_ref[slot, g], preferred_element_type=F32)
            acc_ref[g] = pv if first else acc_ref[g] + pv

        before = jnp.maximum(n - 1, 0)
        for g in range(hg):
            probabilities(g, n, 0, bdiag_ref[g])
            select(g)
        for g in range(hg):
            probabilities(g, before, 1, bsub_ref[g] + selb_ref[g, pl.ds(before, 1), :])
            add_values(g, n, 0, True)

        def trip(j, slot):
            prev = jnp.where(j == 0, before, j - 1)
            for g in range(hg):
                probabilities(g, j, slot, selb_ref[g, pl.ds(j, 1), :])
                add_values(g, prev, 1 - slot, False)

        far_trips(trip, BOUNDED_UNROLL)
        last_block = jnp.where(n >= 2, n - 2, before)
        last_slot = jnp.where(n >= 2, n & 1, 1)
        for g in range(hg):
            add_values(g, last_block, last_slot, False)

    @pl.when(jnp.logical_not(bounded))
    def _():
        for g in range(hg):
            logits(g, n, 0, bdiag_ref[g])
        for g in range(hg):
            select(g)
        for g in range(hg):
            logits(g, jnp.maximum(n - 1, 0), 1, bsub_ref[g])
        for g in range(hg):
            softmax(g, 0, None)

        @pl.when(n == 0)
        def _():
            for g in range(hg):
                weighted_values(g, n, 0, True)

        @pl.when(n >= 1)
        def _():
            for g in range(hg):
                weighted_values(g, n, 0, True)
                softmax(g, 1, selb_ref[g, pl.ds(n - 1, 1), :])
                logits(g, 0, 0, None)

            def trip(j, slot):
                nxt = jnp.minimum(j + 1, n - 2)
                prev = jnp.where(j == 0, n - 1, j - 1)
                for g in range(hg):
                    logits(g, nxt, 1 - slot, None)
                for g in range(hg):
                    weighted_values(g, prev, 1 - slot, False)
                for g in range(hg):
                    softmax(g, slot, selb_ref[g, pl.ds(j, 1), :])

            far_trips(trip, FAR_UNROLL)
            for g in range(hg):
                weighted_values(g, last, n & 1, False)

    for g in range(hg):
        ot_ref[0, g * HEAD_DIM:(g + 1) * HEAD_DIM, :] = (
            acc_ref[g, 0:HEAD_DIM, :] / acc_ref[g, HEAD_DIM:HEAD_DIM + 1, :])


def _logit_bound(rel_bias, q_gain, k_gain):
    qk = 1.02 * HEAD_DIM * Q_SCALE * jnp.max(jnp.abs(q_gain)) * jnp.max(jnp.abs(k_gain))
    table = rel_bias * LOG2E
    top, low = jnp.max(table, axis=0), jnp.min(table, axis=0)
    usable = jnp.all(2.0 * qk + (top - low) <= MAX_SHIFT_RANGE)
    return jnp.stack([qk + top, jnp.broadcast_to(usable.astype(F32), top.shape)])


def _moba_prompt(rel_bias, ctl, qt, kh, vt, kmh):
    b, nh, nb, hd, blk = qt.shape
    t = nb * blk
    hg = HEAD_GROUP
    grp = lambda bi, hi, ni: (bi, hi, 0, 0, 0)
    return pl.pallas_call(
        functools.partial(_moba_prompt_body, nb=nb, hg=hg),
        grid=(b, nh // hg, nb),
        in_specs=[
            pl.BlockSpec(memory_space=pltpu.SMEM),
            pl.BlockSpec(memory_space=pltpu.SMEM),
            pl.BlockSpec((1, hg, 1, hd, blk), lambda bi, hi, ni: (bi, hi, ni, 0, 0)),
            pl.BlockSpec((1, hg, nb, blk, hd), grp, pipeline_mode=pl.Buffered(1)),
            pl.BlockSpec((1, hg, nb, VT_ROWS, blk), grp, pipeline_mode=pl.Buffered(1)),
            pl.BlockSpec((1, hg, nb, hd), lambda bi, hi, ni: (bi, hi, 0, 0)),
        ],
        out_specs=pl.BlockSpec((1, hg * hd, blk), lambda bi, hi, ni: (bi, hi, ni)),
        out_shape=jax.ShapeDtypeStruct((b, nh * hd, t), F32),
        scratch_shapes=[
            pltpu.VMEM((hg, blk, blk), F32),
            pltpu.VMEM((hg, blk, blk), F32),
            pltpu.VMEM((hg, nb, blk), F32),
            pltpu.VMEM((hg, 1, blk), F32),
            pltpu.VMEM((2, hg, 1, blk), F32),
            pltpu.VMEM((hg, VT_ROWS, blk), F32),
            pltpu.VMEM((2, hg, blk, blk), F32),
            pltpu.VMEM((2, hg, blk, blk), BF16),
        ],
        compiler_params=_cparams(3),
        name="prompt_moba",
    )(rel_bias, ctl, qt, kh, vt, kmh)


def _prompt_mix_out_body(a_ref, halo_ref, cw_ref, cb_ref, lg_ref, lb_ref, ot_ref, x_ref, w_ref,
                         y_ref, sh_ref, conv_ref, wb_ref, *, tm):
    t = pl.program_id(1)
    rows = tm + HALO_A

    @pl.when((pl.program_id(0) == 0) & (t == 0))
    def _():
        for j in range(CONV_A):
            wb_ref[j] = jnp.broadcast_to(cw_ref[j:j + 1, :], (SUBLANES, D_A))

    sh_ref[0, 0:HALO_A] = jnp.where(t == 0, 0.0, halo_ref[0])
    sh_ref[0, HALO_A:rows] = a_ref[0]
    for s in range(1, SUBLANES):
        sh_ref[s, 0:rows - SUBLANES] = sh_ref[0, s:s + rows - SUBLANES]

    groups = CONV_ROWS // SUBLANES
    first_off = HALO_A - (CONV_A - 1)

    def chunk(c, carry):
        r0 = pl.multiple_of(c * CONV_ROWS, CONV_ROWS)
        for l0 in range(0, D_A, CONV_LANES):
            lanes = slice(l0, l0 + CONV_LANES)
            acc = [jnp.zeros((SUBLANES, CONV_LANES), F32) for _ in range(groups)]
            for s in range(SUBLANES):
                offs = [o for o in range(first_off, first_off + CONV_A) if o % SUBLANES == s]
                qs = [o // SUBLANES for o in offs]
                slab = {gq: sh_ref[s, pl.ds(r0 + gq * SUBLANES, SUBLANES), lanes]
                        for gq in range(min(qs), max(qs) + groups)}
                for o, q in zip(offs, qs):
                    w = wb_ref[o - first_off, :, lanes]
                    for g in range(groups):
                        acc[g] = acc[g] + w * slab[g + q]
            for g in range(groups):
                conv_ref[pl.ds(r0 + g * SUBLANES, SUBLANES), lanes] = acc[g]
        return carry

    lax.fori_loop(0, tm // CONV_ROWS, chunk, 0)

    ao = _silu(_layernorm(conv_ref[...] + cb_ref[...], lg_ref[...], lb_ref[...])).astype(BF16)
    o = ot_ref[0].T.astype(BF16)
    y = (jnp.dot(ao, w_ref[0:D_A, :], preferred_element_type=F32)
         + jnp.dot(o, w_ref[D_A:D_A + D_B, :], preferred_element_type=F32))
    y_ref[0] = x_ref[0] + y


def _prompt_mix_out(a, cw, cb, lg, lb, ot, x, w):
    b, t, d = x.shape
    tm = TILE_M
    nt = t // tm
    hpt = tm // HALO_A
    tok = lambda bi, ti: (bi, ti, 0)
    return pl.pallas_call(
        functools.partial(_prompt_mix_out_body, tm=tm),
        grid=(b, nt),
        in_specs=[
            pl.BlockSpec((1, tm, D_A), tok),
            pl.BlockSpec((1, HALO_A, D_A), lambda bi, ti: (bi, jnp.maximum(ti * hpt - 1, 0), 0)),
            _const_spec((CONV_A, D_A)),
            _const_spec((1, D_A)),
            _const_spec((1, D_A)),
            _const_spec((1, D_A)),
            pl.BlockSpec((1, D_B, tm), lambda bi, ti: (bi, 0, ti)),
            pl.BlockSpec((1, tm, d), tok),
            _const_spec((D_A + D_B, d)),
        ],
        out_specs=pl.BlockSpec((1, tm, d), tok),
        out_shape=jax.ShapeDtypeStruct((b, t, d), F32),
        scratch_shapes=[
            pltpu.VMEM((SUBLANES, tm + HALO_A, D_A), F32),
            pltpu.VMEM((tm, D_A), F32),
            pltpu.VMEM((CONV_A, SUBLANES, D_A), F32),
        ],
        compiler_params=_cparams(2),
        name="prompt_mix_out",
    )(a, a, cw, cb, lg, lb, ot, x, w)


def _conv3_chunk(u, ext_ref, carry_ref, cw_ref, c0, tm):
    lo = SUBLANES
    carry = carry_ref[:, c0:c0 + CH]
    for k in range(1, 3):
        ext_ref[k - 1, lo:lo + k] = carry[SUBLANES - k:SUBLANES]
        ext_ref[k - 1, lo + k:lo + k + tm] = u
    carry_ref[:, c0:c0 + CH] = u[tm - SUBLANES:tm]
    return (cw_ref[0:1, c0:c0 + CH] * ext_ref[1, lo:lo + tm]
            + cw_ref[1:2, c0:c0 + CH] * ext_ref[0, lo:lo + tm]
            + cw_ref[2:3, c0:c0 + CH] * u)


def _prompt_ffn_body(x_ref, g_ref, wu_ref, cw_ref, cb_ref, wd_ref, y_ref, st_ref,
                     carry_ref, extg_ref, extu_ref, act_ref, *, tm):
    @pl.when(pl.program_id(1) == 0)
    def _():
        carry_ref[...] = jnp.zeros_like(carry_ref)

    x = x_ref[0]
    h = _rmsnorm(x, g_ref[...]).astype(BF16)
    for c in range(D_FF // CH):
        cg = c * CH
        cu = D_FF + c * CH
        ug = jnp.dot(h, wu_ref[0, :, cg:cg + CH], preferred_element_type=F32)
        uu = jnp.dot(h, wu_ref[0, :, cu:cu + CH], preferred_element_type=F32)
        gg = _conv3_chunk(ug, extg_ref, carry_ref, cw_ref, cg, tm) + cb_ref[:, cg:cg + CH]
        gu = _conv3_chunk(uu, extu_ref, carry_ref, cw_ref, cu, tm) + cb_ref[:, cu:cu + CH]
        act_ref[:, cg:cg + CH] = (_silu(gg) * gu).astype(BF16)
    st_ref[0] = carry_ref[...]
    y_ref[0] = x + jnp.dot(act_ref[...], wd_ref[0], preferred_element_type=F32)


def _prompt_ffn(x, g, wu, cw, cb, wd, layer):
    b, t, d = x.shape
    tm = FFN_TILE_M
    nt = t // tm
    tok = lambda bi, ti: (bi, ti, 0)
    return pl.pallas_call(
        functools.partial(_prompt_ffn_body, tm=tm),
        grid=(b, nt),
        in_specs=[
            pl.BlockSpec((1, tm, d), tok),
            _const_spec((1, d)),
            _layer_spec((d, 2 * D_FF), layer),
            _const_spec((CONV_F, 2 * D_FF)),
            _const_spec((1, 2 * D_FF)),
            _layer_spec((D_FF, d), layer),
        ],
        out_specs=[
            pl.BlockSpec((1, tm, d), tok),
            pl.BlockSpec((1, SUBLANES, 2 * D_FF), lambda bi, ti: (bi, 0, 0)),
        ],
        out_shape=[
            jax.ShapeDtypeStruct((b, t, d), F32),
            jax.ShapeDtypeStruct((b, SUBLANES, 2 * D_FF), F32),
        ],
        scratch_shapes=[
            pltpu.VMEM((SUBLANES, 2 * D_FF), F32),
            pltpu.VMEM((2, tm + 2 * SUBLANES, CH), F32),
            pltpu.VMEM((2, tm + 2 * SUBLANES, CH), F32),
            pltpu.VMEM((tm, D_FF), BF16),
        ],
        compiler_params=_cparams(2),
        name="prompt_ffn",
    )(x, g, wu, cw, cb, wd)


def _prompt_mixc_body(x_ref, g_ref, wi_ref, cw_ref, wo_ref, y_ref, st_ref,
                      carry_ref, ext_ref, z_ref, *, tm):
    @pl.when(pl.program_id(1) == 0)
    def _():
        carry_ref[...] = jnp.zeros_like(carry_ref)

    x = x_ref[0]
    h = _rmsnorm(x, g_ref[...]).astype(BF16)
    for c in range(D_C // CH):
        c0 = c * CH
        gb = jnp.dot(h, wi_ref[:, c0:c0 + CH], preferred_element_type=F32)
        gc = jnp.dot(h, wi_ref[:, D_C + c0:D_C + c0 + CH], preferred_element_type=F32)
        u = jnp.dot(h, wi_ref[:, 2 * D_C + c0:2 * D_C + c0 + CH], preferred_element_type=F32)
        conv = _conv3_chunk(gc * u, ext_ref, carry_ref, cw_ref, c0, tm)
        z_ref[:, c0:c0 + CH] = (gb * conv).astype(BF16)
    st_ref[0] = carry_ref[...]
    y_ref[0] = x + jnp.dot(z_ref[...], wo_ref[...], preferred_element_type=F32)


def _prompt_mixc(x, g, wi, cw, wo):
    b, t, d = x.shape
    tm = TILE_M
    nt = t // tm
    tok = lambda bi, ti: (bi, ti, 0)
    return pl.pallas_call(
        functools.partial(_prompt_mixc_body, tm=tm),
        grid=(b, nt),
        in_specs=[
            pl.BlockSpec((1, tm, d), tok),
            _const_spec((1, d)),
            _const_spec((d, 3 * D_C)),
            _const_spec((CONV_C, D_C)),
            _const_spec((D_C, d)),
        ],
        out_specs=[
            pl.BlockSpec((1, tm, d), tok),
            pl.BlockSpec((1, SUBLANES, D_C), lambda bi, ti: (bi, 0, 0)),
        ],
        out_shape=[
            jax.ShapeDtypeStruct((b, t, d), F32),
            jax.ShapeDtypeStruct((b, SUBLANES, D_C), F32),
        ],
        scratch_shapes=[
            pltpu.VMEM((SUBLANES, D_C), F32),
            pltpu.VMEM((2, tm + 2 * SUBLANES, CH), F32),
            pltpu.VMEM((tm, D_C), BF16),
        ],
        compiler_params=_cparams(2),
        name="prompt_mixc",
    )(x, g, wi, cw, wo)


def _sample_even_body(x_ref, g_ref, w_ref, qg_ref, kg_ref, ho_ref, sa_ref, cw_ref, cb_ref, lg_ref, lb_ref,
                      ao_ref, san_ref, q_ref, k_ref, v_ref):
    a, q, k, v = _in_proj_even(x_ref[...], g_ref[...], w_ref[...], qg_ref[...], kg_ref[...], ho_ref[...])
    q_ref[...] = q
    k_ref[...] = k
    v_ref[...] = v
    hist = CONV_A - 1
    conv = cb_ref[...] + cw_ref[hist:hist + 1, :] * a
    for j in range(hist):
        conv = conv + cw_ref[j:j + 1, :] * sa_ref[j]
    ao_ref[...] = _silu(_layernorm(conv, lg_ref[...], lb_ref[...])).astype(BF16)
    san_ref[0:hist - 1] = sa_ref[1:hist]
    san_ref[hist - 1] = a


def _sample_even(x, g, w, qg, kg, head_ones, sa, cw, cb, lg, lb):
    m = x.shape[0]
    return pl.pallas_call(
        _sample_even_body,
        out_shape=[
            jax.ShapeDtypeStruct((m, D_A), BF16),
            jax.ShapeDtypeStruct(sa.shape, F32),
            jax.ShapeDtypeStruct((m, D_B), F32),
            jax.ShapeDtypeStruct((m, D_B), F32),
            jax.ShapeDtypeStruct((m, D_B), F32),
        ],
        compiler_params=pltpu.CompilerParams(vmem_limit_bytes=VMEM_LIMIT),
        name="sample_even",
    )(x, g, w, qg, kg, head_ones, sa, cw, cb, lg, lb)


def _moba_sample_keys_body(pt_ref, q_ref, qdh_ref, kn_ref, rbt_ref, *refs, n_pages, page):
    del pt_ref
    outs = refs[SEQ_PER_STEP * n_pages:]
    bias_ref = outs[4]
    past = n_pages * page

    @pl.when(pl.program_id(0) == 0)
    def _():
        pos = lax.broadcasted_iota(jnp.int32, (N_HEADS, past), 1)
        dist = past - pos
        out = jnp.zeros((N_HEADS, past), F32) + rbt_ref[:, 0:1]
        for i, th in enumerate(T5_THRESH):
            out = jnp.where(dist >= th, rbt_ref[:, i + 1:i + 2], out)
        bias_ref[...] = out * LOG2E

    for sq in range(SEQ_PER_STEP):
        _moba_sample_keys_one(sq, q_ref, qdh_ref, kn_ref, rbt_ref, refs[sq * n_pages:(sq + 1) * n_pages], *outs,
                              n_pages=n_pages, page=page)


def _moba_sample_keys_one(sq, q_ref, qdh_ref, kn_ref, rbt_ref, k_pages, pc_ref, pn_ref, l_ref, sel_ref,
                          bias_ref, qb_ref, s_ref, *, n_pages, page):
    past = n_pages * page
    n_past_blocks = past // MOBA_BLOCK
    n_sel = min(MOBA_TOPK, n_past_blocks)
    qb_ref = qb_ref.at[sq]
    s_ref = s_ref.at[sq]

    head_of_lane = lax.broadcasted_iota(jnp.int32, (N_HEADS, D_B), 1) // HEAD_DIM
    hmask = head_of_lane == lax.broadcasted_iota(jnp.int32, (N_HEADS, D_B), 0)
    qbd = jnp.where(hmask, q_ref[sq], 0.0)

    qdh = qdh_ref[sq]
    for h in range(N_HEADS):
        qb_ref[h] = jnp.broadcast_to(qdh[:, h:h + 1], (HEAD_DIM, page))

    for pg in range(n_pages):
        rows = [jnp.sum(k_pages[pg][0, h] * qb_ref[h], axis=0, keepdims=True) for h in range(N_HEADS)]
        s_ref[:, pg * page:(pg + 1) * page] = jnp.concatenate(rows, axis=0)

    gates = [jnp.sum(s_ref[:, j * MOBA_BLOCK:(j + 1) * MOBA_BLOCK], axis=-1, keepdims=True)
             for j in range(n_past_blocks)]
    ranks = []
    pieces = []
    for j in range(n_past_blocks):
        rank = jnp.zeros((N_HEADS, 1), jnp.int32)
        for i in range(n_past_blocks):
            if i == j:
                continue
            ahead = (gates[i] >= gates[j]) if i < j else (gates[i] > gates[j])
            rank = rank + jnp.where(ahead, 1, 0)
        ranks.append(rank)
        lanes = slice(j * MOBA_BLOCK, (j + 1) * MOBA_BLOCK)
        pieces.append(s_ref[:, lanes] + bias_ref[:, lanes] + jnp.where(rank < n_sel, 0.0, -jnp.inf))
    s = jnp.concatenate(pieces, axis=-1)

    s_new = jnp.sum(qbd * kn_ref[sq], axis=-1, keepdims=True) + rbt_ref[:, 0:1] * LOG2E
    m = jnp.maximum(jnp.max(s, axis=-1, keepdims=True), s_new)
    p = jnp.exp2(s - m)
    p_new = jnp.exp2(s_new - m)
    pn_ref[sq] = jnp.broadcast_to(p_new, (N_HEADS, 128))
    l_ref[sq] = jnp.broadcast_to(jnp.sum(p, axis=-1, keepdims=True) + p_new, (N_HEADS, 128))

    lane = lax.broadcasted_iota(jnp.int32, (N_HEADS, 128), 1)
    sel = jnp.zeros((N_HEADS, 128), jnp.int32)
    for r in range(n_sel):
        picked = jnp.zeros((N_HEADS, MOBA_BLOCK), F32)
        block_id = jnp.zeros((N_HEADS, 1), jnp.int32)
        for j in range(n_past_blocks):
            picked = picked + jnp.where(ranks[j] == r, p[:, j * MOBA_BLOCK:(j + 1) * MOBA_BLOCK], 0.0)
            block_id = block_id + jnp.where(ranks[j] == r, j, 0)
        pc_ref[sq, :, r * MOBA_BLOCK:(r + 1) * MOBA_BLOCK] = picked
        sel = jnp.where(lane == r, block_id, sel)
    sel_ref[sq] = sel


def _moba_sample_values_body(pt_ref, sel_ref, pc_ref, pn_ref, l_ref, vn_ref, v_hbm, o_ref, buf_ref, sem_ref,
                             *, n_sel, ppb, page):
    b = pl.program_id(0)
    n_tiles = n_sel * ppb

    def tile_copy(seq, slot, h, i):
        block = sel_ref[seq * (N_HEADS * n_sel) + h * n_sel + i // ppb]
        pool_page = pt_ref[seq, block * ppb + i % ppb]
        return pltpu.make_async_copy(v_hbm.at[pool_page, h], buf_ref.at[slot, h * n_tiles + i], sem_ref.at[slot])

    def for_all_tiles(seq, slot, act):
        for h in range(N_HEADS):
            for i in range(n_tiles):
                act(tile_copy(seq, slot, h, i))

    @pl.when(b == 0)
    def _():
        for_all_tiles(0, 0, lambda cp: cp.start())

    @pl.when(b + 1 < pl.num_programs(0))
    def _():
        for_all_tiles(b + 1, (b + 1) % 2, lambda cp: cp.start())

    slot = b % 2
    for_all_tiles(b, slot, lambda cp: cp.wait())

    head_of_lane = lax.broadcasted_iota(jnp.int32, (N_HEADS, D_B), 1) // HEAD_DIM
    hmask = head_of_lane == lax.broadcasted_iota(jnp.int32, (N_HEADS, D_B), 0)
    ones = jnp.ones((SUBLANES, page), BF16)
    lane_sum = lambda x: lax.dot_general(ones, x, (((1,), (1,)), ((), ())), preferred_element_type=F32)
    outs = []
    for h in range(N_HEADS):
        acc = jnp.zeros((HEAD_DIM, page), F32)
        for i in range(n_tiles):
            acc = acc + buf_ref[slot, h * n_tiles + i] * pc_ref[0, h:h + 1, i * page:(i + 1) * page]
        hi = acc.astype(BF16)
        lo = (acc - hi.astype(F32)).astype(BF16)
        outs.append((lane_sum(hi) + lane_sum(lo))[0:1, :])
    o_past = jnp.concatenate(outs, axis=-1)
    spread = lambda col: jnp.sum(jnp.where(hmask, col, 0.0), axis=0, keepdims=True)
    o_ref[0] = (o_past + spread(pn_ref[0, :, 0:1]) * vn_ref[0]) / spread(l_ref[0, :, 0:1])


def _moba_sample(page_table, q, k_new, v_new, rel_bias_t, cache_kt, cache_vt):
    m, n_pages = page_table.shape
    page = cache_kt.shape[-1]
    past = n_pages * page
    n_sel = min(MOBA_TOPK, past // MOBA_BLOCK)
    ppb = MOBA_BLOCK // page
    row = lambda bi, *_: (bi, 0, 0)

    sps = SEQ_PER_STEP
    assert m % sps == 0
    keys_spec = pltpu.PrefetchScalarGridSpec(
        num_scalar_prefetch=1,
        grid=(m // sps,),
        in_specs=[
            pl.BlockSpec((sps, 1, D_B), row),
            pl.BlockSpec((sps, HEAD_DIM, N_HEADS), row),
            pl.BlockSpec((sps, 1, D_B), row),
            pl.BlockSpec((N_HEADS, NUM_BUCKETS), lambda bi, pt: (0, 0)),
        ] + [pl.BlockSpec((1, N_HEADS, HEAD_DIM, page), lambda bi, pt, sq=sq, pg=pg: (pt[bi * sps + sq, pg], 0, 0, 0))
             for sq in range(sps) for pg in range(n_pages)],
        out_specs=[
            pl.BlockSpec((sps, N_HEADS, n_sel * MOBA_BLOCK), row),
            pl.BlockSpec((sps, N_HEADS, 128), row),
            pl.BlockSpec((sps, N_HEADS, 128), row),
            pl.BlockSpec((sps, N_HEADS, 128), row),
        ],
        scratch_shapes=[
            pltpu.VMEM((N_HEADS, past), F32),
            pltpu.VMEM((sps, N_HEADS, HEAD_DIM, page), F32),
            pltpu.VMEM((sps, N_HEADS, past), F32),
        ],
    )
    q_dh = q.reshape(m, N_HEADS, HEAD_DIM).transpose(0, 2, 1)
    pc, pn, l, sel = pl.pallas_call(
        functools.partial(_moba_sample_keys_body, n_pages=n_pages, page=page),
        grid_spec=keys_spec,
        out_shape=[
            jax.ShapeDtypeStruct((m, N_HEADS, n_sel * MOBA_BLOCK), F32),
            jax.ShapeDtypeStruct((m, N_HEADS, 128), F32),
            jax.ShapeDtypeStruct((m, N_HEADS, 128), F32),
            jax.ShapeDtypeStruct((m, N_HEADS, 128), jnp.int32),
        ],
        compiler_params=_cparams(1),
        name="sample_moba_keys",
    )(page_table, q.reshape(m, 1, D_B), q_dh, k_new.reshape(m, 1, D_B), rel_bias_t, *([cache_kt] * (sps * n_pages)))

    sel_flat = sel[:, :, :n_sel].reshape(m * N_HEADS * n_sel)
    values_spec = pltpu.PrefetchScalarGridSpec(
        num_scalar_prefetch=2,
        grid=(m,),
        in_specs=[
            pl.BlockSpec((1, N_HEADS, n_sel * MOBA_BLOCK), row),
            pl.BlockSpec((1, N_HEADS, 128), row),
            pl.BlockSpec((1, N_HEADS, 128), row),
            pl.BlockSpec((1, 1, D_B), row),
            pl.BlockSpec(memory_space=pl.ANY),
        ],
        out_specs=pl.BlockSpec((1, 1, D_B), row),
        scratch_shapes=[
            pltpu.VMEM((2, N_HEADS * n_sel * ppb, HEAD_DIM, page), F32),
            pltpu.SemaphoreType.DMA((2,)),
        ],
    )
    return pl.pallas_call(
        functools.partial(_moba_sample_values_body, n_sel=n_sel, ppb=ppb, page=page),
        grid_spec=values_spec,
        out_shape=jax.ShapeDtypeStruct((m, 1, D_B), F32),
        compiler_params=_cparams(1),
        name="sample_moba_values",
    )(page_table, sel_flat, pc, pn, l, v_new.reshape(m, 1, D_B), cache_vt)


def _sample_ffn(x, st_ref, g_ref, wu_ref, cw_ref, cb_ref, wd_ref, stn_ref):
    h = _rmsnorm(x, g_ref[...]).astype(BF16)
    up = jnp.dot(h, wu_ref[0], preferred_element_type=F32)
    older, newer = st_ref[0, :, 0, :], st_ref[0, :, 1, :]
    conv = cw_ref[0:1, :] * older + cw_ref[1:2, :] * newer + cw_ref[2:3, :] * up + cb_ref[...]
    act = (_silu(conv[:, :D_FF]) * conv[:, D_FF:]).astype(BF16)
    stn_ref[:, 0, :] = newer
    stn_ref[:, 1, :] = up
    return x + jnp.dot(act, wd_ref[0], preferred_element_type=F32)


def _sample_out_ffn_body(ao_ref, o_ref, x_ref, wo_ref, st_ref, g_ref, wu_ref, cw_ref, cb_ref, wd_ref,
                         y_ref, stn_ref):
    x1 = (x_ref[...]
          + jnp.dot(ao_ref[...], wo_ref[0:D_A, :], preferred_element_type=F32)
          + jnp.dot(o_ref[...].astype(BF16), wo_ref[D_A:D_A + D_B, :], preferred_element_type=F32))
    y_ref[...] = _sample_ffn(x1, st_ref, g_ref, wu_ref, cw_ref, cb_ref, wd_ref, stn_ref)


def _whole(a):
    return _const_spec(a.shape)


def _whole_out(shape):
    nd = len(shape)
    return pl.BlockSpec(tuple(shape), lambda *_: (0,) * nd)


def _sample_out_ffn(ao, o, x, wo, st, g, wu, cw, cb, wd, layer):
    st_shape = st.shape[1:]
    return pl.pallas_call(
        _sample_out_ffn_body,
        grid=(1,),
        in_specs=[_whole(ao), _whole(o), _whole(x), _whole(wo), _layer_spec(st_shape, layer), _whole(g),
                  _layer_spec(wu.shape[1:], layer), _whole(cw), _whole(cb), _layer_spec(wd.shape[1:], layer)],
        out_specs=[_whole_out(x.shape), _whole_out(st_shape)],
        out_shape=[jax.ShapeDtypeStruct(x.shape, F32), jax.ShapeDtypeStruct(st_shape, F32)],
        compiler_params=_cparams(1),
        name="sample_out_ffn",
    )(ao, o, x, wo, st, g, wu, cw, cb, wd)


def _sample_odd_body(x_ref, gm_ref, wi_ref, ccw_ref, sc_ref, wo_ref, st_ref, g_ref, wu_ref, cw_ref, cb_ref,
                     wd_ref, y_ref, scn_ref, stn_ref):
    x = x_ref[...]
    h = _rmsnorm(x, gm_ref[...]).astype(BF16)
    z = jnp.dot(h, wi_ref[...], preferred_element_type=F32)
    gcu = z[:, D_C:2 * D_C] * z[:, 2 * D_C:]
    older, newer = sc_ref[:, 0, :], sc_ref[:, 1, :]
    conv = ccw_ref[0:1, :] * older + ccw_ref[1:2, :] * newer + ccw_ref[2:3, :] * gcu
    scn_ref[:, 0, :] = newer
    scn_ref[:, 1, :] = gcu
    x1 = x + jnp.dot((z[:, :D_C] * conv).astype(BF16), wo_ref[...], preferred_element_type=F32)
    y_ref[...] = _sample_ffn(x1, st_ref, g_ref, wu_ref, cw_ref, cb_ref, wd_ref, stn_ref)


def _sample_odd(x, gm, wi, ccw, sc, wo, st, g, wu, cw, cb, wd, layer):
    st_shape = st.shape[1:]
    return pl.pallas_call(
        _sample_odd_body,
        grid=(1,),
        in_specs=[_whole(x), _whole(gm), _whole(wi), _whole(ccw), _whole(sc), _whole(wo),
                  _layer_spec(st_shape, layer), _whole(g), _layer_spec(wu.shape[1:], layer), _whole(cw), _whole(cb),
                  _layer_spec(wd.shape[1:], layer)],
        out_specs=[_whole_out(x.shape), _whole_out(sc.shape), _whole_out(st_shape)],
        out_shape=[jax.ShapeDtypeStruct(x.shape, F32), jax.ShapeDtypeStruct(sc.shape, F32),
                   jax.ShapeDtypeStruct(st_shape, F32)],
        compiler_params=_cparams(1),
        name="sample_odd",
    )(x, gm, wi, ccw, sc, wo, st, g, wu, cw, cb, wd)


def kernel(x_prompt, x_sample, cache_k, cache_v, state_conv_a, state_conv_c, state_ffn, page_table, rel_bias,
           norm_mix_e, w_in_e, conv_a_w, conv_a_b, ln_a_g, ln_a_b, q_norm_g, k_norm_g, w_out_e,
           norm_mix_o, w_in_o, conv_c_w, w_out_o, norm_ffn, w_up, conv_f_w, conv_f_b, w_down):
    b, t, d = x_prompt.shape
    m = x_sample.shape[0]
    n_pool, page = cache_k.shape[1], cache_k.shape[2]
    n_pages = page_table.shape[1]
    assert norm_mix_e.shape[0] == 1 and norm_mix_o.shape[0] == 1 and norm_ffn.shape[0] == 2
    assert x_sample.shape[1] == 1 and t % TILE_M == 0 and t % FFN_TILE_M == 0 and t % page == 0
    assert (n_pages * page) % MOBA_BLOCK == 0 and MOBA_BLOCK % page == 0

    row = lambda v: v.reshape(1, -1)
    w_in_e_b = w_in_e[0].astype(BF16)
    w_out_e_b = w_out_e[0].astype(BF16)
    w_in_o_b = w_in_o[0].astype(BF16)
    w_out_o_b = w_out_o[0].astype(BF16)
    w_up_b = w_up.astype(BF16)
    w_down_b = w_down.astype(BF16)
    qg = row(jnp.tile(q_norm_g[0], N_HEADS))
    kg = row(jnp.tile(k_norm_g[0], N_HEADS))
    lane_head = jnp.arange(D_B, dtype=jnp.int32) // HEAD_DIM
    head_ones = (lane_head[:, None] == lane_head[None, :]).astype(BF16)
    g_e, g_o = row(norm_mix_e[0]), row(norm_mix_o[0])
    cab, lag, lab = row(conv_a_b[0]), row(ln_a_g[0]), row(ln_a_b[0])

    a_p, k_p, v_p, qt, kh, vt, km = _prompt_inproj(x_prompt, g_e, w_in_e_b, qg, kg, head_ones, page)
    nb = t // MOBA_BLOCK
    kmh = km.reshape(b, nb, N_HEADS, HEAD_DIM).transpose(0, 2, 1, 3)
    ot = _moba_prompt(rel_bias, _logit_bound(rel_bias, q_norm_g[0], k_norm_g[0]), qt, kh, vt, kmh)
    x1 = _prompt_mix_out(a_p, conv_a_w[0], cab, lag, lab, ot, x_prompt, w_out_e_b)
    x2, f0 = _prompt_ffn(x1, row(norm_ffn[0]), w_up_b, conv_f_w[0], row(conv_f_b[0]), w_down_b, 0)
    x3, c_st = _prompt_mixc(x2, g_o, w_in_o_b, conv_c_w[0], w_out_o_b)
    y_prompt, f1 = _prompt_ffn(x3, row(norm_ffn[1]), w_up_b, conv_f_w[1], row(conv_f_b[1]), w_down_b, 1)

    k_prompt = k_p.transpose(0, 1, 4, 2, 3)[None]
    v_prompt = v_p.transpose(0, 1, 4, 2, 3)[None]
    a_prompt = a_p[:, t - (CONV_A - 1):, :][None]
    c_prompt = c_st[:, SUBLANES - (CONV_C - 1):, :][None]
    f_prompt = jnp.stack([f0[:, SUBLANES - (CONV_F - 1):, :], f1[:, SUBLANES - (CONV_F - 1):, :]])

    xs = x_sample.reshape(m, d)
    sa = state_conv_a[0].transpose(1, 0, 2)
    sc = state_conv_c[0]
    sf = state_ffn
    ao_s, sa_new, q_s, k_s, v_s = _sample_even(xs, g_e, w_in_e_b, qg, kg, head_ones, sa, conv_a_w[0], cab, lag, lab)
    o_s = _moba_sample(page_table, q_s, k_s, v_s, rel_bias.T,
                       cache_k[0].transpose(0, 2, 3, 1), cache_v[0].transpose(0, 2, 3, 1))
    xs1, sf0 = _sample_out_ffn(ao_s, o_s.reshape(m, D_B), xs, w_out_e_b, sf, row(norm_ffn[0]), w_up_b,
                               conv_f_w[0], row(conv_f_b[0]), w_down_b, 0)
    ys, sc_new, sf1 = _sample_odd(xs1, g_o, w_in_o_b, conv_c_w[0], sc, w_out_o_b, sf, row(norm_ffn[1]),
                                  w_up_b, conv_f_w[1], row(conv_f_b[1]), w_down_b, 1)

    y_sample = ys.reshape(m, 1, d)
    k_sample = k_s.reshape(1, m, 1, N_HEADS, HEAD_DIM)
    v_sample = v_s.reshape(1, m, 1, N_HEADS, HEAD_DIM)
    a_sample = sa_new.transpose(1, 0, 2)[None]
    c_sample = sc_new[None]
    f_sample = jnp.stack([sf0, sf1])
    return (y_prompt, y_sample, k_prompt, v_prompt, a_prompt, c_prompt, f_prompt,
            k_sample, v_sample, a_sample, c_sample, f_sample)
```

```python
import functools
import math

import jax
import jax.numpy as jnp
from jax import lax
from jax.experimental import pallas as pl
from jax.experimental.pallas import tpu as pltpu

F32 = jnp.float32
BF16 = jnp.bfloat16

EPS = 1e-6
D_MODEL = 1024
D_A = 512
CONV_A = 31
N_HEADS = 8
HEAD_DIM = 64
D_B = N_HEADS * HEAD_DIM
MOBA_BLOCK = 256
MOBA_TOPK = 3
NUM_BUCKETS = 32
MAX_DISTANCE = 128
D_C = 1024
CONV_C = 3
D_FF = 2816
CONV_F = 3
LOG2E = math.log2(math.e)
Q_SCALE = HEAD_DIM ** -0.5 * LOG2E
VT_ROWS = HEAD_DIM + 16
MAX_SHIFT_RANGE = 100.0

TILE_M = 512
FFN_TILE_M = 1024
CONV_ROWS = 64
CONV_LANES = 256
HALO_A = 32
HEAD_GROUP = 8
SOFTMAX_ROWS = 64
SEQ_PER_STEP = 4
FAR_UNROLL = 2
BOUNDED_UNROLL = 8
CH = 256
SUBLANES = 8
VMEM_LIMIT = 56 * 1024 * 1024


def _t5_thresholds():
    max_exact = NUM_BUCKETS // 2
    th = list(range(1, max_exact + 1))
    for k in range(1, NUM_BUCKETS - max_exact):
        th.append(math.ceil(max_exact * (MAX_DISTANCE / max_exact) ** (k / (NUM_BUCKETS - max_exact))))
    return tuple(th)


T5_THRESH = _t5_thresholds()


def _cparams(n_grid):
    return pltpu.CompilerParams(dimension_semantics=("arbitrary",) * n_grid, vmem_limit_bytes=VMEM_LIMIT)


def _const_spec(shape):
    nd = len(shape)
    return pl.BlockSpec(shape, lambda *_: (0,) * nd, pipeline_mode=pl.Buffered(1))


def _layer_spec(shape, layer):
    nd = len(shape)
    return pl.BlockSpec((1,) + tuple(shape), lambda *_: (layer,) + (0,) * nd, pipeline_mode=pl.Buffered(1))


def _rmsnorm(x, g):
    return x * lax.rsqrt(jnp.mean(x * x, axis=-1, keepdims=True) + EPS) * g


def _split_dot(x, w_bf16):
    hi = x.astype(BF16)
    lo = (x - hi.astype(F32)).astype(BF16)
    return (jnp.dot(hi, w_bf16, preferred_element_type=F32)
            + jnp.dot(lo, w_bf16, preferred_element_type=F32))


def _head_rmsnorm(x, g, head_ones):
    ss = _split_dot(x * x, head_ones)
    return x * lax.rsqrt(ss * (1.0 / HEAD_DIM) + EPS) * g


def _silu(x):
    return x * jax.nn.sigmoid(x)


def _layernorm(x, g, b):
    mu = jnp.mean(x, axis=-1, keepdims=True)
    xc = x - mu
    var = jnp.mean(xc * xc, axis=-1, keepdims=True)
    return xc * lax.rsqrt(var + EPS) * g + b


def _in_proj_even(x, g, w, qg, kg, head_ones):
    h = _rmsnorm(x, g).astype(BF16)
    z = jnp.dot(h, w, preferred_element_type=F32)
    a = z[:, :D_A] * jax.nn.sigmoid(z[:, D_A:2 * D_A])
    q = _head_rmsnorm(z[:, 2 * D_A:2 * D_A + D_B], qg, head_ones) * Q_SCALE
    k = _head_rmsnorm(z[:, 2 * D_A + D_B:2 * D_A + 2 * D_B], kg, head_ones)
    v = z[:, 2 * D_A + 2 * D_B:]
    return a, q, k, v


def _prompt_inproj_body(x_ref, g_ref, w_ref, qg_ref, kg_ref, ho_ref,
                        a_ref, kp_ref, vp_ref, qt_ref, kh_ref, vt_ref, km_ref, *, tm, page):
    a, q, k, v = _in_proj_even(x_ref[0], g_ref[...], w_ref[...], qg_ref[...], kg_ref[...], ho_ref[...])
    a_ref[0] = a
    qt = q.T
    kt = k.T
    vt = v.T
    for pg in range(tm // page):
        kp_ref[0, pg] = kt[:, pg * page:(pg + 1) * page].reshape(N_HEADS, HEAD_DIM, page)
        vp_ref[0, pg] = vt[:, pg * page:(pg + 1) * page].reshape(N_HEADS, HEAD_DIM, page)
    kb = k.astype(BF16)
    pad_row = lax.broadcasted_iota(jnp.int32, (N_HEADS, VT_ROWS - HEAD_DIM, MOBA_BLOCK), 1)
    ones_rows = jnp.where(pad_row == 0, 1.0, 0.0).astype(BF16)
    for i in range(tm // MOBA_BLOCK):
        r0 = i * MOBA_BLOCK
        qt_ref[0, :, i] = qt[:, r0:r0 + MOBA_BLOCK].reshape(N_HEADS, HEAD_DIM, MOBA_BLOCK).astype(BF16)
        vt_ref[0, :, i, 0:HEAD_DIM, :] = (
            vt[:, r0:r0 + MOBA_BLOCK].reshape(N_HEADS, HEAD_DIM, MOBA_BLOCK).astype(BF16))
        vt_ref[0, :, i, HEAD_DIM:VT_ROWS, :] = ones_rows
        for hh in range(N_HEADS):
            kh_ref[0, hh, i] = kb[r0:r0 + MOBA_BLOCK, hh * HEAD_DIM:(hh + 1) * HEAD_DIM]
        km_ref[0, i] = jnp.mean(k[r0:r0 + MOBA_BLOCK], axis=0, keepdims=True)


def _prompt_inproj(x, g, w, qg, kg, head_ones, page):
    b, t, d = x.shape
    tm = TILE_M
    nt = t // tm
    nb = t // MOBA_BLOCK
    bpt = tm // MOBA_BLOCK
    ppt = tm // page
    n_out = w.shape[1]
    tok = lambda bi, ti: (bi, ti, 0)
    blk5 = lambda bi, ti: (bi, 0, ti, 0, 0)
    pages = lambda bi, ti: (bi, ti, 0, 0, 0)
    return pl.pallas_call(
        functools.partial(_prompt_inproj_body, tm=tm, page=page),
        grid=(b, nt),
        in_specs=[
            pl.BlockSpec((1, tm, d), tok),
            _const_spec((1, d)),
            _const_spec((d, n_out)),
            _const_spec((1, D_B)),
            _const_spec((1, D_B)),
            _const_spec((D_B, D_B)),
        ],
        out_specs=[
            pl.BlockSpec((1, tm, D_A), tok),
            pl.BlockSpec((1, ppt, N_HEADS, HEAD_DIM, page), pages),
            pl.BlockSpec((1, ppt, N_HEADS, HEAD_DIM, page), pages),
            pl.BlockSpec((1, N_HEADS, bpt, HEAD_DIM, MOBA_BLOCK), blk5),
            pl.BlockSpec((1, N_HEADS, bpt, MOBA_BLOCK, HEAD_DIM), blk5),
            pl.BlockSpec((1, N_HEADS, bpt, VT_ROWS, MOBA_BLOCK), blk5),
            pl.BlockSpec((1, bpt, 1, D_B), lambda bi, ti: (bi, ti, 0, 0)),
        ],
        out_shape=[
            jax.ShapeDtypeStruct((b, t, D_A), F32),
            jax.ShapeDtypeStruct((b, t // page, N_HEADS, HEAD_DIM, page), F32),
            jax.ShapeDtypeStruct((b, t // page, N_HEADS, HEAD_DIM, page), F32),
            jax.ShapeDtypeStruct((b, N_HEADS, nb, HEAD_DIM, MOBA_BLOCK), BF16),
            jax.ShapeDtypeStruct((b, N_HEADS, nb, MOBA_BLOCK, HEAD_DIM), BF16),
            jax.ShapeDtypeStruct((b, N_HEADS, nb, VT_ROWS, MOBA_BLOCK), BF16),
            jax.ShapeDtypeStruct((b, nb, 1, D_B), F32),
        ],
        compiler_params=_cparams(2),
        name="prompt_inproj",
    )(x, g, w, qg, kg, head_ones)


def _t5_bias_scalar_table(dist, rb_ref, h):
    out = jnp.full(dist.shape, rb_ref[0, h] * LOG2E, F32)
    for i, th in enumerate(T5_THRESH):
        out = jnp.where(dist >= th, rb_ref[i + 1, h] * LOG2E, out)
    return out


def _moba_prompt_body(rb_ref, ctl_ref, qt_ref, kh_ref, vt_ref, km_ref, ot_ref,
                      bdiag_ref, bsub_ref, selb_ref, m_ref, alpha_ref, acc_ref, s_ref, p_ref, *, nb, hg):
    h0 = pl.program_id(1) * hg
    n = pl.program_id(2)
    blk = MOBA_BLOCK
    bounded = ctl_ref[1, 0] > 0.5
    shifts = [jnp.where(bounded, ctl_ref[0, h0 + g], 0.0) for g in range(hg)]

    @pl.when(n == 0)
    def _():
        ki = lax.broadcasted_iota(jnp.int32, (blk, blk), 0)
        qi = lax.broadcasted_iota(jnp.int32, (blk, blk), 1)
        d0 = qi - ki
        for g in range(hg):
            bdiag_ref[g] = jnp.where(d0 >= 0, _t5_bias_scalar_table(jnp.maximum(d0, 0), rb_ref, h0 + g),
                                     -jnp.inf) - shifts[g]
            bsub_ref[g] = _t5_bias_scalar_table(d0 + blk, rb_ref, h0 + g) - shifts[g]

    bi = lax.broadcasted_iota(jnp.int32, (nb, blk), 0)
    qts = [qt_ref[0, g, 0] for g in range(hg)]

    def select(g):
        km = km_ref[0, g]
        km_hi = km.astype(BF16)
        km_lo = (km - km_hi.astype(F32)).astype(BF16)
        gate = (jnp.dot(km_hi, qts[g], preferred_element_type=F32)
                + jnp.dot(km_lo, qts[g], preferred_element_type=F32))
        avail = jnp.where(bi < n, 1.0, 0.0)
        far_bias = rb_ref[NUM_BUCKETS - 1, h0 + g] * LOG2E - shifts[g]
        selb = jnp.full((nb, blk), -jnp.inf, F32)
        for _ in range(MOBA_TOPK):
            gm = jnp.where(avail > 0.0, gate, -jnp.inf)
            top = jnp.max(gm, axis=0, keepdims=True)
            first = jnp.where(avail > 0.0, jnp.where(gm == top, bi, nb), nb)
            pick = bi == jnp.min(first, axis=0, keepdims=True)
            selb = jnp.where(pick, jnp.where(bi == n - 1, 0.0, far_bias), selb)
            avail = jnp.where(pick, 0.0, avail)
        selb_ref[g] = selb

    n_chunks = blk // SOFTMAX_ROWS

    def logits(g, j, slot, bias):
        s = jnp.dot(kh_ref[0, g, j], qts[g], preferred_element_type=F32)
        s_ref[slot, g] = s if bias is None else s + bias

    def chunk(g, slot, c):
        return s_ref[slot, g, c * SOFTMAX_ROWS:(c + 1) * SOFTMAX_ROWS, :].reshape(
            SOFTMAX_ROWS // SUBLANES, SUBLANES, blk)

    def softmax(g, slot, row):
        m8 = jnp.max(chunk(g, slot, 0), axis=0)
        for c in range(1, n_chunks):
            m8 = jnp.maximum(m8, jnp.max(chunk(g, slot, c), axis=0))
        m_blk = jnp.max(m8, axis=0, keepdims=True)
        if row is None:
            mn = m_blk
            shift = mn
        else:
            m_old = m_ref[g]
            mn = jnp.maximum(m_old, m_blk + row)
            alpha_ref[slot, g] = jnp.exp2(m_old - mn)
            shift = mn - row
        for c in range(n_chunks):
            p = jnp.exp2(chunk(g, slot, c) - shift)
            p_ref[slot, g, c * SOFTMAX_ROWS:(c + 1) * SOFTMAX_ROWS, :] = (
                p.reshape(SOFTMAX_ROWS, blk).astype(BF16))
        m_ref[g] = mn

    def weighted_values(g, j, slot, first):
        pv = jnp.dot(vt_ref[0, g, j], p_ref[slot, g], preferred_element_type=F32)
        acc_ref[g] = pv if first else alpha_ref[slot, g] * acc_ref[g] + pv

    def far_trips(trip, unroll):
        def far(i, carry):
            for r in range(unroll):
                trip(unroll * i + r, r & 1)
            return carry

        n_far = jnp.maximum(n - 1, 0)
        lax.fori_loop(0, n_far // unroll, far, 0)
        done = (n_far // unroll) * unroll
        piece = unroll // 2
        while piece >= 1:
            @pl.when(((n_far - done) & piece) != 0)
            def _(piece=piece, done=done):
                for r in range(piece):
                    trip(done + r, r & 1)
            done = done + ((n_far - done) & piece)
            piece //= 2

    last = jnp.where(n >= 2, n - 2, n - 1)

    @pl.when(bounded)
    def _():
        def probabilities(g, j, slot, bias):
            s = jnp.dot(kh_ref[0, g, j], qts[g], preferred_element_type=F32)
            p_ref[slot, g] = jnp.exp2(s + bias).astype(BF16)

        def add_values(g, j, slot, first):
            pv = jnp.dot(vt_ref[0, g, j], p_ref[slot, g], preferred_element_type=F32)
            acc_ref[g] = pv if first else acc_ref[g] + pv

        before = jnp.maximum(n - 1, 0)
        for g in range(hg):
            probabilities(g, n, 0, bdiag_ref[g])
            select(g)
        for g in range(hg):
            probabilities(g, before, 1, bsub_ref[g] + selb_ref[g, pl.ds(before, 1), :])
            add_values(g, n, 0, True)

        def trip(j, slot):
            prev = jnp.where(j == 0, before, j - 1)
            for g in range(hg):
                probabilities(g, j, slot, selb_ref[g, pl.ds(j, 1), :])
                add_values(g, prev, 1 - slot, False)

        far_trips(trip, BOUNDED_UNROLL)
        last_block = jnp.where(n >= 2, n - 2, before)
        last_slot = jnp.where(n >= 2, n & 1, 1)
        for g in range(hg):
            add_values(g, last_block, last_slot, False)

    @pl.when(jnp.logical_not(bounded))
    def _():
        for g in range(hg):
            logits(g, n, 0, bdiag_ref[g])
        for g in range(hg):
            select(g)
        for g in range(hg):
            logits(g, jnp.maximum(n - 1, 0), 1, bsub_ref[g])
        for g in range(hg):
            softmax(g, 0, None)

        @pl.when(n == 0)
        def _():
            for g in range(hg):
                weighted_values(g, n, 0, True)

        @pl.when(n >= 1)
        def _():
            for g in range(hg):
                weighted_values(g, n, 0, True)
                softmax(g, 1, selb_ref[g, pl.ds(n - 1, 1), :])
                logits(g, 0, 0, None)

            def trip(j, slot):
                nxt = jnp.minimum(j + 1, n - 2)
                prev = jnp.where(j == 0, n - 1, j - 1)
                for g in range(hg):
                    logits(g, nxt, 1 - slot, None)
                for g in range(hg):
                    weighted_values(g, prev, 1 - slot, False)
                for g in range(hg):
                    softmax(g, slot, selb_ref[g, pl.ds(j, 1), :])

            far_trips(trip, FAR_UNROLL)
            for g in range(hg):
                weighted_values(g, last, n & 1, False)

    for g in range(hg):
        ot_ref[0, g * HEAD_DIM:(g + 1) * HEAD_DIM, :] = (
            acc_ref[g, 0:HEAD_DIM, :] / acc_ref[g, HEAD_DIM:HEAD_DIM + 1, :])


def _logit_bound(rel_bias, q_gain, k_gain):
    qk = 1.02 * HEAD_DIM * Q_SCALE * jnp.max(jnp.abs(q_gain)) * jnp.max(jnp.abs(k_gain))
    table = rel_bias * LOG2E
    top, low = jnp.max(table, axis=0), jnp.min(table, axis=0)
    usable = jnp.all(2.0 * qk + (top - low) <= MAX_SHIFT_RANGE)
    return jnp.stack([qk + top, jnp.broadcast_to(usable.astype(F32), top.shape)])


def _moba_prompt(rel_bias, ctl, qt, kh, vt, kmh):
    b, nh, nb, hd, blk = qt.shape
    t = nb * blk
    hg = HEAD_GROUP
    grp = lambda bi, hi, ni: (bi, hi, 0, 0, 0)
    return pl.pallas_call(
        functools.partial(_moba_prompt_body, nb=nb, hg=hg),
        grid=(b, nh // hg, nb),
        in_specs=[
            pl.BlockSpec(memory_space=pltpu.SMEM),
            pl.BlockSpec(memory_space=pltpu.SMEM),
            pl.BlockSpec((1, hg, 1, hd, blk), lambda bi, hi, ni: (bi, hi, ni, 0, 0)),
            pl.BlockSpec((1, hg, nb, blk, hd), grp, pipeline_mode=pl.Buffered(1)),
            pl.BlockSpec((1, hg, nb, VT_ROWS, blk), grp, pipeline_mode=pl.Buffered(1)),
            pl.BlockSpec((1, hg, nb, hd), lambda bi, hi, ni: (bi, hi, 0, 0)),
        ],
        out_specs=pl.BlockSpec((1, hg * hd, blk), lambda bi, hi, ni: (bi, hi, ni)),
        out_shape=jax.ShapeDtypeStruct((b, nh * hd, t), F32),
        scratch_shapes=[
            pltpu.VMEM((hg, blk, blk), F32),
            pltpu.VMEM((hg, blk, blk), F32),
            pltpu.VMEM((hg, nb, blk), F32),
            pltpu.VMEM((hg, 1, blk), F32),
            pltpu.VMEM((2, hg, 1, blk), F32),
            pltpu.VMEM((hg, VT_ROWS, blk), F32),
            pltpu.VMEM((2, hg, blk, blk), F32),
            pltpu.VMEM((2, hg, blk, blk), BF16),
        ],
        compiler_params=_cparams(3),
        name="prompt_moba",
    )(rel_bias, ctl, qt, kh, vt, kmh)


def _prompt_mix_out_body(a_ref, halo_ref, cw_ref, cb_ref, lg_ref, lb_ref, ot_ref, x_ref, w_ref,
                         y_ref, sh_ref, conv_ref, wb_ref, *, tm):
    t = pl.program_id(1)
    rows = tm + HALO_A

    @pl.when((pl.program_id(0) == 0) & (t == 0))
    def _():
        for j in range(CONV_A):
            wb_ref[j] = jnp.broadcast_to(cw_ref[j:j + 1, :], (SUBLANES, D_A))

    sh_ref[0, 0:HALO_A] = jnp.where(t == 0, 0.0, halo_ref[0])
    sh_ref[0, HALO_A:rows] = a_ref[0]
    for s in range(1, SUBLANES):
        sh_ref[s, 0:rows - SUBLANES] = sh_ref[0, s:s + rows - SUBLANES]

    groups = CONV_ROWS // SUBLANES
    first_off = HALO_A - (CONV_A - 1)

    def chunk(c, carry):
        r0 = pl.multiple_of(c * CONV_ROWS, CONV_ROWS)
        for l0 in range(0, D_A, CONV_LANES):
            lanes = slice(l0, l0 + CONV_LANES)
            acc = [jnp.zeros((SUBLANES, CONV_LANES), F32) for _ in range(groups)]
            for s in range(SUBLANES):
                offs = [o for o in range(first_off, first_off + CONV_A) if o % SUBLANES == s]
                qs = [o // SUBLANES for o in offs]
                slab = {gq: sh_ref[s, pl.ds(r0 + gq * SUBLANES, SUBLANES), lanes]
                        for gq in range(min(qs), max(qs) + groups)}
                for o, q in zip(offs, qs):
                    w = wb_ref[o - first_off, :, lanes]
                    for g in range(groups):
                        acc[g] = acc[g] + w * slab[g + q]
            for g in range(groups):
                conv_ref[pl.ds(r0 + g * SUBLANES, SUBLANES), lanes] = acc[g]
        return carry

    lax.fori_loop(0, tm // CONV_ROWS, chunk, 0)

    ao = _silu(_layernorm(conv_ref[...] + cb_ref[...], lg_ref[...], lb_ref[...])).astype(BF16)
    o = ot_ref[0].T.astype(BF16)
    y = (jnp.dot(ao, w_ref[0:D_A, :], preferred_element_type=F32)
         + jnp.dot(o, w_ref[D_A:D_A + D_B, :], preferred_element_type=F32))
    y_ref[0] = x_ref[0] + y


def _prompt_mix_out(a, cw, cb, lg, lb, ot, x, w):
    b, t, d = x.shape
    tm = TILE_M
    nt = t // tm
    hpt = tm // HALO_A
    tok = lambda bi, ti: (bi, ti, 0)
    return pl.pallas_call(
        functools.partial(_prompt_mix_out_body, tm=tm),
        grid=(b, nt),
        in_specs=[
            pl.BlockSpec((1, tm, D_A), tok),
            pl.BlockSpec((1, HALO_A, D_A), lambda bi, ti: (bi, jnp.maximum(ti * hpt - 1, 0), 0)),
            _const_spec((CONV_A, D_A)),
            _const_spec((1, D_A)),
            _const_spec((1, D_A)),
            _const_spec((1, D_A)),
            pl.BlockSpec((1, D_B, tm), lambda bi, ti: (bi, 0, ti)),
            pl.BlockSpec((1, tm, d), tok),
            _const_spec((D_A + D_B, d)),
        ],
        out_specs=pl.BlockSpec((1, tm, d), tok),
        out_shape=jax.ShapeDtypeStruct((b, t, d), F32),
        scratch_shapes=[
            pltpu.VMEM((SUBLANES, tm + HALO_A, D_A), F32),
            pltpu.VMEM((tm, D_A), F32),
            pltpu.VMEM((CONV_A, SUBLANES, D_A), F32),
        ],
        compiler_params=_cparams(2),
        name="prompt_mix_out",
    )(a, a, cw, cb, lg, lb, ot, x, w)


def _conv3_chunk(u, ext_ref, carry_ref, cw_ref, c0, tm):
    lo = SUBLANES
    carry = carry_ref[:, c0:c0 + CH]
    for k in range(1, 3):
        ext_ref[k - 1, lo:lo + k] = carry[SUBLANES - k:SUBLANES]
        ext_ref[k - 1, lo + k:lo + k + tm] = u
    carry_ref[:, c0:c0 + CH] = u[tm - SUBLANES:tm]
    return (cw_ref[0:1, c0:c0 + CH] * ext_ref[1, lo:lo + tm]
            + cw_ref[1:2, c0:c0 + CH] * ext_ref[0, lo:lo + tm]
            + cw_ref[2:3, c0:c0 + CH] * u)


def _prompt_ffn_body(x_ref, g_ref, wu_ref, cw_ref, cb_ref, wd_ref, y_ref, st_ref,
                     carry_ref, extg_ref, extu_ref, act_ref, *, tm):
    @pl.when(pl.program_id(1) == 0)
    def _():
        carry_ref[...] = jnp.zeros_like(carry_ref)

    x = x_ref[0]
    h = _rmsnorm(x, g_ref[...]).astype(BF16)
    for c in range(D_FF // CH):
        cg = c * CH
        cu = D_FF + c * CH
        ug = jnp.dot(h, wu_ref[0, :, cg:cg + CH], preferred_element_type=F32)
        uu = jnp.dot(h, wu_ref[0, :, cu:cu + CH], preferred_element_type=F32)
        gg = _conv3_chunk(ug, extg_ref, carry_ref, cw_ref, cg, tm) + cb_ref[:, cg:cg + CH]
        gu = _conv3_chunk(uu, extu_ref, carry_ref, cw_ref, cu, tm) + cb_ref[:, cu:cu + CH]
        act_ref[:, cg:cg + CH] = (_silu(gg) * gu).astype(BF16)
    st_ref[0] = carry_ref[...]
    y_ref[0] = x + jnp.dot(act_ref[...], wd_ref[0], preferred_element_type=F32)


def _prompt_ffn(x, g, wu, cw, cb, wd, layer):
    b, t, d = x.shape
    tm = FFN_TILE_M
    nt = t // tm
    tok = lambda bi, ti: (bi, ti, 0)
    return pl.pallas_call(
        functools.partial(_prompt_ffn_body, tm=tm),
        grid=(b, nt),
        in_specs=[
            pl.BlockSpec((1, tm, d), tok),
            _const_spec((1, d)),
            _layer_spec((d, 2 * D_FF), layer),
            _const_spec((CONV_F, 2 * D_FF)),
            _const_spec((1, 2 * D_FF)),
            _layer_spec((D_FF, d), layer),
        ],
        out_specs=[
            pl.BlockSpec((1, tm, d), tok),
            pl.BlockSpec((1, SUBLANES, 2 * D_FF), lambda bi, ti: (bi, 0, 0)),
        ],
        out_shape=[
            jax.ShapeDtypeStruct((b, t, d), F32),
            jax.ShapeDtypeStruct((b, SUBLANES, 2 * D_FF), F32),
        ],
        scratch_shapes=[
            pltpu.VMEM((SUBLANES, 2 * D_FF), F32),
            pltpu.VMEM((2, tm + 2 * SUBLANES, CH), F32),
            pltpu.VMEM((2, tm + 2 * SUBLANES, CH), F32),
            pltpu.VMEM((tm, D_FF), BF16),
        ],
        compiler_params=_cparams(2),
        name="prompt_ffn",
    )(x, g, wu, cw, cb, wd)


def _prompt_mixc_body(x_ref, g_ref, wi_ref, cw_ref, wo_ref, y_ref, st_ref,
                      carry_ref, ext_ref, z_ref, *, tm):
    @pl.when(pl.program_id(1) == 0)
    def _():
        carry_ref[...] = jnp.zeros_like(carry_ref)

    x = x_ref[0]
    h = _rmsnorm(x, g_ref[...]).astype(BF16)
    for c in range(D_C // CH):
        c0 = c * CH
        gb = jnp.dot(h, wi_ref[:, c0:c0 + CH], preferred_element_type=F32)
        gc = jnp.dot(h, wi_ref[:, D_C + c0:D_C + c0 + CH], preferred_element_type=F32)
        u = jnp.dot(h, wi_ref[:, 2 * D_C + c0:2 * D_C + c0 + CH], preferred_element_type=F32)
        conv = _conv3_chunk(gc * u, ext_ref, carry_ref, cw_ref, c0, tm)
        z_ref[:, c0:c0 + CH] = (gb * conv).astype(BF16)
    st_ref[0] = carry_ref[...]
    y_ref[0] = x + jnp.dot(z_ref[...], wo_ref[...], preferred_element_type=F32)


def _prompt_mixc(x, g, wi, cw, wo):
    b, t, d = x.shape
    tm = TILE_M
    nt = t // tm
    tok = lambda bi, ti: (bi, ti, 0)
    return pl.pallas_call(
        functools.partial(_prompt_mixc_body, tm=tm),
        grid=(b, nt),
        in_specs=[
            pl.BlockSpec((1, tm, d), tok),
            _const_spec((1, d)),
            _const_spec((d, 3 * D_C)),
            _const_spec((CONV_C, D_C)),
            _const_spec((D_C, d)),
        ],
        out_specs=[
            pl.BlockSpec((1, tm, d), tok),
            pl.BlockSpec((1, SUBLANES, D_C), lambda bi, ti: (bi, 0, 0)),
        ],
        out_shape=[
            jax.ShapeDtypeStruct((b, t, d), F32),
            jax.ShapeDtypeStruct((b, SUBLANES, D_C), F32),
        ],
        scratch_shapes=[
            pltpu.VMEM((SUBLANES, D_C), F32),
            pltpu.VMEM((2, tm + 2 * SUBLANES, CH), F32),
            pltpu.VMEM((tm, D_C), BF16),
        ],
        compiler_params=_cparams(2),
        name="prompt_mixc",
    )(x, g, wi, cw, wo)


def _sample_even_body(x_ref, g_ref, w_ref, qg_ref, kg_ref, ho_ref, sa_ref, cw_ref, cb_ref, lg_ref, lb_ref,
                      ao_ref, san_ref, q_ref, k_ref, v_ref):
    a, q, k, v = _in_proj_even(x_ref[...], g_ref[...], w_ref[...], qg_ref[...], kg_ref[...], ho_ref[...])
    q_ref[...] = q
    k_ref[...] = k
    v_ref[...] = v
    hist = CONV_A - 1
    conv = cb_ref[...] + cw_ref[hist:hist + 1, :] * a
    for j in range(hist):
        conv = conv + cw_ref[j:j + 1, :] * sa_ref[j]
    ao_ref[...] = _silu(_layernorm(conv, lg_ref[...], lb_ref[...])).astype(BF16)
    san_ref[0:hist - 1] = sa_ref[1:hist]
    san_ref[hist - 1] = a


def _sample_even(x, g, w, qg, kg, head_ones, sa, cw, cb, lg, lb):
    m = x.shape[0]
    return pl.pallas_call(
        _sample_even_body,
        out_shape=[
            jax.ShapeDtypeStruct((m, D_A), BF16),
            jax.ShapeDtypeStruct(sa.shape, F32),
            jax.ShapeDtypeStruct((m, D_B), F32),
            jax.ShapeDtypeStruct((m, D_B), F32),
            jax.ShapeDtypeStruct((m, D_B), F32),
        ],
        compiler_params=pltpu.CompilerParams(vmem_limit_bytes=VMEM_LIMIT),
        name="sample_even",
    )(x, g, w, qg, kg, head_ones, sa, cw, cb, lg, lb)


def _moba_sample_keys_body(pt_ref, q_ref, qdh_ref, kn_ref, rbt_ref, *refs, n_pages, page):
    del pt_ref
    outs = refs[SEQ_PER_STEP * n_pages:]
    bias_ref = outs[4]
    past = n_pages * page

    @pl.when(pl.program_id(0) == 0)
    def _():
        pos = lax.broadcasted_iota(jnp.int32, (N_HEADS, past), 1)
        dist = past - pos
        out = jnp.zeros((N_HEADS, past), F32) + rbt_ref[:, 0:1]
        for i, th in enumerate(T5_THRESH):
            out = jnp.where(dist >= th, rbt_ref[:, i + 1:i + 2], out)
        bias_ref[...] = out * LOG2E

    for sq in range(SEQ_PER_STEP):
        _moba_sample_keys_one(sq, q_ref, qdh_ref, kn_ref, rbt_ref, refs[sq * n_pages:(sq + 1) * n_pages], *outs,
                              n_pages=n_pages, page=page)


def _moba_sample_keys_one(sq, q_ref, qdh_ref, kn_ref, rbt_ref, k_pages, pc_ref, pn_ref, l_ref, sel_ref,
                          bias_ref, qb_ref, s_ref, *, n_pages, page):
    past = n_pages * page
    n_past_blocks = past // MOBA_BLOCK
    n_sel = min(MOBA_TOPK, n_past_blocks)
    qb_ref = qb_ref.at[sq]
    s_ref = s_ref.at[sq]

    head_of_lane = lax.broadcasted_iota(jnp.int32, (N_HEADS, D_B), 1) // HEAD_DIM
    hmask = head_of_lane == lax.broadcasted_iota(jnp.int32, (N_HEADS, D_B), 0)
    qbd = jnp.where(hmask, q_ref[sq], 0.0)

    qdh = qdh_ref[sq]
    for h in range(N_HEADS):
        qb_ref[h] = jnp.broadcast_to(qdh[:, h:h + 1], (HEAD_DIM, page))

    for pg in range(n_pages):
        rows = [jnp.sum(k_pages[pg][0, h] * qb_ref[h], axis=0, keepdims=True) for h in range(N_HEADS)]
        s_ref[:, pg * page:(pg + 1) * page] = jnp.concatenate(rows, axis=0)

    gates = [jnp.sum(s_ref[:, j * MOBA_BLOCK:(j + 1) * MOBA_BLOCK], axis=-1, keepdims=True)
             for j in range(n_past_blocks)]
    ranks = []
    pieces = []
    for j in range(n_past_blocks):
        rank = jnp.zeros((N_HEADS, 1), jnp.int32)
        for i in range(n_past_blocks):
            if i == j:
                continue
            ahead = (gates[i] >= gates[j]) if i < j else (gates[i] > gates[j])
            rank = rank + jnp.where(ahead, 1, 0)
        ranks.append(rank)
        lanes = slice(j * MOBA_BLOCK, (j + 1) * MOBA_BLOCK)
        pieces.append(s_ref[:, lanes] + bias_ref[:, lanes] + jnp.where(rank < n_sel, 0.0, -jnp.inf))
    s = jnp.concatenate(pieces, axis=-1)

    s_new = jnp.sum(qbd * kn_ref[sq], axis=-1, keepdims=True) + rbt_ref[:, 0:1] * LOG2E
    m = jnp.maximum(jnp.max(s, axis=-1, keepdims=True), s_new)
    p = jnp.exp2(s - m)
    p_new = jnp.exp2(s_new - m)
    pn_ref[sq] = jnp.broadcast_to(p_new, (N_HEADS, 128))
    l_ref[sq] = jnp.broadcast_to(jnp.sum(p, axis=-1, keepdims=True) + p_new, (N_HEADS, 128))

    lane = lax.broadcasted_iota(jnp.int32, (N_HEADS, 128), 1)
    sel = jnp.zeros((N_HEADS, 128), jnp.int32)
    for r in range(n_sel):
        picked = jnp.zeros((N_HEADS, MOBA_BLOCK), F32)
        block_id = jnp.zeros((N_HEADS, 1), jnp.int32)
        for j in range(n_past_blocks):
            picked = picked + jnp.where(ranks[j] == r, p[:, j * MOBA_BLOCK:(j + 1) * MOBA_BLOCK], 0.0)
            block_id = block_id + jnp.where(ranks[j] == r, j, 0)
        pc_ref[sq, :, r * MOBA_BLOCK:(r + 1) * MOBA_BLOCK] = picked
        sel = jnp.where(lane == r, block_id, sel)
    sel_ref[sq] = sel


def _moba_sample_values_body(pt_ref, sel_ref, pc_ref, pn_ref, l_ref, vn_ref, v_hbm, o_ref, buf_ref, sem_ref,
                             *, n_sel, ppb, page):
    b = pl.program_id(0)
    n_tiles = n_sel * ppb

    def tile_copy(seq, slot, h, i):
        block = sel_ref[seq * (N_HEADS * n_sel) + h * n_sel + i // ppb]
        pool_page = pt_ref[seq, block * ppb + i % ppb]
        return pltpu.make_async_copy(v_hbm.at[pool_page, h], buf_ref.at[slot, h * n_tiles + i], sem_ref.at[slot])

    def for_all_tiles(seq, slot, act):
        for h in range(N_HEADS):
            for i in range(n_tiles):
                act(tile_copy(seq, slot, h, i), h * n_tiles + i)

    start = lambda cp, k: cp.start(priority=k % 2)

    @pl.when(b == 0)
    def _():
        for_all_tiles(0, 0, start)

    @pl.when(b + 1 < pl.num_programs(0))
    def _():
        for_all_tiles(b + 1, (b + 1) % 2, start)

    slot = b % 2
    for_all_tiles(b, slot, lambda cp, k: cp.wait())

    head_of_lane = lax.broadcasted_iota(jnp.int32, (N_HEADS, D_B), 1) // HEAD_DIM
    hmask = head_of_lane == lax.broadcasted_iota(jnp.int32, (N_HEADS, D_B), 0)
    ones = jnp.ones((SUBLANES, page), BF16)
    lane_sum = lambda x: lax.dot_general(ones, x, (((1,), (1,)), ((), ())), preferred_element_type=F32)
    outs = []
    for h in range(N_HEADS):
        acc = jnp.zeros((HEAD_DIM, page), F32)
        for i in range(n_tiles):
            acc = acc + buf_ref[slot, h * n_tiles + i] * pc_ref[0, h:h + 1, i * page:(i + 1) * page]
        hi = acc.astype(BF16)
        lo = (acc - hi.astype(F32)).astype(BF16)
        outs.append((lane_sum(hi) + lane_sum(lo))[0:1, :])
    o_past = jnp.concatenate(outs, axis=-1)
    spread = lambda col: jnp.sum(jnp.where(hmask, col, 0.0), axis=0, keepdims=True)
    o_ref[0] = (o_past + spread(pn_ref[0, :, 0:1]) * vn_ref[0]) / spread(l_ref[0, :, 0:1])


def _moba_sample(page_table, q, k_new, v_new, rel_bias_t, cache_kt, cache_vt):
    m, n_pages = page_table.shape
    page = cache_kt.shape[-1]
    past = n_pages * page
    n_sel = min(MOBA_TOPK, past // MOBA_BLOCK)
    ppb = MOBA_BLOCK // page
    row = lambda bi, *_: (bi, 0, 0)

    sps = SEQ_PER_STEP
    assert m % sps == 0
    keys_spec = pltpu.PrefetchScalarGridSpec(
        num_scalar_prefetch=1,
        grid=(m // sps,),
        in_specs=[
            pl.BlockSpec((sps, 1, D_B), row),
            pl.BlockSpec((sps, HEAD_DIM, N_HEADS), row),
            pl.BlockSpec((sps, 1, D_B), row),
            pl.BlockSpec((N_HEADS, NUM_BUCKETS), lambda bi, pt: (0, 0)),
        ] + [pl.BlockSpec((1, N_HEADS, HEAD_DIM, page), lambda bi, pt, sq=sq, pg=pg: (pt[bi * sps + sq, pg], 0, 0, 0))
             for sq in range(sps) for pg in range(n_pages)],
        out_specs=[
            pl.BlockSpec((sps, N_HEADS, n_sel * MOBA_BLOCK), row),
            pl.BlockSpec((sps, N_HEADS, 128), row),
            pl.BlockSpec((sps, N_HEADS, 128), row),
            pl.BlockSpec((sps, N_HEADS, 128), row),
        ],
        scratch_shapes=[
            pltpu.VMEM((N_HEADS, past), F32),
            pltpu.VMEM((sps, N_HEADS, HEAD_DIM, page), F32),
            pltpu.VMEM((sps, N_HEADS, past), F32),
        ],
    )
    q_dh = q.reshape(m, N_HEADS, HEAD_DIM).transpose(0, 2, 1)
    pc, pn, l, sel = pl.pallas_call(
        functools.partial(_moba_sample_keys_body, n_pages=n_pages, page=page),
        grid_spec=keys_spec,
        out_shape=[
            jax.ShapeDtypeStruct((m, N_HEADS, n_sel * MOBA_BLOCK), F32),
            jax.ShapeDtypeStruct((m, N_HEADS, 128), F32),
            jax.ShapeDtypeStruct((m, N_HEADS, 128), F32),
            jax.ShapeDtypeStruct((m, N_HEADS, 128), jnp.int32),
        ],
        compiler_params=_cparams(1),
        name="sample_moba_keys",
    )(page_table, q.reshape(m, 1, D_B), q_dh, k_new.reshape(m, 1, D_B), rel_bias_t, *([cache_kt] * (sps * n_pages)))

    sel_flat = sel[:, :, :n_sel].reshape(m * N_HEADS * n_sel)
    values_spec = pltpu.PrefetchScalarGridSpec(
        num_scalar_prefetch=2,
        grid=(m,),
        in_specs=[
            pl.BlockSpec((1, N_HEADS, n_sel * MOBA_BLOCK), row),
            pl.BlockSpec((1, N_HEADS, 128), row),
            pl.BlockSpec((1, N_HEADS, 128), row),
            pl.BlockSpec((1, 1, D_B), row),
            pl.BlockSpec(memory_space=pl.ANY),
        ],
        out_specs=pl.BlockSpec((1, 1, D_B), row),
        scratch_shapes=[
            pltpu.VMEM((2, N_HEADS * n_sel * ppb, HEAD_DIM, page), F32),
            pltpu.SemaphoreType.DMA((2,)),
        ],
    )
    return pl.pallas_call(
        functools.partial(_moba_sample_values_body, n_sel=n_sel, ppb=ppb, page=page),
        grid_spec=values_spec,
        out_shape=jax.ShapeDtypeStruct((m, 1, D_B), F32),
        compiler_params=_cparams(1),
        name="sample_moba_values",
    )(page_table, sel_flat, pc, pn, l, v_new.reshape(m, 1, D_B), cache_vt)


def _sample_ffn(x, st_ref, g_ref, wu_ref, cw_ref, cb_ref, wd_ref, stn_ref):
    h = _rmsnorm(x, g_ref[...]).astype(BF16)
    up = jnp.dot(h, wu_ref[0], preferred_element_type=F32)
    older, newer = st_ref[0, :, 0, :], st_ref[0, :, 1, :]
    conv = cw_ref[0:1, :] * older + cw_ref[1:2, :] * newer + cw_ref[2:3, :] * up + cb_ref[...]
    act = (_silu(conv[:, :D_FF]) * conv[:, D_FF:]).astype(BF16)
    stn_ref[:, 0, :] = newer
    stn_ref[:, 1, :] = up
    return x + jnp.dot(act, wd_ref[0], preferred_element_type=F32)


def _sample_out_ffn_body(ao_ref, o_ref, x_ref, wo_ref, st_ref, g_ref, wu_ref, cw_ref, cb_ref, wd_ref,
                         y_ref, stn_ref):
    x1 = (x_ref[...]
          + jnp.dot(ao_ref[...], wo_ref[0:D_A, :], preferred_element_type=F32)
          + jnp.dot(o_ref[...].astype(BF16), wo_ref[D_A:D_A + D_B, :], preferred_element_type=F32))
    y_ref[...] = _sample_ffn(x1, st_ref, g_ref, wu_ref, cw_ref, cb_ref, wd_ref, stn_ref)


def _whole(a):
    return _const_spec(a.shape)


def _whole_out(shape):
    nd = len(shape)
    return pl.BlockSpec(tuple(shape), lambda *_: (0,) * nd)


def _sample_out_ffn(ao, o, x, wo, st, g, wu, cw, cb, wd, layer):
    st_shape = st.shape[1:]
    return pl.pallas_call(
        _sample_out_ffn_body,
        grid=(1,),
        in_specs=[_whole(ao), _whole(o), _whole(x), _whole(wo), _layer_spec(st_shape, layer), _whole(g),
                  _layer_spec(wu.shape[1:], layer), _whole(cw), _whole(cb), _layer_spec(wd.shape[1:], layer)],
        out_specs=[_whole_out(x.shape), _whole_out(st_shape)],
        out_shape=[jax.ShapeDtypeStruct(x.shape, F32), jax.ShapeDtypeStruct(st_shape, F32)],
        compiler_params=_cparams(1),
        name="sample_out_ffn",
    )(ao, o, x, wo, st, g, wu, cw, cb, wd)


def _sample_odd_body(x_ref, gm_ref, wi_ref, ccw_ref, sc_ref, wo_ref, st_ref, g_ref, wu_ref, cw_ref, cb_ref,
                     wd_ref, y_ref, scn_ref, stn_ref):
    x = x_ref[...]
    h = _rmsnorm(x, gm_ref[...]).astype(BF16)
    z = jnp.dot(h, wi_ref[...], preferred_element_type=F32)
    gcu = z[:, D_C:2 * D_C] * z[:, 2 * D_C:]
    older, newer = sc_ref[:, 0, :], sc_ref[:, 1, :]
    conv = ccw_ref[0:1, :] * older + ccw_ref[1:2, :] * newer + ccw_ref[2:3, :] * gcu
    scn_ref[:, 0, :] = newer
    scn_ref[:, 1, :] = gcu
    x1 = x + jnp.dot((z[:, :D_C] * conv).astype(BF16), wo_ref[...], preferred_element_type=F32)
    y_ref[...] = _sample_ffn(x1, st_ref, g_ref, wu_ref, cw_ref, cb_ref, wd_ref, stn_ref)


def _sample_odd(x, gm, wi, ccw, sc, wo, st, g, wu, cw, cb, wd, layer):
    st_shape = st.shape[1:]
    return pl.pallas_call(
        _sample_odd_body,
        grid=(1,),
        in_specs=[_whole(x), _whole(gm), _whole(wi), _whole(ccw), _whole(sc), _whole(wo),
                  _layer_spec(st_shape, layer), _whole(g), _layer_spec(wu.shape[1:], layer), _whole(cw), _whole(cb),
                  _layer_spec(wd.shape[1:], layer)],
        out_specs=[_whole_out(x.shape), _whole_out(sc.shape), _whole_out(st_shape)],
        out_shape=[jax.ShapeDtypeStruct(x.shape, F32), jax.ShapeDtypeStruct(sc.shape, F32),
                   jax.ShapeDtypeStruct(st_shape, F32)],
        compiler_params=_cparams(1),
        name="sample_odd",
    )(x, gm, wi, ccw, sc, wo, st, g, wu, cw, cb, wd)


def kernel(x_prompt, x_sample, cache_k, cache_v, state_conv_a, state_conv_c, state_ffn, page_table, rel_bias,
           norm_mix_e, w_in_e, conv_a_w, conv_a_b, ln_a_g, ln_a_b, q_norm_g, k_norm_g, w_out_e,
           norm_mix_o, w_in_o, conv_c_w, w_out_o, norm_ffn, w_up, conv_f_w, conv_f_b, w_down):
    b, t, d = x_prompt.shape
    m = x_sample.shape[0]
    n_pool, page = cache_k.shape[1], cache_k.shape[2]
    n_pages = page_table.shape[1]
    assert norm_mix_e.shape[0] == 1 and norm_mix_o.shape[0] == 1 and norm_ffn.shape[0] == 2
    assert x_sample.shape[1] == 1 and t % TILE_M == 0 and t % FFN_TILE_M == 0 and t % page == 0
    assert (n_pages * page) % MOBA_BLOCK == 0 and MOBA_BLOCK % page == 0

    row = lambda v: v.reshape(1, -1)
    w_in_e_b = w_in_e[0].astype(BF16)
    w_out_e_b = w_out_e[0].astype(BF16)
    w_in_o_b = w_in_o[0].astype(BF16)
    w_out_o_b = w_out_o[0].astype(BF16)
    w_up_b = w_up.astype(BF16)
    w_down_b = w_down.astype(BF16)
    qg = row(jnp.tile(q_norm_g[0], N_HEADS))
    kg = row(jnp.tile(k_norm_g[0], N_HEADS))
    lane_head = jnp.arange(D_B, dtype=jnp.int32) // HEAD_DIM
    head_ones = (lane_head[:, None] == lane_head[None, :]).astype(BF16)
    g_e, g_o = row(norm_mix_e[0]), row(norm_mix_o[0])
    cab, lag, lab = row(conv_a_b[0]), row(ln_a_g[0]), row(ln_a_b[0])

    a_p, k_p, v_p, qt, kh, vt, km = _prompt_inproj(x_prompt, g_e, w_in_e_b, qg, kg, head_ones, page)
    nb = t // MOBA_BLOCK
    kmh = km.reshape(b, nb, N_HEADS, HEAD_DIM).transpose(0, 2, 1, 3)
    ot = _moba_prompt(rel_bias, _logit_bound(rel_bias, q_norm_g[0], k_norm_g[0]), qt, kh, vt, kmh)
    x1 = _prompt_mix_out(a_p, conv_a_w[0], cab, lag, lab, ot, x_prompt, w_out_e_b)
    x2, f0 = _prompt_ffn(x1, row(norm_ffn[0]), w_up_b, conv_f_w[0], row(conv_f_b[0]), w_down_b, 0)
    x3, c_st = _prompt_mixc(x2, g_o, w_in_o_b, conv_c_w[0], w_out_o_b)
    y_prompt, f1 = _prompt_ffn(x3, row(norm_ffn[1]), w_up_b, conv_f_w[1], row(conv_f_b[1]), w_down_b, 1)

    k_prompt = k_p.transpose(0, 1, 4, 2, 3)[None]
    v_prompt = v_p.transpose(0, 1, 4, 2, 3)[None]
    a_prompt = a_p[:, t - (CONV_A - 1):, :][None]
    c_prompt = c_st[:, SUBLANES - (CONV_C - 1):, :][None]
    f_prompt = jnp.stack([f0[:, SUBLANES - (CONV_F - 1):, :], f1[:, SUBLANES - (CONV_F - 1):, :]])

    xs = x_sample.reshape(m, d)
    sa = state_conv_a[0].transpose(1, 0, 2)
    sc = state_conv_c[0]
    sf = state_ffn
    ao_s, sa_new, q_s, k_s, v_s = _sample_even(xs, g_e, w_in_e_b, qg, kg, head_ones, sa, conv_a_w[0], cab, lag, lab)
    o_s = _moba_sample(page_table, q_s, k_s, v_s, rel_bias.T,
                       cache_k[0].transpose(0, 2, 3, 1), cache_v[0].transpose(0, 2, 3, 1))
    xs1, sf0 = _sample_out_ffn(ao_s, o_s.reshape(m, D_B), xs, w_out_e_b, sf, row(norm_ffn[0]), w_up_b,
                               conv_f_w[0], row(conv_f_b[0]), w_down_b, 0)
    ys, sc_new, sf1 = _sample_odd(xs1, g_o, w_in_o_b, conv_c_w[0], sc, w_out_o_b, sf, row(norm_ffn[1]),
                                  w_up_b, conv_f_w[1], row(conv_f_b[1]), w_down_b, 1)

    y_sample = ys.reshape(m, 1, d)
    k_sample = k_s.reshape(1, m, 1, N_HEADS, HEAD_DIM)
    v_sample = v_s.reshape(1, m, 1, N_HEADS, HEAD_DIM)
    a_sample = sa_new.transpose(1, 0, 2)[None]
    c_sample = sc_new[None]
    f_sample = jnp.stack([sf0, sf1])
    return (y_prompt, y_sample, k_prompt, v_prompt, a_prompt, c_prompt, f_prompt,
            k_sample, v_sample, a_sample, c_sample, f_sample)
```
